```python
import math
import jax
import jax.numpy as jnp
from jax import lax
import numpy as np

D_MODEL = 1024
BATCH = 16
SEQ = 256
DEPTH = 4
DEC_BATCH = 8
DEC_SEQ = 4096
PAST_LEN = 256

GRID_W = 64
HEAD_DIM = 64
SSM_CH = 256
SSM_GROUP = 16
SSM_GROUPS = SSM_CH // SSM_GROUP
SSM_STATE = 64
DT_MIN = 0.001
DT_MAX = 0.1
MLA_HEADS = 6
MLA_Q_RANK = 256
MLA_KV_RANK = 128
MLA_NOPE = 64
MLA_ROPE = 32
MLA_V = 64
SWA_HEADS = 6
SWA_KV_HEADS = 2
SWA_GROUP = SWA_HEADS // SWA_KV_HEADS
SWA_WINDOW = 128
SWA_BLOCK = 128
Q_BLOCK = 128
MOE_GROUPS = 4
MOE_PER_GROUP = 8
MOE_TOPK = 2
MOE_HIDDEN = 256
ROPE_BASE = 10000.0
EPS = 1e-6
D_MIX = SSM_CH + MLA_HEADS * MLA_V + SWA_HEADS * HEAD_DIM
N_IN = SSM_CH + MLA_Q_RANK + MLA_KV_RANK + MLA_ROPE + (SWA_HEADS + 2 * SWA_KV_HEADS) * HEAD_DIM
IN_SPLITS = (
    SSM_CH,
    SSM_CH + MLA_Q_RANK,
    SSM_CH + MLA_Q_RANK + MLA_KV_RANK,
    SSM_CH + MLA_Q_RANK + MLA_KV_RANK + MLA_ROPE,
    SSM_CH + MLA_Q_RANK + MLA_KV_RANK + MLA_ROPE + SWA_HEADS * HEAD_DIM,
    SSM_CH + MLA_Q_RANK + MLA_KV_RANK + MLA_ROPE + (SWA_HEADS + SWA_KV_HEADS) * HEAD_DIM,
)
MLA_SCALE = 1.0 / math.sqrt(MLA_NOPE + MLA_ROPE)
SWA_SCALE = 1.0 / math.sqrt(HEAD_DIM)

kernel_name = 'hybrid_s5_mla_swa_hmoe_diffusion_step'


def rmsnorm(x, g):
    xf = x.astype(jnp.float32)
    y = xf * lax.rsqrt(jnp.mean(xf * xf, axis=-1, keepdims=True) + EPS)
    return (y * g.astype(jnp.float32)).astype(x.dtype)


def axial_rope_tables(n_tokens, rot_dim):
    rows = n_tokens // GRID_W
    t = jnp.arange(rows * GRID_W)
    row = (t // GRID_W).astype(jnp.float32)
    col = (t % GRID_W).astype(jnp.float32)
    n_freq = rot_dim // 4
    inv_freq = ROPE_BASE ** (-jnp.arange(n_freq, dtype=jnp.float32) / n_freq)
    ang = jnp.concatenate([row[:, None] * inv_freq, col[:, None] * inv_freq], axis=-1)
    return jnp.cos(ang), jnp.sin(ang)


def apply_rope(x, cos, sin):
    half = x.shape[-1] // 2
    x1 = x[..., :half].astype(jnp.float32)
    x2 = x[..., half:].astype(jnp.float32)
    c = cos[None, :, None, :]
    s = sin[None, :, None, :]
    return jnp.concatenate([x1 * c - x2 * s, x1 * s + x2 * c], axis=-1).astype(x.dtype)


def modulation(cond, P, l):
    return jnp.einsum('...d,de->...e', jax.nn.silu(cond), P['w_ada'][l]) + P['b_ada'][l]


def _complex_affine_combine(e1, e2):
    a1r, a1i, b1r, b1i = e1
    a2r, a2i, b2r, b2i = e2
    return (a1r * a2r - a1i * a2i,
            a1r * a2i + a1i * a2r,
            a2r * b1r - a2i * b1i + b2r,
            a2r * b1i + a2i * b1r + b2i)


def ssm_mixer(u, P, l, h0):
    B, S, _ = u.shape
    f32 = jnp.float32
    uf = u.astype(f32)
    ug = uf.reshape(B, S, SSM_GROUPS, SSM_GROUP)
    y = (P['ssm_d'][l].astype(f32) * uf).reshape(B, S, SSM_GROUPS, SSM_GROUP)
    finals = []
    for d in range(2):
        rev = d == 1
        lam_re = P['ssm_a_re'][l, d].astype(f32)
        lam_im = P['ssm_a_im'][l, d].astype(f32)
        dt = jnp.exp(P['ssm_log_dt'][l, d].astype(f32))[:, None]
        z_re = lam_re * dt
        z_im = lam_im * dt
        mag = jnp.exp(z_re)
        ab_re = mag * jnp.cos(z_im)
        ab_im = mag * jnp.sin(z_im)
        den = lam_re * lam_re + lam_im * lam_im
        f_re = ((ab_re - 1.0) * lam_re + ab_im * lam_im) / den
        f_im = (ab_im * lam_re - (ab_re - 1.0) * lam_im) / den
        b_re = P['ssm_b_re'][l, d].astype(f32)
        b_im = P['ssm_b_im'][l, d].astype(f32)
        bb_re = f_re[..., None] * b_re - f_im[..., None] * b_im
        bb_im = f_re[..., None] * b_im + f_im[..., None] * b_re
        bu_re = jnp.einsum('bsgc,gpc->bsgp', ug, bb_re)
        bu_im = jnp.einsum('bsgc,gpc->bsgp', ug, bb_im)
        a_re = jnp.broadcast_to(ab_re, bu_re.shape)
        a_im = jnp.broadcast_to(ab_im, bu_re.shape)
        _, _, h_re, h_im = lax.associative_scan(
            _complex_affine_combine, (a_re, a_im, bu_re, bu_im), reverse=rev, axis=1)
        if h0 is None:
            idx = 0 if rev else S - 1
            finals.append(jnp.stack([h_re[:, idx], h_im[:, idx]], axis=1))
        else:
            steps = (jnp.arange(S, 0, -1) if rev else jnp.arange(1, S + 1)).astype(f32)[:, None, None]
            pmag = jnp.exp(steps * z_re)
            p_re = pmag * jnp.cos(steps * z_im)
            p_im = pmag * jnp.sin(steps * z_im)
            h0_re = h0[:, d, 0].astype(f32)[:, None]
            h0_im = h0[:, d, 1].astype(f32)[:, None]
            h_re, h_im = (h_re + p_re * h0_re - p_im * h0_im,
                          h_im + p_re * h0_im + p_im * h0_re)
        c_re = P['ssm_c_re'][l, d].astype(f32)
        c_im = P['ssm_c_im'][l, d].astype(f32)
        y = y + jnp.einsum('bsgp,gcp->bsgc', h_re, c_re) - jnp.einsum('bsgp,gcp->bsgc', h_im, c_im)
    ga = jnp.einsum('bsc,ce->bse', jax.nn.gelu(y.reshape(B, S, SSM_CH)), P['w_ssm_glu'][l].astype(f32))
    out = ga[..., :SSM_CH] * jax.nn.sigmoid(ga[..., SSM_CH:])
    state = jnp.stack(finals, axis=1) if h0 is None else None
    return out.astype(u.dtype), state


def mla_keys_values(ckv, k_rope, w_kvb):
    kv = jnp.einsum('bsr,rhe->bshe', ckv, w_kvb)
    k_nope = kv[..., :MLA_NOPE]
    v = kv[..., MLA_NOPE:]
    k_pe = jnp.broadcast_to(k_rope[:, :, None, :], k_rope.shape[:2] + (MLA_HEADS, MLA_ROPE))
    return jnp.concatenate([k_nope, k_pe], axis=-1), v


def dense_attention(q, k, v, scale, sink=None):
    B, Sq, KH, G, Dk = q.shape
    Dv = v.shape[-1]
    nb = Sq // Q_BLOCK
    qb = jnp.moveaxis(q.reshape(B, nb, Q_BLOCK, KH, G, Dk), 1, 0)

    def attend(qi):
        s = jnp.einsum('bqkgd,bskd->bkgqs', qi, k).astype(jnp.float32) * scale
        if sink is not None:
            s_sink = jnp.broadcast_to(sink.astype(jnp.float32)[None, :, :, None, None], s.shape[:-1] + (1,))
            p = jax.nn.softmax(jnp.concatenate([s, s_sink], axis=-1), axis=-1)[..., :-1]
        else:
            p = jax.nn.softmax(s, axis=-1)
        return jnp.einsum('bkgqs,bskd->bqkgd', p.astype(v.dtype), v)

    o = lax.map(attend, qb)
    return jnp.moveaxis(o, 0, 1).reshape(B, Sq, KH, G, Dv)


def banded_attention(q, k, v, k_ctx, v_ctx, sink, scale):
    B, S, KH, G, Dh = q.shape
    nb = S // SWA_BLOCK
    L = 3 * SWA_BLOCK
    C = k_ctx.shape[1]
    pad = ((0, 0), (SWA_BLOCK, SWA_BLOCK), (0, 0), (0, 0))
    kp = jnp.pad(k, pad).reshape(B, nb + 2, SWA_BLOCK, KH, Dh)
    vp = jnp.pad(v, pad).reshape(B, nb + 2, SWA_BLOCK, KH, Dh)
    kb = jnp.concatenate([kp[:, :-2], kp[:, 1:-1], kp[:, 2:]], axis=2)
    vb = jnp.concatenate([vp[:, :-2], vp[:, 1:-1], vp[:, 2:]], axis=2)
    qb = q.reshape(B, nb, SWA_BLOCK, KH, G, Dh)
    s_loc = jnp.einsum('bnqkgd,bnskd->bnkgqs', qb, kb).astype(jnp.float32) * scale
    qpos = jnp.arange(S).reshape(nb, SWA_BLOCK)
    kpos = (jnp.arange(nb)[:, None] - 1) * SWA_BLOCK + jnp.arange(L)[None, :]
    valid = ((jnp.abs(qpos[:, :, None] - kpos[:, None, :]) <= SWA_WINDOW)
             & (kpos[:, None, :] >= 0) & (kpos[:, None, :] < S))
    s_loc = jnp.where(valid[None, :, None, None], s_loc, -jnp.inf)
    s_ctx = jnp.einsum('bnqkgd,bckd->bnkgqc', qb, k_ctx).astype(jnp.float32) * scale
    s_sink = jnp.broadcast_to(sink.astype(jnp.float32)[None, None, :, :, None, None], s_loc.shape[:-1] + (1,))
    p = jax.nn.softmax(jnp.concatenate([s_loc, s_ctx, s_sink], axis=-1), axis=-1)
    p_loc = p[..., :L].astype(v.dtype)
    p_ctx = p[..., L:L + C].astype(v.dtype)
    o = (jnp.einsum('bnkgqs,bnskd->bnqkgd', p_loc, vb)
         + jnp.einsum('bnkgqc,bckd->bnqkgd', p_ctx, v_ctx))
    return o.reshape(B, S, KH, G, Dh)


def hier_moe(h, P, l):
    B, S, D = h.shape
    x = h.reshape(B * S, D)
    f32 = jnp.float32
    lg = (x @ P['moe_w_group'][l] + P['moe_b_group'][l]).astype(f32)
    pg_top, g_idx = lax.top_k(jax.nn.softmax(lg, axis=-1), 1)
    le = (x @ P['moe_w_expert'][l] + P['moe_b_expert'][l]).astype(f32).reshape(-1, MOE_GROUPS, MOE_PER_GROUP)
    le_sel = jnp.take_along_axis(le, g_idx[:, :, None], axis=1)[:, 0]
    pe_top, e_idx = lax.top_k(jax.nn.softmax(le_sel, axis=-1), MOE_TOPK)
    pe_top = pe_top / jnp.sum(pe_top, axis=-1, keepdims=True)
    w_in_group = jnp.sum(pe_top[..., None] * jax.nn.one_hot(e_idx, MOE_PER_GROUP, dtype=f32), axis=1)
    gate = (pg_top[:, :, None] * jax.nn.one_hot(g_idx[:, 0], MOE_GROUPS, dtype=f32)[:, :, None]
            * w_in_group[:, None, :])
    out = jnp.zeros((B * S, D), f32)
    for g in range(MOE_GROUPS):
        a = jnp.einsum('td,edh->teh', x, P['moe_w_gate'][l, g])
        b = jnp.einsum('td,edh->teh', x, P['moe_w_up'][l, g])
        hid = jax.nn.silu(a) * b * gate[:, g, :, None].astype(x.dtype)
        out = out + jnp.einsum('teh,ehd->td', hid, P['moe_w_down'][l, g]).astype(f32)
    return out.astype(h.dtype).reshape(B, S, D)


def trunk_layer(x, mod, P, l, rope=None, ctx=None):
    B, S, _ = x.shape
    sh1, sc1, g1, sh2, sc2, g2 = jnp.split(mod, 6, axis=-1)
    h = rmsnorm(x, P['norm1'][l]) * (1 + sc1) + sh1
    proj = jnp.einsum('bsd,dn->bsn', h, P['w_in'][l])
    u, q_lora, kv_lora, k_rope, q_s, k_s, v_s = jnp.split(proj, IN_SPLITS, axis=-1)

    y_ssm, ssm_state = ssm_mixer(u, P, l, None if ctx is None else ctx[4])

    q_m = jnp.einsum('bsr,rhe->bshe', rmsnorm(q_lora, P['mla_q_norm'][l]), P['w_mla_qb'][l])
    ckv = rmsnorm(kv_lora, P['mla_kv_norm'][l])
    w_kvb = P['w_mla_kvb'][l]

    q_s = q_s.reshape(B, S, SWA_HEADS, HEAD_DIM)
    k_s = k_s.reshape(B, S, SWA_KV_HEADS, HEAD_DIM)
    v_s = v_s.reshape(B, S, SWA_KV_HEADS, HEAD_DIM)
    sink = P['swa_sink'][l].reshape(SWA_KV_HEADS, SWA_GROUP)

    if ctx is None:
        k_m, v_m = mla_keys_values(ckv, k_rope, w_kvb)
        o_mla = dense_attention(q_m[:, :, :, None], k_m, v_m, MLA_SCALE)
        o_swa = dense_attention(q_s.reshape(B, S, SWA_KV_HEADS, SWA_GROUP, HEAD_DIM), k_s, v_s, SWA_SCALE, sink)
        new_ctx = (ckv, k_rope, k_s, v_s, ssm_state)
    else:
        ckv_c, krope_c, k_c, v_c, _ = ctx
        cos_m, sin_m, cos_s, sin_s = rope
        q_m = jnp.concatenate([q_m[..., :MLA_NOPE], apply_rope(q_m[..., MLA_NOPE:], cos_m, sin_m)], axis=-1)
        k_rope = apply_rope(k_rope[:, :, None, :], cos_m, sin_m)[:, :, 0]
        k_lat, v_lat = mla_keys_values(ckv, k_rope, w_kvb)
        k_ctx, v_ctx = mla_keys_values(ckv_c, krope_c, w_kvb)
        o_mla = dense_attention(q_m[:, :, :, None],
                                jnp.concatenate([k_lat, k_ctx], axis=1),
                                jnp.concatenate([v_lat, v_ctx], axis=1), MLA_SCALE)
        q_s = apply_rope(q_s, cos_s, sin_s)
        k_s = apply_rope(k_s, cos_s, sin_s)
        o_swa = banded_attention(q_s.reshape(B, S, SWA_KV_HEADS, SWA_GROUP, HEAD_DIM),
                                 k_s, v_s, k_c, v_c, sink, SWA_SCALE)
        new_ctx = None

    mixed = jnp.concatenate([
        rmsnorm(y_ssm, P['gn_ssm'][l]),
        rmsnorm(o_mla.reshape(B, S, MLA_HEADS * MLA_V), P['gn_mla'][l]),
        rmsnorm(o_swa.reshape(B, S, SWA_HEADS * HEAD_DIM), P['gn_swa'][l]),
    ], axis=-1)
    x = x + g1 * jnp.einsum('bsm,md->bsd', mixed, P['w_out'][l])
    h2 = rmsnorm(x, P['norm2'][l]) * (1 + sc2) + sh2
    x = x + g2 * hier_moe(h2, P, l)
    return x, new_ctx


def setup_inputs(seed: int = 0) -> dict:
    key = jax.random.key(seed)
    keys = iter(jax.random.split(key, 64))
    f32 = jnp.float32
    D = D_MODEL

    def nrm(shape, scale):
        return scale * jax.random.normal(next(keys), shape, f32)

    def gain(shape):
        return 1.0 + 0.02 * jax.random.normal(next(keys), shape, f32)

    return {
        'x_prompt': nrm((BATCH, SEQ, D), 1.0),
        'x_sample': nrm((DEC_BATCH, DEC_SEQ, D), 1.0),
        'c': nrm((DEC_BATCH, D), 1.0),
        'cache_mla_ckv': nrm((DEC_BATCH, DEPTH, PAST_LEN, MLA_KV_RANK), 1.0),
        'cache_mla_krope': nrm((DEC_BATCH, DEPTH, PAST_LEN, MLA_ROPE), 1.0),
        'cache_swa_k': nrm((DEC_BATCH, DEPTH, PAST_LEN, SWA_KV_HEADS, HEAD_DIM), 1.0),
        'cache_swa_v': nrm((DEC_BATCH, DEPTH, PAST_LEN, SWA_KV_HEADS, HEAD_DIM), 1.0),
        'state_ssm': nrm((DEC_BATCH, DEPTH, 2, 2, SSM_GROUPS, SSM_STATE), 0.3),
        'c_ctx': nrm((D,), 1.0),
        'w_ada': nrm((DEPTH, D, 6 * D), 0.5 * D ** -0.5),
        'b_ada': nrm((DEPTH, 6 * D), 0.02),
        'norm1': gain((DEPTH, D)),
        'norm2': gain((DEPTH, D)),
        'w_in': nrm((DEPTH, D, N_IN), D ** -0.5),
        'ssm_a_re': -0.5 + nrm((DEPTH, 2, SSM_GROUPS, SSM_STATE), 0.01),
        'ssm_a_im': jnp.pi * jnp.arange(SSM_STATE, dtype=f32) + nrm((DEPTH, 2, SSM_GROUPS, SSM_STATE), 0.01),
        'ssm_log_dt': jax.random.uniform(next(keys), (DEPTH, 2, SSM_GROUPS), f32,
                                         minval=math.log(DT_MIN), maxval=math.log(DT_MAX)),
        'ssm_b_re': nrm((DEPTH, 2, SSM_GROUPS, SSM_STATE, SSM_GROUP), (2 * SSM_GROUP) ** -0.5),
        'ssm_b_im': nrm((DEPTH, 2, SSM_GROUPS, SSM_STATE, SSM_GROUP), (2 * SSM_GROUP) ** -0.5),
        'ssm_c_re': nrm((DEPTH, 2, SSM_GROUPS, SSM_GROUP, SSM_STATE), SSM_STATE ** -0.5),
        'ssm_c_im': nrm((DEPTH, 2, SSM_GROUPS, SSM_GROUP, SSM_STATE), SSM_STATE ** -0.5),
        'ssm_d': nrm((DEPTH, SSM_CH), 1.0),
        'w_ssm_glu': nrm((DEPTH, SSM_CH, 2 * SSM_CH), SSM_CH ** -0.5),
        'mla_q_norm': gain((DEPTH, MLA_Q_RANK)),
        'w_mla_qb': nrm((DEPTH, MLA_Q_RANK, MLA_HEADS, MLA_NOPE + MLA_ROPE), MLA_Q_RANK ** -0.5),
        'mla_kv_norm': gain((DEPTH, MLA_KV_RANK)),
        'w_mla_kvb': nrm((DEPTH, MLA_KV_RANK, MLA_HEADS, MLA_NOPE + MLA_V), MLA_KV_RANK ** -0.5),
        'swa_sink': nrm((DEPTH, SWA_HEADS), 0.5),
        'gn_ssm': gain((DEPTH, SSM_CH)),
        'gn_mla': gain((DEPTH, MLA_HEADS * MLA_V)),
        'gn_swa': gain((DEPTH, SWA_HEADS * HEAD_DIM)),
        'w_out': nrm((DEPTH, D_MIX, D), D_MIX ** -0.5),
        'moe_w_group': nrm((DEPTH, D, MOE_GROUPS), D ** -0.5),
        'moe_b_group': nrm((DEPTH, MOE_GROUPS), 0.01),
        'moe_w_expert': nrm((DEPTH, D, MOE_GROUPS * MOE_PER_GROUP), D ** -0.5),
        'moe_b_expert': nrm((DEPTH, MOE_GROUPS * MOE_PER_GROUP), 0.01),
        'moe_w_gate': nrm((DEPTH, MOE_GROUPS, MOE_PER_GROUP, D, MOE_HIDDEN), D ** -0.5),
        'moe_w_up': nrm((DEPTH, MOE_GROUPS, MOE_PER_GROUP, D, MOE_HIDDEN), D ** -0.5),
        'moe_w_down': nrm((DEPTH, MOE_GROUPS, MOE_PER_GROUP, MOE_HIDDEN, D), MOE_HIDDEN ** -0.5),
        'final_norm': gain((D,)),
    }


def reference(x_prompt, x_sample, c, cache_mla_ckv, cache_mla_krope, cache_swa_k, cache_swa_v, state_ssm,
              c_ctx, w_ada, b_ada, norm1, norm2, w_in, ssm_a_re, ssm_a_im, ssm_log_dt, ssm_b_re, ssm_b_im,
              ssm_c_re, ssm_c_im, ssm_d, w_ssm_glu, mla_q_norm, w_mla_qb, mla_kv_norm, w_mla_kvb, swa_sink,
              gn_ssm, gn_mla, gn_swa, w_out, moe_w_group, moe_b_group, moe_w_expert, moe_b_expert,
              moe_w_gate, moe_w_up, moe_w_down, final_norm):
    P = dict(w_ada=w_ada, b_ada=b_ada, norm1=norm1, norm2=norm2, w_in=w_in,
             ssm_a_re=ssm_a_re, ssm_a_im=ssm_a_im, ssm_log_dt=ssm_log_dt, ssm_b_re=ssm_b_re,
             ssm_b_im=ssm_b_im, ssm_c_re=ssm_c_re, ssm_c_im=ssm_c_im, ssm_d=ssm_d, w_ssm_glu=w_ssm_glu,
             mla_q_norm=mla_q_norm, w_mla_qb=w_mla_qb, mla_kv_norm=mla_kv_norm, w_mla_kvb=w_mla_kvb,
             swa_sink=swa_sink, gn_ssm=gn_ssm, gn_mla=gn_mla, gn_swa=gn_swa, w_out=w_out,
             moe_w_group=moe_w_group, moe_b_group=moe_b_group, moe_w_expert=moe_w_expert,
             moe_b_expert=moe_b_expert, moe_w_gate=moe_w_gate, moe_w_up=moe_w_up, moe_w_down=moe_w_down)

    xp = x_prompt
    ctx_states = []
    for l in range(DEPTH):
        xp, new = trunk_layer(xp, modulation(c_ctx, P, l), P, l)
        ctx_states.append(new)
    y_prompt = rmsnorm(xp, final_norm)
    new_mla_ckv = jnp.stack([s[0] for s in ctx_states], axis=1)
    new_mla_krope = jnp.stack([s[1] for s in ctx_states], axis=1)
    new_swa_k = jnp.stack([s[2] for s in ctx_states], axis=1)
    new_swa_v = jnp.stack([s[3] for s in ctx_states], axis=1)
    new_state_ssm = jnp.stack([s[4] for s in ctx_states], axis=1)

    n_lat = x_sample.shape[1]
    cos_m, sin_m = axial_rope_tables(n_lat, MLA_ROPE)
    cos_s, sin_s = axial_rope_tables(n_lat, HEAD_DIM)
    rope = (cos_m, sin_m, cos_s, sin_s)
    xs = x_sample
    for l in range(DEPTH):
        ctx = (cache_mla_ckv[:, l], cache_mla_krope[:, l], cache_swa_k[:, l], cache_swa_v[:, l], state_ssm[:, l])
        xs, _ = trunk_layer(xs, modulation(c, P, l)[:, None, :], P, l, rope, ctx)
    y_sample = rmsnorm(xs, final_norm)

    return (y_prompt, y_sample, new_mla_ckv, new_mla_krope, new_swa_k, new_swa_v, new_state_ssm)
```

```python
import functools
import math

import jax
import jax.numpy as jnp
from jax import lax
from jax.experimental import pallas as pl
from jax.experimental.pallas import tpu as pltpu

F32 = jnp.float32
BF16 = jnp.bfloat16

D_MODEL = 1024
DEPTH = 4
GRID_W = 64
HEAD_DIM = 64
SSM_CH = 256
SSM_GROUP = 16
SSM_GROUPS = SSM_CH // SSM_GROUP
SSM_STATE = 64
SSM_N = SSM_GROUPS * SSM_STATE
MLA_HEADS = 6
MLA_Q_RANK = 256
MLA_KV_RANK = 128
MLA_NOPE = 64
MLA_ROPE = 32
MLA_V = 64
SWA_HEADS = 6
SWA_KV_HEADS = 2
SWA_GROUP = SWA_HEADS // SWA_KV_HEADS
SWA_WINDOW = 128
SWA_BLOCK = 128
MOE_GROUPS = 4
MOE_PER_GROUP = 8
MOE_EXPERTS = MOE_GROUPS * MOE_PER_GROUP
MOE_HIDDEN = 256
ROPE_BASE = 10000.0
EPS = 1e-6
MLA_SCALE = 1.0 / math.sqrt(MLA_NOPE + MLA_ROPE)
SWA_SCALE = 1.0 / math.sqrt(HEAD_DIM)

LANES = 128
SUBLANES = 8
VMEM_LIMIT = 52 * 1024 * 1024

OFF_U = 0
OFF_QL = OFF_U + SSM_CH
OFF_KVL = OFF_QL + MLA_Q_RANK
OFF_QS = OFF_KVL + MLA_KV_RANK
OFF_KS = OFF_QS + SWA_HEADS * HEAD_DIM
OFF_VS = OFF_KS + SWA_KV_HEADS * HEAD_DIM
OFF_KR = OFF_VS + SWA_KV_HEADS * HEAD_DIM
N_PACK_CTX = OFF_KR + LANES
OFF_QS_SW = N_PACK_CTX
OFF_KS_SW = OFF_QS_SW + SWA_HEADS * HEAD_DIM
OFF_KR_SW = OFF_KS_SW + SWA_KV_HEADS * HEAD_DIM
N_PACK_LAT = OFF_KR_SW + LANES
Q_ABS = MLA_HEADS * LANES


def _cparams(sem):
    return pltpu.CompilerParams(dimension_semantics=sem, vmem_limit_bytes=VMEM_LIMIT)


def _dot(a, b):
    return jnp.dot(a, b, preferred_element_type=F32)


def _dot_nt(a, b):
    return lax.dot_general(a, b, (((1,), (1,)), ((), ())), preferred_element_type=F32)


def _split(x):
    hi = x.astype(BF16)
    lo = (x - hi.astype(F32)).astype(BF16)
    return hi, lo


def _dot3(a, b):
    ah, al = _split(a)
    bh, bl = _split(b)
    return _dot(ah, bh) + _dot(al, bh) + _dot(ah, bl)


def _rms(x, g):
    return x * lax.rsqrt(jnp.mean(x * x, axis=-1, keepdims=True) + EPS) * g


def _silu(x):
    return x * jax.nn.sigmoid(x)


def _gelu_tanh(x):
    return 0.5 * x * (1.0 + jnp.tanh(math.sqrt(2.0 / math.pi) * (x + 0.044715 * (x * x * x))))


def _mod_kernel(c_ref, w_ref, b_ref, o_ref):
    o_ref[...] = _dot3(_silu(c_ref[...]), w_ref[...]) + b_ref[...]


def _modulation(conds, w_ada, b_ada):
    n = conds.shape[0]
    tn = 1536
    return pl.pallas_call(
        _mod_kernel,
        grid=(DEPTH, 6 * D_MODEL // tn),
        in_specs=[
            pl.BlockSpec((n, D_MODEL), lambda l, j: (0, 0)),
            pl.BlockSpec((None, D_MODEL, tn), lambda l, j: (l, 0, j)),
            pl.BlockSpec((None, 1, tn), lambda l, j: (l, 0, j)),
        ],
        out_specs=pl.BlockSpec((None, n, tn), lambda l, j: (l, 0, j)),
        out_shape=jax.ShapeDtypeStruct((DEPTH, n, 6 * D_MODEL), F32),
        compiler_params=_cparams(("arbitrary", "arbitrary")),
        name="modulation",
    )(conds, w_ada, b_ada.reshape(DEPTH, 1, 6 * D_MODEL))


def _ssm_disc_kernel(are_ref, aim_ref, ldt_ref, bre_ref, bim_ref, abre_ref, abim_ref, bbre_ref, bbim_ref):
    lam_re = are_ref[...]
    lam_im = aim_ref[...]
    dt = jnp.exp(ldt_ref[...])
    z_re = lam_re * dt
    z_im = lam_im * dt
    mag = jnp.exp(z_re)
    ab_re = mag * jnp.cos(z_im)
    ab_im = mag * jnp.sin(z_im)
    den = lam_re * lam_re + lam_im * lam_im
    f_re = ((ab_re - 1.0) * lam_re + ab_im * lam_im) / den
    f_im = (ab_im * lam_re - (ab_re - 1.0) * lam_im) / den
    b_re = bre_ref[...]
    b_im = bim_ref[...]
    abre_ref[...] = ab_re
    abim_ref[...] = ab_im
    bbre_ref[...] = f_re * b_re - f_im * b_im
    bbim_ref[...] = f_re * b_im + f_im * b_re


def _ssm_discretise(a_re, a_im, log_dt, b_re, b_im):
    n = DEPTH * 2
    col = lambda v: v.reshape(n, SSM_N, 1)
    ldt = jnp.broadcast_to(log_dt[..., None], (DEPTH, 2, SSM_GROUPS, SSM_STATE))
    cspec = pl.BlockSpec((None, SSM_N, 1), lambda i: (i, 0, 0))
    bspec = pl.BlockSpec((None, SSM_N, SSM_GROUP), lambda i: (i, 0, 0))
    return pl.pallas_call(
        _ssm_disc_kernel,
        grid=(n,),
        in_specs=[cspec, cspec, cspec, bspec, bspec],
        out_specs=[cspec, cspec, bspec, bspec],
        out_shape=[jax.ShapeDtypeStruct((n, SSM_N, 1), F32)] * 2
        + [jax.ShapeDtypeStruct((n, SSM_N, SSM_GROUP), F32)] * 2,
        compiler_params=_cparams(("arbitrary",)),
        name="ssm_discretise",
    )(col(a_re), col(a_im), col(ldt), b_re.reshape(n, SSM_N, SSM_GROUP), b_im.reshape(n, SSM_N, SSM_GROUP))


def _absorb_kernel(wq_ref, wk_ref, o_ref):
    a = wq_ref[...]
    b = wk_ref[...]
    ah, al = _split(a)
    bh, bl = _split(b)
    o_ref[...] = _dot_nt(ah, bh) + _dot_nt(al, bh) + _dot_nt(ah, bl)


def _absorb_q(wq_nope, wk_nope):
    return pl.pallas_call(
        _absorb_kernel,
        grid=(DEPTH, MLA_HEADS),
        in_specs=[
            pl.BlockSpec((None, None, MLA_Q_RANK, MLA_NOPE), lambda l, h: (l, h, 0, 0)),
            pl.BlockSpec((None, None, MLA_KV_RANK, MLA_NOPE), lambda l, h: (l, h, 0, 0)),
        ],
        out_specs=pl.BlockSpec((None, None, MLA_Q_RANK, MLA_KV_RANK), lambda l, h: (l, h, 0, 0)),
        out_shape=jax.ShapeDtypeStruct((DEPTH, MLA_HEADS, MLA_Q_RANK, MLA_KV_RANK), F32),
        compiler_params=_cparams(("arbitrary", "arbitrary")),
        name="mla_absorb",
    )(wq_nope, wk_nope)


def _pre_kernel(*refs, rope, emit_ctx):
    it = iter(refs)
    x_ref, mod_ref, n1_ref, win_ref, qn_ref, kvn_ref, wq_ref = (next(it) for _ in range(7))
    if rope:
        cm_ref, sm_ref, cs_ref, ss_ref = (next(it) for _ in range(4))
    u_ref, qabs_ref, qrope_ref, kcat_ref, qs_ref, ks_ref, vs_ref = (next(it) for _ in range(7))
    if emit_ctx:
        ckv_o, kr_o, ks_o, vs_o = (next(it) for _ in range(4))

    x = x_ref[...]
    mod = mod_ref[...]
    h = _rms(x, n1_ref[...]) * (1.0 + mod[1:2]) + mod[0:1]
    proj = _dot(h.astype(BF16), win_ref[...])

    u_ref[...] = proj[:, OFF_U:OFF_U + SSM_CH]

    qln = _rms(proj[:, OFF_QL:OFF_QL + MLA_Q_RANK], qn_ref[...]).astype(BF16)
    qall = _dot(qln, wq_ref[...])
    qabs_ref[...] = (qall[:, :Q_ABS] * MLA_SCALE).astype(BF16)
    qr = qall[:, Q_ABS:2 * Q_ABS]
    if rope:
        qr_sw = qall[:, 2 * Q_ABS:3 * Q_ABS]
        cm = cm_ref[...]
        sm = sm_ref[...]
        qr = jnp.concatenate(
            [qr[:, i * LANES:(i + 1) * LANES] * cm + qr_sw[:, i * LANES:(i + 1) * LANES] * sm
             for i in range(MLA_HEADS)], axis=-1)
    qrope_ref[...] = (qr * MLA_SCALE).astype(BF16)

    ckv = _rms(proj[:, OFF_KVL:OFF_KVL + MLA_KV_RANK], kvn_ref[...])
    kr = proj[:, OFF_KR:OFF_KR + LANES]
    if emit_ctx:
        ckv_o[...] = ckv
        kr_o[...] = kr[:, :MLA_ROPE]
    if rope:
        kr = kr * cm + proj[:, OFF_KR_SW:OFF_KR_SW + LANES] * sm
    kcat_ref[...] = jnp.concatenate([ckv, kr], axis=-1).astype(BF16)

    qs = proj[:, OFF_QS:OFF_QS + SWA_HEADS * HEAD_DIM]
    ks = proj[:, OFF_KS:OFF_KS + SWA_KV_HEADS * HEAD_DIM]
    vs = proj[:, OFF_VS:OFF_VS + SWA_KV_HEADS * HEAD_DIM]
    if emit_ctx:
        ks_o[...] = ks
        vs_o[...] = vs
    if rope:
        cs = cs_ref[...]
        ss = ss_ref[...]
        qs_sw = proj[:, OFF_QS_SW:OFF_QS_SW + SWA_HEADS * HEAD_DIM]
        ks = ks * cs + proj[:, OFF_KS_SW:OFF_KS_SW + SWA_KV_HEADS * HEAD_DIM] * ss
        qs = jnp.concatenate(
            [qs[:, i * LANES:(i + 1) * LANES] * cs + qs_sw[:, i * LANES:(i + 1) * LANES] * ss
             for i in range(SWA_HEADS * HEAD_DIM // LANES)], axis=-1)
    qs_ref[...] = (qs * SWA_SCALE).astype(BF16)
    ks_ref[...] = ks.astype(BF16)
    vs_ref[...] = vs.astype(BF16)


def _pre(x, mod, lw, tables, *, per_batch_mod, emit_ctx):
    B, S, D = x.shape
    rope = tables is not None
    ts = min(S, 512)
    n_pack = N_PACK_LAT if rope else N_PACK_CTX
    win = lw["win_lat"] if rope else lw["win_ctx"]
    wq = lw["wq_lat"] if rope else lw["wq_ctx"]
    tok = lambda w: pl.BlockSpec((None, ts, w), lambda b, s: (b, s, 0))
    full = lambda a: pl.BlockSpec(a.shape, lambda b, s: (0,) * a.ndim)
    in_specs = [
        tok(D),
        pl.BlockSpec((None, 6, D), (lambda b, s: (b, 0, 0)) if per_batch_mod else (lambda b, s: (0, 0, 0))),
        full(lw["norm1"]), full(win), full(lw["q_norm"]), full(lw["kv_norm"]), full(wq),
    ]
    args = [x, mod, lw["norm1"], win, lw["q_norm"], lw["kv_norm"], wq]
    if rope:
        in_specs += [pl.BlockSpec((ts, LANES), lambda b, s: (s, 0))] * 4
        args += list(tables)
    out_specs = [
        pl.BlockSpec((ts, SSM_CH), lambda b, s: (s, b)),
        tok(Q_ABS), tok(Q_ABS), tok(2 * LANES), tok(SWA_HEADS * HEAD_DIM), tok(LANES), tok(LANES),
    ]
    out_shape = [
        jax.ShapeDtypeStruct((S, B * SSM_CH), F32),
        jax.ShapeDtypeStruct((B, S, Q_ABS), BF16),
        jax.ShapeDtypeStruct((B, S, Q_ABS), BF16),
        jax.ShapeDtypeStruct((B, S, 2 * LANES), BF16),
        jax.ShapeDtypeStruct((B, S, SWA_HEADS * HEAD_DIM), BF16),
        jax.ShapeDtypeStruct((B, S, LANES), BF16),
        jax.ShapeDtypeStruct((B, S, LANES), BF16),
    ]
    if emit_ctx:
        out_specs += [tok(MLA_KV_RANK), tok(MLA_ROPE), tok(LANES), tok(LANES)]
        out_shape += [
            jax.ShapeDtypeStruct((B, S, MLA_KV_RANK), F32),
            jax.ShapeDtypeStruct((B, S, MLA_ROPE), F32),
            jax.ShapeDtypeStruct((B, S, LANES), F32),
            jax.ShapeDtypeStruct((B, S, LANES), F32),
        ]
    return pl.pallas_call(
        functools.partial(_pre_kernel, rope=rope, emit_ctx=emit_ctx),
        grid=(B, S // ts),
        in_specs=in_specs,
        out_specs=out_specs,
        out_shape=out_shape,
        compiler_params=_cparams(("arbitrary", "arbitrary")),
        name="pre_lat" if rope else "pre_ctx",
    )(*args)


def _ssm_kernel(u_ref, wb_ref, a_ref, wc_ref, h0_ref, y_ref, hfin_ref, hre_s, him_s, st_s, *, tc):
    d = pl.program_id(0)
    i = pl.program_id(2)
    n = pl.num_programs(2)

    @pl.when(i == 0)
    def _():
        st_s[...] = h0_ref[...]

    u = u_ref[...].reshape(tc * SUBLANES, SSM_CH).astype(BF16)
    bu = _dot(u, wb_ref[...])
    hre_s[...] = bu[:, :SSM_N]
    him_s[...] = bu[:, SSM_N:]

    a = a_ref[...]
    a_re = jnp.broadcast_to(a[:, :SSM_N], (SUBLANES, SSM_N))
    a_im = jnp.broadcast_to(a[:, SSM_N:], (SUBLANES, SSM_N))
    st = st_s[...]

    def step(t, carry):
        h_re, h_im = carry
        tt = jnp.where(d == 0, t, tc - 1 - t)
        rows = pl.ds(pl.multiple_of(tt * SUBLANES, SUBLANES), SUBLANES)
        n_re = a_re * h_re - a_im * h_im + hre_s[rows, :]
        n_im = a_re * h_im + a_im * h_re + him_s[rows, :]
        hre_s[rows, :] = n_re
        him_s[rows, :] = n_im
        return n_re, n_im

    h_re, h_im = lax.fori_loop(0, tc, step, (st[:, :SSM_N], st[:, SSM_N:]), unroll=4)
    st_s[...] = jnp.concatenate([h_re, h_im], axis=-1)

    wc = wc_ref[...]
    y = _dot(hre_s[...].astype(BF16), wc[:SSM_N]) + _dot(him_s[...].astype(BF16), wc[SSM_N:])
    y_ref[...] = y.reshape(tc, SUBLANES, SSM_CH)

    @pl.when(i == n - 1)
    def _():
        hfin_ref[...] = st_s[...]


def _ssm(u_tm, lw, h0, B, S):
    nb = B // SUBLANES
    tc = 128
    nchunk = S // tc
    chunk = lambda d, i: i + d * (nchunk - 1 - 2 * i)
    u4 = u_tm.reshape(S, nb, SUBLANES, SSM_CH)
    y, hfin = pl.pallas_call(
        functools.partial(_ssm_kernel, tc=tc),
        grid=(2, nb, nchunk),
        in_specs=[
            pl.BlockSpec((tc, None, SUBLANES, SSM_CH), lambda d, b, i: (chunk(d, i), b, 0, 0)),
            pl.BlockSpec((None, SSM_CH, 2 * SSM_N), lambda d, b, i: (d, 0, 0)),
            pl.BlockSpec((None, 1, 2 * SSM_N), lambda d, b, i: (d, 0, 0)),
            pl.BlockSpec((None, 2 * SSM_N, SSM_CH), lambda d, b, i: (d, 0, 0)),
            pl.BlockSpec((None, None, SUBLANES, 2 * SSM_N), lambda d, b, i: (d, b, 0, 0)),
        ],
        out_specs=[
            pl.BlockSpec((None, tc, None, SUBLANES, SSM_CH), lambda d, b, i: (d, chunk(d, i), b, 0, 0)),
            pl.BlockSpec((None, None, SUBLANES, 2 * SSM_N), lambda d, b, i: (d, b, 0, 0)),
        ],
        out_shape=[
            jax.ShapeDtypeStruct((2, S, nb, SUBLANES, SSM_CH), F32),
            jax.ShapeDtypeStruct((2, nb, SUBLANES, 2 * SSM_N), F32),
        ],
        scratch_shapes=[
            pltpu.VMEM((tc * SUBLANES, SSM_N), F32),
            pltpu.VMEM((tc * SUBLANES, SSM_N), F32),
            pltpu.VMEM((SUBLANES, 2 * SSM_N), F32),
        ],
        compiler_params=_cparams(("arbitrary", "arbitrary", "arbitrary")),
        name="ssm_scan",
    )(u4, lw["ssm_wb"], lw["ssm_a"], lw["ssm_wc"], h0)
    return y.reshape(2, S, B * SSM_CH), hfin


def _mla_kernel(*refs, with_latent, tk):
    if with_latent:
        qabs_ref, qrope_ref, ka_ref, kb_ref, wv_ref, o_ref = refs
    else:
        qabs_ref, qrope_ref, ka_ref, wv_ref, o_ref = refs
        kb_ref = None
    tq = qabs_ref.shape[0]
    outs = []
    for hd in range(MLA_HEADS):
        sl = slice(hd * LANES, (hd + 1) * LANES)
        q = jnp.concatenate([qabs_ref[:, sl], qrope_ref[:, sl]], axis=-1)

        def update(carry, k):
            m, l, acc = carry
            s = _dot_nt(q, k)
            m_new = jnp.maximum(m, jnp.max(s, axis=-1, keepdims=True))
            alpha = jnp.exp(m - m_new)
            p = jnp.exp(s - m_new)
            l = alpha * l + jnp.sum(p, axis=-1, keepdims=True)
            acc = alpha * acc + _dot(p.astype(BF16), k[:, :MLA_KV_RANK])
            return m_new, l, acc

        carry = (jnp.full((tq, 1), -jnp.inf, F32), jnp.zeros((tq, 1), F32), jnp.zeros((tq, MLA_KV_RANK), F32))
        carry = update(carry, ka_ref[...])
        if with_latent:
            def body(j, c):
                return update(c, kb_ref[pl.ds(pl.multiple_of(j * tk, tk), tk), :])
            carry = lax.fori_loop(0, kb_ref.shape[0] // tk, body, carry)
        _, l, acc = carry
        o_lat = (acc / l).astype(BF16)
        outs.append(_dot(o_lat, wv_ref[hd]))
    o_ref[...] = jnp.concatenate(outs, axis=-1)


def _mla(qabs, qrope, ka, kb, wv):
    B, S, _ = qabs.shape
    tq = 256
    with_latent = kb is not None
    tokq = pl.BlockSpec((None, tq, Q_ABS), lambda b, s: (b, s, 0))
    kspec = lambda a: pl.BlockSpec((None,) + a.shape[1:], lambda b, s: (b, 0, 0))
    in_specs = [tokq, tokq, kspec(ka)]
    args = [qabs, qrope, ka]
    if with_latent:
        in_specs.append(kspec(kb))
        args.append(kb)
    in_specs.append(pl.BlockSpec(wv.shape, lambda b, s: (0, 0, 0)))
    args.append(wv)
    return pl.pallas_call(
        functools.partial(_mla_kernel, with_latent=with_latent, tk=512),
        grid=(B, S // tq),
        in_specs=in_specs,
        out_specs=pl.BlockSpec((None, tq, MLA_HEADS * MLA_V), lambda b, s: (b, s, 0)),
        out_shape=jax.ShapeDtypeStruct((B, S, MLA_HEADS * MLA_V), F32),
        compiler_params=_cparams(("arbitrary", "arbitrary")),
        name="mla_lat" if with_latent else "mla_ctx",
    )(*args)


def _swa_kernel(*refs, with_latent):
    if with_latent:
        sink_ref, q_ref, ka_ref, va_ref, kb_ref, vb_ref, o_ref = refs
    else:
        sink_ref, q_ref, ka_ref, va_ref, o_ref = refs
    tq = q_ref.shape[0]
    q = q_ref[...]
    ka = ka_ref[...]
    va = va_ref[...]
    if with_latent:
        n = pl.program_id(1)
        nblk = kb_ref.shape[0] // SWA_BLOCK
        start = pl.multiple_of(jnp.clip(n - 1, 0, nblk - 3) * SWA_BLOCK, SWA_BLOCK)
        kb = kb_ref[pl.ds(start, 3 * SWA_BLOCK), :]
        vb = vb_ref[pl.ds(start, 3 * SWA_BLOCK), :]
        qpos = n * SWA_BLOCK + lax.broadcasted_iota(jnp.int32, (tq, 3 * SWA_BLOCK), 0)
        kpos = start + lax.broadcasted_iota(jnp.int32, (tq, 3 * SWA_BLOCK), 1)
        valid = jnp.abs(qpos - kpos) <= SWA_WINDOW
    outs = []
    for hd in range(SWA_HEADS):
        kh = hd // SWA_GROUP
        qh = q[:, hd * HEAD_DIM:(hd + 1) * HEAD_DIM]
        ksl = slice(kh * HEAD_DIM, (kh + 1) * HEAD_DIM)
        sink = sink_ref[hd]
        s_a = _dot_nt(qh, ka[:, ksl])
        m = jnp.maximum(jnp.max(s_a, axis=-1, keepdims=True), sink)
        if with_latent:
            s_b = jnp.where(valid, _dot_nt(qh, kb[:, ksl]), -jnp.inf)
            m = jnp.maximum(m, jnp.max(s_b, axis=-1, keepdims=True))
        p_a = jnp.exp(s_a - m)
        den = jnp.sum(p_a, axis=-1, keepdims=True) + jnp.exp(sink - m)
        o = _dot(p_a.astype(BF16), va[:, ksl])
        if with_latent:
            p_b = jnp.exp(s_b - m)
            den = den + jnp.sum(p_b, axis=-1, keepdims=True)
            o = o + _dot(p_b.astype(BF16), vb[:, ksl])
        outs.append(o / den)
    o_ref[...] = jnp.concatenate(outs, axis=-1)


def _swa(sink, q, ka, va, kb, vb):
    B, S, W = q.shape
    with_latent = kb is not None
    tq = SWA_BLOCK if with_latent else S
    tok = pl.BlockSpec((None, tq, W), lambda b, s: (b, s, 0))
    kspec = lambda a: pl.BlockSpec((None,) + a.shape[1:], lambda b, s: (b, 0, 0))
    in_specs = [pl.BlockSpec(memory_space=pltpu.SMEM), tok, kspec(ka), kspec(va)]
    args = [sink, q, ka, va]
    if with_latent:
        in_specs += [kspec(kb), kspec(vb)]
        args += [kb, vb]
    return pl.pallas_call(
        functools.partial(_swa_kernel, with_latent=with_latent),
        grid=(B, S // tq),
        in_specs=in_specs,
        out_specs=tok,
        out_shape=jax.ShapeDtypeStruct((B, S, W), F32),
        compiler_params=_cparams(("arbitrary", "arbitrary")),
        name="swa_lat" if with_latent else "swa_ctx",
    )(*args)


def _route(logits):
    lane = lax.broadcasted_iota(jnp.int32, logits.shape, 1)
    big = jnp.int32(1 << 20)
    is_g = (lane >= MOE_EXPERTS) & (lane < MOE_EXPERTS + MOE_GROUPS)
    lg = jnp.where(is_g, logits, -jnp.inf)
    mg = jnp.max(lg, axis=-1, keepdims=True)
    g_idx = jnp.min(jnp.where(lg == mg, lane - MOE_EXPERTS, big), axis=-1, keepdims=True)
    pg_top = 1.0 / jnp.sum(jnp.exp(lg - mg), axis=-1, keepdims=True)

    is_e = (lane < MOE_EXPERTS) & ((lane // MOE_PER_GROUP) == g_idx)
    le = jnp.where(is_e, logits, -jnp.inf)
    m1 = jnp.max(le, axis=-1, keepdims=True)
    e1 = jnp.min(jnp.where(le == m1, lane, big), axis=-1, keepdims=True)
    z = jnp.sum(jnp.exp(le - m1), axis=-1, keepdims=True)
    le2 = jnp.where(lane == e1, -jnp.inf, le)
    m2 = jnp.max(le2, axis=-1, keepdims=True)
    e2 = jnp.min(jnp.where(le2 == m2, lane, big), axis=-1, keepdims=True)
    p1 = 1.0 / z
    p2 = jnp.exp(m2 - m1) / z
    tot = p1 + p2
    return pg_top * (jnp.where(lane == e1, p1 / tot, 0.0) + jnp.where(lane == e2, p2 / tot, 0.0))


def _post_kernel(x_ref, u_ref, yf_ref, yb_ref, om_ref, os_ref, mod_ref, d_ref, wglu_ref, gs_ref, gm_ref,
                 gw_ref, wout_ref, n2_ref, wr_ref, br_ref, x1_ref, h2_ref, gate_ref):
    mod = mod_ref[...]
    y = d_ref[...] * u_ref[...] + yf_ref[...] + yb_ref[...]
    ga = _dot(_gelu_tanh(y).astype(BF16), wglu_ref[...])
    y_ssm = ga[:, :SSM_CH] * jax.nn.sigmoid(ga[:, SSM_CH:])
    wout = wout_ref
    n_mla = MLA_HEADS * MLA_V
    mixed = (_dot(_rms(y_ssm, gs_ref[...]).astype(BF16), wout[0:SSM_CH, :])
             + _dot(_rms(om_ref[...], gm_ref[...]).astype(BF16), wout[SSM_CH:SSM_CH + n_mla, :])
             + _dot(_rms(os_ref[...], gw_ref[...]).astype(BF16), wout[SSM_CH + n_mla:, :]))
    x1 = x_ref[...] + mod[2:3] * mixed
    x1_ref[...] = x1
    h2 = _rms(x1, n2_ref[...]) * (1.0 + mod[4:5]) + mod[3:4]
    h2_ref[...] = h2.astype(BF16)
    gate_ref[...] = _route(_dot3(h2, wr_ref[...]) + br_ref[...])


def _post(x, u_tm, y2, o_mla, o_swa, mod, lw, *, per_batch_mod):
    B, S, D = x.shape
    ts = min(S, 512)
    tok = lambda w: pl.BlockSpec((None, ts, w), lambda b, s: (b, s, 0))
    full = lambda a: pl.BlockSpec(a.shape, lambda b, s: (0,) * a.ndim)
    names = ["ssm_d", "w_glu", "gn_ssm", "gn_mla", "gn_swa", "w_out", "norm2", "w_router", "b_router"]
    in_specs = [
        tok(D),
        pl.BlockSpec((ts, SSM_CH), lambda b, s: (s, b)),
        pl.BlockSpec((None, ts, SSM_CH), lambda b, s: (0, s, b)),
        pl.BlockSpec((None, ts, SSM_CH), lambda b, s: (1, s, b)),
        tok(MLA_HEADS * MLA_V), tok(SWA_HEADS * HEAD_DIM),
        pl.BlockSpec((None, 6, D), (lambda b, s: (b, 0, 0)) if per_batch_mod else (lambda b, s: (0, 0, 0))),
    ] + [full(lw[k]) for k in names]
    return pl.pallas_call(
        _post_kernel,
        grid=(B, S // ts),
        in_specs=in_specs,
        out_specs=[tok(D), tok(D), tok(LANES)],
        out_shape=[
            jax.ShapeDtypeStruct((B, S, D), F32),
            jax.ShapeDtypeStruct((B, S, D), BF16),
            jax.ShapeDtypeStruct((B, S, LANES), F32),
        ],
        compiler_params=_cparams(("arbitrary", "arbitrary")),
        name="post",
    )(x, u_tm, y2, y2, o_mla, o_swa, mod, *[lw[k] for k in names])


def _moe_kernel(h2_ref, gate_ref, x1_ref, mod_ref, wg_ref, wu_ref, wd_ref, fn_ref, o_ref, acc_s, *, final):
    e = pl.program_id(2)

    @pl.when(e == 0)
    def _():
        acc_s[...] = jnp.zeros_like(acc_s)

    h2 = h2_ref[...]
    gates = gate_ref[...]
    lane = lax.broadcasted_iota(jnp.int32, gates.shape, 1)
    g = jnp.sum(jnp.where(lane == e, gates, 0.0), axis=-1, keepdims=True)
    a = _dot(h2, wg_ref[...])
    b = _dot(h2, wu_ref[...])
    hid = _silu(a) * b * g
    acc_s[...] += _dot(hid.astype(BF16), wd_ref[...])

    @pl.when(e == MOE_EXPERTS - 1)
    def _():
        xo = x1_ref[...] + mod_ref[5:6, :] * acc_s[...]
        o_ref[...] = _rms(xo, fn_ref[...]) if final else xo


def _moe(h2, gates, x1, mod, lw, final_norm, *, per_batch_mod, final):
    B, S, D = x1.shape
    tm = min(S, 1024)
    tok = lambda w: pl.BlockSpec((None, tm, w), lambda b, s, e: (b, s, 0))
    wspec = lambda a: pl.BlockSpec((None,) + a.shape[1:], lambda b, s, e: (e, 0, 0))
    return pl.pallas_call(
        functools.partial(_moe_kernel, final=final),
        grid=(B, S // tm, MOE_EXPERTS),
        in_specs=[
            tok(D), tok(LANES), tok(D),
            pl.BlockSpec((None, 6, D), (lambda b, s, e: (b, 0, 0)) if per_batch_mod else (lambda b, s, e: (0, 0, 0))),
            wspec(lw["moe_wg"]), wspec(lw["moe_wu"]), wspec(lw["moe_wd"]),
            pl.BlockSpec(final_norm.shape, lambda b, s, e: (0, 0)),
        ],
        out_specs=tok(D),
        out_shape=jax.ShapeDtypeStruct((B, S, D), F32),
        scratch_shapes=[pltpu.VMEM((tm, D), F32)],
        compiler_params=_cparams(("arbitrary", "arbitrary", "arbitrary")),
        name="moe",
    )(h2, gates, x1, mod, lw["moe_wg"], lw["moe_wu"], lw["moe_wd"], final_norm)


def _swap_halves(w, n_heads, dim):
    k = w.shape[0]
    w = w.reshape(k, n_heads, 2, dim // 2)
    return w[:, :, ::-1, :].reshape(k, n_heads * dim)


def _pad_cols(w, width):
    return jnp.pad(w, ((0, 0), (0, width - w.shape[1])))


def _rope_tables(n_tokens, rot_dim, reps):
    t = jnp.arange(n_tokens)
    row = (t // GRID_W).astype(F32)
    col = (t % GRID_W).astype(F32)
    n_freq = rot_dim // 4
    inv_freq = ROPE_BASE ** (-jnp.arange(n_freq, dtype=F32) / n_freq)
    ang = jnp.concatenate([row[:, None] * inv_freq, col[:, None] * inv_freq], axis=-1)
    cos, sin = jnp.cos(ang), jnp.sin(ang)
    c = jnp.tile(jnp.concatenate([cos, cos], axis=-1), (1, reps))
    s = jnp.tile(jnp.concatenate([-sin, sin], axis=-1), (1, reps))
    return _pad_cols(c, LANES), _pad_cols(s, LANES)


def _layer_weights(l, P, q_abs, ab_re, ab_im, bb_re, bb_im):
    w_in = P["w_in"][l]
    seg = {}
    o = 0
    for name, width in (("u", SSM_CH), ("ql", MLA_Q_RANK), ("kvl", MLA_KV_RANK), ("kr", MLA_ROPE),
                        ("qs", SWA_HEADS * HEAD_DIM), ("ks", SWA_KV_HEADS * HEAD_DIM),
                        ("vs", SWA_KV_HEADS * HEAD_DIM)):
        seg[name] = w_in[:, o:o + width]
        o += width
    ctx_cols = [seg["u"], seg["ql"], seg["kvl"], seg["qs"], seg["ks"], seg["vs"], _pad_cols(seg["kr"], LANES)]
    lat_cols = ctx_cols + [
        _swap_halves(seg["qs"], SWA_HEADS, HEAD_DIM),
        _swap_halves(seg["ks"], SWA_KV_HEADS, HEAD_DIM),
        _pad_cols(_swap_halves(seg["kr"], 1, MLA_ROPE), LANES),
    ]
    w_qb = P["w_mla_qb"][l]
    w_rope = w_qb[:, :, MLA_NOPE:]
    slot = lambda w: jnp.pad(w, ((0, 0), (0, 0), (0, LANES - MLA_ROPE))).reshape(MLA_Q_RANK, Q_ABS)
    w_rope_sw = w_rope.reshape(MLA_Q_RANK, MLA_HEADS, 2, MLA_ROPE // 2)[:, :, ::-1, :].reshape(w_rope.shape)
    wq_abs = jnp.transpose(q_abs[l], (1, 0, 2)).reshape(MLA_Q_RANK, Q_ABS)
    wq_ctx = jnp.concatenate([wq_abs, slot(w_rope)], axis=1)
    wq_lat = jnp.concatenate([wq_ctx, slot(w_rope_sw)], axis=1)

    eye = jnp.eye(SSM_GROUPS, dtype=F32)

    def block_diag_b(bb):
        bb = bb.reshape(2, SSM_GROUPS, SSM_STATE, SSM_GROUP)
        return jnp.einsum("dgpc,gh->dgchp", bb, eye).reshape(2, SSM_CH, SSM_N)

    def block_diag_c(cc):
        return jnp.einsum("dgcp,gh->dgphc", cc, eye).reshape(2, SSM_N, SSM_CH)

    sl = slice(2 * l, 2 * l + 2)
    w_router = jnp.concatenate([P["moe_w_expert"][l], P["moe_w_group"][l]], axis=1)
    b_router = jnp.concatenate([P["moe_b_expert"][l], P["moe_b_group"][l]])
    row = lambda v: v.reshape(1, -1)
    return dict(
        norm1=row(P["norm1"][l]), norm2=row(P["norm2"][l]),
        win_ctx=jnp.concatenate(ctx_cols, axis=1).astype(BF16),
        win_lat=jnp.concatenate(lat_cols, axis=1).astype(BF16),
        q_norm=row(P["mla_q_norm"][l]), kv_norm=row(P["mla_kv_norm"][l]),
        wq_ctx=wq_ctx.astype(BF16), wq_lat=wq_lat.astype(BF16),
        wv=jnp.transpose(P["w_mla_kvb"][l][:, :, MLA_NOPE:], (1, 0, 2)).astype(BF16),
        ssm_wb=jnp.concatenate([block_diag_b(bb_re[sl]), block_diag_b(bb_im[sl])], axis=2).astype(BF16),
        ssm_a=jnp.concatenate([ab_re[sl], ab_im[sl]], axis=1).reshape(2, 1, 2 * SSM_N),
        ssm_wc=jnp.concatenate([block_diag_c(P["ssm_c_re"][l]), -block_diag_c(P["ssm_c_im"][l])],
                               axis=1).astype(BF16),
        ssm_d=row(P["ssm_d"][l]), w_glu=P["w_ssm_glu"][l].astype(BF16),
        gn_ssm=row(P["gn_ssm"][l]), gn_mla=row(P["gn_mla"][l]), gn_swa=row(P["gn_swa"][l]),
        w_out=P["w_out"][l].astype(BF16),
        w_router=_pad_cols(w_router, LANES), b_router=_pad_cols(row(b_router), LANES),
        moe_wg=P["moe_w_gate"][l].reshape(MOE_EXPERTS, D_MODEL, MOE_HIDDEN).astype(BF16),
        moe_wu=P["moe_w_up"][l].reshape(MOE_EXPERTS, D_MODEL, MOE_HIDDEN).astype(BF16),
        moe_wd=P["moe_w_down"][l].reshape(MOE_EXPERTS, MOE_HIDDEN, D_MODEL).astype(BF16),
        sink=P["swa_sink"][l],
    )


def _layer(x, mod, lw, final_norm, *, tables, ctx, per_batch_mod, final):
    B, S, _ = x.shape
    context_pass = ctx is None
    pre = _pre(x, mod, lw, tables, per_batch_mod=per_batch_mod, emit_ctx=context_pass)
    u_tm, qabs, qrope, kcat, qs, ks, vs = pre[:7]
    nb = B // SUBLANES
    if context_pass:
        h0 = jnp.zeros((2, nb, SUBLANES, 2 * SSM_N), F32)
        y2, hfin = _ssm(u_tm, lw, h0, B, S)
        o_mla = _mla(qabs, qrope, kcat, None, lw["wv"])
        o_swa = _swa(lw["sink"], qs, ks, vs, None, None)
        state = hfin.reshape(2, B, 2, SSM_GROUPS, SSM_STATE).transpose(1, 0, 2, 3, 4)
        new_ctx = (pre[7], pre[8], pre[9].reshape(B, S, SWA_KV_HEADS, HEAD_DIM),
                   pre[10].reshape(B, S, SWA_KV_HEADS, HEAD_DIM), state)
    else:
        kcat_c, ks_c, vs_c, h0 = ctx
        y2, _ = _ssm(u_tm, lw, h0, B, S)
        o_mla = _mla(qabs, qrope, kcat_c, kcat, lw["wv"])
        o_swa = _swa(lw["sink"], qs, ks_c, vs_c, ks, vs)
        new_ctx = None
    x1, h2, gates = _post(x, u_tm, y2, o_mla, o_swa, mod, lw, per_batch_mod=per_batch_mod)
    if not per_batch_mod:
        shp = lambda a: a.reshape(1, B * S, a.shape[-1])
        xo = _moe(shp(h2), shp(gates), shp(x1), mod, lw, final_norm, per_batch_mod=False, final=final)
        xo = xo.reshape(B, S, D_MODEL)
    else:
        xo = _moe(h2, gates, x1, mod, lw, final_norm, per_batch_mod=True, final=final)
    return xo, new_ctx


def kernel(x_prompt, x_sample, c, cache_mla_ckv, cache_mla_krope, cache_swa_k, cache_swa_v, state_ssm, c_ctx, w_ada, b_ada, norm1, norm2, w_in, ssm_a_re, ssm_a_im, ssm_log_dt, ssm_b_re, ssm_b_im, ssm_c_re, ssm_c_im, ssm_d, w_ssm_glu, mla_q_norm, w_mla_qb, mla_kv_norm, w_mla_kvb, swa_sink, gn_ssm, gn_mla, gn_swa, w_out, moe_w_group, moe_b_group, moe_w_expert, moe_b_expert, moe_w_gate, moe_w_up, moe_w_down, final_norm):
    P = dict(w_ada=w_ada, b_ada=b_ada, norm1=norm1, norm2=norm2, w_in=w_in,
             ssm_c_re=ssm_c_re, ssm_c_im=ssm_c_im, ssm_d=ssm_d, w_ssm_glu=w_ssm_glu,
             mla_q_norm=mla_q_norm, w_mla_qb=w_mla_qb, mla_kv_norm=mla_kv_norm, w_mla_kvb=w_mla_kvb,
             swa_sink=swa_sink, gn_ssm=gn_ssm, gn_mla=gn_mla, gn_swa=gn_swa, w_out=w_out,
             moe_w_group=moe_w_group, moe_b_group=moe_b_group, moe_w_expert=moe_w_expert,
             moe_b_expert=moe_b_expert, moe_w_gate=moe_w_gate, moe_w_up=moe_w_up, moe_w_down=moe_w_down)
    n_dec = c.shape[0]
    n_cond = 2 * SUBLANES
    conds = jnp.zeros((n_cond, D_MODEL), F32).at[:n_dec].set(c).at[n_dec].set(c_ctx)
    mods = _modulation(conds, w_ada, b_ada).reshape(DEPTH, n_cond, 6, D_MODEL)

    ab_re, ab_im, bb_re, bb_im = _ssm_discretise(ssm_a_re, ssm_a_im, ssm_log_dt, ssm_b_re, ssm_b_im)
    q_abs = _absorb_q(jnp.transpose(w_mla_qb[..., :MLA_NOPE], (0, 2, 1, 3)),
                      jnp.transpose(w_mla_kvb[..., :MLA_NOPE], (0, 2, 1, 3)))
    lws = [_layer_weights(l, P, q_abs, ab_re.reshape(2 * DEPTH, SSM_N), ab_im.reshape(2 * DEPTH, SSM_N),
                          bb_re, bb_im) for l in range(DEPTH)]
    fnorm = final_norm.reshape(1, D_MODEL)

    xp = x_prompt
    ctx_states = []
    for l in range(DEPTH):
        xp, new = _layer(xp, mods[l, n_dec:n_dec + 1], lws[l], fnorm, tables=None, ctx=None,
                         per_batch_mod=False, final=l == DEPTH - 1)
        ctx_states.append(new)
    outs_ctx = tuple(jnp.stack([s[k] for s in ctx_states], axis=1) for k in range(5))

    n_lat = x_sample.shape[1]
    cm, sm = _rope_tables(n_lat, MLA_ROPE, 1)
    cs, ss = _rope_tables(n_lat, HEAD_DIM, LANES // HEAD_DIM)
    tables = (cm, sm, cs, ss)
    xs = x_sample
    past = cache_mla_ckv.shape[2]
    for l in range(DEPTH):
        kcat_c = jnp.concatenate(
            [cache_mla_ckv[:, l], cache_mla_krope[:, l],
             jnp.zeros((n_dec, past, LANES - MLA_ROPE), F32)], axis=-1).astype(BF16)
        ks_c = cache_swa_k[:, l].reshape(n_dec, past, LANES).astype(BF16)
        vs_c = cache_swa_v[:, l].reshape(n_dec, past, LANES).astype(BF16)
        h0 = state_ssm[:, l].transpose(1, 0, 2, 3, 4).reshape(2, n_dec // SUBLANES, SUBLANES, 2 * SSM_N)
        xs, _ = _layer(xs, mods[l, :n_dec], lws[l], fnorm, tables=tables, ctx=(kcat_c, ks_c, vs_c, h0),
                       per_batch_mod=True, final=l == DEPTH - 1)
    return (xp, xs) + outs_ctx
```

```python
import functools
import math

import jax
import jax.numpy as jnp
from jax import lax
from jax.experimental import pallas as pl
from jax.experimental.pallas import tpu as pltpu

F32 = jnp.float32
BF16 = jnp.bfloat16

D_MODEL = 1024
DEPTH = 4
GRID_W = 64
HEAD_DIM = 64
SSM_CH = 256
SSM_GROUP = 16
SSM_GROUPS = SSM_CH // SSM_GROUP
SSM_STATE = 64
SSM_N = SSM_GROUPS * SSM_STATE
MLA_HEADS = 6
MLA_Q_RANK = 256
MLA_KV_RANK = 128
MLA_NOPE = 64
MLA_ROPE = 32
MLA_V = 64
SWA_HEADS = 6
SWA_KV_HEADS = 2
SWA_GROUP = SWA_HEADS // SWA_KV_HEADS
SWA_WINDOW = 128
SWA_BLOCK = 128
MOE_GROUPS = 4
MOE_PER_GROUP = 8
MOE_EXPERTS = MOE_GROUPS * MOE_PER_GROUP
MOE_HIDDEN = 256
ROPE_BASE = 10000.0
EPS = 1e-6
MLA_SCALE = 1.0 / math.sqrt(MLA_NOPE + MLA_ROPE)
MLA_SCALE_LOG2 = MLA_SCALE * math.log2(math.e)
SWA_SCALE = 1.0 / math.sqrt(HEAD_DIM)

LANES = 128
SUBLANES = 8
VMEM_LIMIT = 52 * 1024 * 1024

OFF_U = 0
OFF_QL = OFF_U + SSM_CH
OFF_KVL = OFF_QL + MLA_Q_RANK
OFF_QS = OFF_KVL + MLA_KV_RANK
OFF_KS = OFF_QS + SWA_HEADS * HEAD_DIM
OFF_VS = OFF_KS + SWA_KV_HEADS * HEAD_DIM
OFF_KR = OFF_VS + SWA_KV_HEADS * HEAD_DIM
N_PACK_CTX = OFF_KR + LANES
OFF_QS_SW = N_PACK_CTX
OFF_KS_SW = OFF_QS_SW + SWA_HEADS * HEAD_DIM
OFF_KR_SW = OFF_KS_SW + SWA_KV_HEADS * HEAD_DIM
N_PACK_LAT = OFF_KR_SW + LANES
Q_ABS = MLA_HEADS * MLA_KV_RANK
MLA_QK = 2 * LANES


def _cparams(sem):
    return pltpu.CompilerParams(dimension_semantics=sem, vmem_limit_bytes=VMEM_LIMIT)


def _dot(a, b):
    return jnp.dot(a, b, preferred_element_type=F32)


def _dot_nt(a, b):
    return lax.dot_general(a, b, (((1,), (1,)), ((), ())), preferred_element_type=F32)


def _split(x):
    hi = x.astype(BF16)
    lo = (x - hi.astype(F32)).astype(BF16)
    return hi, lo


def _dot3(a, b):
    ah, al = _split(a)
    bh, bl = _split(b)
    return _dot(ah, bh) + _dot(al, bh) + _dot(ah, bl)


def _tree(op, x3):
    parts = [x3[i] for i in range(x3.shape[0])]
    while len(parts) > 1:
        parts = [op(parts[i], parts[i + 1]) for i in range(0, len(parts), 2)]
    return parts[0]


def _rms(x, g):
    return x * lax.rsqrt(jnp.mean(x * x, axis=-1, keepdims=True) + EPS) * g


def _silu(x):
    return x * jax.nn.sigmoid(x)


def _gelu_tanh(x):
    return 0.5 * x * (1.0 + jnp.tanh(math.sqrt(2.0 / math.pi) * (x + 0.044715 * (x * x * x))))


def _mod_kernel(c_ref, w_ref, b_ref, o_ref):
    o_ref[...] = _dot3(_silu(c_ref[...]), w_ref[...]) + b_ref[...]


def _modulation(conds, w_ada, b_ada):
    n = conds.shape[0]
    tn = 1536
    return pl.pallas_call(
        _mod_kernel,
        grid=(DEPTH, 6 * D_MODEL // tn),
        in_specs=[
            pl.BlockSpec((n, D_MODEL), lambda l, j: (0, 0)),
            pl.BlockSpec((None, D_MODEL, tn), lambda l, j: (l, 0, j)),
            pl.BlockSpec((None, 1, tn), lambda l, j: (l, 0, j)),
        ],
        out_specs=pl.BlockSpec((None, n, tn), lambda l, j: (l, 0, j)),
        out_shape=jax.ShapeDtypeStruct((DEPTH, n, 6 * D_MODEL), F32),
        compiler_params=_cparams(("arbitrary", "arbitrary")),
        name="modulation",
    )(conds, w_ada, b_ada.reshape(DEPTH, 1, 6 * D_MODEL))


def _ssm_disc_kernel(are_ref, aim_ref, ldt_ref, bre_ref, bim_ref, abre_ref, abim_ref, bbre_ref, bbim_ref):
    lam_re = are_ref[...]
    lam_im = aim_ref[...]
    dt = jnp.exp(ldt_ref[...])
    z_re = lam_re * dt
    z_im = lam_im * dt
    mag = jnp.exp(z_re)
    ab_re = mag * jnp.cos(z_im)
    ab_im = mag * jnp.sin(z_im)
    den = lam_re * lam_re + lam_im * lam_im
    f_re = ((ab_re - 1.0) * lam_re + ab_im * lam_im) / den
    f_im = (ab_im * lam_re - (ab_re - 1.0) * lam_im) / den
    b_re = bre_ref[...]
    b_im = bim_ref[...]
    abre_ref[...] = ab_re
    abim_ref[...] = ab_im
    bbre_ref[...] = f_re * b_re - f_im * b_im
    bbim_ref[...] = f_re * b_im + f_im * b_re


def _ssm_discretise(a_re, a_im, log_dt, b_re, b_im):
    n = DEPTH * 2
    col = lambda v: v.reshape(n, SSM_N, 1)
    ldt = jnp.broadcast_to(log_dt[..., None], (DEPTH, 2, SSM_GROUPS, SSM_STATE))
    cspec = pl.BlockSpec((None, SSM_N, 1), lambda i: (i, 0, 0))
    bspec = pl.BlockSpec((None, SSM_N, SSM_GROUP), lambda i: (i, 0, 0))
    return pl.pallas_call(
        _ssm_disc_kernel,
        grid=(n,),
        in_specs=[cspec, cspec, cspec, bspec, bspec],
        out_specs=[cspec, cspec, bspec, bspec],
        out_shape=[jax.ShapeDtypeStruct((n, SSM_N, 1), F32)] * 2
        + [jax.ShapeDtypeStruct((n, SSM_N, SSM_GROUP), F32)] * 2,
        compiler_params=_cparams(("arbitrary",)),
        name="ssm_discretise",
    )(col(a_re), col(a_im), col(ldt), b_re.reshape(n, SSM_N, SSM_GROUP), b_im.reshape(n, SSM_N, SSM_GROUP))


def _absorb_kernel(wq_ref, wk_ref, o_ref):
    a = wq_ref[...]
    b = wk_ref[...]
    ah, al = _split(a)
    bh, bl = _split(b)
    o_ref[...] = _dot_nt(ah, bh) + _dot_nt(al, bh) + _dot_nt(ah, bl)


def _absorb_q(wq_nope, wk_nope):
    return pl.pallas_call(
        _absorb_kernel,
        grid=(DEPTH, MLA_HEADS),
        in_specs=[
            pl.BlockSpec((None, None, MLA_Q_RANK, MLA_NOPE), lambda l, h: (l, h, 0, 0)),
            pl.BlockSpec((None, None, MLA_KV_RANK, MLA_NOPE), lambda l, h: (l, h, 0, 0)),
        ],
        out_specs=pl.BlockSpec((None, None, MLA_Q_RANK, MLA_KV_RANK), lambda l, h: (l, h, 0, 0)),
        out_shape=jax.ShapeDtypeStruct((DEPTH, MLA_HEADS, MLA_Q_RANK, MLA_KV_RANK), F32),
        compiler_params=_cparams(("arbitrary", "arbitrary")),
        name="mla_absorb",
    )(wq_nope, wk_nope)


def _pre_kernel(*refs, rope, emit_ctx):
    it = iter(refs)
    x_ref, mod_ref, n1_ref, win_ref, qn_ref, kvn_ref, wq_ref = (next(it) for _ in range(7))
    if rope:
        cm_ref, sm_ref, cmt_ref, smt_ref, cs_ref, ss_ref = (next(it) for _ in range(6))
    u_ref, qt_ref, kcat_ref, ckvt_ref, qs_ref, ks_ref, vs_ref = (next(it) for _ in range(7))
    if emit_ctx:
        ckv_o, kr_o, ks_o, vs_o = (next(it) for _ in range(4))

    x = x_ref[...]
    mod = mod_ref[...]
    h = _rms(x, n1_ref[...]) * (1.0 + mod[1:2]) + mod[0:1]
    proj = _dot(h.astype(BF16), win_ref[...])

    u_ref[...] = proj[:, OFF_U:OFF_U + SSM_CH]

    qln = _rms(proj[:, OFF_QL:OFF_QL + MLA_Q_RANK], qn_ref[...]).astype(BF16)
    qall = _dot_nt(wq_ref[...], qln)
    n_rope = MLA_HEADS * MLA_ROPE
    zero_rows = jnp.zeros((MLA_QK - MLA_KV_RANK - MLA_ROPE, qall.shape[1]), BF16)
    for i in range(MLA_HEADS):
        qa = qall[i * MLA_KV_RANK:(i + 1) * MLA_KV_RANK]
        qr = qall[Q_ABS + i * MLA_ROPE:Q_ABS + (i + 1) * MLA_ROPE]
        if rope:
            qr_sw = qall[Q_ABS + n_rope + i * MLA_ROPE:Q_ABS + n_rope + (i + 1) * MLA_ROPE]
            qr = qr * cmt_ref[...] + qr_sw * smt_ref[...]
        base = i * MLA_QK
        qt_ref[base:base + MLA_KV_RANK, :] = (qa * MLA_SCALE_LOG2).astype(BF16)
        qt_ref[base + MLA_KV_RANK:base + MLA_KV_RANK + MLA_ROPE, :] = (qr * MLA_SCALE_LOG2).astype(BF16)
        qt_ref[base + MLA_KV_RANK + MLA_ROPE:base + MLA_QK, :] = zero_rows

    ckv = _rms(proj[:, OFF_KVL:OFF_KVL + MLA_KV_RANK], kvn_ref[...])
    kr = proj[:, OFF_KR:OFF_KR + LANES]
    if emit_ctx:
        ckv_o[...] = ckv
        kr_o[...] = kr[:, :MLA_ROPE]
    if rope:
        kr = kr * cm_ref[...] + proj[:, OFF_KR_SW:OFF_KR_SW + LANES] * sm_ref[...]
    kcat_ref[...] = jnp.concatenate([ckv, kr], axis=-1).astype(BF16)
    ckvt_ref[...] = ckv.T.astype(BF16)

    qs = proj[:, OFF_QS:OFF_QS + SWA_HEADS * HEAD_DIM]
    ks = proj[:, OFF_KS:OFF_KS + SWA_KV_HEADS * HEAD_DIM]
    vs = proj[:, OFF_VS:OFF_VS + SWA_KV_HEADS * HEAD_DIM]
    if emit_ctx:
        ks_o[...] = ks
        vs_o[...] = vs
    if rope:
        cs = cs_ref[...]
        ss = ss_ref[...]
        qs_sw = proj[:, OFF_QS_SW:OFF_QS_SW + SWA_HEADS * HEAD_DIM]
        ks = ks * cs + proj[:, OFF_KS_SW:OFF_KS_SW + SWA_KV_HEADS * HEAD_DIM] * ss
        qs = jnp.concatenate(
            [qs[:, i * LANES:(i + 1) * LANES] * cs + qs_sw[:, i * LANES:(i + 1) * LANES] * ss
             for i in range(SWA_HEADS * HEAD_DIM // LANES)], axis=-1)
    qs_ref[...] = (qs * SWA_SCALE).astype(BF16)
    ks_ref[...] = ks.astype(BF16)
    vs_ref[...] = vs.astype(BF16)


def _pre(x, mod, lw, tables, *, per_batch_mod, emit_ctx):
    B, S, D = x.shape
    rope = tables is not None
    ts = min(S, 512)
    n_pack = N_PACK_LAT if rope else N_PACK_CTX
    win = lw["win_lat"] if rope else lw["win_ctx"]
    wq = lw["wq_lat"] if rope else lw["wq_ctx"]
    tok = lambda w: pl.BlockSpec((None, ts, w), lambda b, s: (b, s, 0))
    full = lambda a: pl.BlockSpec(a.shape, lambda b, s: (0,) * a.ndim)
    in_specs = [
        tok(D),
        pl.BlockSpec((None, 6, D), (lambda b, s: (b, 0, 0)) if per_batch_mod else (lambda b, s: (0, 0, 0))),
        full(lw["norm1"]), full(win), full(lw["q_norm"]), full(lw["kv_norm"]), full(wq),
    ]
    args = [x, mod, lw["norm1"], win, lw["q_norm"], lw["kv_norm"], wq]
    if rope:
        row_tab = pl.BlockSpec((ts, LANES), lambda b, s: (s, 0))
        col_tab = pl.BlockSpec((MLA_ROPE, ts), lambda b, s: (0, s))
        in_specs += [row_tab, row_tab, col_tab, col_tab, row_tab, row_tab]
        args += list(tables)
    feat = lambda w: pl.BlockSpec((None, w, ts), lambda b, s: (b, 0, s))
    out_specs = [
        pl.BlockSpec((ts, SSM_CH), lambda b, s: (s, b)),
        feat(MLA_HEADS * MLA_QK), tok(MLA_QK), feat(MLA_KV_RANK), tok(SWA_HEADS * HEAD_DIM), tok(LANES), tok(LANES),
    ]
    out_shape = [
        jax.ShapeDtypeStruct((S, B * SSM_CH), F32),
        jax.ShapeDtypeStruct((B, MLA_HEADS * MLA_QK, S), BF16),
        jax.ShapeDtypeStruct((B, S, MLA_QK), BF16),
        jax.ShapeDtypeStruct((B, MLA_KV_RANK, S), BF16),
        jax.ShapeDtypeStruct((B, S, SWA_HEADS * HEAD_DIM), BF16),
        jax.ShapeDtypeStruct((B, S, LANES), BF16),
        jax.ShapeDtypeStruct((B, S, LANES), BF16),
    ]
    if emit_ctx:
        out_specs += [tok(MLA_KV_RANK), tok(MLA_ROPE), tok(LANES), tok(LANES)]
        out_shape += [
            jax.ShapeDtypeStruct((B, S, MLA_KV_RANK), F32),
            jax.ShapeDtypeStruct((B, S, MLA_ROPE), F32),
            jax.ShapeDtypeStruct((B, S, LANES), F32),
            jax.ShapeDtypeStruct((B, S, LANES), F32),
        ]
    return pl.pallas_call(
        functools.partial(_pre_kernel, rope=rope, emit_ctx=emit_ctx),
        grid=(B, S // ts),
        in_specs=in_specs,
        out_specs=out_specs,
        out_shape=out_shape,
        compiler_params=_cparams(("arbitrary", "arbitrary")),
        name="pre_lat" if rope else "pre_ctx",
    )(*args)


def _ssm_kernel(u_ref, wb_ref, a_ref, wc_ref, h0_ref, y_ref, hfin_ref, hre_s, him_s, st_s, *, tc):
    d = pl.program_id(0)
    i = pl.program_id(2)
    n = pl.num_programs(2)

    @pl.when(i == 0)
    def _():
        st_s[...] = h0_ref[...]

    u = u_ref[...].reshape(tc * SUBLANES, SSM_CH).astype(BF16)
    bu = _dot(u, wb_ref[...])
    hre_s[...] = bu[:, :SSM_N]
    him_s[...] = bu[:, SSM_N:]

    a = a_ref[...]
    a_re = jnp.broadcast_to(a[:, :SSM_N], (SUBLANES, SSM_N))
    a_im = jnp.broadcast_to(a[:, SSM_N:], (SUBLANES, SSM_N))
    st = st_s[...]

    def step(t, carry):
        h_re, h_im = carry
        tt = jnp.where(d == 0, t, tc - 1 - t)
        rows = pl.ds(pl.multiple_of(tt * SUBLANES, SUBLANES), SUBLANES)
        n_re = a_re * h_re - a_im * h_im + hre_s[rows, :]
        n_im = a_re * h_im + a_im * h_re + him_s[rows, :]
        hre_s[rows, :] = n_re
        him_s[rows, :] = n_im
        return n_re, n_im

    h_re, h_im = lax.fori_loop(0, tc, step, (st[:, :SSM_N], st[:, SSM_N:]), unroll=4)
    st_s[...] = jnp.concatenate([h_re, h_im], axis=-1)

    wc = wc_ref[...]
    y = _dot(hre_s[...].astype(BF16), wc[:SSM_N]) + _dot(him_s[...].astype(BF16), wc[SSM_N:])
    y_ref[...] = y.reshape(tc, SUBLANES, SSM_CH)

    @pl.when(i == n - 1)
    def _():
        hfin_ref[...] = st_s[...]


def _ssm(u_tm, lw, h0, B, S):
    nb = B // SUBLANES
    tc = 128
    nchunk = S // tc
    chunk = lambda d, i: i + d * (nchunk - 1 - 2 * i)
    u4 = u_tm.reshape(S, nb, SUBLANES, SSM_CH)
    y, hfin = pl.pallas_call(
        functools.partial(_ssm_kernel, tc=tc),
        grid=(2, nb, nchunk),
        in_specs=[
            pl.BlockSpec((tc, None, SUBLANES, SSM_CH), lambda d, b, i: (chunk(d, i), b, 0, 0)),
            pl.BlockSpec((None, SSM_CH, 2 * SSM_N), lambda d, b, i: (d, 0, 0)),
            pl.BlockSpec((None, 1, 2 * SSM_N), lambda d, b, i: (d, 0, 0)),
            pl.BlockSpec((None, 2 * SSM_N, SSM_CH), lambda d, b, i: (d, 0, 0)),
            pl.BlockSpec((None, None, SUBLANES, 2 * SSM_N), lambda d, b, i: (d, b, 0, 0)),
        ],
        out_specs=[
            pl.BlockSpec((None, tc, None, SUBLANES, SSM_CH), lambda d, b, i: (d, chunk(d, i), b, 0, 0)),
            pl.BlockSpec((None, None, SUBLANES, 2 * SSM_N), lambda d, b, i: (d, b, 0, 0)),
        ],
        out_shape=[
            jax.ShapeDtypeStruct((2, S, nb, SUBLANES, SSM_CH), F32),
            jax.ShapeDtypeStruct((2, nb, SUBLANES, 2 * SSM_N), F32),
        ],
        scratch_shapes=[
            pltpu.VMEM((tc * SUBLANES, SSM_N), F32),
            pltpu.VMEM((tc * SUBLANES, SSM_N), F32),
            pltpu.VMEM((SUBLANES, 2 * SSM_N), F32),
        ],
        compiler_params=_cparams(("arbitrary", "arbitrary", "arbitrary")),
        name="ssm_scan",
    )(u4, lw["ssm_wb"], lw["ssm_a"], lw["ssm_wc"], h0)
    return y.reshape(2, S, B * SSM_CH), hfin


def _mla_kernel(*refs, with_latent, tk):
    if with_latent:
        qt_ref, ka_ref, vat_ref, kb_ref, vbt_ref, wv_ref, o_ref = refs
    else:
        qt_ref, ka_ref, vat_ref, wv_ref, o_ref = refs
    tq = qt_ref.shape[1]
    q_of = lambda hd: qt_ref[hd * MLA_QK:(hd + 1) * MLA_QK, :]

    def tile(state, k, vt):
        new_state = []
        s_next = _dot(k, q_of(0))
        for hd in range(MLA_HEADS):
            m8, l8, acc = state[hd]
            s = s_next
            if hd + 1 < MLA_HEADS:
                s_next = _dot(k, q_of(hd + 1))
            s3 = s.reshape(s.shape[0] // SUBLANES, SUBLANES, tq)
            mloc = jnp.max(_tree(jnp.maximum, s3), axis=0, keepdims=True)
            m8_new = jnp.maximum(m8, jnp.broadcast_to(mloc, (SUBLANES, tq)))
            alpha8 = jnp.exp2(m8 - m8_new)
            p3 = jnp.exp2(s3 - m8_new[None])
            l8 = alpha8 * l8 + _tree(jnp.add, p3)
            pv = _dot(vt, p3.reshape(s.shape).astype(BF16))
            acc3 = acc.reshape(MLA_KV_RANK // SUBLANES, SUBLANES, tq) * alpha8[None]
            new_state.append((m8_new, l8, acc3.reshape(MLA_KV_RANK, tq) + pv))
        return tuple(new_state)

    init = tuple((jnp.full((SUBLANES, tq), -jnp.inf, F32), jnp.zeros((SUBLANES, tq), F32),
                  jnp.zeros((MLA_KV_RANK, tq), F32)) for _ in range(MLA_HEADS))
    state = tile(init, ka_ref[...], vat_ref[...])
    if with_latent:
        def body(j, st):
            off = pl.multiple_of(j * tk, tk)
            return tile(st, kb_ref[pl.ds(off, tk), :], vbt_ref[:, pl.ds(off, tk)])
        state = lax.fori_loop(0, kb_ref.shape[0] // tk, body, state)

    outs = []
    for hd in range(MLA_HEADS):
        _, l8, acc = state[hd]
        o_lat = (acc / jnp.sum(l8, axis=0, keepdims=True)).T.astype(BF16)
        outs.append(_dot(o_lat, wv_ref[hd]))
    o_ref[...] = jnp.concatenate(outs, axis=-1)


def _mla(qt, ka, vat, kb, vbt, wv):
    B, _, S = qt.shape
    tq = 256
    with_latent = kb is not None
    kspec = lambda a: pl.BlockSpec((None,) + a.shape[1:], lambda b, s: (b, 0, 0))
    in_specs = [pl.BlockSpec((None, MLA_HEADS * MLA_QK, tq), lambda b, s: (b, 0, s)), kspec(ka), kspec(vat)]
    args = [qt, ka, vat]
    if with_latent:
        in_specs += [kspec(kb), kspec(vbt)]
        args += [kb, vbt]
    in_specs.append(pl.BlockSpec(wv.shape, lambda b, s: (0, 0, 0)))
    args.append(wv)
    return pl.pallas_call(
        functools.partial(_mla_kernel, with_latent=with_latent, tk=1024),
        grid=(B, S // tq),
        in_specs=in_specs,
        out_specs=pl.BlockSpec((None, tq, MLA_HEADS * MLA_V), lambda b, s: (b, s, 0)),
        out_shape=jax.ShapeDtypeStruct((B, S, MLA_HEADS * MLA_V), F32),
        compiler_params=_cparams(("arbitrary", "arbitrary")),
        name="mla_lat" if with_latent else "mla_ctx",
    )(*args)


def _swa_kernel(*refs, with_latent):
    if with_latent:
        sink_ref, q_ref, ka_ref, va_ref, kb_ref, vb_ref, o_ref = refs
    else:
        sink_ref, q_ref, ka_ref, va_ref, o_ref = refs
    tq = q_ref.shape[0]
    q = q_ref[...]
    ka = ka_ref[...]
    va = va_ref[...]
    if with_latent:
        n = pl.program_id(1)
        nblk = kb_ref.shape[0] // SWA_BLOCK
        start = pl.multiple_of(jnp.clip(n - 1, 0, nblk - 3) * SWA_BLOCK, SWA_BLOCK)
        kb = kb_ref[pl.ds(start, 3 * SWA_BLOCK), :]
        vb = vb_ref[pl.ds(start, 3 * SWA_BLOCK), :]
        qpos = n * SWA_BLOCK + lax.broadcasted_iota(jnp.int32, (tq, 3 * SWA_BLOCK), 0)
        kpos = start + lax.broadcasted_iota(jnp.int32, (tq, 3 * SWA_BLOCK), 1)
        valid = jnp.abs(qpos - kpos) <= SWA_WINDOW
    outs = []
    for hd in range(SWA_HEADS):
        kh = hd // SWA_GROUP
        qh = q[:, hd * HEAD_DIM:(hd + 1) * HEAD_DIM]
        ksl = slice(kh * HEAD_DIM, (kh + 1) * HEAD_DIM)
        sink = sink_ref[hd]
        s_a = _dot_nt(qh, ka[:, ksl])
        m = jnp.maximum(jnp.max(s_a, axis=-1, keepdims=True), sink)
        if with_latent:
            s_b = jnp.where(valid, _dot_nt(qh, kb[:, ksl]), -jnp.inf)
            m = jnp.maximum(m, jnp.max(s_b, axis=-1, keepdims=True))
        p_a = jnp.exp(s_a - m)
        den = jnp.sum(p_a, axis=-1, keepdims=True) + jnp.exp(sink - m)
        o = _dot(p_a.astype(BF16), va[:, ksl])
        if with_latent:
            p_b = jnp.exp(s_b - m)
            den = den + jnp.sum(p_b, axis=-1, keepdims=True)
            o = o + _dot(p_b.astype(BF16), vb[:, ksl])
        outs.append(o / den)
    o_ref[...] = jnp.concatenate(outs, axis=-1)


def _swa(sink, q, ka, va, kb, vb):
    B, S, W = q.shape
    with_latent = kb is not None
    tq = SWA_BLOCK if with_latent else S
    tok = pl.BlockSpec((None, tq, W), lambda b, s: (b, s, 0))
    kspec = lambda a: pl.BlockSpec((None,) + a.shape[1:], lambda b, s: (b, 0, 0))
    in_specs = [pl.BlockSpec(memory_space=pltpu.SMEM), tok, kspec(ka), kspec(va)]
    args = [sink, q, ka, va]
    if with_latent:
        in_specs += [kspec(kb), kspec(vb)]
        args += [kb, vb]
    return pl.pallas_call(
        functools.partial(_swa_kernel, with_latent=with_latent),
        grid=(B, S // tq),
        in_specs=in_specs,
        out_specs=tok,
        out_shape=jax.ShapeDtypeStruct((B, S, W), F32),
        compiler_params=_cparams(("arbitrary", "arbitrary")),
        name="swa_lat" if with_latent else "swa_ctx",
    )(*args)


def _route(logits):
    lane = lax.broadcasted_iota(jnp.int32, logits.shape, 1)
    big = jnp.int32(1 << 20)
    is_g = (lane >= MOE_EXPERTS) & (lane < MOE_EXPERTS + MOE_GROUPS)
    lg = jnp.where(is_g, logits, -jnp.inf)
    mg = jnp.max(lg, axis=-1, keepdims=True)
    g_idx = jnp.min(jnp.where(lg == mg, lane - MOE_EXPERTS, big), axis=-1, keepdims=True)
    pg_top = 1.0 / jnp.sum(jnp.exp(lg - mg), axis=-1, keepdims=True)

    is_e = (lane < MOE_EXPERTS) & ((lane // MOE_PER_GROUP) == g_idx)
    le = jnp.where(is_e, logits, -jnp.inf)
    m1 = jnp.max(le, axis=-1, keepdims=True)
    e1 = jnp.min(jnp.where(le == m1, lane, big), axis=-1, keepdims=True)
    z = jnp.sum(jnp.exp(le - m1), axis=-1, keepdims=True)
    le2 = jnp.where(lane == e1, -jnp.inf, le)
    m2 = jnp.max(le2, axis=-1, keepdims=True)
    e2 = jnp.min(jnp.where(le2 == m2, lane, big), axis=-1, keepdims=True)
    p1 = 1.0 / z
    p2 = jnp.exp(m2 - m1) / z
    tot = p1 + p2
    return pg_top * (jnp.where(lane == e1, p1 / tot, 0.0) + jnp.where(lane == e2, p2 / tot, 0.0))


def _post_kernel(x_ref, u_ref, yf_ref, yb_ref, om_ref, os_ref, mod_ref, d_ref, wglu_ref, gs_ref, gm_ref,
                 gw_ref, wout_ref, n2_ref, wr_ref, br_ref, x1_ref, h2_ref, gate_ref):
    mod = mod_ref[...]
    y = d_ref[...] * u_ref[...] + yf_ref[...] + yb_ref[...]
    ga = _dot(_gelu_tanh(y).astype(BF16), wglu_ref[...])
    y_ssm = ga[:, :SSM_CH] * jax.nn.sigmoid(ga[:, SSM_CH:])
    wout = wout_ref
    n_mla = MLA_HEADS * MLA_V
    mixed = (_dot(_rms(y_ssm, gs_ref[...]).astype(BF16), wout[0:SSM_CH, :])
             + _dot(_rms(om_ref[...], gm_ref[...]).astype(BF16), wout[SSM_CH:SSM_CH + n_mla, :])
             + _dot(_rms(os_ref[...], gw_ref[...]).astype(BF16), wout[SSM_CH + n_mla:, :]))
    x1 = x_ref[...] + mod[2:3] * mixed
    x1_ref[...] = x1
    h2 = _rms(x1, n2_ref[...]) * (1.0 + mod[4:5]) + mod[3:4]
    h2_ref[...] = h2.astype(BF16)
    gate_ref[...] = _route(_dot3(h2, wr_ref[...]) + br_ref[...])


def _post(x, u_tm, y2, o_mla, o_swa, mod, lw, *, per_batch_mod):
    B, S, D = x.shape
    ts = min(S, 512)
    tok = lambda w: pl.BlockSpec((None, ts, w), lambda b, s: (b, s, 0))
    full = lambda a: pl.BlockSpec(a.shape, lambda b, s: (0,) * a.ndim)
    names = ["ssm_d", "w_glu", "gn_ssm", "gn_mla", "gn_swa", "w_out", "norm2", "w_router", "b_router"]
    in_specs = [
        tok(D),
        pl.BlockSpec((ts, SSM_CH), lambda b, s: (s, b)),
        pl.BlockSpec((None, ts, SSM_CH), lambda b, s: (0, s, b)),
        pl.BlockSpec((None, ts, SSM_CH), lambda b, s: (1, s, b)),
        tok(MLA_HEADS * MLA_V), tok(SWA_HEADS * HEAD_DIM),
        pl.BlockSpec((None, 6, D), (lambda b, s: (b, 0, 0)) if per_batch_mod else (lambda b, s: (0, 0, 0))),
    ] + [full(lw[k]) for k in names]
    return pl.pallas_call(
        _post_kernel,
        grid=(B, S // ts),
        in_specs=in_specs,
        out_specs=[tok(D), tok(D), tok(LANES)],
        out_shape=[
            jax.ShapeDtypeStruct((B, S, D), F32),
            jax.ShapeDtypeStruct((B, S, D), BF16),
            jax.ShapeDtypeStruct((B, S, LANES), F32),
        ],
        compiler_params=_cparams(("arbitrary", "arbitrary")),
        name="post",
    )(x, u_tm, y2, y2, o_mla, o_swa, mod, *[lw[k] for k in names])


def _moe_kernel(h2_ref, gate_ref, x1_ref, mod_ref, wg_ref, wu_ref, wd_ref, fn_ref, o_ref, acc_s, *, final):
    e = pl.program_id(2)

    @pl.when(e == 0)
    def _():
        acc_s[...] = jnp.zeros_like(acc_s)

    h2 = h2_ref[...]
    gates = gate_ref[...]
    lane = lax.broadcasted_iota(jnp.int32, gates.shape, 1)
    g = jnp.sum(jnp.where(lane == e, gates, 0.0), axis=-1, keepdims=True)
    a = _dot(h2, wg_ref[...])
    b = _dot(h2, wu_ref[...])
    hid = _silu(a) * b * g
    acc_s[...] += _dot(hid.astype(BF16), wd_ref[...])

    @pl.when(e == MOE_EXPERTS - 1)
    def _():
        xo = x1_ref[...] + mod_ref[5:6, :] * acc_s[...]
        o_ref[...] = _rms(xo, fn_ref[...]) if final else xo


def _moe(h2, gates, x1, mod, lw, final_norm, *, per_batch_mod, final):
    B, S, D = x1.shape
    tm = min(S, 1024)
    tok = lambda w: pl.BlockSpec((None, tm, w), lambda b, s, e: (b, s, 0))
    wspec = lambda a: pl.BlockSpec((None,) + a.shape[1:], lambda b, s, e: (e, 0, 0))
    return pl.pallas_call(
        functools.partial(_moe_kernel, final=final),
        grid=(B, S // tm, MOE_EXPERTS),
        in_specs=[
            tok(D), tok(LANES), tok(D),
            pl.BlockSpec((None, 6, D), (lambda b, s, e: (b, 0, 0)) if per_batch_mod else (lambda b, s, e: (0, 0, 0))),
            wspec(lw["moe_wg"]), wspec(lw["moe_wu"]), wspec(lw["moe_wd"]),
            pl.BlockSpec(final_norm.shape, lambda b, s, e: (0, 0)),
        ],
        out_specs=tok(D),
        out_shape=jax.ShapeDtypeStruct((B, S, D), F32),
        scratch_shapes=[pltpu.VMEM((tm, D), F32)],
        compiler_params=_cparams(("arbitrary", "arbitrary", "arbitrary")),
        name="moe",
    )(h2, gates, x1, mod, lw["moe_wg"], lw["moe_wu"], lw["moe_wd"], final_norm)


def _swap_halves(w, n_heads, dim):
    k = w.shape[0]
    w = w.reshape(k, n_heads, 2, dim // 2)
    return w[:, :, ::-1, :].reshape(k, n_heads * dim)


def _pad_cols(w, width):
    return jnp.pad(w, ((0, 0), (0, width - w.shape[1])))


def _rope_tables(n_tokens, rot_dim, reps):
    t = jnp.arange(n_tokens)
    row = (t // GRID_W).astype(F32)
    col = (t % GRID_W).astype(F32)
    n_freq = rot_dim // 4
    inv_freq = ROPE_BASE ** (-jnp.arange(n_freq, dtype=F32) / n_freq)
    ang = jnp.concatenate([row[:, None] * inv_freq, col[:, None] * inv_freq], axis=-1)
    cos, sin = jnp.cos(ang), jnp.sin(ang)
    c = jnp.tile(jnp.concatenate([cos, cos], axis=-1), (1, reps))
    s = jnp.tile(jnp.concatenate([-sin, sin], axis=-1), (1, reps))
    return _pad_cols(c, LANES), _pad_cols(s, LANES)


def _layer_weights(l, P, q_abs, ab_re, ab_im, bb_re, bb_im):
    w_in = P["w_in"][l]
    seg = {}
    o = 0
    for name, width in (("u", SSM_CH), ("ql", MLA_Q_RANK), ("kvl", MLA_KV_RANK), ("kr", MLA_ROPE),
                        ("qs", SWA_HEADS * HEAD_DIM), ("ks", SWA_KV_HEADS * HEAD_DIM),
                        ("vs", SWA_KV_HEADS * HEAD_DIM)):
        seg[name] = w_in[:, o:o + width]
        o += width
    ctx_cols = [seg["u"], seg["ql"], seg["kvl"], seg["qs"], seg["ks"], seg["vs"], _pad_cols(seg["kr"], LANES)]
    lat_cols = ctx_cols + [
        _swap_halves(seg["qs"], SWA_HEADS, HEAD_DIM),
        _swap_halves(seg["ks"], SWA_KV_HEADS, HEAD_DIM),
        _pad_cols(_swap_halves(seg["kr"], 1, MLA_ROPE), LANES),
    ]
    w_qb = P["w_mla_qb"][l]
    w_rope = w_qb[:, :, MLA_NOPE:]
    w_rope_sw = w_rope.reshape(MLA_Q_RANK, MLA_HEADS, 2, MLA_ROPE // 2)[:, :, ::-1, :].reshape(w_rope.shape)
    flat_t = lambda w: w.reshape(MLA_Q_RANK, MLA_HEADS * MLA_ROPE).T
    wq_abs = jnp.transpose(q_abs[l], (0, 2, 1)).reshape(Q_ABS, MLA_Q_RANK)
    wq_ctx = jnp.concatenate([wq_abs, flat_t(w_rope)], axis=0)
    wq_lat = jnp.concatenate([wq_ctx, flat_t(w_rope_sw)], axis=0)

    eye = jnp.eye(SSM_GROUPS, dtype=F32)

    def block_diag_b(bb):
        bb = bb.reshape(2, SSM_GROUPS, SSM_STATE, SSM_GROUP)
        return jnp.einsum("dgpc,gh->dgchp", bb, eye).reshape(2, SSM_CH, SSM_N)

    def block_diag_c(cc):
        return jnp.einsum("dgcp,gh->dgphc", cc, eye).reshape(2, SSM_N, SSM_CH)

    sl = slice(2 * l, 2 * l + 2)
    w_router = jnp.concatenate([P["moe_w_expert"][l], P["moe_w_group"][l]], axis=1)
    b_router = jnp.concatenate([P["moe_b_expert"][l], P["moe_b_group"][l]])
    row = lambda v: v.reshape(1, -1)
    return dict(
        norm1=row(P["norm1"][l]), norm2=row(P["norm2"][l]),
        win_ctx=jnp.concatenate(ctx_cols, axis=1).astype(BF16),
        win_lat=jnp.concatenate(lat_cols, axis=1).astype(BF16),
        q_norm=row(P["mla_q_norm"][l]), kv_norm=row(P["mla_kv_norm"][l]),
        wq_ctx=wq_ctx.astype(BF16), wq_lat=wq_lat.astype(BF16),
        wv=jnp.transpose(P["w_mla_kvb"][l][:, :, MLA_NOPE:], (1, 0, 2)).astype(BF16),
        ssm_wb=jnp.concatenate([block_diag_b(bb_re[sl]), block_diag_b(bb_im[sl])], axis=2).astype(BF16),
        ssm_a=jnp.concatenate([ab_re[sl], ab_im[sl]], axis=1).reshape(2, 1, 2 * SSM_N),
        ssm_wc=jnp.concatenate([block_diag_c(P["ssm_c_re"][l]), -block_diag_c(P["ssm_c_im"][l])],
                               axis=1).astype(BF16),
        ssm_d=row(P["ssm_d"][l]), w_glu=P["w_ssm_glu"][l].astype(BF16),
        gn_ssm=row(P["gn_ssm"][l]), gn_mla=row(P["gn_mla"][l]), gn_swa=row(P["gn_swa"][l]),
        w_out=P["w_out"][l].astype(BF16),
        w_router=_pad_cols(w_router, LANES), b_router=_pad_cols(row(b_router), LANES),
        moe_wg=P["moe_w_gate"][l].reshape(MOE_EXPERTS, D_MODEL, MOE_HIDDEN).astype(BF16),
        moe_wu=P["moe_w_up"][l].reshape(MOE_EXPERTS, D_MODEL, MOE_HIDDEN).astype(BF16),
        moe_wd=P["moe_w_down"][l].reshape(MOE_EXPERTS, MOE_HIDDEN, D_MODEL).astype(BF16),
        sink=P["swa_sink"][l],
    )


def _layer(x, mod, lw, final_norm, *, tables, ctx, per_batch_mod, final):
    B, S, _ = x.shape
    context_pass = ctx is None
    pre = _pre(x, mod, lw, tables, per_batch_mod=per_batch_mod, emit_ctx=context_pass)
    u_tm, qt, kcat, ckvt, qs, ks, vs = pre[:7]
    nb = B // SUBLANES
    if context_pass:
        h0 = jnp.zeros((2, nb, SUBLANES, 2 * SSM_N), F32)
        y2, hfin = _ssm(u_tm, lw, h0, B, S)
        o_mla = _mla(qt, kcat, ckvt, None, None, lw["wv"])
        o_swa = _swa(lw["sink"], qs, ks, vs, None, None)
        state = hfin.reshape(2, B, 2, SSM_GROUPS, SSM_STATE).transpose(1, 0, 2, 3, 4)
        new_ctx = (pre[7], pre[8], pre[9].reshape(B, S, SWA_KV_HEADS, HEAD_DIM),
                   pre[10].reshape(B, S, SWA_KV_HEADS, HEAD_DIM), state)
    else:
        kcat_c, ckvt_c, ks_c, vs_c, h0 = ctx
        y2, _ = _ssm(u_tm, lw, h0, B, S)
        o_mla = _mla(qt, kcat_c, ckvt_c, kcat, ckvt, lw["wv"])
        o_swa = _swa(lw["sink"], qs, ks_c, vs_c, ks, vs)
        new_ctx = None
    x1, h2, gates = _post(x, u_tm, y2, o_mla, o_swa, mod, lw, per_batch_mod=per_batch_mod)
    if not per_batch_mod:
        shp = lambda a: a.reshape(1, B * S, a.shape[-1])
        xo = _moe(shp(h2), shp(gates), shp(x1), mod, lw, final_norm, per_batch_mod=False, final=final)
        xo = xo.reshape(B, S, D_MODEL)
    else:
        xo = _moe(h2, gates, x1, mod, lw, final_norm, per_batch_mod=True, final=final)
    return xo, new_ctx


def kernel(x_prompt, x_sample, c, cache_mla_ckv, cache_mla_krope, cache_swa_k, cache_swa_v, state_ssm, c_ctx, w_ada, b_ada, norm1, norm2, w_in, ssm_a_re, ssm_a_im, ssm_log_dt, ssm_b_re, ssm_b_im, ssm_c_re, ssm_c_im, ssm_d, w_ssm_glu, mla_q_norm, w_mla_qb, mla_kv_norm, w_mla_kvb, swa_sink, gn_ssm, gn_mla, gn_swa, w_out, moe_w_group, moe_b_group, moe_w_expert, moe_b_expert, moe_w_gate, moe_w_up, moe_w_down, final_norm):
    P = dict(w_ada=w_ada, b_ada=b_ada, norm1=norm1, norm2=norm2, w_in=w_in,
             ssm_c_re=ssm_c_re, ssm_c_im=ssm_c_im, ssm_d=ssm_d, w_ssm_glu=w_ssm_glu,
             mla_q_norm=mla_q_norm, w_mla_qb=w_mla_qb, mla_kv_norm=mla_kv_norm, w_mla_kvb=w_mla_kvb,
             swa_sink=swa_sink, gn_ssm=gn_ssm, gn_mla=gn_mla, gn_swa=gn_swa, w_out=w_out,
             moe_w_group=moe_w_group, moe_b_group=moe_b_group, moe_w_expert=moe_w_expert,
             moe_b_expert=moe_b_expert, moe_w_gate=moe_w_gate, moe_w_up=moe_w_up, moe_w_down=moe_w_down)
    n_dec = c.shape[0]
    n_cond = 2 * SUBLANES
    conds = jnp.zeros((n_cond, D_MODEL), F32).at[:n_dec].set(c).at[n_dec].set(c_ctx)
    mods = _modulation(conds, w_ada, b_ada).reshape(DEPTH, n_cond, 6, D_MODEL)

    ab_re, ab_im, bb_re, bb_im = _ssm_discretise(ssm_a_re, ssm_a_im, ssm_log_dt, ssm_b_re, ssm_b_im)
    q_abs = _absorb_q(jnp.transpose(w_mla_qb[..., :MLA_NOPE], (0, 2, 1, 3)),
                      jnp.transpose(w_mla_kvb[..., :MLA_NOPE], (0, 2, 1, 3)))
    lws = [_layer_weights(l, P, q_abs, ab_re.reshape(2 * DEPTH, SSM_N), ab_im.reshape(2 * DEPTH, SSM_N),
                          bb_re, bb_im) for l in range(DEPTH)]
    fnorm = final_norm.reshape(1, D_MODEL)

    xp = x_prompt
    ctx_states = []
    for l in range(DEPTH):
        xp, new = _layer(xp, mods[l, n_dec:n_dec + 1], lws[l], fnorm, tables=None, ctx=None,
                         per_batch_mod=False, final=l == DEPTH - 1)
        ctx_states.append(new)
    outs_ctx = tuple(jnp.stack([s[k] for s in ctx_states], axis=1) for k in range(5))

    n_lat = x_sample.shape[1]
    cm, sm = _rope_tables(n_lat, MLA_ROPE, 1)
    cs, ss = _rope_tables(n_lat, HEAD_DIM, LANES // HEAD_DIM)
    tables = (cm, sm, cm[:, :MLA_ROPE].T, sm[:, :MLA_ROPE].T, cs, ss)
    xs = x_sample
    past = cache_mla_ckv.shape[2]
    for l in range(DEPTH):
        kcat_c = jnp.concatenate(
            [cache_mla_ckv[:, l], cache_mla_krope[:, l],
             jnp.zeros((n_dec, past, LANES - MLA_ROPE), F32)], axis=-1).astype(BF16)
        ks_c = cache_swa_k[:, l].reshape(n_dec, past, LANES).astype(BF16)
        vs_c = cache_swa_v[:, l].reshape(n_dec, past, LANES).astype(BF16)
        h0 = state_ssm[:, l].transpose(1, 0, 2, 3, 4).reshape(2, n_dec // SUBLANES, SUBLANES, 2 * SSM_N)
        ckvt_c = jnp.transpose(cache_mla_ckv[:, l], (0, 2, 1)).astype(BF16)
        xs, _ = _layer(xs, mods[l, :n_dec], lws[l], fnorm, tables=tables, ctx=(kcat_c, ckvt_c, ks_c, vs_c, h0),
                       per_batch_mod=True, final=l == DEPTH - 1)
    return (xp, xs) + outs_ctx
```

```python
import functools
import math

import jax
import jax.numpy as jnp
from jax import lax
from jax.experimental import pallas as pl
from jax.experimental.pallas import tpu as pltpu

F32 = jnp.float32
BF16 = jnp.bfloat16

D_MODEL = 1024
DEPTH = 4
GRID_W = 64
HEAD_DIM = 64
SSM_CH = 256
SSM_GROUP = 16
SSM_GROUPS = SSM_CH // SSM_GROUP
SSM_STATE = 64
SSM_N = SSM_GROUPS * SSM_STATE
MLA_HEADS = 6
MLA_Q_RANK = 256
MLA_KV_RANK = 128
MLA_NOPE = 64
MLA_ROPE = 32
MLA_V = 64
SWA_HEADS = 6
SWA_KV_HEADS = 2
SWA_GROUP = SWA_HEADS // SWA_KV_HEADS
SWA_WINDOW = 128
SWA_BLOCK = 128
MOE_GROUPS = 4
MOE_PER_GROUP = 8
MOE_EXPERTS = MOE_GROUPS * MOE_PER_GROUP
MOE_HIDDEN = 256
GROUP_LANE = MOE_EXPERTS
MOE_ROW_BLOCK = 256
MOE_CODE_STRIDE = 1 << 16
ROPE_BASE = 10000.0
EPS = 1e-6
MLA_SCALE = 1.0 / math.sqrt(MLA_NOPE + MLA_ROPE)
MLA_SCALE_LOG2 = MLA_SCALE * math.log2(math.e)
SWA_SCALE = 1.0 / math.sqrt(HEAD_DIM)

LANES = 128
SUBLANES = 8
VMEM_LIMIT = 52 * 1024 * 1024

OFF_U = 0
OFF_QL = OFF_U + SSM_CH
OFF_KVL = OFF_QL + MLA_Q_RANK
OFF_QS = OFF_KVL + MLA_KV_RANK
OFF_KS = OFF_QS + SWA_HEADS * HEAD_DIM
OFF_VS = OFF_KS + SWA_KV_HEADS * HEAD_DIM
OFF_KR = OFF_VS + SWA_KV_HEADS * HEAD_DIM
N_PACK_CTX = OFF_KR + LANES
OFF_QS_SW = N_PACK_CTX
OFF_KS_SW = OFF_QS_SW + SWA_HEADS * HEAD_DIM
OFF_KR_SW = OFF_KS_SW + SWA_KV_HEADS * HEAD_DIM
N_PACK_LAT = OFF_KR_SW + LANES
Q_ABS = MLA_HEADS * MLA_KV_RANK
MLA_QK = 2 * LANES


def _cparams(sem):
    return pltpu.CompilerParams(dimension_semantics=sem, vmem_limit_bytes=VMEM_LIMIT)


def _dot(a, b):
    return jnp.dot(a, b, preferred_element_type=F32)


def _dot_nt(a, b):
    return lax.dot_general(a, b, (((1,), (1,)), ((), ())), preferred_element_type=F32)


def _split(x):
    hi = x.astype(BF16)
    lo = (x - hi.astype(F32)).astype(BF16)
    return hi, lo


def _dot3(a, b):
    ah, al = _split(a)
    bh, bl = _split(b)
    return _dot(ah, bh) + _dot(al, bh) + _dot(ah, bl)


def _tree(op, x3):
    parts = [x3[i] for i in range(x3.shape[0])]
    while len(parts) > 1:
        parts = [op(parts[i], parts[i + 1]) for i in range(0, len(parts), 2)]
    return parts[0]


def _rms(x, g):
    return x * lax.rsqrt(jnp.mean(x * x, axis=-1, keepdims=True) + EPS) * g


def _silu(x):
    return x * jax.nn.sigmoid(x)


def _gelu_tanh(x):
    return 0.5 * x * (1.0 + jnp.tanh(math.sqrt(2.0 / math.pi) * (x + 0.044715 * (x * x * x))))


def _mod_kernel(c_ref, w_ref, b_ref, o_ref):
    o_ref[...] = _dot3(_silu(c_ref[...]), w_ref[...]) + b_ref[...]


def _modulation(conds, w_ada, b_ada):
    n = conds.shape[0]
    tn = 1536
    return pl.pallas_call(
        _mod_kernel,
        grid=(DEPTH, 6 * D_MODEL // tn),
        in_specs=[
            pl.BlockSpec((n, D_MODEL), lambda l, j: (0, 0)),
            pl.BlockSpec((None, D_MODEL, tn), lambda l, j: (l, 0, j)),
            pl.BlockSpec((None, 1, tn), lambda l, j: (l, 0, j)),
        ],
        out_specs=pl.BlockSpec((None, n, tn), lambda l, j: (l, 0, j)),
        out_shape=jax.ShapeDtypeStruct((DEPTH, n, 6 * D_MODEL), F32),
        compiler_params=_cparams(("arbitrary", "arbitrary")),
        name="modulation",
    )(conds, w_ada, b_ada.reshape(DEPTH, 1, 6 * D_MODEL))


def _ssm_disc_kernel(are_ref, aim_ref, ldt_ref, bre_ref, bim_ref, abre_ref, abim_ref, bbre_ref, bbim_ref):
    lam_re = are_ref[...]
    lam_im = aim_ref[...]
    dt = jnp.exp(ldt_ref[...])
    z_re = lam_re * dt
    z_im = lam_im * dt
    mag = jnp.exp(z_re)
    ab_re = mag * jnp.cos(z_im)
    ab_im = mag * jnp.sin(z_im)
    den = lam_re * lam_re + lam_im * lam_im
    f_re = ((ab_re - 1.0) * lam_re + ab_im * lam_im) / den
    f_im = (ab_im * lam_re - (ab_re - 1.0) * lam_im) / den
    b_re = bre_ref[...]
    b_im = bim_ref[...]
    abre_ref[...] = ab_re
    abim_ref[...] = ab_im
    bbre_ref[...] = f_re * b_re - f_im * b_im
    bbim_ref[...] = f_re * b_im + f_im * b_re


def _ssm_discretise(a_re, a_im, log_dt, b_re, b_im):
    n = DEPTH * 2
    col = lambda v: v.reshape(n, SSM_N, 1)
    ldt = jnp.broadcast_to(log_dt[..., None], (DEPTH, 2, SSM_GROUPS, SSM_STATE))
    cspec = pl.BlockSpec((None, SSM_N, 1), lambda i: (i, 0, 0))
    bspec = pl.BlockSpec((None, SSM_N, SSM_GROUP), lambda i: (i, 0, 0))
    return pl.pallas_call(
        _ssm_disc_kernel,
        grid=(n,),
        in_specs=[cspec, cspec, cspec, bspec, bspec],
        out_specs=[cspec, cspec, bspec, bspec],
        out_shape=[jax.ShapeDtypeStruct((n, SSM_N, 1), F32)] * 2
        + [jax.ShapeDtypeStruct((n, SSM_N, SSM_GROUP), F32)] * 2,
        compiler_params=_cparams(("arbitrary",)),
        name="ssm_discretise",
    )(col(a_re), col(a_im), col(ldt), b_re.reshape(n, SSM_N, SSM_GROUP), b_im.reshape(n, SSM_N, SSM_GROUP))


def _absorb_kernel(wq_ref, wk_ref, o_ref):
    a = wq_ref[...]
    b = wk_ref[...]
    ah, al = _split(a)
    bh, bl = _split(b)
    o_ref[...] = _dot_nt(ah, bh) + _dot_nt(al, bh) + _dot_nt(ah, bl)


def _absorb_q(wq_nope, wk_nope):
    return pl.pallas_call(
        _absorb_kernel,
        grid=(DEPTH, MLA_HEADS),
        in_specs=[
            pl.BlockSpec((None, None, MLA_Q_RANK, MLA_NOPE), lambda l, h: (l, h, 0, 0)),
            pl.BlockSpec((None, None, MLA_KV_RANK, MLA_NOPE), lambda l, h: (l, h, 0, 0)),
        ],
        out_specs=pl.BlockSpec((None, None, MLA_Q_RANK, MLA_KV_RANK), lambda l, h: (l, h, 0, 0)),
        out_shape=jax.ShapeDtypeStruct((DEPTH, MLA_HEADS, MLA_Q_RANK, MLA_KV_RANK), F32),
        compiler_params=_cparams(("arbitrary", "arbitrary")),
        name="mla_absorb",
    )(wq_nope, wk_nope)


def _pre_kernel(*refs, rope, emit_ctx):
    it = iter(refs)
    x_ref, mod_ref, n1_ref, win_ref, qn_ref, kvn_ref, wq_ref = (next(it) for _ in range(7))
    if rope:
        cm_ref, sm_ref, cmt_ref, smt_ref, cs_ref, ss_ref = (next(it) for _ in range(6))
    u_ref, qt_ref, kcat_ref, ckvt_ref, qs_ref, ks_ref, vs_ref = (next(it) for _ in range(7))
    if emit_ctx:
        ckv_o, kr_o, ks_o, vs_o = (next(it) for _ in range(4))

    x = x_ref[...]
    mod = mod_ref[...]
    h = _rms(x, n1_ref[...]) * (1.0 + mod[1:2]) + mod[0:1]
    proj = _dot(h.astype(BF16), win_ref[...])

    u_ref[...] = proj[:, OFF_U:OFF_U + SSM_CH]

    qln = _rms(proj[:, OFF_QL:OFF_QL + MLA_Q_RANK], qn_ref[...]).astype(BF16)
    qall = _dot_nt(wq_ref[...], qln)
    n_rope = MLA_HEADS * MLA_ROPE
    zero_rows = jnp.zeros((MLA_QK - MLA_KV_RANK - MLA_ROPE, qall.shape[1]), BF16)
    for i in range(MLA_HEADS):
        qa = qall[i * MLA_KV_RANK:(i + 1) * MLA_KV_RANK]
        qr = qall[Q_ABS + i * MLA_ROPE:Q_ABS + (i + 1) * MLA_ROPE]
        if rope:
            qr_sw = qall[Q_ABS + n_rope + i * MLA_ROPE:Q_ABS + n_rope + (i + 1) * MLA_ROPE]
            qr = qr * cmt_ref[...] + qr_sw * smt_ref[...]
        base = i * MLA_QK
        qt_ref[base:base + MLA_KV_RANK, :] = (qa * MLA_SCALE_LOG2).astype(BF16)
        qt_ref[base + MLA_KV_RANK:base + MLA_KV_RANK + MLA_ROPE, :] = (qr * MLA_SCALE_LOG2).astype(BF16)
        qt_ref[base + MLA_KV_RANK + MLA_ROPE:base + MLA_QK, :] = zero_rows

    ckv = _rms(proj[:, OFF_KVL:OFF_KVL + MLA_KV_RANK], kvn_ref[...])
    kr = proj[:, OFF_KR:OFF_KR + LANES]
    if emit_ctx:
        ckv_o[...] = ckv
        kr_o[...] = kr[:, :MLA_ROPE]
    if rope:
        kr = kr * cm_ref[...] + proj[:, OFF_KR_SW:OFF_KR_SW + LANES] * sm_ref[...]
    kcat_ref[...] = jnp.concatenate([ckv, kr], axis=-1).astype(BF16)
    ckvt_ref[...] = ckv.T.astype(BF16)

    qs = proj[:, OFF_QS:OFF_QS + SWA_HEADS * HEAD_DIM]
    ks = proj[:, OFF_KS:OFF_KS + SWA_KV_HEADS * HEAD_DIM]
    vs = proj[:, OFF_VS:OFF_VS + SWA_KV_HEADS * HEAD_DIM]
    if emit_ctx:
        ks_o[...] = ks
        vs_o[...] = vs
    if rope:
        cs = cs_ref[...]
        ss = ss_ref[...]
        qs_sw = proj[:, OFF_QS_SW:OFF_QS_SW + SWA_HEADS * HEAD_DIM]
        ks = ks * cs + proj[:, OFF_KS_SW:OFF_KS_SW + SWA_KV_HEADS * HEAD_DIM] * ss
        qs = jnp.concatenate(
            [qs[:, i * LANES:(i + 1) * LANES] * cs + qs_sw[:, i * LANES:(i + 1) * LANES] * ss
             for i in range(SWA_HEADS * HEAD_DIM // LANES)], axis=-1)
    qs_ref[...] = (qs * SWA_SCALE).astype(BF16)
    ks_ref[...] = ks.astype(BF16)
    vs_ref[...] = vs.astype(BF16)


def _pre(x, mod, lw, tables, *, per_batch_mod, emit_ctx):
    B, S, D = x.shape
    rope = tables is not None
    ts = min(S, 512)
    n_pack = N_PACK_LAT if rope else N_PACK_CTX
    win = lw["win_lat"] if rope else lw["win_ctx"]
    wq = lw["wq_lat"] if rope else lw["wq_ctx"]
    tok = lambda w: pl.BlockSpec((None, ts, w), lambda b, s: (b, s, 0))
    full = lambda a: pl.BlockSpec(a.shape, lambda b, s: (0,) * a.ndim)
    in_specs = [
        tok(D),
        pl.BlockSpec((None, 6, D), (lambda b, s: (b, 0, 0)) if per_batch_mod else (lambda b, s: (0, 0, 0))),
        full(lw["norm1"]), full(win), full(lw["q_norm"]), full(lw["kv_norm"]), full(wq),
    ]
    args = [x, mod, lw["norm1"], win, lw["q_norm"], lw["kv_norm"], wq]
    if rope:
        row_tab = pl.BlockSpec((ts, LANES), lambda b, s: (s, 0))
        col_tab = pl.BlockSpec((MLA_ROPE, ts), lambda b, s: (0, s))
        in_specs += [row_tab, row_tab, col_tab, col_tab, row_tab, row_tab]
        args += list(tables)
    feat = lambda w: pl.BlockSpec((None, w, ts), lambda b, s: (b, 0, s))
    out_specs = [
        pl.BlockSpec((ts, SSM_CH), lambda b, s: (s, b)),
        feat(MLA_HEADS * MLA_QK), tok(MLA_QK), feat(MLA_KV_RANK), tok(SWA_HEADS * HEAD_DIM), tok(LANES), tok(LANES),
    ]
    out_shape = [
        jax.ShapeDtypeStruct((S, B * SSM_CH), F32),
        jax.ShapeDtypeStruct((B, MLA_HEADS * MLA_QK, S), BF16),
        jax.ShapeDtypeStruct((B, S, MLA_QK), BF16),
        jax.ShapeDtypeStruct((B, MLA_KV_RANK, S), BF16),
        jax.ShapeDtypeStruct((B, S, SWA_HEADS * HEAD_DIM), BF16),
        jax.ShapeDtypeStruct((B, S, LANES), BF16),
        jax.ShapeDtypeStruct((B, S, LANES), BF16),
    ]
    if emit_ctx:
        out_specs += [tok(MLA_KV_RANK), tok(MLA_ROPE), tok(LANES), tok(LANES)]
        out_shape += [
            jax.ShapeDtypeStruct((B, S, MLA_KV_RANK), F32),
            jax.ShapeDtypeStruct((B, S, MLA_ROPE), F32),
            jax.ShapeDtypeStruct((B, S, LANES), F32),
            jax.ShapeDtypeStruct((B, S, LANES), F32),
        ]
    return pl.pallas_call(
        functools.partial(_pre_kernel, rope=rope, emit_ctx=emit_ctx),
        grid=(B, S // ts),
        in_specs=in_specs,
        out_specs=out_specs,
        out_shape=out_shape,
        compiler_params=_cparams(("arbitrary", "arbitrary")),
        name="pre_lat" if rope else "pre_ctx",
    )(*args)


def _ssm_kernel(u_ref, wb_ref, a_ref, wc_ref, h0_ref, y_ref, hfin_ref, hre_s, him_s, st_s, *, tc):
    d = pl.program_id(0)
    i = pl.program_id(2)
    n = pl.num_programs(2)

    @pl.when(i == 0)
    def _():
        st_s[...] = h0_ref[...]

    u = u_ref[...].reshape(tc * SUBLANES, SSM_CH).astype(BF16)
    bu = _dot(u, wb_ref[...])
    hre_s[...] = bu[:, :SSM_N]
    him_s[...] = bu[:, SSM_N:]

    a = a_ref[...]
    a_re = jnp.broadcast_to(a[:, :SSM_N], (SUBLANES, SSM_N))
    a_im = jnp.broadcast_to(a[:, SSM_N:], (SUBLANES, SSM_N))
    st = st_s[...]

    def step(t, carry):
        h_re, h_im = carry
        tt = jnp.where(d == 0, t, tc - 1 - t)
        rows = pl.ds(pl.multiple_of(tt * SUBLANES, SUBLANES), SUBLANES)
        n_re = a_re * h_re - a_im * h_im + hre_s[rows, :]
        n_im = a_re * h_im + a_im * h_re + him_s[rows, :]
        hre_s[rows, :] = n_re
        him_s[rows, :] = n_im
        return n_re, n_im

    h_re, h_im = lax.fori_loop(0, tc, step, (st[:, :SSM_N], st[:, SSM_N:]), unroll=4)
    st_s[...] = jnp.concatenate([h_re, h_im], axis=-1)

    wc = wc_ref[...]
    y = _dot(hre_s[...].astype(BF16), wc[:SSM_N]) + _dot(him_s[...].astype(BF16), wc[SSM_N:])
    y_ref[...] = y.reshape(tc, SUBLANES, SSM_CH)

    @pl.when(i == n - 1)
    def _():
        hfin_ref[...] = st_s[...]


def _ssm(u_tm, lw, h0, B, S):
    nb = B // SUBLANES
    tc = 128
    nchunk = S // tc
    chunk = lambda d, i: i + d * (nchunk - 1 - 2 * i)
    u4 = u_tm.reshape(S, nb, SUBLANES, SSM_CH)
    y, hfin = pl.pallas_call(
        functools.partial(_ssm_kernel, tc=tc),
        grid=(2, nb, nchunk),
        in_specs=[
            pl.BlockSpec((tc, None, SUBLANES, SSM_CH), lambda d, b, i: (chunk(d, i), b, 0, 0)),
            pl.BlockSpec((None, SSM_CH, 2 * SSM_N), lambda d, b, i: (d, 0, 0)),
            pl.BlockSpec((None, 1, 2 * SSM_N), lambda d, b, i: (d, 0, 0)),
            pl.BlockSpec((None, 2 * SSM_N, SSM_CH), lambda d, b, i: (d, 0, 0)),
            pl.BlockSpec((None, None, SUBLANES, 2 * SSM_N), lambda d, b, i: (d, b, 0, 0)),
        ],
        out_specs=[
            pl.BlockSpec((None, tc, None, SUBLANES, SSM_CH), lambda d, b, i: (d, chunk(d, i), b, 0, 0)),
            pl.BlockSpec((None, None, SUBLANES, 2 * SSM_N), lambda d, b, i: (d, b, 0, 0)),
        ],
        out_shape=[
            jax.ShapeDtypeStruct((2, S, nb, SUBLANES, SSM_CH), F32),
            jax.ShapeDtypeStruct((2, nb, SUBLANES, 2 * SSM_N), F32),
        ],
        scratch_shapes=[
            pltpu.VMEM((tc * SUBLANES, SSM_N), F32),
            pltpu.VMEM((tc * SUBLANES, SSM_N), F32),
            pltpu.VMEM((SUBLANES, 2 * SSM_N), F32),
        ],
        compiler_params=_cparams(("arbitrary", "arbitrary", "arbitrary")),
        name="ssm_scan",
    )(u4, lw["ssm_wb"], lw["ssm_a"], lw["ssm_wc"], h0)
    return y.reshape(2, S, B * SSM_CH), hfin


def _mla_kernel(*refs, with_latent, tk):
    if with_latent:
        qt_ref, ka_ref, vat_ref, kb_ref, vbt_ref, wv_ref, o_ref = refs
    else:
        qt_ref, ka_ref, vat_ref, wv_ref, o_ref = refs
    tq = qt_ref.shape[1]
    q_of = lambda hd: qt_ref[hd * MLA_QK:(hd + 1) * MLA_QK, :]

    def tile(state, k, vt):
        new_state = []
        s_next = _dot(k, q_of(0))
        for hd in range(MLA_HEADS):
            m8, l8, acc = state[hd]
            s = s_next
            if hd + 1 < MLA_HEADS:
                s_next = _dot(k, q_of(hd + 1))
            s3 = s.reshape(s.shape[0] // SUBLANES, SUBLANES, tq)
            mloc = jnp.max(_tree(jnp.maximum, s3), axis=0, keepdims=True)
            m8_new = jnp.maximum(m8, jnp.broadcast_to(mloc, (SUBLANES, tq)))
            alpha8 = jnp.exp2(m8 - m8_new)
            p3 = jnp.exp2(s3 - m8_new[None])
            l8 = alpha8 * l8 + _tree(jnp.add, p3)
            pv = _dot(vt, p3.reshape(s.shape).astype(BF16))
            acc3 = acc.reshape(MLA_KV_RANK // SUBLANES, SUBLANES, tq) * alpha8[None]
            new_state.append((m8_new, l8, acc3.reshape(MLA_KV_RANK, tq) + pv))
        return tuple(new_state)

    init = tuple((jnp.full((SUBLANES, tq), -jnp.inf, F32), jnp.zeros((SUBLANES, tq), F32),
                  jnp.zeros((MLA_KV_RANK, tq), F32)) for _ in range(MLA_HEADS))
    state = tile(init, ka_ref[...], vat_ref[...])
    if with_latent:
        def body(j, st):
            off = pl.multiple_of(j * tk, tk)
            return tile(st, kb_ref[pl.ds(off, tk), :], vbt_ref[:, pl.ds(off, tk)])
        state = lax.fori_loop(0, kb_ref.shape[0] // tk, body, state)

    outs = []
    for hd in range(MLA_HEADS):
        _, l8, acc = state[hd]
        o_lat = (acc / jnp.sum(l8, axis=0, keepdims=True)).T.astype(BF16)
        outs.append(_dot(o_lat, wv_ref[hd]))
    o_ref[...] = jnp.concatenate(outs, axis=-1)


def _mla(qt, ka, vat, kb, vbt, wv):
    B, _, S = qt.shape
    tq = 256
    with_latent = kb is not None
    kspec = lambda a: pl.BlockSpec((None,) + a.shape[1:], lambda b, s: (b, 0, 0))
    in_specs = [pl.BlockSpec((None, MLA_HEADS * MLA_QK, tq), lambda b, s: (b, 0, s)), kspec(ka), kspec(vat)]
    args = [qt, ka, vat]
    if with_latent:
        in_specs += [kspec(kb), kspec(vbt)]
        args += [kb, vbt]
    in_specs.append(pl.BlockSpec(wv.shape, lambda b, s: (0, 0, 0)))
    args.append(wv)
    return pl.pallas_call(
        functools.partial(_mla_kernel, with_latent=with_latent, tk=1024),
        grid=(B, S // tq),
        in_specs=in_specs,
        out_specs=pl.BlockSpec((None, tq, MLA_HEADS * MLA_V), lambda b, s: (b, s, 0)),
        out_shape=jax.ShapeDtypeStruct((B, S, MLA_HEADS * MLA_V), F32),
        compiler_params=_cparams(("arbitrary", "arbitrary")),
        name="mla_lat" if with_latent else "mla_ctx",
    )(*args)


def _swa_kernel(*refs, with_latent):
    if with_latent:
        sink_ref, q_ref, ka_ref, va_ref, kb_ref, vb_ref, o_ref = refs
    else:
        sink_ref, q_ref, ka_ref, va_ref, o_ref = refs
    tq = q_ref.shape[0]
    q = q_ref[...]
    ka = ka_ref[...]
    va = va_ref[...]
    if with_latent:
        n = pl.program_id(1)
        nblk = kb_ref.shape[0] // SWA_BLOCK
        start = pl.multiple_of(jnp.clip(n - 1, 0, nblk - 3) * SWA_BLOCK, SWA_BLOCK)
        kb = kb_ref[pl.ds(start, 3 * SWA_BLOCK), :]
        vb = vb_ref[pl.ds(start, 3 * SWA_BLOCK), :]
        qpos = n * SWA_BLOCK + lax.broadcasted_iota(jnp.int32, (tq, 3 * SWA_BLOCK), 0)
        kpos = start + lax.broadcasted_iota(jnp.int32, (tq, 3 * SWA_BLOCK), 1)
        valid = jnp.abs(qpos - kpos) <= SWA_WINDOW
    outs = []
    for hd in range(SWA_HEADS):
        kh = hd // SWA_GROUP
        qh = q[:, hd * HEAD_DIM:(hd + 1) * HEAD_DIM]
        ksl = slice(kh * HEAD_DIM, (kh + 1) * HEAD_DIM)
        sink = sink_ref[hd]
        s_a = _dot_nt(qh, ka[:, ksl])
        m = jnp.maximum(jnp.max(s_a, axis=-1, keepdims=True), sink)
        if with_latent:
            s_b = jnp.where(valid, _dot_nt(qh, kb[:, ksl]), -jnp.inf)
            m = jnp.maximum(m, jnp.max(s_b, axis=-1, keepdims=True))
        p_a = jnp.exp(s_a - m)
        den = jnp.sum(p_a, axis=-1, keepdims=True) + jnp.exp(sink - m)
        o = _dot(p_a.astype(BF16), va[:, ksl])
        if with_latent:
            p_b = jnp.exp(s_b - m)
            den = den + jnp.sum(p_b, axis=-1, keepdims=True)
            o = o + _dot(p_b.astype(BF16), vb[:, ksl])
        outs.append(o / den)
    o_ref[...] = jnp.concatenate(outs, axis=-1)


def _swa(sink, q, ka, va, kb, vb):
    B, S, W = q.shape
    with_latent = kb is not None
    tq = SWA_BLOCK if with_latent else S
    tok = pl.BlockSpec((None, tq, W), lambda b, s: (b, s, 0))
    kspec = lambda a: pl.BlockSpec((None,) + a.shape[1:], lambda b, s: (b, 0, 0))
    in_specs = [pl.BlockSpec(memory_space=pltpu.SMEM), tok, kspec(ka), kspec(va)]
    args = [sink, q, ka, va]
    if with_latent:
        in_specs += [kspec(kb), kspec(vb)]
        args += [kb, vb]
    return pl.pallas_call(
        functools.partial(_swa_kernel, with_latent=with_latent),
        grid=(B, S // tq),
        in_specs=in_specs,
        out_specs=tok,
        out_shape=jax.ShapeDtypeStruct((B, S, W), F32),
        compiler_params=_cparams(("arbitrary", "arbitrary")),
        name="swa_lat" if with_latent else "swa_ctx",
    )(*args)


def _route(logits):
    lane = lax.broadcasted_iota(jnp.int32, logits.shape, 1)
    big = jnp.int32(1 << 20)
    is_g = (lane >= MOE_EXPERTS) & (lane < MOE_EXPERTS + MOE_GROUPS)
    lg = jnp.where(is_g, logits, -jnp.inf)
    mg = jnp.max(lg, axis=-1, keepdims=True)
    g_idx = jnp.min(jnp.where(lg == mg, lane - MOE_EXPERTS, big), axis=-1, keepdims=True)
    pg_top = 1.0 / jnp.sum(jnp.exp(lg - mg), axis=-1, keepdims=True)

    is_e = (lane < MOE_EXPERTS) & ((lane // MOE_PER_GROUP) == g_idx)
    le = jnp.where(is_e, logits, -jnp.inf)
    m1 = jnp.max(le, axis=-1, keepdims=True)
    e1 = jnp.min(jnp.where(le == m1, lane, big), axis=-1, keepdims=True)
    z = jnp.sum(jnp.exp(le - m1), axis=-1, keepdims=True)
    le2 = jnp.where(lane == e1, -jnp.inf, le)
    m2 = jnp.max(le2, axis=-1, keepdims=True)
    e2 = jnp.min(jnp.where(le2 == m2, lane, big), axis=-1, keepdims=True)
    p1 = 1.0 / z
    p2 = jnp.exp(m2 - m1) / z
    tot = p1 + p2
    gates = pg_top * (jnp.where(lane == e1, p1 / tot, 0.0) + jnp.where(lane == e2, p2 / tot, 0.0))
    return gates + jnp.where(lane == GROUP_LANE, g_idx.astype(F32), 0.0)


def _post_kernel(x_ref, u_ref, yf_ref, yb_ref, om_ref, os_ref, mod_ref, d_ref, wglu_ref, gs_ref, gm_ref,
                 gw_ref, wout_ref, n2_ref, wr_ref, br_ref, x1_ref, h2_ref, gate_ref):
    mod = mod_ref[...]
    y = d_ref[...] * u_ref[...] + yf_ref[...] + yb_ref[...]
    ga = _dot(_gelu_tanh(y).astype(BF16), wglu_ref[...])
    y_ssm = ga[:, :SSM_CH] * jax.nn.sigmoid(ga[:, SSM_CH:])
    wout = wout_ref
    n_mla = MLA_HEADS * MLA_V
    mixed = (_dot(_rms(y_ssm, gs_ref[...]).astype(BF16), wout[0:SSM_CH, :])
             + _dot(_rms(om_ref[...], gm_ref[...]).astype(BF16), wout[SSM_CH:SSM_CH + n_mla, :])
             + _dot(_rms(os_ref[...], gw_ref[...]).astype(BF16), wout[SSM_CH + n_mla:, :]))
    x1 = x_ref[...] + mod[2:3] * mixed
    x1_ref[...] = x1
    h2 = _rms(x1, n2_ref[...]) * (1.0 + mod[4:5]) + mod[3:4]
    h2_ref[...] = h2.astype(BF16)
    gate_ref[...] = _route(_dot3(h2, wr_ref[...]) + br_ref[...])


def _post(x, u_tm, y2, o_mla, o_swa, mod, lw, *, per_batch_mod):
    B, S, D = x.shape
    ts = min(S, 512)
    tok = lambda w: pl.BlockSpec((None, ts, w), lambda b, s: (b, s, 0))
    full = lambda a: pl.BlockSpec(a.shape, lambda b, s: (0,) * a.ndim)
    names = ["ssm_d", "w_glu", "gn_ssm", "gn_mla", "gn_swa", "w_out", "norm2", "w_router", "b_router"]
    in_specs = [
        tok(D),
        pl.BlockSpec((ts, SSM_CH), lambda b, s: (s, b)),
        pl.BlockSpec((None, ts, SSM_CH), lambda b, s: (0, s, b)),
        pl.BlockSpec((None, ts, SSM_CH), lambda b, s: (1, s, b)),
        tok(MLA_HEADS * MLA_V), tok(SWA_HEADS * HEAD_DIM),
        pl.BlockSpec((None, 6, D), (lambda b, s: (b, 0, 0)) if per_batch_mod else (lambda b, s: (0, 0, 0))),
    ] + [full(lw[k]) for k in names]
    return pl.pallas_call(
        _post_kernel,
        grid=(B, S // ts),
        in_specs=in_specs,
        out_specs=[tok(D), tok(D), tok(LANES)],
        out_shape=[
            jax.ShapeDtypeStruct((B, S, D), F32),
            jax.ShapeDtypeStruct((B, S, D), BF16),
            jax.ShapeDtypeStruct((B, S, LANES), F32),
        ],
        compiler_params=_cparams(("arbitrary", "arbitrary")),
        name="post",
    )(x, u_tm, y2, y2, o_mla, o_swa, mod, *[lw[k] for k in names])


def _split3(x):
    hi = x.astype(BF16)
    r = x - hi.astype(F32)
    mid = r.astype(BF16)
    lo = (r - mid.astype(F32)).astype(BF16)
    return hi, mid, lo


def _moe_kernel(cnt_ref, h2_ref, gate_ref, crow_ref, ccol_ref, x1_ref, mod_ref, wg_ref, wu_ref, wd_ref, fn_ref,
                o_ref, xs_s, gs_s, ys_s, acc_s, *, final, n_tiles):
    tm = h2_ref.shape[0]
    rb = MOE_ROW_BLOCK
    g = pl.program_id(2)
    e = pl.program_id(3)
    cnt = cnt_ref[(pl.program_id(0) * n_tiles + pl.program_id(1)) * MOE_GROUPS + g]
    n_blocks = (cnt + rb - 1) // rb
    rows_of = lambda blk: pl.ds(pl.multiple_of(blk * rb, rb), rb)

    def for_blocks(body):
        def step(blk, carry):
            body(blk, rows_of(blk))
            return carry
        lax.fori_loop(0, n_blocks, step, 0)

    @pl.when((g == 0) & (e == 0))
    def _():
        acc_s[...] = jnp.zeros_like(acc_s)

    @pl.when(e == 0)
    def _():
        rank = crow_ref[...] - g * MOE_CODE_STRIDE

        def gather(blk, rows):
            row_id = lax.broadcasted_iota(jnp.int32, (rb, tm), 0) + blk * rb
            onehot = jnp.where(row_id == rank, 1.0, 0.0).astype(BF16)
            xs_s[rows, :] = _dot(onehot, h2_ref[...]).astype(BF16)
            g_hi, g_mid, g_lo = _split3(gate_ref[...])
            gs_s[rows, :] = _dot(onehot, g_hi) + _dot(onehot, g_mid) + _dot(onehot, g_lo)
            ys_s[rows, :] = jnp.zeros((rb, ys_s.shape[1]), F32)
        for_blocks(gather)

    def expert(blk, rows):
        x = xs_s[rows, :]
        gs = gs_s[rows, :]
        lane = lax.broadcasted_iota(jnp.int32, gs.shape, 1)
        gate = jnp.sum(jnp.where(lane == g * MOE_PER_GROUP + e, gs, 0.0), axis=-1, keepdims=True)
        hid = _silu(_dot(x, wg_ref[...])) * _dot(x, wu_ref[...]) * gate
        ys_s[rows, :] += _dot(hid.astype(BF16), wd_ref[...])
    for_blocks(expert)

    @pl.when(e == MOE_PER_GROUP - 1)
    def _():
        rank = ccol_ref[...] - g * MOE_CODE_STRIDE

        def scatter(blk, rows):
            col_id = lax.broadcasted_iota(jnp.int32, (tm, rb), 1) + blk * rb
            onehot = jnp.where(col_id == rank, 1.0, 0.0).astype(BF16)
            acc_s[...] += _dot(onehot, ys_s[rows, :].astype(BF16))
        for_blocks(scatter)

    @pl.when((g == MOE_GROUPS - 1) & (e == MOE_PER_GROUP - 1))
    def _():
        xo = x1_ref[...] + mod_ref[5:6, :] * acc_s[...]
        o_ref[...] = _rms(xo, fn_ref[...]) if final else xo


def _moe(h2, gates, x1, mod, lw, final_norm, *, per_batch_mod, final):
    B, S, D = x1.shape
    tm = min(S, 1024)
    nt = S // tm
    grp = gates[..., GROUP_LANE].astype(jnp.int32).reshape(B, nt, tm)
    onehot = (grp[..., None] == jnp.arange(MOE_GROUPS, dtype=jnp.int32)).astype(jnp.int32)
    csum = jnp.cumsum(onehot, axis=2)
    rank = jnp.sum(onehot * (csum - 1), axis=-1)
    counts = csum[:, :, -1, :].reshape(-1)
    code = grp * MOE_CODE_STRIDE + rank
    code_row = code.reshape(B, nt, 1, tm)
    code_col = code.reshape(B, S, 1)

    tok = lambda w: pl.BlockSpec((None, tm, w), lambda b, s, g, e, c: (b, s, 0))
    wspec = lambda a: pl.BlockSpec((None,) + a.shape[1:], lambda b, s, g, e, c: (g * MOE_PER_GROUP + e, 0, 0))
    mod_map = (lambda b, s, g, e, c: (b, 0, 0)) if per_batch_mod else (lambda b, s, g, e, c: (0, 0, 0))
    grid_spec = pltpu.PrefetchScalarGridSpec(
        num_scalar_prefetch=1,
        grid=(B, nt, MOE_GROUPS, MOE_PER_GROUP),
        in_specs=[
            tok(D), tok(LANES),
            pl.BlockSpec((None, None, 1, tm), lambda b, s, g, e, c: (b, s, 0, 0)),
            tok(1), tok(D),
            pl.BlockSpec((None, 6, D), mod_map),
            wspec(lw["moe_wg"]), wspec(lw["moe_wu"]), wspec(lw["moe_wd"]),
            pl.BlockSpec(final_norm.shape, lambda b, s, g, e, c: (0, 0)),
        ],
        out_specs=tok(D),
        scratch_shapes=[
            pltpu.VMEM((tm, D), BF16),
            pltpu.VMEM((tm, LANES), F32),
            pltpu.VMEM((tm, D), F32),
            pltpu.VMEM((tm, D), F32),
        ],
    )
    return pl.pallas_call(
        functools.partial(_moe_kernel, final=final, n_tiles=nt),
        grid_spec=grid_spec,
        out_shape=jax.ShapeDtypeStruct((B, S, D), F32),
        compiler_params=_cparams(("arbitrary", "arbitrary", "arbitrary", "arbitrary")),
        name="moe",
    )(counts, h2, gates, code_row, code_col, x1, mod, lw["moe_wg"], lw["moe_wu"], lw["moe_wd"], final_norm)


def _swap_halves(w, n_heads, dim):
    k = w.shape[0]
    w = w.reshape(k, n_heads, 2, dim // 2)
    return w[:, :, ::-1, :].reshape(k, n_heads * dim)


def _pad_cols(w, width):
    return jnp.pad(w, ((0, 0), (0, width - w.shape[1])))


def _rope_tables(n_tokens, rot_dim, reps):
    t = jnp.arange(n_tokens)
    row = (t // GRID_W).astype(F32)
    col = (t % GRID_W).astype(F32)
    n_freq = rot_dim // 4
    inv_freq = ROPE_BASE ** (-jnp.arange(n_freq, dtype=F32) / n_freq)
    ang = jnp.concatenate([row[:, None] * inv_freq, col[:, None] * inv_freq], axis=-1)
    cos, sin = jnp.cos(ang), jnp.sin(ang)
    c = jnp.tile(jnp.concatenate([cos, cos], axis=-1), (1, reps))
    s = jnp.tile(jnp.concatenate([-sin, sin], axis=-1), (1, reps))
    return _pad_cols(c, LANES), _pad_cols(s, LANES)


def _layer_weights(l, P, q_abs, ab_re, ab_im, bb_re, bb_im):
    w_in = P["w_in"][l]
    seg = {}
    o = 0
    for name, width in (("u", SSM_CH), ("ql", MLA_Q_RANK), ("kvl", MLA_KV_RANK), ("kr", MLA_ROPE),
                        ("qs", SWA_HEADS * HEAD_DIM), ("ks", SWA_KV_HEADS * HEAD_DIM),
                        ("vs", SWA_KV_HEADS * HEAD_DIM)):
        seg[name] = w_in[:, o:o + width]
        o += width
    ctx_cols = [seg["u"], seg["ql"], seg["kvl"], seg["qs"], seg["ks"], seg["vs"], _pad_cols(seg["kr"], LANES)]
    lat_cols = ctx_cols + [
        _swap_halves(seg["qs"], SWA_HEADS, HEAD_DIM),
        _swap_halves(seg["ks"], SWA_KV_HEADS, HEAD_DIM),
        _pad_cols(_swap_halves(seg["kr"], 1, MLA_ROPE), LANES),
    ]
    w_qb = P["w_mla_qb"][l]
    w_rope = w_qb[:, :, MLA_NOPE:]
    w_rope_sw = w_rope.reshape(MLA_Q_RANK, MLA_HEADS, 2, MLA_ROPE // 2)[:, :, ::-1, :].reshape(w_rope.shape)
    flat_t = lambda w: w.reshape(MLA_Q_RANK, MLA_HEADS * MLA_ROPE).T
    wq_abs = jnp.transpose(q_abs[l], (0, 2, 1)).reshape(Q_ABS, MLA_Q_RANK)
    wq_ctx = jnp.concatenate([wq_abs, flat_t(w_rope)], axis=0)
    wq_lat = jnp.concatenate([wq_ctx, flat_t(w_rope_sw)], axis=0)

    eye = jnp.eye(SSM_GROUPS, dtype=F32)

    def block_diag_b(bb):
        bb = bb.reshape(2, SSM_GROUPS, SSM_STATE, SSM_GROUP)
        return jnp.einsum("dgpc,gh->dgchp", bb, eye).reshape(2, SSM_CH, SSM_N)

    def block_diag_c(cc):
        return jnp.einsum("dgcp,gh->dgphc", cc, eye).reshape(2, SSM_N, SSM_CH)

    sl = slice(2 * l, 2 * l + 2)
    w_router = jnp.concatenate([P["moe_w_expert"][l], P["moe_w_group"][l]], axis=1)
    b_router = jnp.concatenate([P["moe_b_expert"][l], P["moe_b_group"][l]])
    row = lambda v: v.reshape(1, -1)
    return dict(
        norm1=row(P["norm1"][l]), norm2=row(P["norm2"][l]),
        win_ctx=jnp.concatenate(ctx_cols, axis=1).astype(BF16),
        win_lat=jnp.concatenate(lat_cols, axis=1).astype(BF16),
        q_norm=row(P["mla_q_norm"][l]), kv_norm=row(P["mla_kv_norm"][l]),
        wq_ctx=wq_ctx.astype(BF16), wq_lat=wq_lat.astype(BF16),
        wv=jnp.transpose(P["w_mla_kvb"][l][:, :, MLA_NOPE:], (1, 0, 2)).astype(BF16),
        ssm_wb=jnp.concatenate([block_diag_b(bb_re[sl]), block_diag_b(bb_im[sl])], axis=2).astype(BF16),
        ssm_a=jnp.concatenate([ab_re[sl], ab_im[sl]], axis=1).reshape(2, 1, 2 * SSM_N),
        ssm_wc=jnp.concatenate([block_diag_c(P["ssm_c_re"][l]), -block_diag_c(P["ssm_c_im"][l])],
                               axis=1).astype(BF16),
        ssm_d=row(P["ssm_d"][l]), w_glu=P["w_ssm_glu"][l].astype(BF16),
        gn_ssm=row(P["gn_ssm"][l]), gn_mla=row(P["gn_mla"][l]), gn_swa=row(P["gn_swa"][l]),
        w_out=P["w_out"][l].astype(BF16),
        w_router=_pad_cols(w_router, LANES), b_router=_pad_cols(row(b_router), LANES),
        moe_wg=P["moe_w_gate"][l].reshape(MOE_EXPERTS, D_MODEL, MOE_HIDDEN).astype(BF16),
        moe_wu=P["moe_w_up"][l].reshape(MOE_EXPERTS, D_MODEL, MOE_HIDDEN).astype(BF16),
        moe_wd=P["moe_w_down"][l].reshape(MOE_EXPERTS, MOE_HIDDEN, D_MODEL).astype(BF16),
        sink=P["swa_sink"][l],
    )


def _layer(x, mod, lw, final_norm, *, tables, ctx, per_batch_mod, final):
    B, S, _ = x.shape
    context_pass = ctx is None
    pre = _pre(x, mod, lw, tables, per_batch_mod=per_batch_mod, emit_ctx=context_pass)
    u_tm, qt, kcat, ckvt, qs, ks, vs = pre[:7]
    nb = B // SUBLANES
    if context_pass:
        h0 = jnp.zeros((2, nb, SUBLANES, 2 * SSM_N), F32)
        y2, hfin = _ssm(u_tm, lw, h0, B, S)
        o_mla = _mla(qt, kcat, ckvt, None, None, lw["wv"])
        o_swa = _swa(lw["sink"], qs, ks, vs, None, None)
        state = hfin.reshape(2, B, 2, SSM_GROUPS, SSM_STATE).transpose(1, 0, 2, 3, 4)
        new_ctx = (pre[7], pre[8], pre[9].reshape(B, S, SWA_KV_HEADS, HEAD_DIM),
                   pre[10].reshape(B, S, SWA_KV_HEADS, HEAD_DIM), state)
    else:
        kcat_c, ckvt_c, ks_c, vs_c, h0 = ctx
        y2, _ = _ssm(u_tm, lw, h0, B, S)
        o_mla = _mla(qt, kcat_c, ckvt_c, kcat, ckvt, lw["wv"])
        o_swa = _swa(lw["sink"], qs, ks_c, vs_c, ks, vs)
        new_ctx = None
    x1, h2, gates = _post(x, u_tm, y2, o_mla, o_swa, mod, lw, per_batch_mod=per_batch_mod)
    if not per_batch_mod:
        shp = lambda a: a.reshape(1, B * S, a.shape[-1])
        xo = _moe(shp(h2), shp(gates), shp(x1), mod, lw, final_norm, per_batch_mod=False, final=final)
        xo = xo.reshape(B, S, D_MODEL)
    else:
        xo = _moe(h2, gates, x1, mod, lw, final_norm, per_batch_mod=True, final=final)
    return xo, new_ctx


def kernel(x_prompt, x_sample, c, cache_mla_ckv, cache_mla_krope, cache_swa_k, cache_swa_v, state_ssm, c_ctx, w_ada, b_ada, norm1, norm2, w_in, ssm_a_re, ssm_a_im, ssm_log_dt, ssm_b_re, ssm_b_im, ssm_c_re, ssm_c_im, ssm_d, w_ssm_glu, mla_q_norm, w_mla_qb, mla_kv_norm, w_mla_kvb, swa_sink, gn_ssm, gn_mla, gn_swa, w_out, moe_w_group, moe_b_group, moe_w_expert, moe_b_expert, moe_w_gate, moe_w_up, moe_w_down, final_norm):
    P = dict(w_ada=w_ada, b_ada=b_ada, norm1=norm1, norm2=norm2, w_in=w_in,
             ssm_c_re=ssm_c_re, ssm_c_im=ssm_c_im, ssm_d=ssm_d, w_ssm_glu=w_ssm_glu,
             mla_q_norm=mla_q_norm, w_mla_qb=w_mla_qb, mla_kv_norm=mla_kv_norm, w_mla_kvb=w_mla_kvb,
             swa_sink=swa_sink, gn_ssm=gn_ssm, gn_mla=gn_mla, gn_swa=gn_swa, w_out=w_out,
             moe_w_group=moe_w_group, moe_b_group=moe_b_group, moe_w_expert=moe_w_expert,
             moe_b_expert=moe_b_expert, moe_w_gate=moe_w_gate, moe_w_up=moe_w_up, moe_w_down=moe_w_down)
    n_dec = c.shape[0]
    n_cond = 2 * SUBLANES
    conds = jnp.zeros((n_cond, D_MODEL), F32).at[:n_dec].set(c).at[n_dec].set(c_ctx)
    mods = _modulation(conds, w_ada, b_ada).reshape(DEPTH, n_cond, 6, D_MODEL)

    ab_re, ab_im, bb_re, bb_im = _ssm_discretise(ssm_a_re, ssm_a_im, ssm_log_dt, ssm_b_re, ssm_b_im)
    q_abs = _absorb_q(jnp.transpose(w_mla_qb[..., :MLA_NOPE], (0, 2, 1, 3)),
                      jnp.transpose(w_mla_kvb[..., :MLA_NOPE], (0, 2, 1, 3)))
    lws = [_layer_weights(l, P, q_abs, ab_re.reshape(2 * DEPTH, SSM_N), ab_im.reshape(2 * DEPTH, SSM_N),
                          bb_re, bb_im) for l in range(DEPTH)]
    fnorm = final_norm.reshape(1, D_MODEL)

    xp = x_prompt
    ctx_states = []
    for l in range(DEPTH):
        xp, new = _layer(xp, mods[l, n_dec:n_dec + 1], lws[l], fnorm, tables=None, ctx=None,
                         per_batch_mod=False, final=l == DEPTH - 1)
        ctx_states.append(new)
    outs_ctx = tuple(jnp.stack([s[k] for s in ctx_states], axis=1) for k in range(5))

    n_lat = x_sample.shape[1]
    cm, sm = _rope_tables(n_lat, MLA_ROPE, 1)
    cs, ss = _rope_tables(n_lat, HEAD_DIM, LANES // HEAD_DIM)
    tables = (cm, sm, cm[:, :MLA_ROPE].T, sm[:, :MLA_ROPE].T, cs, ss)
    xs = x_sample
    past = cache_mla_ckv.shape[2]
    for l in range(DEPTH):
        kcat_c = jnp.concatenate(
            [cache_mla_ckv[:, l], cache_mla_krope[:, l],
             jnp.zeros((n_dec, past, LANES - MLA_ROPE), F32)], axis=-1).astype(BF16)
        ks_c = cache_swa_k[:, l].reshape(n_dec, past, LANES).astype(BF16)
        vs_c = cache_swa_v[:, l].reshape(n_dec, past, LANES).astype(BF16)
        h0 = state_ssm[:, l].transpose(1, 0, 2, 3, 4).reshape(2, n_dec // SUBLANES, SUBLANES, 2 * SSM_N)
        ckvt_c = jnp.transpose(cache_mla_ckv[:, l], (0, 2, 1)).astype(BF16)
        xs, _ = _layer(xs, mods[l, :n_dec], lws[l], fnorm, tables=tables, ctx=(kcat_c, ckvt_c, ks_c, vs_c, h0),
                       per_batch_mod=True, final=l == DEPTH - 1)
    return (xp, xs) + outs_ctx
```

```python
import functools
import math

import jax
import jax.numpy as jnp
from jax import lax
from jax.experimental import pallas as pl
from jax.experimental.pallas import tpu as pltpu

F32 = jnp.float32
BF16 = jnp.bfloat16

D_MODEL = 1024
DEPTH = 4
GRID_W = 64
HEAD_DIM = 64
SSM_CH = 256
SSM_GROUP = 16
SSM_GROUPS = SSM_CH // SSM_GROUP
SSM_STATE = 64
SSM_N = SSM_GROUPS * SSM_STATE
MLA_HEADS = 6
MLA_Q_RANK = 256
MLA_KV_RANK = 128
MLA_NOPE = 64
MLA_ROPE = 32
MLA_V = 64
SWA_HEADS = 6
SWA_KV_HEADS = 2
SWA_GROUP = SWA_HEADS // SWA_KV_HEADS
SWA_WINDOW = 128
SWA_BLOCK = 128
MOE_GROUPS = 4
MOE_PER_GROUP = 8
MOE_EXPERTS = MOE_GROUPS * MOE_PER_GROUP
MOE_HIDDEN = 256
GROUP_LANE = MOE_EXPERTS
RANK_LANE = GROUP_LANE + 1
COUNT_LANE = GROUP_LANE + 2
POST_TILE = 512
MOE_SPLIT = 2
MOE_ROW_BLOCK = 256
MOE_CODE_STRIDE = 1 << 16
ROPE_BASE = 10000.0
EPS = 1e-6
MLA_SCALE = 1.0 / math.sqrt(MLA_NOPE + MLA_ROPE)
MLA_SCALE_LOG2 = MLA_SCALE * math.log2(math.e)
SWA_SCALE = 1.0 / math.sqrt(HEAD_DIM)

LANES = 128
SUBLANES = 8
VMEM_LIMIT = 52 * 1024 * 1024

OFF_U = 0
OFF_QL = OFF_U + SSM_CH
OFF_KVL = OFF_QL + MLA_Q_RANK
OFF_QS = OFF_KVL + MLA_KV_RANK
OFF_KS = OFF_QS + SWA_HEADS * HEAD_DIM
OFF_VS = OFF_KS + SWA_KV_HEADS * HEAD_DIM
OFF_KR = OFF_VS + SWA_KV_HEADS * HEAD_DIM
N_PACK_CTX = OFF_KR + LANES
OFF_QS_SW = N_PACK_CTX
OFF_KS_SW = OFF_QS_SW + SWA_HEADS * HEAD_DIM
OFF_KR_SW = OFF_KS_SW + SWA_KV_HEADS * HEAD_DIM
N_PACK_LAT = OFF_KR_SW + LANES
Q_ABS = MLA_HEADS * MLA_KV_RANK
MLA_QK = 2 * LANES


def _cparams(sem):
    return pltpu.CompilerParams(dimension_semantics=sem, vmem_limit_bytes=VMEM_LIMIT)


def _dot(a, b):
    return jnp.dot(a, b, preferred_element_type=F32)


def _dot_nt(a, b):
    return lax.dot_general(a, b, (((1,), (1,)), ((), ())), preferred_element_type=F32)


def _split(x):
    hi = x.astype(BF16)
    lo = (x - hi.astype(F32)).astype(BF16)
    return hi, lo


def _dot3(a, b):
    ah, al = _split(a)
    bh, bl = _split(b)
    return _dot(ah, bh) + _dot(al, bh) + _dot(ah, bl)


def _tree(op, x3):
    parts = [x3[i] for i in range(x3.shape[0])]
    while len(parts) > 1:
        parts = [op(parts[i], parts[i + 1]) for i in range(0, len(parts), 2)]
    return parts[0]


def _rms(x, g):
    return x * lax.rsqrt(jnp.mean(x * x, axis=-1, keepdims=True) + EPS) * g


def _silu(x):
    return x * jax.nn.sigmoid(x)


def _gelu_tanh(x):
    return 0.5 * x * (1.0 + jnp.tanh(math.sqrt(2.0 / math.pi) * (x + 0.044715 * (x * x * x))))


def _mod_kernel(c_ref, w_ref, b_ref, o_ref):
    o_ref[...] = _dot3(_silu(c_ref[...]), w_ref[...]) + b_ref[...]


def _modulation(conds, w_ada, b_ada):
    n = conds.shape[0]
    tn = 1536
    return pl.pallas_call(
        _mod_kernel,
        grid=(DEPTH, 6 * D_MODEL // tn),
        in_specs=[
            pl.BlockSpec((n, D_MODEL), lambda l, j: (0, 0)),
            pl.BlockSpec((None, D_MODEL, tn), lambda l, j: (l, 0, j)),
            pl.BlockSpec((None, 1, tn), lambda l, j: (l, 0, j)),
        ],
        out_specs=pl.BlockSpec((None, n, tn), lambda l, j: (l, 0, j)),
        out_shape=jax.ShapeDtypeStruct((DEPTH, n, 6 * D_MODEL), F32),
        compiler_params=_cparams(("arbitrary", "arbitrary")),
        name="modulation",
    )(conds, w_ada, b_ada.reshape(DEPTH, 1, 6 * D_MODEL))


def _ssm_disc_kernel(are_ref, aim_ref, ldt_ref, bre_ref, bim_ref, abre_ref, abim_ref, bbre_ref, bbim_ref):
    lam_re = are_ref[...]
    lam_im = aim_ref[...]
    dt = jnp.exp(ldt_ref[...])
    z_re = lam_re * dt
    z_im = lam_im * dt
    mag = jnp.exp(z_re)
    ab_re = mag * jnp.cos(z_im)
    ab_im = mag * jnp.sin(z_im)
    den = lam_re * lam_re + lam_im * lam_im
    f_re = ((ab_re - 1.0) * lam_re + ab_im * lam_im) / den
    f_im = (ab_im * lam_re - (ab_re - 1.0) * lam_im) / den
    b_re = bre_ref[...]
    b_im = bim_ref[...]
    abre_ref[...] = ab_re
    abim_ref[...] = ab_im
    bbre_ref[...] = f_re * b_re - f_im * b_im
    bbim_ref[...] = f_re * b_im + f_im * b_re


def _ssm_discretise(a_re, a_im, log_dt, b_re, b_im):
    n = DEPTH * 2
    col = lambda v: v.reshape(n, SSM_N, 1)
    ldt = jnp.broadcast_to(log_dt[..., None], (DEPTH, 2, SSM_GROUPS, SSM_STATE))
    cspec = pl.BlockSpec((None, SSM_N, 1), lambda i: (i, 0, 0))
    bspec = pl.BlockSpec((None, SSM_N, SSM_GROUP), lambda i: (i, 0, 0))
    return pl.pallas_call(
        _ssm_disc_kernel,
        grid=(n,),
        in_specs=[cspec, cspec, cspec, bspec, bspec],
        out_specs=[cspec, cspec, bspec, bspec],
        out_shape=[jax.ShapeDtypeStruct((n, SSM_N, 1), F32)] * 2
        + [jax.ShapeDtypeStruct((n, SSM_N, SSM_GROUP), F32)] * 2,
        compiler_params=_cparams(("arbitrary",)),
        name="ssm_discretise",
    )(col(a_re), col(a_im), col(ldt), b_re.reshape(n, SSM_N, SSM_GROUP), b_im.reshape(n, SSM_N, SSM_GROUP))


def _absorb_kernel(wq_ref, wk_ref, o_ref):
    a = wq_ref[...]
    b = wk_ref[...]
    ah, al = _split(a)
    bh, bl = _split(b)
    o_ref[...] = _dot_nt(ah, bh) + _dot_nt(al, bh) + _dot_nt(ah, bl)


def _absorb_q(wq_nope, wk_nope):
    return pl.pallas_call(
        _absorb_kernel,
        grid=(DEPTH, MLA_HEADS),
        in_specs=[
            pl.BlockSpec((None, None, MLA_Q_RANK, MLA_NOPE), lambda l, h: (l, h, 0, 0)),
            pl.BlockSpec((None, None, MLA_KV_RANK, MLA_NOPE), lambda l, h: (l, h, 0, 0)),
        ],
        out_specs=pl.BlockSpec((None, None, MLA_Q_RANK, MLA_KV_RANK), lambda l, h: (l, h, 0, 0)),
        out_shape=jax.ShapeDtypeStruct((DEPTH, MLA_HEADS, MLA_Q_RANK, MLA_KV_RANK), F32),
        compiler_params=_cparams(("arbitrary", "arbitrary")),
        name="mla_absorb",
    )(wq_nope, wk_nope)


def _pre_kernel(*refs, rope, emit_ctx):
    it = iter(refs)
    x_ref, mod_ref, n1_ref, win_ref, qn_ref, kvn_ref, wq_ref = (next(it) for _ in range(7))
    if rope:
        cm_ref, sm_ref, cmt_ref, smt_ref, cs_ref, ss_ref = (next(it) for _ in range(6))
    u_ref, qt_ref, kcat_ref, ckvt_ref, qs_ref, ks_ref, vs_ref = (next(it) for _ in range(7))
    if emit_ctx:
        ckv_o, kr_o, ks_o, vs_o = (next(it) for _ in range(4))

    x = x_ref[...]
    mod = mod_ref[...]
    h = _rms(x, n1_ref[...]) * (1.0 + mod[1:2]) + mod[0:1]
    proj = _dot(h.astype(BF16), win_ref[...])

    u_ref[...] = proj[:, OFF_U:OFF_U + SSM_CH]

    qln = _rms(proj[:, OFF_QL:OFF_QL + MLA_Q_RANK], qn_ref[...]).astype(BF16)
    qall = _dot_nt(wq_ref[...], qln)
    n_rope = MLA_HEADS * MLA_ROPE
    zero_rows = jnp.zeros((MLA_QK - MLA_KV_RANK - MLA_ROPE, qall.shape[1]), BF16)
    for i in range(MLA_HEADS):
        qa = qall[i * MLA_KV_RANK:(i + 1) * MLA_KV_RANK]
        qr = qall[Q_ABS + i * MLA_ROPE:Q_ABS + (i + 1) * MLA_ROPE]
        if rope:
            qr_sw = qall[Q_ABS + n_rope + i * MLA_ROPE:Q_ABS + n_rope + (i + 1) * MLA_ROPE]
            qr = qr * cmt_ref[...] + qr_sw * smt_ref[...]
        base = i * MLA_QK
        qt_ref[base:base + MLA_KV_RANK, :] = (qa * MLA_SCALE_LOG2).astype(BF16)
        qt_ref[base + MLA_KV_RANK:base + MLA_KV_RANK + MLA_ROPE, :] = (qr * MLA_SCALE_LOG2).astype(BF16)
        qt_ref[base + MLA_KV_RANK + MLA_ROPE:base + MLA_QK, :] = zero_rows

    ckv = _rms(proj[:, OFF_KVL:OFF_KVL + MLA_KV_RANK], kvn_ref[...])
    kr = proj[:, OFF_KR:OFF_KR + LANES]
    if emit_ctx:
        ckv_o[...] = ckv
        kr_o[...] = kr[:, :MLA_ROPE]
    if rope:
        kr = kr * cm_ref[...] + proj[:, OFF_KR_SW:OFF_KR_SW + LANES] * sm_ref[...]
    kcat_ref[...] = jnp.concatenate([ckv, kr], axis=-1).astype(BF16)
    ckvt_ref[...] = ckv.T.astype(BF16)

    qs = proj[:, OFF_QS:OFF_QS + SWA_HEADS * HEAD_DIM]
    ks = proj[:, OFF_KS:OFF_KS + SWA_KV_HEADS * HEAD_DIM]
    vs = proj[:, OFF_VS:OFF_VS + SWA_KV_HEADS * HEAD_DIM]
    if emit_ctx:
        ks_o[...] = ks
        vs_o[...] = vs
    if rope:
        cs = cs_ref[...]
        ss = ss_ref[...]
        qs_sw = proj[:, OFF_QS_SW:OFF_QS_SW + SWA_HEADS * HEAD_DIM]
        ks = ks * cs + proj[:, OFF_KS_SW:OFF_KS_SW + SWA_KV_HEADS * HEAD_DIM] * ss
        qs = jnp.concatenate(
            [qs[:, i * LANES:(i + 1) * LANES] * cs + qs_sw[:, i * LANES:(i + 1) * LANES] * ss
             for i in range(SWA_HEADS * HEAD_DIM // LANES)], axis=-1)
    qs_ref[...] = (qs * SWA_SCALE).astype(BF16)
    ks_ref[...] = ks.astype(BF16)
    vs_ref[...] = vs.astype(BF16)


def _pre(x, mod, lw, tables, *, per_batch_mod, emit_ctx):
    B, S, D = x.shape
    rope = tables is not None
    ts = min(S, 512)
    n_pack = N_PACK_LAT if rope else N_PACK_CTX
    win = lw["win_lat"] if rope else lw["win_ctx"]
    wq = lw["wq_lat"] if rope else lw["wq_ctx"]
    tok = lambda w: pl.BlockSpec((None, ts, w), lambda b, s: (b, s, 0))
    full = lambda a: pl.BlockSpec(a.shape, lambda b, s: (0,) * a.ndim)
    in_specs = [
        tok(D),
        pl.BlockSpec((None, 6, D), (lambda b, s: (b, 0, 0)) if per_batch_mod else (lambda b, s: (0, 0, 0))),
        full(lw["norm1"]), full(win), full(lw["q_norm"]), full(lw["kv_norm"]), full(wq),
    ]
    args = [x, mod, lw["norm1"], win, lw["q_norm"], lw["kv_norm"], wq]
    if rope:
        row_tab = pl.BlockSpec((ts, LANES), lambda b, s: (s, 0))
        col_tab = pl.BlockSpec((MLA_ROPE, ts), lambda b, s: (0, s))
        in_specs += [row_tab, row_tab, col_tab, col_tab, row_tab, row_tab]
        args += list(tables)
    feat = lambda w: pl.BlockSpec((None, w, ts), lambda b, s: (b, 0, s))
    out_specs = [
        pl.BlockSpec((ts, SSM_CH), lambda b, s: (s, b)),
        feat(MLA_HEADS * MLA_QK), tok(MLA_QK), feat(MLA_KV_RANK), tok(SWA_HEADS * HEAD_DIM), tok(LANES), tok(LANES),
    ]
    out_shape = [
        jax.ShapeDtypeStruct((S, B * SSM_CH), F32),
        jax.ShapeDtypeStruct((B, MLA_HEADS * MLA_QK, S), BF16),
        jax.ShapeDtypeStruct((B, S, MLA_QK), BF16),
        jax.ShapeDtypeStruct((B, MLA_KV_RANK, S), BF16),
        jax.ShapeDtypeStruct((B, S, SWA_HEADS * HEAD_DIM), BF16),
        jax.ShapeDtypeStruct((B, S, LANES), BF16),
        jax.ShapeDtypeStruct((B, S, LANES), BF16),
    ]
    if emit_ctx:
        out_specs += [tok(MLA_KV_RANK), tok(MLA_ROPE), tok(LANES), tok(LANES)]
        out_shape += [
            jax.ShapeDtypeStruct((B, S, MLA_KV_RANK), F32),
            jax.ShapeDtypeStruct((B, S, MLA_ROPE), F32),
            jax.ShapeDtypeStruct((B, S, LANES), F32),
            jax.ShapeDtypeStruct((B, S, LANES), F32),
        ]
    return pl.pallas_call(
        functools.partial(_pre_kernel, rope=rope, emit_ctx=emit_ctx),
        grid=(B, S // ts),
        in_specs=in_specs,
        out_specs=out_specs,
        out_shape=out_shape,
        compiler_params=_cparams(("arbitrary", "arbitrary")),
        name="pre_lat" if rope else "pre_ctx",
    )(*args)


def _ssm_kernel(u_ref, wb_ref, a_ref, wc_ref, h0_ref, y_ref, hfin_ref, hre_s, him_s, st_s, *, tc):
    d = pl.program_id(0)
    i = pl.program_id(2)
    n = pl.num_programs(2)

    @pl.when(i == 0)
    def _():
        st_s[...] = h0_ref[...]

    u = u_ref[...].reshape(tc * SUBLANES, SSM_CH).astype(BF16)
    bu = _dot(u, wb_ref[...])
    hre_s[...] = bu[:, :SSM_N]
    him_s[...] = bu[:, SSM_N:]

    a = a_ref[...]
    a_re = jnp.broadcast_to(a[:, :SSM_N], (SUBLANES, SSM_N))
    a_im = jnp.broadcast_to(a[:, SSM_N:], (SUBLANES, SSM_N))
    st = st_s[...]

    def step(t, carry):
        h_re, h_im = carry
        tt = jnp.where(d == 0, t, tc - 1 - t)
        rows = pl.ds(pl.multiple_of(tt * SUBLANES, SUBLANES), SUBLANES)
        n_re = a_re * h_re - a_im * h_im + hre_s[rows, :]
        n_im = a_re * h_im + a_im * h_re + him_s[rows, :]
        hre_s[rows, :] = n_re
        him_s[rows, :] = n_im
        return n_re, n_im

    h_re, h_im = lax.fori_loop(0, tc, step, (st[:, :SSM_N], st[:, SSM_N:]), unroll=4)
    st_s[...] = jnp.concatenate([h_re, h_im], axis=-1)

    wc = wc_ref[...]
    y = _dot(hre_s[...].astype(BF16), wc[:SSM_N]) + _dot(him_s[...].astype(BF16), wc[SSM_N:])
    y_ref[...] = y.reshape(tc, SUBLANES, SSM_CH)

    @pl.when(i == n - 1)
    def _():
        hfin_ref[...] = st_s[...]


def _ssm(u_tm, lw, h0, B, S):
    nb = B // SUBLANES
    tc = 128
    nchunk = S // tc
    chunk = lambda d, i: i + d * (nchunk - 1 - 2 * i)
    u4 = u_tm.reshape(S, nb, SUBLANES, SSM_CH)
    y, hfin = pl.pallas_call(
        functools.partial(_ssm_kernel, tc=tc),
        grid=(2, nb, nchunk),
        in_specs=[
            pl.BlockSpec((tc, None, SUBLANES, SSM_CH), lambda d, b, i: (chunk(d, i), b, 0, 0)),
            pl.BlockSpec((None, SSM_CH, 2 * SSM_N), lambda d, b, i: (d, 0, 0)),
            pl.BlockSpec((None, 1, 2 * SSM_N), lambda d, b, i: (d, 0, 0)),
            pl.BlockSpec((None, 2 * SSM_N, SSM_CH), lambda d, b, i: (d, 0, 0)),
            pl.BlockSpec((None, None, SUBLANES, 2 * SSM_N), lambda d, b, i: (d, b, 0, 0)),
        ],
        out_specs=[
            pl.BlockSpec((None, tc, None, SUBLANES, SSM_CH), lambda d, b, i: (d, chunk(d, i), b, 0, 0)),
            pl.BlockSpec((None, None, SUBLANES, 2 * SSM_N), lambda d, b, i: (d, b, 0, 0)),
        ],
        out_shape=[
            jax.ShapeDtypeStruct((2, S, nb, SUBLANES, SSM_CH), F32),
            jax.ShapeDtypeStruct((2, nb, SUBLANES, 2 * SSM_N), F32),
        ],
        scratch_shapes=[
            pltpu.VMEM((tc * SUBLANES, SSM_N), F32),
            pltpu.VMEM((tc * SUBLANES, SSM_N), F32),
            pltpu.VMEM((SUBLANES, 2 * SSM_N), F32),
        ],
        compiler_params=_cparams(("arbitrary", "arbitrary", "arbitrary")),
        name="ssm_scan",
    )(u4, lw["ssm_wb"], lw["ssm_a"], lw["ssm_wc"], h0)
    return y.reshape(2, S, B * SSM_CH), hfin


def _mla_kernel(*refs, with_latent, tk):
    if with_latent:
        qt_ref, ka_ref, vat_ref, kb_ref, vbt_ref, wv_ref, o_ref = refs
    else:
        qt_ref, ka_ref, vat_ref, wv_ref, o_ref = refs
    tq = qt_ref.shape[1]
    q_of = lambda hd: qt_ref[hd * MLA_QK:(hd + 1) * MLA_QK, :]

    def tile(state, k, vt):
        new_state = []
        s_next = _dot(k, q_of(0))
        for hd in range(MLA_HEADS):
            m8, l8, acc = state[hd]
            s = s_next
            if hd + 1 < MLA_HEADS:
                s_next = _dot(k, q_of(hd + 1))
            s3 = s.reshape(s.shape[0] // SUBLANES, SUBLANES, tq)
            mloc = jnp.max(_tree(jnp.maximum, s3), axis=0, keepdims=True)
            m8_new = jnp.maximum(m8, jnp.broadcast_to(mloc, (SUBLANES, tq)))
            alpha8 = jnp.exp2(m8 - m8_new)
            p3 = jnp.exp2(s3 - m8_new[None])
            l8 = alpha8 * l8 + _tree(jnp.add, p3)
            pv = _dot(vt, p3.reshape(s.shape).astype(BF16))
            acc3 = acc.reshape(MLA_KV_RANK // SUBLANES, SUBLANES, tq) * alpha8[None]
            new_state.append((m8_new, l8, acc3.reshape(MLA_KV_RANK, tq) + pv))
        return tuple(new_state)

    init = tuple((jnp.full((SUBLANES, tq), -jnp.inf, F32), jnp.zeros((SUBLANES, tq), F32),
                  jnp.zeros((MLA_KV_RANK, tq), F32)) for _ in range(MLA_HEADS))
    state = tile(init, ka_ref[...], vat_ref[...])
    if with_latent:
        def body(j, st):
            off = pl.multiple_of(j * tk, tk)
            return tile(st, kb_ref[pl.ds(off, tk), :], vbt_ref[:, pl.ds(off, tk)])
        state = lax.fori_loop(0, kb_ref.shape[0] // tk, body, state)

    outs = []
    for hd in range(MLA_HEADS):
        _, l8, acc = state[hd]
        o_lat = (acc / jnp.sum(l8, axis=0, keepdims=True)).T.astype(BF16)
        outs.append(_dot(o_lat, wv_ref[hd]))
    o_ref[...] = jnp.concatenate(outs, axis=-1)


def _mla(qt, ka, vat, kb, vbt, wv):
    B, _, S = qt.shape
    tq = 256
    with_latent = kb is not None
    kspec = lambda a: pl.BlockSpec((None,) + a.shape[1:], lambda b, s: (b, 0, 0))
    in_specs = [pl.BlockSpec((None, MLA_HEADS * MLA_QK, tq), lambda b, s: (b, 0, s)), kspec(ka), kspec(vat)]
    args = [qt, ka, vat]
    if with_latent:
        in_specs += [kspec(kb), kspec(vbt)]
        args += [kb, vbt]
    in_specs.append(pl.BlockSpec(wv.shape, lambda b, s: (0, 0, 0)))
    args.append(wv)
    return pl.pallas_call(
        functools.partial(_mla_kernel, with_latent=with_latent, tk=1024),
        grid=(B, S // tq),
        in_specs=in_specs,
        out_specs=pl.BlockSpec((None, tq, MLA_HEADS * MLA_V), lambda b, s: (b, s, 0)),
        out_shape=jax.ShapeDtypeStruct((B, S, MLA_HEADS * MLA_V), F32),
        compiler_params=_cparams(("arbitrary", "arbitrary")),
        name="mla_lat" if with_latent else "mla_ctx",
    )(*args)


def _swa_kernel(*refs, with_latent):
    if with_latent:
        sink_ref, q_ref, ka_ref, va_ref, kb_ref, vb_ref, o_ref = refs
    else:
        sink_ref, q_ref, ka_ref, va_ref, o_ref = refs
    tq = q_ref.shape[0]
    q = q_ref[...]
    ka = ka_ref[...]
    va = va_ref[...]
    if with_latent:
        n = pl.program_id(1)
        nblk = kb_ref.shape[0] // SWA_BLOCK
        start = pl.multiple_of(jnp.clip(n - 1, 0, nblk - 3) * SWA_BLOCK, SWA_BLOCK)
        kb = kb_ref[pl.ds(start, 3 * SWA_BLOCK), :]
        vb = vb_ref[pl.ds(start, 3 * SWA_BLOCK), :]
        qpos = n * SWA_BLOCK + lax.broadcasted_iota(jnp.int32, (tq, 3 * SWA_BLOCK), 0)
        kpos = start + lax.broadcasted_iota(jnp.int32, (tq, 3 * SWA_BLOCK), 1)
        valid = jnp.abs(qpos - kpos) <= SWA_WINDOW
    outs = []
    for hd in range(SWA_HEADS):
        kh = hd // SWA_GROUP
        qh = q[:, hd * HEAD_DIM:(hd + 1) * HEAD_DIM]
        ksl = slice(kh * HEAD_DIM, (kh + 1) * HEAD_DIM)
        sink = sink_ref[hd]
        s_a = _dot_nt(qh, ka[:, ksl])
        m = jnp.maximum(jnp.max(s_a, axis=-1, keepdims=True), sink)
        if with_latent:
            s_b = jnp.where(valid, _dot_nt(qh, kb[:, ksl]), -jnp.inf)
            m = jnp.maximum(m, jnp.max(s_b, axis=-1, keepdims=True))
        p_a = jnp.exp(s_a - m)
        den = jnp.sum(p_a, axis=-1, keepdims=True) + jnp.exp(sink - m)
        o = _dot(p_a.astype(BF16), va[:, ksl])
        if with_latent:
            p_b = jnp.exp(s_b - m)
            den = den + jnp.sum(p_b, axis=-1, keepdims=True)
            o = o + _dot(p_b.astype(BF16), vb[:, ksl])
        outs.append(o / den)
    o_ref[...] = jnp.concatenate(outs, axis=-1)


def _swa(sink, q, ka, va, kb, vb):
    B, S, W = q.shape
    with_latent = kb is not None
    tq = SWA_BLOCK if with_latent else S
    tok = pl.BlockSpec((None, tq, W), lambda b, s: (b, s, 0))
    kspec = lambda a: pl.BlockSpec((None,) + a.shape[1:], lambda b, s: (b, 0, 0))
    in_specs = [pl.BlockSpec(memory_space=pltpu.SMEM), tok, kspec(ka), kspec(va)]
    args = [sink, q, ka, va]
    if with_latent:
        in_specs += [kspec(kb), kspec(vb)]
        args += [kb, vb]
    return pl.pallas_call(
        functools.partial(_swa_kernel, with_latent=with_latent),
        grid=(B, S // tq),
        in_specs=in_specs,
        out_specs=tok,
        out_shape=jax.ShapeDtypeStruct((B, S, W), F32),
        compiler_params=_cparams(("arbitrary", "arbitrary")),
        name="swa_lat" if with_latent else "swa_ctx",
    )(*args)


def _route(logits):
    lane = lax.broadcasted_iota(jnp.int32, logits.shape, 1)
    big = jnp.int32(1 << 20)
    is_g = (lane >= MOE_EXPERTS) & (lane < MOE_EXPERTS + MOE_GROUPS)
    lg = jnp.where(is_g, logits, -jnp.inf)
    mg = jnp.max(lg, axis=-1, keepdims=True)
    g_idx = jnp.min(jnp.where(lg == mg, lane - MOE_EXPERTS, big), axis=-1, keepdims=True)
    pg_top = 1.0 / jnp.sum(jnp.exp(lg - mg), axis=-1, keepdims=True)

    is_e = (lane < MOE_EXPERTS) & ((lane // MOE_PER_GROUP) == g_idx)
    le = jnp.where(is_e, logits, -jnp.inf)
    m1 = jnp.max(le, axis=-1, keepdims=True)
    e1 = jnp.min(jnp.where(le == m1, lane, big), axis=-1, keepdims=True)
    z = jnp.sum(jnp.exp(le - m1), axis=-1, keepdims=True)
    le2 = jnp.where(lane == e1, -jnp.inf, le)
    m2 = jnp.max(le2, axis=-1, keepdims=True)
    e2 = jnp.min(jnp.where(le2 == m2, lane, big), axis=-1, keepdims=True)
    p1 = 1.0 / z
    p2 = jnp.exp(m2 - m1) / z
    tot = p1 + p2
    gates = pg_top * (jnp.where(lane == e1, p1 / tot, 0.0) + jnp.where(lane == e2, p2 / tot, 0.0))
    t = logits.shape[0]
    chose = jnp.where(lane == g_idx + COUNT_LANE, 1.0, 0.0)
    tri = jnp.where(lax.broadcasted_iota(jnp.int32, (t, t), 0) >= lax.broadcasted_iota(jnp.int32, (t, t), 1),
                    1.0, 0.0).astype(BF16)
    counts = _dot(tri, chose.astype(BF16))
    rank = jnp.sum(chose * (counts - 1.0), axis=-1, keepdims=True)
    return (gates + counts + jnp.where(lane == GROUP_LANE, g_idx.astype(F32), 0.0)
            + jnp.where(lane == RANK_LANE, rank, 0.0))


def _post_kernel(x_ref, u_ref, yf_ref, yb_ref, om_ref, os_ref, mod_ref, d_ref, wglu_ref, gs_ref, gm_ref,
                 gw_ref, wout_ref, n2_ref, wr_ref, br_ref, x1_ref, h2_ref, gate_ref):
    mod = mod_ref[...]
    y = d_ref[...] * u_ref[...] + yf_ref[...] + yb_ref[...]
    ga = _dot(_gelu_tanh(y).astype(BF16), wglu_ref[...])
    y_ssm = ga[:, :SSM_CH] * jax.nn.sigmoid(ga[:, SSM_CH:])
    wout = wout_ref
    n_mla = MLA_HEADS * MLA_V
    mixed = (_dot(_rms(y_ssm, gs_ref[...]).astype(BF16), wout[0:SSM_CH, :])
             + _dot(_rms(om_ref[...], gm_ref[...]).astype(BF16), wout[SSM_CH:SSM_CH + n_mla, :])
             + _dot(_rms(os_ref[...], gw_ref[...]).astype(BF16), wout[SSM_CH + n_mla:, :]))
    x1 = x_ref[...] + mod[2:3] * mixed
    x1_ref[...] = x1
    h2 = _rms(x1, n2_ref[...]) * (1.0 + mod[4:5]) + mod[3:4]
    h2_ref[...] = h2.astype(BF16)
    gate_ref[...] = _route(_dot3(h2, wr_ref[...]) + br_ref[...])


def _post(x, u_tm, y2, o_mla, o_swa, mod, lw, *, per_batch_mod):
    B, S, D = x.shape
    ts = min(S, POST_TILE)
    tok = lambda w: pl.BlockSpec((None, ts, w), lambda b, s: (b, s, 0))
    full = lambda a: pl.BlockSpec(a.shape, lambda b, s: (0,) * a.ndim)
    names = ["ssm_d", "w_glu", "gn_ssm", "gn_mla", "gn_swa", "w_out", "norm2", "w_router", "b_router"]
    in_specs = [
        tok(D),
        pl.BlockSpec((ts, SSM_CH), lambda b, s: (s, b)),
        pl.BlockSpec((None, ts, SSM_CH), lambda b, s: (0, s, b)),
        pl.BlockSpec((None, ts, SSM_CH), lambda b, s: (1, s, b)),
        tok(MLA_HEADS * MLA_V), tok(SWA_HEADS * HEAD_DIM),
        pl.BlockSpec((None, 6, D), (lambda b, s: (b, 0, 0)) if per_batch_mod else (lambda b, s: (0, 0, 0))),
    ] + [full(lw[k]) for k in names]
    return pl.pallas_call(
        _post_kernel,
        grid=(B, S // ts),
        in_specs=in_specs,
        out_specs=[tok(D), tok(D), tok(LANES)],
        out_shape=[
            jax.ShapeDtypeStruct((B, S, D), F32),
            jax.ShapeDtypeStruct((B, S, D), BF16),
            jax.ShapeDtypeStruct((B, S, LANES), F32),
        ],
        compiler_params=_cparams(("arbitrary", "arbitrary")),
        name="post",
    )(x, u_tm, y2, y2, o_mla, o_swa, mod, *[lw[k] for k in names])


def _split3(x):
    hi = x.astype(BF16)
    r = x - hi.astype(F32)
    mid = r.astype(BF16)
    lo = (r - mid.astype(F32)).astype(BF16)
    return hi, mid, lo


def _moe_kernel(cnt_ref, h2_ref, gate_ref, crow_ref, ccol_ref, x1_ref, mod_ref, wg_ref, wu_ref, wd_ref, fn_ref,
                o_ref, xs_s, gs_s, ys_s, acc_s, *, final, n_tiles):
    tm = h2_ref.shape[0]
    rb = MOE_ROW_BLOCK
    g = pl.program_id(2)
    e = pl.program_id(3)
    cnt = cnt_ref[(pl.program_id(0) * n_tiles + pl.program_id(1)) * MOE_GROUPS + g]
    n_blocks = (cnt + rb - 1) // rb
    rows_of = lambda blk: pl.ds(pl.multiple_of(blk * rb, rb), rb)

    def for_blocks(body):
        def step(blk, carry):
            body(blk, rows_of(blk))
            return carry
        lax.fori_loop(0, n_blocks, step, 0)

    @pl.when((g == 0) & (e == 0))
    def _():
        acc_s[...] = jnp.zeros_like(acc_s)

    @pl.when(e == 0)
    def _():
        rank = crow_ref[...] - g * MOE_CODE_STRIDE

        def gather(blk, rows):
            row_id = lax.broadcasted_iota(jnp.int32, (rb, tm), 0) + blk * rb
            onehot = jnp.where(row_id == rank, 1.0, 0.0).astype(BF16)
            xs_s[rows, :] = _dot(onehot, h2_ref[...]).astype(BF16)
            g_hi, g_mid, g_lo = _split3(gate_ref[...])
            gs_s[rows, :] = _dot(onehot, g_hi) + _dot(onehot, g_mid) + _dot(onehot, g_lo)
        for_blocks(gather)

    n_exp = MOE_PER_GROUP // MOE_SPLIT

    def experts(blk, rows, first):
        x = xs_s[rows, :]
        gs = gs_s[rows, :]
        lane = lax.broadcasted_iota(jnp.int32, gs.shape, 1)
        first_expert = g * MOE_PER_GROUP + e * n_exp
        gate = jnp.concatenate(
            [jnp.broadcast_to(jnp.sum(jnp.where(lane == first_expert + k, gs, 0.0), axis=-1, keepdims=True),
                              (rb, MOE_HIDDEN)) for k in range(n_exp)], axis=-1)
        hid = _silu(_dot(x, wg_ref[...])) * _dot(x, wu_ref[...]) * gate
        y = _dot(hid.astype(BF16), wd_ref[...])
        if first:
            ys_s[rows, :] = y
        else:
            ys_s[rows, :] += y

    @pl.when(e == 0)
    def _():
        for_blocks(lambda blk, rows: experts(blk, rows, True))

    @pl.when(e != 0)
    def _():
        for_blocks(lambda blk, rows: experts(blk, rows, False))

    @pl.when(e == MOE_SPLIT - 1)
    def _():
        rank = ccol_ref[...] - g * MOE_CODE_STRIDE

        def scatter(blk, rows):
            col_id = lax.broadcasted_iota(jnp.int32, (tm, rb), 1) + blk * rb
            onehot = jnp.where(col_id == rank, 1.0, 0.0).astype(BF16)
            acc_s[...] += _dot(onehot, ys_s[rows, :].astype(BF16))
        for_blocks(scatter)

    @pl.when((g == MOE_GROUPS - 1) & (e == MOE_SPLIT - 1))
    def _():
        xo = x1_ref[...] + mod_ref[5:6, :] * acc_s[...]
        o_ref[...] = _rms(xo, fn_ref[...]) if final else xo


def _moe(h2, gates, x1, mod, lw, final_norm, *, per_batch_mod, final, rt):
    B, S, D = x1.shape
    tm = min(S, 1024)
    nt = S // tm
    sub = tm // rt
    info = gates[..., GROUP_LANE:COUNT_LANE + MOE_GROUPS].astype(jnp.int32)
    grp = info[..., 0].reshape(B, nt, sub, rt)
    sub_counts = info[:, rt - 1::rt, 2:].reshape(B, nt, sub, MOE_GROUPS)
    offsets = jnp.cumsum(sub_counts, axis=2) - sub_counts
    onehot = grp[..., None] == jnp.arange(MOE_GROUPS, dtype=jnp.int32)
    rank = info[..., 1].reshape(B, nt, sub, rt) + jnp.sum(jnp.where(onehot, offsets[:, :, :, None, :], 0), axis=-1)
    counts = jnp.sum(sub_counts, axis=2).reshape(-1)
    code = grp * MOE_CODE_STRIDE + rank
    code_row = code.reshape(B, nt, 1, tm)
    code_col = code.reshape(B, S, 1)

    tok = lambda w: pl.BlockSpec((None, tm, w), lambda b, s, g, e, c: (b, s, 0))
    wspec = lambda a: pl.BlockSpec((None,) + a.shape[1:], lambda b, s, g, e, c: (g * MOE_SPLIT + e, 0, 0))
    mod_map = (lambda b, s, g, e, c: (b, 0, 0)) if per_batch_mod else (lambda b, s, g, e, c: (0, 0, 0))
    grid_spec = pltpu.PrefetchScalarGridSpec(
        num_scalar_prefetch=1,
        grid=(B, nt, MOE_GROUPS, MOE_SPLIT),
        in_specs=[
            tok(D), tok(LANES),
            pl.BlockSpec((None, None, 1, tm), lambda b, s, g, e, c: (b, s, 0, 0)),
            tok(1), tok(D),
            pl.BlockSpec((None, 6, D), mod_map),
            wspec(lw["moe_wg"]), wspec(lw["moe_wu"]), wspec(lw["moe_wd"]),
            pl.BlockSpec(final_norm.shape, lambda b, s, g, e, c: (0, 0)),
        ],
        out_specs=tok(D),
        scratch_shapes=[
            pltpu.VMEM((tm, D), BF16),
            pltpu.VMEM((tm, LANES), F32),
            pltpu.VMEM((tm, D), F32),
            pltpu.VMEM((tm, D), F32),
        ],
    )
    return pl.pallas_call(
        functools.partial(_moe_kernel, final=final, n_tiles=nt),
        grid_spec=grid_spec,
        out_shape=jax.ShapeDtypeStruct((B, S, D), F32),
        compiler_params=_cparams(("arbitrary", "arbitrary", "arbitrary", "arbitrary")),
        name="moe",
    )(counts, h2, gates, code_row, code_col, x1, mod, lw["moe_wg"], lw["moe_wu"], lw["moe_wd"], final_norm)


def _swap_halves(w, n_heads, dim):
    k = w.shape[0]
    w = w.reshape(k, n_heads, 2, dim // 2)
    return w[:, :, ::-1, :].reshape(k, n_heads * dim)


def _side_by_side(w):
    n_exp = MOE_PER_GROUP // MOE_SPLIT
    w = w.astype(BF16).reshape(MOE_GROUPS, MOE_SPLIT, n_exp, D_MODEL, MOE_HIDDEN)
    return jnp.transpose(w, (0, 1, 3, 2, 4)).reshape(MOE_GROUPS * MOE_SPLIT, D_MODEL, n_exp * MOE_HIDDEN)


def _pad_cols(w, width):
    return jnp.pad(w, ((0, 0), (0, width - w.shape[1])))


def _rope_tables(n_tokens, rot_dim, reps):
    t = jnp.arange(n_tokens)
    row = (t // GRID_W).astype(F32)
    col = (t % GRID_W).astype(F32)
    n_freq = rot_dim // 4
    inv_freq = ROPE_BASE ** (-jnp.arange(n_freq, dtype=F32) / n_freq)
    ang = jnp.concatenate([row[:, None] * inv_freq, col[:, None] * inv_freq], axis=-1)
    cos, sin = jnp.cos(ang), jnp.sin(ang)
    c = jnp.tile(jnp.concatenate([cos, cos], axis=-1), (1, reps))
    s = jnp.tile(jnp.concatenate([-sin, sin], axis=-1), (1, reps))
    return _pad_cols(c, LANES), _pad_cols(s, LANES)


def _layer_weights(l, P, q_abs, ab_re, ab_im, bb_re, bb_im):
    w_in = P["w_in"][l]
    seg = {}
    o = 0
    for name, width in (("u", SSM_CH), ("ql", MLA_Q_RANK), ("kvl", MLA_KV_RANK), ("kr", MLA_ROPE),
                        ("qs", SWA_HEADS * HEAD_DIM), ("ks", SWA_KV_HEADS * HEAD_DIM),
                        ("vs", SWA_KV_HEADS * HEAD_DIM)):
        seg[name] = w_in[:, o:o + width]
        o += width
    ctx_cols = [seg["u"], seg["ql"], seg["kvl"], seg["qs"], seg["ks"], seg["vs"], _pad_cols(seg["kr"], LANES)]
    lat_cols = ctx_cols + [
        _swap_halves(seg["qs"], SWA_HEADS, HEAD_DIM),
        _swap_halves(seg["ks"], SWA_KV_HEADS, HEAD_DIM),
        _pad_cols(_swap_halves(seg["kr"], 1, MLA_ROPE), LANES),
    ]
    w_qb = P["w_mla_qb"][l]
    w_rope = w_qb[:, :, MLA_NOPE:]
    w_rope_sw = w_rope.reshape(MLA_Q_RANK, MLA_HEADS, 2, MLA_ROPE // 2)[:, :, ::-1, :].reshape(w_rope.shape)
    flat_t = lambda w: w.reshape(MLA_Q_RANK, MLA_HEADS * MLA_ROPE).T
    wq_abs = jnp.transpose(q_abs[l], (0, 2, 1)).reshape(Q_ABS, MLA_Q_RANK)
    wq_ctx = jnp.concatenate([wq_abs, flat_t(w_rope)], axis=0)
    wq_lat = jnp.concatenate([wq_ctx, flat_t(w_rope_sw)], axis=0)

    eye = jnp.eye(SSM_GROUPS, dtype=F32)

    def block_diag_b(bb):
        bb = bb.reshape(2, SSM_GROUPS, SSM_STATE, SSM_GROUP)
        return jnp.einsum("dgpc,gh->dgchp", bb, eye).reshape(2, SSM_CH, SSM_N)

    def block_diag_c(cc):
        return jnp.einsum("dgcp,gh->dgphc", cc, eye).reshape(2, SSM_N, SSM_CH)

    sl = slice(2 * l, 2 * l + 2)
    w_router = jnp.concatenate([P["moe_w_expert"][l], P["moe_w_group"][l]], axis=1)
    b_router = jnp.concatenate([P["moe_b_expert"][l], P["moe_b_group"][l]])
    row = lambda v: v.reshape(1, -1)
    return dict(
        norm1=row(P["norm1"][l]), norm2=row(P["norm2"][l]),
        win_ctx=jnp.concatenate(ctx_cols, axis=1).astype(BF16),
        win_lat=jnp.concatenate(lat_cols, axis=1).astype(BF16),
        q_norm=row(P["mla_q_norm"][l]), kv_norm=row(P["mla_kv_norm"][l]),
        wq_ctx=wq_ctx.astype(BF16), wq_lat=wq_lat.astype(BF16),
        wv=jnp.transpose(P["w_mla_kvb"][l][:, :, MLA_NOPE:], (1, 0, 2)).astype(BF16),
        ssm_wb=jnp.concatenate([block_diag_b(bb_re[sl]), block_diag_b(bb_im[sl])], axis=2).astype(BF16),
        ssm_a=jnp.concatenate([ab_re[sl], ab_im[sl]], axis=1).reshape(2, 1, 2 * SSM_N),
        ssm_wc=jnp.concatenate([block_diag_c(P["ssm_c_re"][l]), -block_diag_c(P["ssm_c_im"][l])],
                               axis=1).astype(BF16),
        ssm_d=row(P["ssm_d"][l]), w_glu=P["w_ssm_glu"][l].astype(BF16),
        gn_ssm=row(P["gn_ssm"][l]), gn_mla=row(P["gn_mla"][l]), gn_swa=row(P["gn_swa"][l]),
        w_out=P["w_out"][l].astype(BF16),
        w_router=_pad_cols(w_router, LANES), b_router=_pad_cols(row(b_router), LANES),
        moe_wg=_side_by_side(P["moe_w_gate"][l]), moe_wu=_side_by_side(P["moe_w_up"][l]),
        moe_wd=P["moe_w_down"][l].astype(BF16).reshape(
            MOE_GROUPS * MOE_SPLIT, MOE_PER_GROUP // MOE_SPLIT * MOE_HIDDEN, D_MODEL),
        sink=P["swa_sink"][l],
    )


def _layer(x, mod, lw, final_norm, *, tables, ctx, per_batch_mod, final):
    B, S, _ = x.shape
    context_pass = ctx is None
    pre = _pre(x, mod, lw, tables, per_batch_mod=per_batch_mod, emit_ctx=context_pass)
    u_tm, qt, kcat, ckvt, qs, ks, vs = pre[:7]
    nb = B // SUBLANES
    if context_pass:
        h0 = jnp.zeros((2, nb, SUBLANES, 2 * SSM_N), F32)
        y2, hfin = _ssm(u_tm, lw, h0, B, S)
        o_mla = _mla(qt, kcat, ckvt, None, None, lw["wv"])
        o_swa = _swa(lw["sink"], qs, ks, vs, None, None)
        state = hfin.reshape(2, B, 2, SSM_GROUPS, SSM_STATE).transpose(1, 0, 2, 3, 4)
        new_ctx = (pre[7], pre[8], pre[9].reshape(B, S, SWA_KV_HEADS, HEAD_DIM),
                   pre[10].reshape(B, S, SWA_KV_HEADS, HEAD_DIM), state)
    else:
        kcat_c, ckvt_c, ks_c, vs_c, h0 = ctx
        y2, _ = _ssm(u_tm, lw, h0, B, S)
        o_mla = _mla(qt, kcat_c, ckvt_c, kcat, ckvt, lw["wv"])
        o_swa = _swa(lw["sink"], qs, ks_c, vs_c, ks, vs)
        new_ctx = None
    x1, h2, gates = _post(x, u_tm, y2, o_mla, o_swa, mod, lw, per_batch_mod=per_batch_mod)
    if not per_batch_mod:
        shp = lambda a: a.reshape(1, B * S, a.shape[-1])
        xo = _moe(shp(h2), shp(gates), shp(x1), mod, lw, final_norm, per_batch_mod=False, final=final,
                  rt=min(S, POST_TILE))
        xo = xo.reshape(B, S, D_MODEL)
    else:
        xo = _moe(h2, gates, x1, mod, lw, final_norm, per_batch_mod=True, final=final, rt=min(S, POST_TILE))
    return xo, new_ctx


def kernel(x_prompt, x_sample, c, cache_mla_ckv, cache_mla_krope, cache_swa_k, cache_swa_v, state_ssm, c_ctx, w_ada, b_ada, norm1, norm2, w_in, ssm_a_re, ssm_a_im, ssm_log_dt, ssm_b_re, ssm_b_im, ssm_c_re, ssm_c_im, ssm_d, w_ssm_glu, mla_q_norm, w_mla_qb, mla_kv_norm, w_mla_kvb, swa_sink, gn_ssm, gn_mla, gn_swa, w_out, moe_w_group, moe_b_group, moe_w_expert, moe_b_expert, moe_w_gate, moe_w_up, moe_w_down, final_norm):
    P = dict(w_ada=w_ada, b_ada=b_ada, norm1=norm1, norm2=norm2, w_in=w_in,
             ssm_c_re=ssm_c_re, ssm_c_im=ssm_c_im, ssm_d=ssm_d, w_ssm_glu=w_ssm_glu,
             mla_q_norm=mla_q_norm, w_mla_qb=w_mla_qb, mla_kv_norm=mla_kv_norm, w_mla_kvb=w_mla_kvb,
             swa_sink=swa_sink, gn_ssm=gn_ssm, gn_mla=gn_mla, gn_swa=gn_swa, w_out=w_out,
             moe_w_group=moe_w_group, moe_b_group=moe_b_group, moe_w_expert=moe_w_expert,
             moe_b_expert=moe_b_expert, moe_w_gate=moe_w_gate, moe_w_up=moe_w_up, moe_w_down=moe_w_down)
    n_dec = c.shape[0]
    n_cond = 2 * SUBLANES
    conds = jnp.zeros((n_cond, D_MODEL), F32).at[:n_dec].set(c).at[n_dec].set(c_ctx)
    mods = _modulation(conds, w_ada, b_ada).reshape(DEPTH, n_cond, 6, D_MODEL)

    ab_re, ab_im, bb_re, bb_im = _ssm_discretise(ssm_a_re, ssm_a_im, ssm_log_dt, ssm_b_re, ssm_b_im)
    q_abs = _absorb_q(jnp.transpose(w_mla_qb[..., :MLA_NOPE], (0, 2, 1, 3)),
                      jnp.transpose(w_mla_kvb[..., :MLA_NOPE], (0, 2, 1, 3)))
    lws = [_layer_weights(l, P, q_abs, ab_re.reshape(2 * DEPTH, SSM_N), ab_im.reshape(2 * DEPTH, SSM_N),
                          bb_re, bb_im) for l in range(DEPTH)]
    fnorm = final_norm.reshape(1, D_MODEL)

    xp = x_prompt
    ctx_states = []
    for l in range(DEPTH):
        xp, new = _layer(xp, mods[l, n_dec:n_dec + 1], lws[l], fnorm, tables=None, ctx=None,
                         per_batch_mod=False, final=l == DEPTH - 1)
        ctx_states.append(new)
    outs_ctx = tuple(jnp.stack([s[k] for s in ctx_states], axis=1) for k in range(5))

    n_lat = x_sample.shape[1]
    cm, sm = _rope_tables(n_lat, MLA_ROPE, 1)
    cs, ss = _rope_tables(n_lat, HEAD_DIM, LANES // HEAD_DIM)
    tables = (cm, sm, cm[:, :MLA_ROPE].T, sm[:, :MLA_ROPE].T, cs, ss)
    xs = x_sample
    past = cache_mla_ckv.shape[2]
    for l in range(DEPTH):
        kcat_c = jnp.concatenate(
            [cache_mla_ckv[:, l], cache_mla_krope[:, l],
             jnp.zeros((n_dec, past, LANES - MLA_ROPE), F32)], axis=-1).astype(BF16)
        ks_c = cache_swa_k[:, l].reshape(n_dec, past, LANES).astype(BF16)
        vs_c = cache_swa_v[:, l].reshape(n_dec, past, LANES).astype(BF16)
        h0 = state_ssm[:, l].transpose(1, 0, 2, 3, 4).reshape(2, n_dec // SUBLANES, SUBLANES, 2 * SSM_N)
        ckvt_c = jnp.transpose(cache_mla_ckv[:, l], (0, 2, 1)).astype(BF16)
        xs, _ = _layer(xs, mods[l, :n_dec], lws[l], fnorm, tables=tables, ctx=(kcat_c, ckvt_c, ks_c, vs_c, h0),
                       per_batch_mod=True, final=l == DEPTH - 1)
    return (xp, xs) + outs_ctx
```

```python
import functools
import math

import jax
import jax.numpy as jnp
from jax import lax
from jax.experimental import pallas as pl
from jax.experimental.pallas import tpu as pltpu

F32 = jnp.float32
BF16 = jnp.bfloat16

D_MODEL = 1024
DEPTH = 4
GRID_W = 64
HEAD_DIM = 64
SSM_CH = 256
SSM_GROUP = 16
SSM_GROUPS = SSM_CH // SSM_GROUP
SSM_STATE = 64
SSM_N = SSM_GROUPS * SSM_STATE
MLA_HEADS = 6
MLA_Q_RANK = 256
MLA_KV_RANK = 128
MLA_NOPE = 64
MLA_ROPE = 32
MLA_V = 64
SWA_HEADS = 6
SWA_KV_HEADS = 2
SWA_GROUP = SWA_HEADS // SWA_KV_HEADS
SWA_WINDOW = 128
SWA_BLOCK = 128
MOE_GROUPS = 4
MOE_PER_GROUP = 8
MOE_EXPERTS = MOE_GROUPS * MOE_PER_GROUP
MOE_HIDDEN = 256
GROUP_LANE = MOE_EXPERTS
RANK_LANE = GROUP_LANE + 1
COUNT_LANE = GROUP_LANE + 2
POST_TILE = 512
MOE_SPLIT = 2
MOE_ROW_BLOCK = 256
MOE_CODE_STRIDE = 1 << 16
ROPE_BASE = 10000.0
EPS = 1e-6
MLA_SCALE = 1.0 / math.sqrt(MLA_NOPE + MLA_ROPE)
MLA_SCALE_LOG2 = MLA_SCALE * math.log2(math.e)
SWA_SCALE = 1.0 / math.sqrt(HEAD_DIM)

LANES = 128
SUBLANES = 8
VMEM_LIMIT = 52 * 1024 * 1024

OFF_U = 0
OFF_QL = OFF_U + SSM_CH
OFF_KVL = OFF_QL + MLA_Q_RANK
OFF_QS = OFF_KVL + MLA_KV_RANK
OFF_KS = OFF_QS + SWA_HEADS * HEAD_DIM
OFF_VS = OFF_KS + SWA_KV_HEADS * HEAD_DIM
OFF_KR = OFF_VS + SWA_KV_HEADS * HEAD_DIM
N_PACK_CTX = OFF_KR + LANES
OFF_QS_SW = N_PACK_CTX
OFF_KS_SW = OFF_QS_SW + SWA_HEADS * HEAD_DIM
OFF_KR_SW = OFF_KS_SW + SWA_KV_HEADS * HEAD_DIM
N_PACK_LAT = OFF_KR_SW + LANES
Q_ABS = MLA_HEADS * MLA_KV_RANK
MLA_QK = 2 * LANES


def _cparams(sem):
    return pltpu.CompilerParams(dimension_semantics=sem, vmem_limit_bytes=VMEM_LIMIT)


def _dot(a, b):
    return jnp.dot(a, b, preferred_element_type=F32)


def _dot_nt(a, b):
    return lax.dot_general(a, b, (((1,), (1,)), ((), ())), preferred_element_type=F32)


def _split(x):
    hi = x.astype(BF16)
    lo = (x - hi.astype(F32)).astype(BF16)
    return hi, lo


def _dot3(a, b):
    ah, al = _split(a)
    bh, bl = _split(b)
    return _dot(ah, bh) + _dot(al, bh) + _dot(ah, bl)


def _tree(op, x3):
    parts = [x3[i] for i in range(x3.shape[0])]
    while len(parts) > 1:
        parts = [op(parts[i], parts[i + 1]) for i in range(0, len(parts), 2)]
    return parts[0]


def _rms(x, g):
    return x * lax.rsqrt(jnp.mean(x * x, axis=-1, keepdims=True) + EPS) * g


def _silu(x):
    return x * jax.nn.sigmoid(x)


def _gelu_tanh(x):
    return 0.5 * x * (1.0 + jnp.tanh(math.sqrt(2.0 / math.pi) * (x + 0.044715 * (x * x * x))))


def _mod_kernel(c_ref, w_ref, b_ref, o_ref):
    o_ref[...] = _dot3(_silu(c_ref[...]), w_ref[...]) + b_ref[...]


def _modulation(conds, w_ada, b_ada):
    n = conds.shape[0]
    tn = 1536
    return pl.pallas_call(
        _mod_kernel,
        grid=(DEPTH, 6 * D_MODEL // tn),
        in_specs=[
            pl.BlockSpec((n, D_MODEL), lambda l, j: (0, 0)),
            pl.BlockSpec((None, D_MODEL, tn), lambda l, j: (l, 0, j)),
            pl.BlockSpec((None, 1, tn), lambda l, j: (l, 0, j)),
        ],
        out_specs=pl.BlockSpec((None, n, tn), lambda l, j: (l, 0, j)),
        out_shape=jax.ShapeDtypeStruct((DEPTH, n, 6 * D_MODEL), F32),
        compiler_params=_cparams(("arbitrary", "arbitrary")),
        name="modulation",
    )(conds, w_ada, b_ada.reshape(DEPTH, 1, 6 * D_MODEL))


def _ssm_disc_kernel(are_ref, aim_ref, ldt_ref, bre_ref, bim_ref, abre_ref, abim_ref, bbre_ref, bbim_ref):
    lam_re = are_ref[...]
    lam_im = aim_ref[...]
    dt = jnp.exp(ldt_ref[...])
    z_re = lam_re * dt
    z_im = lam_im * dt
    mag = jnp.exp(z_re)
    ab_re = mag * jnp.cos(z_im)
    ab_im = mag * jnp.sin(z_im)
    den = lam_re * lam_re + lam_im * lam_im
    f_re = ((ab_re - 1.0) * lam_re + ab_im * lam_im) / den
    f_im = (ab_im * lam_re - (ab_re - 1.0) * lam_im) / den
    b_re = bre_ref[...]
    b_im = bim_ref[...]
    abre_ref[...] = ab_re
    abim_ref[...] = ab_im
    bbre_ref[...] = f_re * b_re - f_im * b_im
    bbim_ref[...] = f_re * b_im + f_im * b_re


def _ssm_discretise(a_re, a_im, log_dt, b_re, b_im):
    n = DEPTH * 2
    col = lambda v: v.reshape(n, SSM_N, 1)
    ldt = jnp.broadcast_to(log_dt[..., None], (DEPTH, 2, SSM_GROUPS, SSM_STATE))
    cspec = pl.BlockSpec((None, SSM_N, 1), lambda i: (i, 0, 0))
    bspec = pl.BlockSpec((None, SSM_N, SSM_GROUP), lambda i: (i, 0, 0))
    return pl.pallas_call(
        _ssm_disc_kernel,
        grid=(n,),
        in_specs=[cspec, cspec, cspec, bspec, bspec],
        out_specs=[cspec, cspec, bspec, bspec],
        out_shape=[jax.ShapeDtypeStruct((n, SSM_N, 1), F32)] * 2
        + [jax.ShapeDtypeStruct((n, SSM_N, SSM_GROUP), F32)] * 2,
        compiler_params=_cparams(("arbitrary",)),
        name="ssm_discretise",
    )(col(a_re), col(a_im), col(ldt), b_re.reshape(n, SSM_N, SSM_GROUP), b_im.reshape(n, SSM_N, SSM_GROUP))


def _absorb_kernel(wq_ref, wk_ref, o_ref):
    a = wq_ref[...]
    b = wk_ref[...]
    ah, al = _split(a)
    bh, bl = _split(b)
    o_ref[...] = _dot_nt(ah, bh) + _dot_nt(al, bh) + _dot_nt(ah, bl)


def _absorb_q(wq_nope, wk_nope):
    return pl.pallas_call(
        _absorb_kernel,
        grid=(DEPTH, MLA_HEADS),
        in_specs=[
            pl.BlockSpec((None, None, MLA_Q_RANK, MLA_NOPE), lambda l, h: (l, h, 0, 0)),
            pl.BlockSpec((None, None, MLA_KV_RANK, MLA_NOPE), lambda l, h: (l, h, 0, 0)),
        ],
        out_specs=pl.BlockSpec((None, None, MLA_Q_RANK, MLA_KV_RANK), lambda l, h: (l, h, 0, 0)),
        out_shape=jax.ShapeDtypeStruct((DEPTH, MLA_HEADS, MLA_Q_RANK, MLA_KV_RANK), F32),
        compiler_params=_cparams(("arbitrary", "arbitrary")),
        name="mla_absorb",
    )(wq_nope, wk_nope)


def _pre_kernel(*refs, rope, emit_ctx):
    it = iter(refs)
    x_ref, mod_ref, n1_ref, win_ref, qn_ref, kvn_ref, wq_ref = (next(it) for _ in range(7))
    if rope:
        cm_ref, sm_ref, cmt_ref, smt_ref, cs_ref, ss_ref = (next(it) for _ in range(6))
    u_ref, qt_ref, kcat_ref, ckvt_ref, qs_ref, ks_ref, vs_ref = (next(it) for _ in range(7))
    if emit_ctx:
        ckv_o, kr_o, ks_o, vs_o = (next(it) for _ in range(4))

    x = x_ref[...]
    mod = mod_ref[...]
    h = _rms(x, n1_ref[...]) * (1.0 + mod[1:2]) + mod[0:1]
    proj = _dot(h.astype(BF16), win_ref[...])

    u_ref[...] = proj[:, OFF_U:OFF_U + SSM_CH]

    qln = _rms(proj[:, OFF_QL:OFF_QL + MLA_Q_RANK], qn_ref[...]).astype(BF16)
    qall = _dot_nt(wq_ref[...], qln)
    n_rope = MLA_HEADS * MLA_ROPE
    zero_rows = jnp.zeros((MLA_QK - MLA_KV_RANK - MLA_ROPE, qall.shape[1]), BF16)
    for i in range(MLA_HEADS):
        qa = qall[i * MLA_KV_RANK:(i + 1) * MLA_KV_RANK]
        qr = qall[Q_ABS + i * MLA_ROPE:Q_ABS + (i + 1) * MLA_ROPE]
        if rope:
            qr_sw = qall[Q_ABS + n_rope + i * MLA_ROPE:Q_ABS + n_rope + (i + 1) * MLA_ROPE]
            qr = qr * cmt_ref[...] + qr_sw * smt_ref[...]
        base = i * MLA_QK
        qt_ref[base:base + MLA_KV_RANK, :] = (qa * MLA_SCALE_LOG2).astype(BF16)
        qt_ref[base + MLA_KV_RANK:base + MLA_KV_RANK + MLA_ROPE, :] = (qr * MLA_SCALE_LOG2).astype(BF16)
        qt_ref[base + MLA_KV_RANK + MLA_ROPE:base + MLA_QK, :] = zero_rows

    ckv = _rms(proj[:, OFF_KVL:OFF_KVL + MLA_KV_RANK], kvn_ref[...])
    kr = proj[:, OFF_KR:OFF_KR + LANES]
    if emit_ctx:
        ckv_o[...] = ckv
        kr_o[...] = kr[:, :MLA_ROPE]
    if rope:
        kr = kr * cm_ref[...] + proj[:, OFF_KR_SW:OFF_KR_SW + LANES] * sm_ref[...]
    kcat_ref[...] = jnp.concatenate([ckv, kr], axis=-1).astype(BF16)
    ckvt_ref[...] = ckv.T.astype(BF16)

    qs = proj[:, OFF_QS:OFF_QS + SWA_HEADS * HEAD_DIM]
    ks = proj[:, OFF_KS:OFF_KS + SWA_KV_HEADS * HEAD_DIM]
    vs = proj[:, OFF_VS:OFF_VS + SWA_KV_HEADS * HEAD_DIM]
    if emit_ctx:
        ks_o[...] = ks
        vs_o[...] = vs
    if rope:
        cs = cs_ref[...]
        ss = ss_ref[...]
        qs_sw = proj[:, OFF_QS_SW:OFF_QS_SW + SWA_HEADS * HEAD_DIM]
        ks = ks * cs + proj[:, OFF_KS_SW:OFF_KS_SW + SWA_KV_HEADS * HEAD_DIM] * ss
        qs = jnp.concatenate(
            [qs[:, i * LANES:(i + 1) * LANES] * cs + qs_sw[:, i * LANES:(i + 1) * LANES] * ss
             for i in range(SWA_HEADS * HEAD_DIM // LANES)], axis=-1)
    qs_ref[...] = (qs * SWA_SCALE).astype(BF16)
    ks_ref[...] = ks.astype(BF16)
    vs_ref[...] = vs.astype(BF16)


def _pre(x, mod, lw, tables, *, per_batch_mod, emit_ctx):
    B, S, D = x.shape
    rope = tables is not None
    ts = min(S, 512)
    n_pack = N_PACK_LAT if rope else N_PACK_CTX
    win = lw["win_lat"] if rope else lw["win_ctx"]
    wq = lw["wq_lat"] if rope else lw["wq_ctx"]
    tok = lambda w: pl.BlockSpec((None, ts, w), lambda b, s: (b, s, 0))
    full = lambda a: pl.BlockSpec(a.shape, lambda b, s: (0,) * a.ndim)
    in_specs = [
        tok(D),
        pl.BlockSpec((None, 6, D), (lambda b, s: (b, 0, 0)) if per_batch_mod else (lambda b, s: (0, 0, 0))),
        full(lw["norm1"]), full(win), full(lw["q_norm"]), full(lw["kv_norm"]), full(wq),
    ]
    args = [x, mod, lw["norm1"], win, lw["q_norm"], lw["kv_norm"], wq]
    if rope:
        row_tab = pl.BlockSpec((ts, LANES), lambda b, s: (s, 0))
        col_tab = pl.BlockSpec((MLA_ROPE, ts), lambda b, s: (0, s))
        in_specs += [row_tab, row_tab, col_tab, col_tab, row_tab, row_tab]
        args += list(tables)
    feat = lambda w: pl.BlockSpec((None, w, ts), lambda b, s: (b, 0, s))
    out_specs = [
        tok(SSM_CH),
        feat(MLA_HEADS * MLA_QK), tok(MLA_QK), feat(MLA_KV_RANK), tok(SWA_HEADS * HEAD_DIM), tok(LANES), tok(LANES),
    ]
    out_shape = [
        jax.ShapeDtypeStruct((B, S, SSM_CH), F32),
        jax.ShapeDtypeStruct((B, MLA_HEADS * MLA_QK, S), BF16),
        jax.ShapeDtypeStruct((B, S, MLA_QK), BF16),
        jax.ShapeDtypeStruct((B, MLA_KV_RANK, S), BF16),
        jax.ShapeDtypeStruct((B, S, SWA_HEADS * HEAD_DIM), BF16),
        jax.ShapeDtypeStruct((B, S, LANES), BF16),
        jax.ShapeDtypeStruct((B, S, LANES), BF16),
    ]
    if emit_ctx:
        out_specs += [tok(MLA_KV_RANK), tok(MLA_ROPE), tok(LANES), tok(LANES)]
        out_shape += [
            jax.ShapeDtypeStruct((B, S, MLA_KV_RANK), F32),
            jax.ShapeDtypeStruct((B, S, MLA_ROPE), F32),
            jax.ShapeDtypeStruct((B, S, LANES), F32),
            jax.ShapeDtypeStruct((B, S, LANES), F32),
        ]
    return pl.pallas_call(
        functools.partial(_pre_kernel, rope=rope, emit_ctx=emit_ctx),
        grid=(B, S // ts),
        in_specs=in_specs,
        out_specs=out_specs,
        out_shape=out_shape,
        compiler_params=_cparams(("arbitrary", "arbitrary")),
        name="pre_lat" if rope else "pre_ctx",
    )(*args)


def _ssm_kernel(u_ref, wb_ref, a_ref, wc_ref, h0_ref, y_ref, hfin_ref, hre_s, him_s, st_s, ut_s, *, tc):
    d = pl.program_id(0)
    i = pl.program_id(2)
    n = pl.num_programs(2)

    @pl.when(i == 0)
    def _():
        st_s[...] = h0_ref[...]

    cblk = SSM_CH // LANES
    ub = u_ref[...]
    for b in range(SUBLANES):
        for j in range(cblk):
            ut_s[j, pl.ds(b, tc, stride=SUBLANES), :] = ub[b, :, j * LANES:(j + 1) * LANES]
    u = jnp.concatenate([ut_s[j] for j in range(cblk)], axis=-1).astype(BF16)
    bu = _dot(u, wb_ref[...])
    hre_s[...] = bu[:, :SSM_N]
    him_s[...] = bu[:, SSM_N:]

    a = a_ref[...]
    a_re = jnp.broadcast_to(a[:, :SSM_N], (SUBLANES, SSM_N))
    a_im = jnp.broadcast_to(a[:, SSM_N:], (SUBLANES, SSM_N))
    st = st_s[...]

    def step(t, carry):
        h_re, h_im = carry
        tt = jnp.where(d == 0, t, tc - 1 - t)
        rows = pl.ds(pl.multiple_of(tt * SUBLANES, SUBLANES), SUBLANES)
        n_re = a_re * h_re - a_im * h_im + hre_s[rows, :]
        n_im = a_re * h_im + a_im * h_re + him_s[rows, :]
        hre_s[rows, :] = n_re
        him_s[rows, :] = n_im
        return n_re, n_im

    h_re, h_im = lax.fori_loop(0, tc, step, (st[:, :SSM_N], st[:, SSM_N:]), unroll=4)
    st_s[...] = jnp.concatenate([h_re, h_im], axis=-1)

    wc = wc_ref[...]
    y = _dot(hre_s[...].astype(BF16), wc[:SSM_N]) + _dot(him_s[...].astype(BF16), wc[SSM_N:])
    for j in range(cblk):
        ut_s[j] = y[:, j * LANES:(j + 1) * LANES]
    for b in range(SUBLANES):
        y_ref[b] = jnp.concatenate([ut_s[j, pl.ds(b, tc, stride=SUBLANES), :] for j in range(cblk)], axis=-1)

    @pl.when(i == n - 1)
    def _():
        hfin_ref[...] = st_s[...]


def _ssm(u, lw, h0, B, S):
    nb = B // SUBLANES
    tc = 128
    nchunk = S // tc
    chunk = lambda d, i: i + d * (nchunk - 1 - 2 * i)
    y, hfin = pl.pallas_call(
        functools.partial(_ssm_kernel, tc=tc),
        grid=(2, nb, nchunk),
        in_specs=[
            pl.BlockSpec((SUBLANES, tc, SSM_CH), lambda d, b, i: (b, chunk(d, i), 0)),
            pl.BlockSpec((None, SSM_CH, 2 * SSM_N), lambda d, b, i: (d, 0, 0)),
            pl.BlockSpec((None, 1, 2 * SSM_N), lambda d, b, i: (d, 0, 0)),
            pl.BlockSpec((None, 2 * SSM_N, SSM_CH), lambda d, b, i: (d, 0, 0)),
            pl.BlockSpec((None, None, SUBLANES, 2 * SSM_N), lambda d, b, i: (d, b, 0, 0)),
        ],
        out_specs=[
            pl.BlockSpec((None, SUBLANES, tc, SSM_CH), lambda d, b, i: (d, b, chunk(d, i), 0)),
            pl.BlockSpec((None, None, SUBLANES, 2 * SSM_N), lambda d, b, i: (d, b, 0, 0)),
        ],
        out_shape=[
            jax.ShapeDtypeStruct((2, B, S, SSM_CH), F32),
            jax.ShapeDtypeStruct((2, nb, SUBLANES, 2 * SSM_N), F32),
        ],
        scratch_shapes=[
            pltpu.VMEM((tc * SUBLANES, SSM_N), F32),
            pltpu.VMEM((tc * SUBLANES, SSM_N), F32),
            pltpu.VMEM((SUBLANES, 2 * SSM_N), F32),
            pltpu.VMEM((SSM_CH // LANES, tc * SUBLANES, LANES), F32),
        ],
        compiler_params=_cparams(("arbitrary", "arbitrary", "arbitrary")),
        name="ssm_scan",
    )(u, lw["ssm_wb"], lw["ssm_a"], lw["ssm_wc"], h0)
    return y, hfin


def _mla_kernel(*refs, with_latent, tk):
    if with_latent:
        qt_ref, ka_ref, vat_ref, kb_ref, vbt_ref, wv_ref, o_ref = refs
    else:
        qt_ref, ka_ref, vat_ref, wv_ref, o_ref = refs
    tq = qt_ref.shape[1]
    q_of = lambda hd: qt_ref[hd * MLA_QK:(hd + 1) * MLA_QK, :]

    def tile(state, k, vt):
        new_state = []
        s_next = _dot(k, q_of(0))
        for hd in range(MLA_HEADS):
            m8, l8, acc = state[hd]
            s = s_next
            if hd + 1 < MLA_HEADS:
                s_next = _dot(k, q_of(hd + 1))
            s3 = s.reshape(s.shape[0] // SUBLANES, SUBLANES, tq)
            mloc = jnp.max(_tree(jnp.maximum, s3), axis=0, keepdims=True)
            m8_new = jnp.maximum(m8, jnp.broadcast_to(mloc, (SUBLANES, tq)))
            alpha8 = jnp.exp2(m8 - m8_new)
            p3 = jnp.exp2(s3 - m8_new[None])
            l8 = alpha8 * l8 + _tree(jnp.add, p3)
            pv = _dot(vt, p3.reshape(s.shape).astype(BF16))
            acc3 = acc.reshape(MLA_KV_RANK // SUBLANES, SUBLANES, tq) * alpha8[None]
            new_state.append((m8_new, l8, acc3.reshape(MLA_KV_RANK, tq) + pv))
        return tuple(new_state)

    init = tuple((jnp.full((SUBLANES, tq), -jnp.inf, F32), jnp.zeros((SUBLANES, tq), F32),
                  jnp.zeros((MLA_KV_RANK, tq), F32)) for _ in range(MLA_HEADS))
    state = tile(init, ka_ref[...], vat_ref[...])
    if with_latent:
        for j in range(kb_ref.shape[0] // tk):
            state = tile(state, kb_ref[j * tk:(j + 1) * tk, :], vbt_ref[:, j * tk:(j + 1) * tk])

    outs = []
    for hd in range(MLA_HEADS):
        _, l8, acc = state[hd]
        o_lat = (acc / jnp.sum(l8, axis=0, keepdims=True)).T.astype(BF16)
        outs.append(_dot(o_lat, wv_ref[hd]))
    o_ref[...] = jnp.concatenate(outs, axis=-1)


def _mla(qt, ka, vat, kb, vbt, wv):
    B, _, S = qt.shape
    tq = 256
    with_latent = kb is not None
    kspec = lambda a: pl.BlockSpec((None,) + a.shape[1:], lambda b, s: (b, 0, 0))
    in_specs = [pl.BlockSpec((None, MLA_HEADS * MLA_QK, tq), lambda b, s: (b, 0, s)), kspec(ka), kspec(vat)]
    args = [qt, ka, vat]
    if with_latent:
        in_specs += [kspec(kb), kspec(vbt)]
        args += [kb, vbt]
    in_specs.append(pl.BlockSpec(wv.shape, lambda b, s: (0, 0, 0)))
    args.append(wv)
    return pl.pallas_call(
        functools.partial(_mla_kernel, with_latent=with_latent, tk=4096),
        grid=(B, S // tq),
        in_specs=in_specs,
        out_specs=pl.BlockSpec((None, tq, MLA_HEADS * MLA_V), lambda b, s: (b, s, 0)),
        out_shape=jax.ShapeDtypeStruct((B, S, MLA_HEADS * MLA_V), F32),
        compiler_params=_cparams(("arbitrary", "arbitrary")),
        name="mla_lat" if with_latent else "mla_ctx",
    )(*args)


def _swa_kernel(*refs, with_latent):
    if with_latent:
        sink_ref, q_ref, ka_ref, va_ref, kb_ref, vb_ref, o_ref = refs
    else:
        sink_ref, q_ref, ka_ref, va_ref, o_ref = refs
    tq = q_ref.shape[0]
    q = q_ref[...]
    ka = ka_ref[...]
    va = va_ref[...]
    if with_latent:
        n = pl.program_id(1)
        nblk = kb_ref.shape[0] // SWA_BLOCK
        start = pl.multiple_of(jnp.clip(n - 1, 0, nblk - 3) * SWA_BLOCK, SWA_BLOCK)
        kb = kb_ref[pl.ds(start, 3 * SWA_BLOCK), :]
        vb = vb_ref[pl.ds(start, 3 * SWA_BLOCK), :]
        qpos = n * SWA_BLOCK + lax.broadcasted_iota(jnp.int32, (tq, 3 * SWA_BLOCK), 0)
        kpos = start + lax.broadcasted_iota(jnp.int32, (tq, 3 * SWA_BLOCK), 1)
        valid = jnp.abs(qpos - kpos) <= SWA_WINDOW
    outs = []
    for hd in range(SWA_HEADS):
        kh = hd // SWA_GROUP
        qh = q[:, hd * HEAD_DIM:(hd + 1) * HEAD_DIM]
        ksl = slice(kh * HEAD_DIM, (kh + 1) * HEAD_DIM)
        sink = sink_ref[hd]
        s_a = _dot_nt(qh, ka[:, ksl])
        m = jnp.maximum(jnp.max(s_a, axis=-1, keepdims=True), sink)
        if with_latent:
            s_b = jnp.where(valid, _dot_nt(qh, kb[:, ksl]), -jnp.inf)
            m = jnp.maximum(m, jnp.max(s_b, axis=-1, keepdims=True))
        p_a = jnp.exp(s_a - m)
        den = jnp.sum(p_a, axis=-1, keepdims=True) + jnp.exp(sink - m)
        o = _dot(p_a.astype(BF16), va[:, ksl])
        if with_latent:
            p_b = jnp.exp(s_b - m)
            den = den + jnp.sum(p_b, axis=-1, keepdims=True)
            o = o + _dot(p_b.astype(BF16), vb[:, ksl])
        outs.append(o / den)
    o_ref[...] = jnp.concatenate(outs, axis=-1)


def _swa(sink, q, ka, va, kb, vb):
    B, S, W = q.shape
    with_latent = kb is not None
    tq = SWA_BLOCK if with_latent else S
    tok = pl.BlockSpec((None, tq, W), lambda b, s: (b, s, 0))
    kspec = lambda a: pl.BlockSpec((None,) + a.shape[1:], lambda b, s: (b, 0, 0))
    in_specs = [pl.BlockSpec(memory_space=pltpu.SMEM), tok, kspec(ka), kspec(va)]
    args = [sink, q, ka, va]
    if with_latent:
        in_specs += [kspec(kb), kspec(vb)]
        args += [kb, vb]
    return pl.pallas_call(
        functools.partial(_swa_kernel, with_latent=with_latent),
        grid=(B, S // tq),
        in_specs=in_specs,
        out_specs=tok,
        out_shape=jax.ShapeDtypeStruct((B, S, W), F32),
        compiler_params=_cparams(("arbitrary", "arbitrary")),
        name="swa_lat" if with_latent else "swa_ctx",
    )(*args)


def _route(logits):
    lane = lax.broadcasted_iota(jnp.int32, logits.shape, 1)
    big = jnp.int32(1 << 20)
    is_g = (lane >= MOE_EXPERTS) & (lane < MOE_EXPERTS + MOE_GROUPS)
    lg = jnp.where(is_g, logits, -jnp.inf)
    mg = jnp.max(lg, axis=-1, keepdims=True)
    g_idx = jnp.min(jnp.where(lg == mg, lane - MOE_EXPERTS, big), axis=-1, keepdims=True)
    pg_top = 1.0 / jnp.sum(jnp.exp(lg - mg), axis=-1, keepdims=True)

    is_e = (lane < MOE_EXPERTS) & ((lane // MOE_PER_GROUP) == g_idx)
    le = jnp.where(is_e, logits, -jnp.inf)
    m1 = jnp.max(le, axis=-1, keepdims=True)
    e1 = jnp.min(jnp.where(le == m1, lane, big), axis=-1, keepdims=True)
    z = jnp.sum(jnp.exp(le - m1), axis=-1, keepdims=True)
    le2 = jnp.where(lane == e1, -jnp.inf, le)
    m2 = jnp.max(le2, axis=-1, keepdims=True)
    e2 = jnp.min(jnp.where(le2 == m2, lane, big), axis=-1, keepdims=True)
    p1 = 1.0 / z
    p2 = jnp.exp(m2 - m1) / z
    tot = p1 + p2
    gates = pg_top * (jnp.where(lane == e1, p1 / tot, 0.0) + jnp.where(lane == e2, p2 / tot, 0.0))
    t = logits.shape[0]
    chose = jnp.where(lane == g_idx + COUNT_LANE, 1.0, 0.0)
    tri = jnp.where(lax.broadcasted_iota(jnp.int32, (t, t), 0) >= lax.broadcasted_iota(jnp.int32, (t, t), 1),
                    1.0, 0.0).astype(BF16)
    counts = _dot(tri, chose.astype(BF16))
    rank = jnp.sum(chose * (counts - 1.0), axis=-1, keepdims=True)
    return (gates + counts + jnp.where(lane == GROUP_LANE, g_idx.astype(F32), 0.0)
            + jnp.where(lane == RANK_LANE, rank, 0.0))


def _post_kernel(x_ref, u_ref, yf_ref, yb_ref, om_ref, os_ref, mod_ref, d_ref, wglu_ref, gs_ref, gm_ref,
                 gw_ref, wout_ref, n2_ref, wr_ref, br_ref, x1_ref, h2_ref, gate_ref):
    mod = mod_ref[...]
    y = d_ref[...] * u_ref[...] + yf_ref[...] + yb_ref[...]
    ga = _dot(_gelu_tanh(y).astype(BF16), wglu_ref[...])
    y_ssm = ga[:, :SSM_CH] * jax.nn.sigmoid(ga[:, SSM_CH:])
    wout = wout_ref
    n_mla = MLA_HEADS * MLA_V
    mixed = (_dot(_rms(y_ssm, gs_ref[...]).astype(BF16), wout[0:SSM_CH, :])
             + _dot(_rms(om_ref[...], gm_ref[...]).astype(BF16), wout[SSM_CH:SSM_CH + n_mla, :])
             + _dot(_rms(os_ref[...], gw_ref[...]).astype(BF16), wout[SSM_CH + n_mla:, :]))
    x1 = x_ref[...] + mod[2:3] * mixed
    x1_ref[...] = x1
    h2 = _rms(x1, n2_ref[...]) * (1.0 + mod[4:5]) + mod[3:4]
    h2_ref[...] = h2.astype(BF16)
    gate_ref[...] = _route(_dot3(h2, wr_ref[...]) + br_ref[...])


def _post(x, u_tm, y2, o_mla, o_swa, mod, lw, *, per_batch_mod):
    B, S, D = x.shape
    ts = min(S, POST_TILE)
    tok = lambda w: pl.BlockSpec((None, ts, w), lambda b, s: (b, s, 0))
    full = lambda a: pl.BlockSpec(a.shape, lambda b, s: (0,) * a.ndim)
    names = ["ssm_d", "w_glu", "gn_ssm", "gn_mla", "gn_swa", "w_out", "norm2", "w_router", "b_router"]
    in_specs = [
        tok(D),
        tok(SSM_CH),
        pl.BlockSpec((None, None, ts, SSM_CH), lambda b, s: (0, b, s, 0)),
        pl.BlockSpec((None, None, ts, SSM_CH), lambda b, s: (1, b, s, 0)),
        tok(MLA_HEADS * MLA_V), tok(SWA_HEADS * HEAD_DIM),
        pl.BlockSpec((None, 6, D), (lambda b, s: (b, 0, 0)) if per_batch_mod else (lambda b, s: (0, 0, 0))),
    ] + [full(lw[k]) for k in names]
    return pl.pallas_call(
        _post_kernel,
        grid=(B, S // ts),
        in_specs=in_specs,
        out_specs=[tok(D), tok(D), tok(LANES)],
        out_shape=[
            jax.ShapeDtypeStruct((B, S, D), F32),
            jax.ShapeDtypeStruct((B, S, D), BF16),
            jax.ShapeDtypeStruct((B, S, LANES), F32),
        ],
        compiler_params=_cparams(("arbitrary", "arbitrary")),
        name="post",
    )(x, u_tm, y2, y2, o_mla, o_swa, mod, *[lw[k] for k in names])


def _split3(x):
    hi = x.astype(BF16)
    r = x - hi.astype(F32)
    mid = r.astype(BF16)
    lo = (r - mid.astype(F32)).astype(BF16)
    return hi, mid, lo


def _moe_kernel(cnt_ref, h2_ref, gate_ref, crow_ref, x1_ref, mod_ref, wg_ref, wu_ref, wd_ref, fn_ref,
                o_ref, xs_s, gs_s, ys_s, acc_s, *, final, n_tiles):
    tm = h2_ref.shape[0]
    rb = MOE_ROW_BLOCK
    g = pl.program_id(2)
    e = pl.program_id(3)
    cnt = cnt_ref[(pl.program_id(0) * n_tiles + pl.program_id(1)) * MOE_GROUPS + g]
    n_blocks = (cnt + rb - 1) // rb
    rows_of = lambda blk: pl.ds(pl.multiple_of(blk * rb, rb), rb)

    def for_blocks(body):
        def step(blk, carry):
            body(blk, rows_of(blk))
            return carry
        lax.fori_loop(0, n_blocks, step, 0)

    @pl.when((g == 0) & (e == 0))
    def _():
        acc_s[...] = jnp.zeros_like(acc_s)

    @pl.when(e == 0)
    def _():
        rank = crow_ref[...] - g * MOE_CODE_STRIDE

        def gather(blk, rows):
            row_id = lax.broadcasted_iota(jnp.int32, (rb, tm), 0) + blk * rb
            onehot = jnp.where(row_id == rank, 1.0, 0.0).astype(BF16)
            xs_s[rows, :] = _dot(onehot, h2_ref[...]).astype(BF16)
            g_hi, g_mid, g_lo = _split3(gate_ref[...])
            gs_s[rows, :] = _dot(onehot, g_hi) + _dot(onehot, g_mid) + _dot(onehot, g_lo)
        for_blocks(gather)

    n_exp = MOE_PER_GROUP // MOE_SPLIT

    def experts(blk, rows, first):
        x = xs_s[rows, :]
        gs = gs_s[rows, :]
        lane = lax.broadcasted_iota(jnp.int32, gs.shape, 1)
        first_expert = g * MOE_PER_GROUP + e * n_exp
        gate = jnp.concatenate(
            [jnp.broadcast_to(jnp.sum(jnp.where(lane == first_expert + k, gs, 0.0), axis=-1, keepdims=True),
                              (rb, MOE_HIDDEN)) for k in range(n_exp)], axis=-1)
        hid = _silu(_dot(x, wg_ref[...])) * _dot(x, wu_ref[...]) * gate
        y = _dot(hid.astype(BF16), wd_ref[...])
        if first:
            ys_s[rows, :] = y
        else:
            ys_s[rows, :] += y

    @pl.when(e == 0)
    def _():
        for_blocks(lambda blk, rows: experts(blk, rows, True))

    @pl.when(e != 0)
    def _():
        for_blocks(lambda blk, rows: experts(blk, rows, False))

    @pl.when(e == MOE_SPLIT - 1)
    def _():
        code_t = jnp.broadcast_to(crow_ref[...].astype(F32), (LANES, tm)).T
        rank = code_t.astype(jnp.int32) - g * MOE_CODE_STRIDE
        lane = lax.broadcasted_iota(jnp.int32, (tm, LANES), 1)

        def scatter(blk, rows):
            onehot = jnp.concatenate(
                [jnp.where(lane + (blk * rb + part * LANES) == rank, 1.0, 0.0) for part in range(rb // LANES)],
                axis=-1).astype(BF16)
            acc_s[...] += _dot(onehot, ys_s[rows, :].astype(BF16))
        for_blocks(scatter)

    @pl.when((g == MOE_GROUPS - 1) & (e == MOE_SPLIT - 1))
    def _():
        xo = x1_ref[...] + mod_ref[5:6, :] * acc_s[...]
        o_ref[...] = _rms(xo, fn_ref[...]) if final else xo


def _moe(h2, gates, x1, mod, lw, final_norm, *, per_batch_mod, final, rt):
    B, S, D = x1.shape
    tm = min(S, 1024)
    nt = S // tm
    sub = tm // rt
    info = gates[..., GROUP_LANE:COUNT_LANE + MOE_GROUPS].astype(jnp.int32)
    grp = info[..., 0].reshape(B, nt, sub, rt)
    sub_counts = info[:, rt - 1::rt, 2:].reshape(B, nt, sub, MOE_GROUPS)
    offsets = jnp.cumsum(sub_counts, axis=2) - sub_counts
    onehot = grp[..., None] == jnp.arange(MOE_GROUPS, dtype=jnp.int32)
    rank = info[..., 1].reshape(B, nt, sub, rt) + jnp.sum(jnp.where(onehot, offsets[:, :, :, None, :], 0), axis=-1)
    counts = jnp.sum(sub_counts, axis=2).reshape(-1)
    code = grp * MOE_CODE_STRIDE + rank
    code_row = code.reshape(B, nt, 1, tm)

    tok = lambda w: pl.BlockSpec((None, tm, w), lambda b, s, g, e, c: (b, s, 0))
    wspec = lambda a: pl.BlockSpec((None,) + a.shape[1:], lambda b, s, g, e, c: (g * MOE_SPLIT + e, 0, 0))
    mod_map = (lambda b, s, g, e, c: (b, 0, 0)) if per_batch_mod else (lambda b, s, g, e, c: (0, 0, 0))
    grid_spec = pltpu.PrefetchScalarGridSpec(
        num_scalar_prefetch=1,
        grid=(B, nt, MOE_GROUPS, MOE_SPLIT),
        in_specs=[
            tok(D), tok(LANES),
            pl.BlockSpec((None, None, 1, tm), lambda b, s, g, e, c: (b, s, 0, 0)),
            tok(D),
            pl.BlockSpec((None, 6, D), mod_map),
            wspec(lw["moe_wg"]), wspec(lw["moe_wu"]), wspec(lw["moe_wd"]),
            pl.BlockSpec(final_norm.shape, lambda b, s, g, e, c: (0, 0)),
        ],
        out_specs=tok(D),
        scratch_shapes=[
            pltpu.VMEM((tm, D), BF16),
            pltpu.VMEM((tm, LANES), F32),
            pltpu.VMEM((tm, D), F32),
            pltpu.VMEM((tm, D), F32),
        ],
    )
    return pl.pallas_call(
        functools.partial(_moe_kernel, final=final, n_tiles=nt),
        grid_spec=grid_spec,
        out_shape=jax.ShapeDtypeStruct((B, S, D), F32),
        compiler_params=_cparams(("arbitrary", "arbitrary", "arbitrary", "arbitrary")),
        name="moe",
    )(counts, h2, gates, code_row, x1, mod, lw["moe_wg"], lw["moe_wu"], lw["moe_wd"], final_norm)


def _swap_halves(w, n_heads, dim):
    k = w.shape[0]
    w = w.reshape(k, n_heads, 2, dim // 2)
    return w[:, :, ::-1, :].reshape(k, n_heads * dim)


def _side_by_side(w):
    n_exp = MOE_PER_GROUP // MOE_SPLIT
    w = w.astype(BF16).reshape(MOE_GROUPS, MOE_SPLIT, n_exp, D_MODEL, MOE_HIDDEN)
    return jnp.transpose(w, (0, 1, 3, 2, 4)).reshape(MOE_GROUPS * MOE_SPLIT, D_MODEL, n_exp * MOE_HIDDEN)


def _pad_cols(w, width):
    return jnp.pad(w, ((0, 0), (0, width - w.shape[1])))


def _rope_tables(n_tokens, rot_dim, reps):
    t = jnp.arange(n_tokens)
    row = (t // GRID_W).astype(F32)
    col = (t % GRID_W).astype(F32)
    n_freq = rot_dim // 4
    inv_freq = ROPE_BASE ** (-jnp.arange(n_freq, dtype=F32) / n_freq)
    ang = jnp.concatenate([row[:, None] * inv_freq, col[:, None] * inv_freq], axis=-1)
    cos, sin = jnp.cos(ang), jnp.sin(ang)
    c = jnp.tile(jnp.concatenate([cos, cos], axis=-1), (1, reps))
    s = jnp.tile(jnp.concatenate([-sin, sin], axis=-1), (1, reps))
    return _pad_cols(c, LANES), _pad_cols(s, LANES)


def _layer_weights(l, P, q_abs, ab_re, ab_im, bb_re, bb_im):
    w_in = P["w_in"][l]
    seg = {}
    o = 0
    for name, width in (("u", SSM_CH), ("ql", MLA_Q_RANK), ("kvl", MLA_KV_RANK), ("kr", MLA_ROPE),
                        ("qs", SWA_HEADS * HEAD_DIM), ("ks", SWA_KV_HEADS * HEAD_DIM),
                        ("vs", SWA_KV_HEADS * HEAD_DIM)):
        seg[name] = w_in[:, o:o + width]
        o += width
    ctx_cols = [seg["u"], seg["ql"], seg["kvl"], seg["qs"], seg["ks"], seg["vs"], _pad_cols(seg["kr"], LANES)]
    lat_cols = ctx_cols + [
        _swap_halves(seg["qs"], SWA_HEADS, HEAD_DIM),
        _swap_halves(seg["ks"], SWA_KV_HEADS, HEAD_DIM),
        _pad_cols(_swap_halves(seg["kr"], 1, MLA_ROPE), LANES),
    ]
    w_qb = P["w_mla_qb"][l]
    w_rope = w_qb[:, :, MLA_NOPE:]
    w_rope_sw = w_rope.reshape(MLA_Q_RANK, MLA_HEADS, 2, MLA_ROPE // 2)[:, :, ::-1, :].reshape(w_rope.shape)
    flat_t = lambda w: w.reshape(MLA_Q_RANK, MLA_HEADS * MLA_ROPE).T
    wq_abs = jnp.transpose(q_abs[l], (0, 2, 1)).reshape(Q_ABS, MLA_Q_RANK)
    wq_ctx = jnp.concatenate([wq_abs, flat_t(w_rope)], axis=0)
    wq_lat = jnp.concatenate([wq_ctx, flat_t(w_rope_sw)], axis=0)

    eye = jnp.eye(SSM_GROUPS, dtype=F32)

    def block_diag_b(bb):
        bb = bb.reshape(2, SSM_GROUPS, SSM_STATE, SSM_GROUP)
        return jnp.einsum("dgpc,gh->dgchp", bb, eye).reshape(2, SSM_CH, SSM_N)

    def block_diag_c(cc):
        return jnp.einsum("dgcp,gh->dgphc", cc, eye).reshape(2, SSM_N, SSM_CH)

    sl = slice(2 * l, 2 * l + 2)
    w_router = jnp.concatenate([P["moe_w_expert"][l], P["moe_w_group"][l]], axis=1)
    b_router = jnp.concatenate([P["moe_b_expert"][l], P["moe_b_group"][l]])
    row = lambda v: v.reshape(1, -1)
    return dict(
        norm1=row(P["norm1"][l]), norm2=row(P["norm2"][l]),
        win_ctx=jnp.concatenate(ctx_cols, axis=1).astype(BF16),
        win_lat=jnp.concatenate(lat_cols, axis=1).astype(BF16),
        q_norm=row(P["mla_q_norm"][l]), kv_norm=row(P["mla_kv_norm"][l]),
        wq_ctx=wq_ctx.astype(BF16), wq_lat=wq_lat.astype(BF16),
        wv=jnp.transpose(P["w_mla_kvb"][l][:, :, MLA_NOPE:], (1, 0, 2)).astype(BF16),
        ssm_wb=jnp.concatenate([block_diag_b(bb_re[sl]), block_diag_b(bb_im[sl])], axis=2).astype(BF16),
        ssm_a=jnp.concatenate([ab_re[sl], ab_im[sl]], axis=1).reshape(2, 1, 2 * SSM_N),
        ssm_wc=jnp.concatenate([block_diag_c(P["ssm_c_re"][l]), -block_diag_c(P["ssm_c_im"][l])],
                               axis=1).astype(BF16),
        ssm_d=row(P["ssm_d"][l]), w_glu=P["w_ssm_glu"][l].astype(BF16),
        gn_ssm=row(P["gn_ssm"][l]), gn_mla=row(P["gn_mla"][l]), gn_swa=row(P["gn_swa"][l]),
        w_out=P["w_out"][l].astype(BF16),
        w_router=_pad_cols(w_router, LANES), b_router=_pad_cols(row(b_router), LANES),
        moe_wg=_side_by_side(P["moe_w_gate"][l]), moe_wu=_side_by_side(P["moe_w_up"][l]),
        moe_wd=P["moe_w_down"][l].astype(BF16).reshape(
            MOE_GROUPS * MOE_SPLIT, MOE_PER_GROUP // MOE_SPLIT * MOE_HIDDEN, D_MODEL),
        sink=P["swa_sink"][l],
    )


def _layer(x, mod, lw, final_norm, *, tables, ctx, per_batch_mod, final):
    B, S, _ = x.shape
    context_pass = ctx is None
    pre = _pre(x, mod, lw, tables, per_batch_mod=per_batch_mod, emit_ctx=context_pass)
    u_tm, qt, kcat, ckvt, qs, ks, vs = pre[:7]
    nb = B // SUBLANES
    if context_pass:
        h0 = jnp.zeros((2, nb, SUBLANES, 2 * SSM_N), F32)
        y2, hfin = _ssm(u_tm, lw, h0, B, S)
        o_mla = _mla(qt, kcat, ckvt, None, None, lw["wv"])
        o_swa = _swa(lw["sink"], qs, ks, vs, None, None)
        state = hfin.reshape(2, B, 2, SSM_GROUPS, SSM_STATE).transpose(1, 0, 2, 3, 4)
        new_ctx = (pre[7], pre[8], pre[9].reshape(B, S, SWA_KV_HEADS, HEAD_DIM),
                   pre[10].reshape(B, S, SWA_KV_HEADS, HEAD_DIM), state)
    else:
        kcat_c, ckvt_c, ks_c, vs_c, h0 = ctx
        y2, _ = _ssm(u_tm, lw, h0, B, S)
        o_mla = _mla(qt, kcat_c, ckvt_c, kcat, ckvt, lw["wv"])
        o_swa = _swa(lw["sink"], qs, ks_c, vs_c, ks, vs)
        new_ctx = None
    x1, h2, gates = _post(x, u_tm, y2, o_mla, o_swa, mod, lw, per_batch_mod=per_batch_mod)
    if not per_batch_mod:
        shp = lambda a: a.reshape(1, B * S, a.shape[-1])
        xo = _moe(shp(h2), shp(gates), shp(x1), mod, lw, final_norm, per_batch_mod=False, final=final,
                  rt=min(S, POST_TILE))
        xo = xo.reshape(B, S, D_MODEL)
    else:
        xo = _moe(h2, gates, x1, mod, lw, final_norm, per_batch_mod=True, final=final, rt=min(S, POST_TILE))
    return xo, new_ctx


def kernel(x_prompt, x_sample, c, cache_mla_ckv, cache_mla_krope, cache_swa_k, cache_swa_v, state_ssm, c_ctx, w_ada, b_ada, norm1, norm2, w_in, ssm_a_re, ssm_a_im, ssm_log_dt, ssm_b_re, ssm_b_im, ssm_c_re, ssm_c_im, ssm_d, w_ssm_glu, mla_q_norm, w_mla_qb, mla_kv_norm, w_mla_kvb, swa_sink, gn_ssm, gn_mla, gn_swa, w_out, moe_w_group, moe_b_group, moe_w_expert, moe_b_expert, moe_w_gate, moe_w_up, moe_w_down, final_norm):
    P = dict(w_ada=w_ada, b_ada=b_ada, norm1=norm1, norm2=norm2, w_in=w_in,
             ssm_c_re=ssm_c_re, ssm_c_im=ssm_c_im, ssm_d=ssm_d, w_ssm_glu=w_ssm_glu,
             mla_q_norm=mla_q_norm, w_mla_qb=w_mla_qb, mla_kv_norm=mla_kv_norm, w_mla_kvb=w_mla_kvb,
             swa_sink=swa_sink, gn_ssm=gn_ssm, gn_mla=gn_mla, gn_swa=gn_swa, w_out=w_out,
             moe_w_group=moe_w_group, moe_b_group=moe_b_group, moe_w_expert=moe_w_expert,
             moe_b_expert=moe_b_expert, moe_w_gate=moe_w_gate, moe_w_up=moe_w_up, moe_w_down=moe_w_down)
    n_dec = c.shape[0]
    n_cond = 2 * SUBLANES
    conds = jnp.zeros((n_cond, D_MODEL), F32).at[:n_dec].set(c).at[n_dec].set(c_ctx)
    mods = _modulation(conds, w_ada, b_ada).reshape(DEPTH, n_cond, 6, D_MODEL)

    ab_re, ab_im, bb_re, bb_im = _ssm_discretise(ssm_a_re, ssm_a_im, ssm_log_dt, ssm_b_re, ssm_b_im)
    q_abs = _absorb_q(jnp.transpose(w_mla_qb[..., :MLA_NOPE], (0, 2, 1, 3)),
                      jnp.transpose(w_mla_kvb[..., :MLA_NOPE], (0, 2, 1, 3)))
    lws = [_layer_weights(l, P, q_abs, ab_re.reshape(2 * DEPTH, SSM_N), ab_im.reshape(2 * DEPTH, SSM_N),
                          bb_re, bb_im) for l in range(DEPTH)]
    fnorm = final_norm.reshape(1, D_MODEL)

    xp = x_prompt
    ctx_states = []
    for l in range(DEPTH):
        xp, new = _layer(xp, mods[l, n_dec:n_dec + 1], lws[l], fnorm, tables=None, ctx=None,
                         per_batch_mod=False, final=l == DEPTH - 1)
        ctx_states.append(new)
    outs_ctx = tuple(jnp.stack([s[k] for s in ctx_states], axis=1) for k in range(5))

    n_lat = x_sample.shape[1]
    cm, sm = _rope_tables(n_lat, MLA_ROPE, 1)
    cs, ss = _rope_tables(n_lat, HEAD_DIM, LANES // HEAD_DIM)
    tables = (cm, sm, cm[:, :MLA_ROPE].T, sm[:, :MLA_ROPE].T, cs, ss)
    xs = x_sample
    past = cache_mla_ckv.shape[2]
    for l in range(DEPTH):
        kcat_c = jnp.concatenate(
            [cache_mla_ckv[:, l], cache_mla_krope[:, l],
             jnp.zeros((n_dec, past, LANES - MLA_ROPE), F32)], axis=-1).astype(BF16)
        ks_c = cache_swa_k[:, l].reshape(n_dec, past, LANES).astype(BF16)
        vs_c = cache_swa_v[:, l].reshape(n_dec, past, LANES).astype(BF16)
        h0 = state_ssm[:, l].transpose(1, 0, 2, 3, 4).reshape(2, n_dec // SUBLANES, SUBLANES, 2 * SSM_N)
        ckvt_c = jnp.transpose(cache_mla_ckv[:, l], (0, 2, 1)).astype(BF16)
        xs, _ = _layer(xs, mods[l, :n_dec], lws[l], fnorm, tables=tables, ctx=(kcat_c, ckvt_c, ks_c, vs_c, h0),
                       per_batch_mod=True, final=l == DEPTH - 1)
    return (xp, xs) + outs_ctx
```

```python
import functools
import math

import jax
import jax.numpy as jnp
from jax import lax
from jax.experimental import pallas as pl
from jax.experimental.pallas import tpu as pltpu

F32 = jnp.float32
BF16 = jnp.bfloat16

D_MODEL = 1024
DEPTH = 4
GRID_W = 64
HEAD_DIM = 64
SSM_CH = 256
SSM_GROUP = 16
SSM_GROUPS = SSM_CH // SSM_GROUP
SSM_STATE = 64
SSM_N = SSM_GROUPS * SSM_STATE
MLA_HEADS = 6
MLA_Q_RANK = 256
MLA_KV_RANK = 128
MLA_NOPE = 64
MLA_ROPE = 32
MLA_V = 64
SWA_HEADS = 6
SWA_KV_HEADS = 2
SWA_GROUP = SWA_HEADS // SWA_KV_HEADS
SWA_WINDOW = 128
SWA_BLOCK = 128
MOE_GROUPS = 4
MOE_PER_GROUP = 8
MOE_EXPERTS = MOE_GROUPS * MOE_PER_GROUP
MOE_HIDDEN = 256
GROUP_LANE = MOE_EXPERTS
RANK_LANE = GROUP_LANE + 1
COUNT_LANE = GROUP_LANE + 2
POST_TILE = 512
MOE_SPLIT = 2
MOE_TILE = 1024
MOE_ROW_BLOCK = 128
MOE_SCATTER_BLOCK = 256
MOE_CODE_STRIDE = 1 << 16
ROPE_BASE = 10000.0
EPS = 1e-6
MLA_SCALE = 1.0 / math.sqrt(MLA_NOPE + MLA_ROPE)
MLA_SCALE_LOG2 = MLA_SCALE * math.log2(math.e)
SWA_SCALE = 1.0 / math.sqrt(HEAD_DIM)

LANES = 128
SUBLANES = 8
VMEM_LIMIT = 52 * 1024 * 1024

OFF_U = 0
OFF_QL = OFF_U + SSM_CH
OFF_KVL = OFF_QL + MLA_Q_RANK
OFF_QS = OFF_KVL + MLA_KV_RANK
OFF_KS = OFF_QS + SWA_HEADS * HEAD_DIM
OFF_VS = OFF_KS + SWA_KV_HEADS * HEAD_DIM
OFF_KR = OFF_VS + SWA_KV_HEADS * HEAD_DIM
N_PACK_CTX = OFF_KR + LANES
OFF_QS_SW = N_PACK_CTX
OFF_KS_SW = OFF_QS_SW + SWA_HEADS * HEAD_DIM
OFF_KR_SW = OFF_KS_SW + SWA_KV_HEADS * HEAD_DIM
N_PACK_LAT = OFF_KR_SW + LANES
Q_ABS = MLA_HEADS * MLA_KV_RANK
MLA_QK = 2 * LANES


def _cparams(sem):
    return pltpu.CompilerParams(dimension_semantics=sem, vmem_limit_bytes=VMEM_LIMIT)


def _dot(a, b):
    return jnp.dot(a, b, preferred_element_type=F32)


def _dot_nt(a, b):
    return lax.dot_general(a, b, (((1,), (1,)), ((), ())), preferred_element_type=F32)


def _split(x):
    hi = x.astype(BF16)
    lo = (x - hi.astype(F32)).astype(BF16)
    return hi, lo


def _dot3(a, b):
    ah, al = _split(a)
    bh, bl = _split(b)
    return _dot(ah, bh) + _dot(al, bh) + _dot(ah, bl)


def _tree(op, x3):
    parts = [x3[i] for i in range(x3.shape[0])]
    while len(parts) > 1:
        parts = [op(parts[i], parts[i + 1]) for i in range(0, len(parts), 2)]
    return parts[0]


def _rms(x, g):
    return x * lax.rsqrt(jnp.mean(x * x, axis=-1, keepdims=True) + EPS) * g


def _silu(x):
    return x * jax.nn.sigmoid(x)


def _gelu_tanh(x):
    return 0.5 * x * (1.0 + jnp.tanh(math.sqrt(2.0 / math.pi) * (x + 0.044715 * (x * x * x))))


def _mod_kernel(c_ref, w_ref, b_ref, o_ref):
    o_ref[...] = _dot3(_silu(c_ref[...]), w_ref[...]) + b_ref[...]


def _modulation(conds, w_ada, b_ada):
    n = conds.shape[0]
    tn = 1536
    return pl.pallas_call(
        _mod_kernel,
        grid=(DEPTH, 6 * D_MODEL // tn),
        in_specs=[
            pl.BlockSpec((n, D_MODEL), lambda l, j: (0, 0)),
            pl.BlockSpec((None, D_MODEL, tn), lambda l, j: (l, 0, j)),
            pl.BlockSpec((None, 1, tn), lambda l, j: (l, 0, j)),
        ],
        out_specs=pl.BlockSpec((None, n, tn), lambda l, j: (l, 0, j)),
        out_shape=jax.ShapeDtypeStruct((DEPTH, n, 6 * D_MODEL), F32),
        compiler_params=_cparams(("arbitrary", "arbitrary")),
        name="modulation",
    )(conds, w_ada, b_ada.reshape(DEPTH, 1, 6 * D_MODEL))


def _ssm_disc_kernel(are_ref, aim_ref, ldt_ref, bre_ref, bim_ref, abre_ref, abim_ref, bbre_ref, bbim_ref):
    lam_re = are_ref[...]
    lam_im = aim_ref[...]
    dt = jnp.exp(ldt_ref[...])
    z_re = lam_re * dt
    z_im = lam_im * dt
    mag = jnp.exp(z_re)
    ab_re = mag * jnp.cos(z_im)
    ab_im = mag * jnp.sin(z_im)
    den = lam_re * lam_re + lam_im * lam_im
    f_re = ((ab_re - 1.0) * lam_re + ab_im * lam_im) / den
    f_im = (ab_im * lam_re - (ab_re - 1.0) * lam_im) / den
    b_re = bre_ref[...]
    b_im = bim_ref[...]
    abre_ref[...] = ab_re
    abim_ref[...] = ab_im
    bbre_ref[...] = f_re * b_re - f_im * b_im
    bbim_ref[...] = f_re * b_im + f_im * b_re


def _ssm_discretise(a_re, a_im, log_dt, b_re, b_im):
    n = DEPTH * 2
    col = lambda v: v.reshape(n, SSM_N, 1)
    ldt = jnp.broadcast_to(log_dt[..., None], (DEPTH, 2, SSM_GROUPS, SSM_STATE))
    cspec = pl.BlockSpec((None, SSM_N, 1), lambda i: (i, 0, 0))
    bspec = pl.BlockSpec((None, SSM_N, SSM_GROUP), lambda i: (i, 0, 0))
    return pl.pallas_call(
        _ssm_disc_kernel,
        grid=(n,),
        in_specs=[cspec, cspec, cspec, bspec, bspec],
        out_specs=[cspec, cspec, bspec, bspec],
        out_shape=[jax.ShapeDtypeStruct((n, SSM_N, 1), F32)] * 2
        + [jax.ShapeDtypeStruct((n, SSM_N, SSM_GROUP), F32)] * 2,
        compiler_params=_cparams(("arbitrary",)),
        name="ssm_discretise",
    )(col(a_re), col(a_im), col(ldt), b_re.reshape(n, SSM_N, SSM_GROUP), b_im.reshape(n, SSM_N, SSM_GROUP))


def _absorb_kernel(wq_ref, wk_ref, o_ref):
    a = wq_ref[...]
    b = wk_ref[...]
    ah, al = _split(a)
    bh, bl = _split(b)
    o_ref[...] = _dot_nt(ah, bh) + _dot_nt(al, bh) + _dot_nt(ah, bl)


def _absorb_q(wq_nope, wk_nope):
    return pl.pallas_call(
        _absorb_kernel,
        grid=(DEPTH, MLA_HEADS),
        in_specs=[
            pl.BlockSpec((None, None, MLA_Q_RANK, MLA_NOPE), lambda l, h: (l, h, 0, 0)),
            pl.BlockSpec((None, None, MLA_KV_RANK, MLA_NOPE), lambda l, h: (l, h, 0, 0)),
        ],
        out_specs=pl.BlockSpec((None, None, MLA_Q_RANK, MLA_KV_RANK), lambda l, h: (l, h, 0, 0)),
        out_shape=jax.ShapeDtypeStruct((DEPTH, MLA_HEADS, MLA_Q_RANK, MLA_KV_RANK), F32),
        compiler_params=_cparams(("arbitrary", "arbitrary")),
        name="mla_absorb",
    )(wq_nope, wk_nope)


def _pre_kernel(*refs, rope, emit_ctx):
    it = iter(refs)
    x_ref, mod_ref, n1_ref, win_ref, qn_ref, kvn_ref, wq_ref = (next(it) for _ in range(7))
    if rope:
        cm_ref, sm_ref, cmt_ref, smt_ref, cs_ref, ss_ref = (next(it) for _ in range(6))
    u_ref, qt_ref, kcat_ref, ckvt_ref, qs_ref, ks_ref, vs_ref = (next(it) for _ in range(7))
    if emit_ctx:
        ckv_o, kr_o, ks_o, vs_o = (next(it) for _ in range(4))

    x = x_ref[...]
    mod = mod_ref[...]
    h = _rms(x, n1_ref[...]) * (1.0 + mod[1:2]) + mod[0:1]
    proj = _dot(h.astype(BF16), win_ref[...])

    u_ref[...] = proj[:, OFF_U:OFF_U + SSM_CH]

    qln = _rms(proj[:, OFF_QL:OFF_QL + MLA_Q_RANK], qn_ref[...]).astype(BF16)
    qall = _dot_nt(wq_ref[...], qln)
    n_rope = MLA_HEADS * MLA_ROPE
    zero_rows = jnp.zeros((MLA_QK - MLA_KV_RANK - MLA_ROPE, qall.shape[1]), BF16)
    for i in range(MLA_HEADS):
        qa = qall[i * MLA_KV_RANK:(i + 1) * MLA_KV_RANK]
        qr = qall[Q_ABS + i * MLA_ROPE:Q_ABS + (i + 1) * MLA_ROPE]
        if rope:
            qr_sw = qall[Q_ABS + n_rope + i * MLA_ROPE:Q_ABS + n_rope + (i + 1) * MLA_ROPE]
            qr = qr * cmt_ref[...] + qr_sw * smt_ref[...]
        base = i * MLA_QK
        qt_ref[base:base + MLA_KV_RANK, :] = (qa * MLA_SCALE_LOG2).astype(BF16)
        qt_ref[base + MLA_KV_RANK:base + MLA_KV_RANK + MLA_ROPE, :] = (qr * MLA_SCALE_LOG2).astype(BF16)
        qt_ref[base + MLA_KV_RANK + MLA_ROPE:base + MLA_QK, :] = zero_rows

    ckv = _rms(proj[:, OFF_KVL:OFF_KVL + MLA_KV_RANK], kvn_ref[...])
    kr = proj[:, OFF_KR:OFF_KR + LANES]
    if emit_ctx:
        ckv_o[...] = ckv
        kr_o[...] = kr[:, :MLA_ROPE]
    if rope:
        kr = kr * cm_ref[...] + proj[:, OFF_KR_SW:OFF_KR_SW + LANES] * sm_ref[...]
    kcat_ref[...] = jnp.concatenate([ckv, kr], axis=-1).astype(BF16)
    ckvt_ref[...] = ckv.T.astype(BF16)

    qs = proj[:, OFF_QS:OFF_QS + SWA_HEADS * HEAD_DIM]
    ks = proj[:, OFF_KS:OFF_KS + SWA_KV_HEADS * HEAD_DIM]
    vs = proj[:, OFF_VS:OFF_VS + SWA_KV_HEADS * HEAD_DIM]
    if emit_ctx:
        ks_o[...] = ks
        vs_o[...] = vs
    if rope:
        cs = cs_ref[...]
        ss = ss_ref[...]
        qs_sw = proj[:, OFF_QS_SW:OFF_QS_SW + SWA_HEADS * HEAD_DIM]
        ks = ks * cs + proj[:, OFF_KS_SW:OFF_KS_SW + SWA_KV_HEADS * HEAD_DIM] * ss
        qs = jnp.concatenate(
            [qs[:, i * LANES:(i + 1) * LANES] * cs + qs_sw[:, i * LANES:(i + 1) * LANES] * ss
             for i in range(SWA_HEADS * HEAD_DIM // LANES)], axis=-1)
    qs_ref[...] = (qs * SWA_SCALE).astype(BF16)
    ks_ref[...] = ks.astype(BF16)
    vs_ref[...] = vs.astype(BF16)


def _pre(x, mod, lw, tables, *, per_batch_mod, emit_ctx):
    B, S, D = x.shape
    rope = tables is not None
    ts = min(S, 512)
    n_pack = N_PACK_LAT if rope else N_PACK_CTX
    win = lw["win_lat"] if rope else lw["win_ctx"]
    wq = lw["wq_lat"] if rope else lw["wq_ctx"]
    tok = lambda w: pl.BlockSpec((None, ts, w), lambda b, s: (b, s, 0))
    full = lambda a: pl.BlockSpec(a.shape, lambda b, s: (0,) * a.ndim)
    in_specs = [
        tok(D),
        pl.BlockSpec((None, 6, D), (lambda b, s: (b, 0, 0)) if per_batch_mod else (lambda b, s: (0, 0, 0))),
        full(lw["norm1"]), full(win), full(lw["q_norm"]), full(lw["kv_norm"]), full(wq),
    ]
    args = [x, mod, lw["norm1"], win, lw["q_norm"], lw["kv_norm"], wq]
    if rope:
        row_tab = pl.BlockSpec((ts, LANES), lambda b, s: (s, 0))
        col_tab = pl.BlockSpec((MLA_ROPE, ts), lambda b, s: (0, s))
        in_specs += [row_tab, row_tab, col_tab, col_tab, row_tab, row_tab]
        args += list(tables)
    feat = lambda w: pl.BlockSpec((None, w, ts), lambda b, s: (b, 0, s))
    out_specs = [
        tok(SSM_CH),
        feat(MLA_HEADS * MLA_QK), tok(MLA_QK), feat(MLA_KV_RANK), tok(SWA_HEADS * HEAD_DIM), tok(LANES), tok(LANES),
    ]
    out_shape = [
        jax.ShapeDtypeStruct((B, S, SSM_CH), F32),
        jax.ShapeDtypeStruct((B, MLA_HEADS * MLA_QK, S), BF16),
        jax.ShapeDtypeStruct((B, S, MLA_QK), BF16),
        jax.ShapeDtypeStruct((B, MLA_KV_RANK, S), BF16),
        jax.ShapeDtypeStruct((B, S, SWA_HEADS * HEAD_DIM), BF16),
        jax.ShapeDtypeStruct((B, S, LANES), BF16),
        jax.ShapeDtypeStruct((B, S, LANES), BF16),
    ]
    if emit_ctx:
        out_specs += [tok(MLA_KV_RANK), tok(MLA_ROPE), tok(LANES), tok(LANES)]
        out_shape += [
            jax.ShapeDtypeStruct((B, S, MLA_KV_RANK), F32),
            jax.ShapeDtypeStruct((B, S, MLA_ROPE), F32),
            jax.ShapeDtypeStruct((B, S, LANES), F32),
            jax.ShapeDtypeStruct((B, S, LANES), F32),
        ]
    return pl.pallas_call(
        functools.partial(_pre_kernel, rope=rope, emit_ctx=emit_ctx),
        grid=(B, S // ts),
        in_specs=in_specs,
        out_specs=out_specs,
        out_shape=out_shape,
        compiler_params=_cparams(("arbitrary", "arbitrary")),
        name="pre_lat" if rope else "pre_ctx",
    )(*args)


def _ssm_kernel(u_ref, wb_ref, a_ref, wc_ref, h0_ref, y_ref, hfin_ref, hre_s, him_s, st_s, ut_s, *, tc):
    d = pl.program_id(0)
    i = pl.program_id(2)
    n = pl.num_programs(2)

    @pl.when(i == 0)
    def _():
        st_s[...] = h0_ref[...]

    cblk = SSM_CH // LANES
    ub = u_ref[...]
    for b in range(SUBLANES):
        for j in range(cblk):
            ut_s[j, pl.ds(b, tc, stride=SUBLANES), :] = ub[b, :, j * LANES:(j + 1) * LANES]
    u = jnp.concatenate([ut_s[j] for j in range(cblk)], axis=-1).astype(BF16)
    bu = _dot(u, wb_ref[...])
    hre_s[...] = bu[:, :SSM_N]
    him_s[...] = bu[:, SSM_N:]

    a = a_ref[...]
    a_re = jnp.broadcast_to(a[:, :SSM_N], (SUBLANES, SSM_N))
    a_im = jnp.broadcast_to(a[:, SSM_N:], (SUBLANES, SSM_N))
    st = st_s[...]

    def step(t, carry):
        h_re, h_im = carry
        tt = jnp.where(d == 0, t, tc - 1 - t)
        rows = pl.ds(pl.multiple_of(tt * SUBLANES, SUBLANES), SUBLANES)
        n_re = a_re * h_re - a_im * h_im + hre_s[rows, :]
        n_im = a_re * h_im + a_im * h_re + him_s[rows, :]
        hre_s[rows, :] = n_re
        him_s[rows, :] = n_im
        return n_re, n_im

    h_re, h_im = lax.fori_loop(0, tc, step, (st[:, :SSM_N], st[:, SSM_N:]), unroll=4)
    st_s[...] = jnp.concatenate([h_re, h_im], axis=-1)

    wc = wc_ref[...]
    y = _dot(hre_s[...].astype(BF16), wc[:SSM_N]) + _dot(him_s[...].astype(BF16), wc[SSM_N:])
    for j in range(cblk):
        ut_s[j] = y[:, j * LANES:(j + 1) * LANES]
    for b in range(SUBLANES):
        y_ref[b] = jnp.concatenate([ut_s[j, pl.ds(b, tc, stride=SUBLANES), :] for j in range(cblk)], axis=-1)

    @pl.when(i == n - 1)
    def _():
        hfin_ref[...] = st_s[...]


def _ssm(u, lw, h0, B, S):
    nb = B // SUBLANES
    tc = 128
    nchunk = S // tc
    chunk = lambda d, i: i + d * (nchunk - 1 - 2 * i)
    y, hfin = pl.pallas_call(
        functools.partial(_ssm_kernel, tc=tc),
        grid=(2, nb, nchunk),
        in_specs=[
            pl.BlockSpec((SUBLANES, tc, SSM_CH), lambda d, b, i: (b, chunk(d, i), 0)),
            pl.BlockSpec((None, SSM_CH, 2 * SSM_N), lambda d, b, i: (d, 0, 0)),
            pl.BlockSpec((None, 1, 2 * SSM_N), lambda d, b, i: (d, 0, 0)),
            pl.BlockSpec((None, 2 * SSM_N, SSM_CH), lambda d, b, i: (d, 0, 0)),
            pl.BlockSpec((None, None, SUBLANES, 2 * SSM_N), lambda d, b, i: (d, b, 0, 0)),
        ],
        out_specs=[
            pl.BlockSpec((None, SUBLANES, tc, SSM_CH), lambda d, b, i: (d, b, chunk(d, i), 0)),
            pl.BlockSpec((None, None, SUBLANES, 2 * SSM_N), lambda d, b, i: (d, b, 0, 0)),
        ],
        out_shape=[
            jax.ShapeDtypeStruct((2, B, S, SSM_CH), F32),
            jax.ShapeDtypeStruct((2, nb, SUBLANES, 2 * SSM_N), F32),
        ],
        scratch_shapes=[
            pltpu.VMEM((tc * SUBLANES, SSM_N), F32),
            pltpu.VMEM((tc * SUBLANES, SSM_N), F32),
            pltpu.VMEM((SUBLANES, 2 * SSM_N), F32),
            pltpu.VMEM((SSM_CH // LANES, tc * SUBLANES, LANES), F32),
        ],
        compiler_params=_cparams(("arbitrary", "arbitrary", "arbitrary")),
        name="ssm_scan",
    )(u, lw["ssm_wb"], lw["ssm_a"], lw["ssm_wc"], h0)
    return y, hfin


def _mla_kernel(*refs, with_latent, tk):
    if with_latent:
        qt_ref, ka_ref, vat_ref, kb_ref, vbt_ref, wv_ref, o_ref = refs
    else:
        qt_ref, ka_ref, vat_ref, wv_ref, o_ref = refs
    tq = qt_ref.shape[1]
    q_of = lambda hd: qt_ref[hd * MLA_QK:(hd + 1) * MLA_QK, :]

    def tile(state, k, vt):
        new_state = []
        s_next = _dot(k, q_of(0))
        for hd in range(MLA_HEADS):
            m8, l8, acc = state[hd]
            s = s_next
            if hd + 1 < MLA_HEADS:
                s_next = _dot(k, q_of(hd + 1))
            s3 = s.reshape(s.shape[0] // SUBLANES, SUBLANES, tq)
            mloc = jnp.max(_tree(jnp.maximum, s3), axis=0, keepdims=True)
            m8_new = jnp.maximum(m8, jnp.broadcast_to(mloc, (SUBLANES, tq)))
            alpha8 = jnp.exp2(m8 - m8_new)
            p3 = jnp.exp2(s3 - m8_new[None])
            l8 = alpha8 * l8 + _tree(jnp.add, p3)
            pv = _dot(vt, p3.reshape(s.shape).astype(BF16))
            acc3 = acc.reshape(MLA_KV_RANK // SUBLANES, SUBLANES, tq) * alpha8[None]
            new_state.append((m8_new, l8, acc3.reshape(MLA_KV_RANK, tq) + pv))
        return tuple(new_state)

    init = tuple((jnp.full((SUBLANES, tq), -jnp.inf, F32), jnp.zeros((SUBLANES, tq), F32),
                  jnp.zeros((MLA_KV_RANK, tq), F32)) for _ in range(MLA_HEADS))
    state = tile(init, ka_ref[...], vat_ref[...])
    if with_latent:
        for j in range(kb_ref.shape[0] // tk):
            state = tile(state, kb_ref[j * tk:(j + 1) * tk, :], vbt_ref[:, j * tk:(j + 1) * tk])

    outs = []
    for hd in range(MLA_HEADS):
        _, l8, acc = state[hd]
        o_lat = (acc / jnp.sum(l8, axis=0, keepdims=True)).T.astype(BF16)
        outs.append(_dot(o_lat, wv_ref[hd]))
    o_ref[...] = jnp.concatenate(outs, axis=-1)


def _mla(qt, ka, vat, kb, vbt, wv):
    B, _, S = qt.shape
    tq = 256
    with_latent = kb is not None
    kspec = lambda a: pl.BlockSpec((None,) + a.shape[1:], lambda b, s: (b, 0, 0))
    in_specs = [pl.BlockSpec((None, MLA_HEADS * MLA_QK, tq), lambda b, s: (b, 0, s)), kspec(ka), kspec(vat)]
    args = [qt, ka, vat]
    if with_latent:
        in_specs += [kspec(kb), kspec(vbt)]
        args += [kb, vbt]
    in_specs.append(pl.BlockSpec(wv.shape, lambda b, s: (0, 0, 0)))
    args.append(wv)
    return pl.pallas_call(
        functools.partial(_mla_kernel, with_latent=with_latent, tk=4096),
        grid=(B, S // tq),
        in_specs=in_specs,
        out_specs=pl.BlockSpec((None, tq, MLA_HEADS * MLA_V), lambda b, s: (b, s, 0)),
        out_shape=jax.ShapeDtypeStruct((B, S, MLA_HEADS * MLA_V), F32),
        compiler_params=_cparams(("arbitrary", "arbitrary")),
        name="mla_lat" if with_latent else "mla_ctx",
    )(*args)


def _swa_kernel(*refs, with_latent):
    if with_latent:
        sink_ref, q_ref, ka_ref, va_ref, kb_ref, vb_ref, o_ref = refs
    else:
        sink_ref, q_ref, ka_ref, va_ref, o_ref = refs
    tq = q_ref.shape[0]
    q = q_ref[...]
    ka = ka_ref[...]
    va = va_ref[...]
    if with_latent:
        n = pl.program_id(1)
        nblk = kb_ref.shape[0] // SWA_BLOCK
        start = pl.multiple_of(jnp.clip(n - 1, 0, nblk - 3) * SWA_BLOCK, SWA_BLOCK)
        kb = kb_ref[pl.ds(start, 3 * SWA_BLOCK), :]
        vb = vb_ref[pl.ds(start, 3 * SWA_BLOCK), :]
        qpos = n * SWA_BLOCK + lax.broadcasted_iota(jnp.int32, (tq, 3 * SWA_BLOCK), 0)
        kpos = start + lax.broadcasted_iota(jnp.int32, (tq, 3 * SWA_BLOCK), 1)
        valid = jnp.abs(qpos - kpos) <= SWA_WINDOW
    outs = []
    for hd in range(SWA_HEADS):
        kh = hd // SWA_GROUP
        qh = q[:, hd * HEAD_DIM:(hd + 1) * HEAD_DIM]
        ksl = slice(kh * HEAD_DIM, (kh + 1) * HEAD_DIM)
        sink = sink_ref[hd]
        s_a = _dot_nt(qh, ka[:, ksl])
        m = jnp.maximum(jnp.max(s_a, axis=-1, keepdims=True), sink)
        if with_latent:
            s_b = jnp.where(valid, _dot_nt(qh, kb[:, ksl]), -jnp.inf)
            m = jnp.maximum(m, jnp.max(s_b, axis=-1, keepdims=True))
        p_a = jnp.exp(s_a - m)
        den = jnp.sum(p_a, axis=-1, keepdims=True) + jnp.exp(sink - m)
        o = _dot(p_a.astype(BF16), va[:, ksl])
        if with_latent:
            p_b = jnp.exp(s_b - m)
            den = den + jnp.sum(p_b, axis=-1, keepdims=True)
            o = o + _dot(p_b.astype(BF16), vb[:, ksl])
        outs.append(o / den)
    o_ref[...] = jnp.concatenate(outs, axis=-1)


def _swa(sink, q, ka, va, kb, vb):
    B, S, W = q.shape
    with_latent = kb is not None
    tq = SWA_BLOCK if with_latent else S
    tok = pl.BlockSpec((None, tq, W), lambda b, s: (b, s, 0))
    kspec = lambda a: pl.BlockSpec((None,) + a.shape[1:], lambda b, s: (b, 0, 0))
    in_specs = [pl.BlockSpec(memory_space=pltpu.SMEM), tok, kspec(ka), kspec(va)]
    args = [sink, q, ka, va]
    if with_latent:
        in_specs += [kspec(kb), kspec(vb)]
        args += [kb, vb]
    return pl.pallas_call(
        functools.partial(_swa_kernel, with_latent=with_latent),
        grid=(B, S // tq),
        in_specs=in_specs,
        out_specs=tok,
        out_shape=jax.ShapeDtypeStruct((B, S, W), F32),
        compiler_params=_cparams(("arbitrary", "arbitrary")),
        name="swa_lat" if with_latent else "swa_ctx",
    )(*args)


def _route(logits):
    lane = lax.broadcasted_iota(jnp.int32, logits.shape, 1)
    big = jnp.int32(1 << 20)
    is_g = (lane >= MOE_EXPERTS) & (lane < MOE_EXPERTS + MOE_GROUPS)
    lg = jnp.where(is_g, logits, -jnp.inf)
    mg = jnp.max(lg, axis=-1, keepdims=True)
    g_idx = jnp.min(jnp.where(lg == mg, lane - MOE_EXPERTS, big), axis=-1, keepdims=True)
    pg_top = 1.0 / jnp.sum(jnp.exp(lg - mg), axis=-1, keepdims=True)

    is_e = (lane < MOE_EXPERTS) & ((lane // MOE_PER_GROUP) == g_idx)
    le = jnp.where(is_e, logits, -jnp.inf)
    m1 = jnp.max(le, axis=-1, keepdims=True)
    e1 = jnp.min(jnp.where(le == m1, lane, big), axis=-1, keepdims=True)
    z = jnp.sum(jnp.exp(le - m1), axis=-1, keepdims=True)
    le2 = jnp.where(lane == e1, -jnp.inf, le)
    m2 = jnp.max(le2, axis=-1, keepdims=True)
    e2 = jnp.min(jnp.where(le2 == m2, lane, big), axis=-1, keepdims=True)
    p1 = 1.0 / z
    p2 = jnp.exp(m2 - m1) / z
    tot = p1 + p2
    gates = pg_top * (jnp.where(lane == e1, p1 / tot, 0.0) + jnp.where(lane == e2, p2 / tot, 0.0))
    t = logits.shape[0]
    chose = jnp.where(lane == g_idx + COUNT_LANE, 1.0, 0.0)
    tri = jnp.where(lax.broadcasted_iota(jnp.int32, (t, t), 0) >= lax.broadcasted_iota(jnp.int32, (t, t), 1),
                    1.0, 0.0).astype(BF16)
    counts = _dot(tri, chose.astype(BF16))
    rank = jnp.sum(chose * (counts - 1.0), axis=-1, keepdims=True)
    return (gates + counts + jnp.where(lane == GROUP_LANE, g_idx.astype(F32), 0.0)
            + jnp.where(lane == RANK_LANE, rank, 0.0))


def _post_kernel(x_ref, u_ref, yf_ref, yb_ref, om_ref, os_ref, mod_ref, d_ref, wglu_ref, gs_ref, gm_ref,
                 gw_ref, wout_ref, n2_ref, wr_ref, br_ref, x1_ref, h2_ref, gate_ref):
    mod = mod_ref[...]
    y = d_ref[...] * u_ref[...] + yf_ref[...] + yb_ref[...]
    ga = _dot(_gelu_tanh(y).astype(BF16), wglu_ref[...])
    y_ssm = ga[:, :SSM_CH] * jax.nn.sigmoid(ga[:, SSM_CH:])
    wout = wout_ref
    n_mla = MLA_HEADS * MLA_V
    mixed = (_dot(_rms(y_ssm, gs_ref[...]).astype(BF16), wout[0:SSM_CH, :])
             + _dot(_rms(om_ref[...], gm_ref[...]).astype(BF16), wout[SSM_CH:SSM_CH + n_mla, :])
             + _dot(_rms(os_ref[...], gw_ref[...]).astype(BF16), wout[SSM_CH + n_mla:, :]))
    x1 = x_ref[...] + mod[2:3] * mixed
    x1_ref[...] = x1
    h2 = _rms(x1, n2_ref[...]) * (1.0 + mod[4:5]) + mod[3:4]
    h2_ref[...] = h2.astype(BF16)
    gate_ref[...] = _route(_dot3(h2, wr_ref[...]) + br_ref[...])


def _post(x, u_tm, y2, o_mla, o_swa, mod, lw, *, per_batch_mod):
    B, S, D = x.shape
    ts = min(S, POST_TILE)
    tok = lambda w: pl.BlockSpec((None, ts, w), lambda b, s: (b, s, 0))
    full = lambda a: pl.BlockSpec(a.shape, lambda b, s: (0,) * a.ndim)
    names = ["ssm_d", "w_glu", "gn_ssm", "gn_mla", "gn_swa", "w_out", "norm2", "w_router", "b_router"]
    in_specs = [
        tok(D),
        tok(SSM_CH),
        pl.BlockSpec((None, None, ts, SSM_CH), lambda b, s: (0, b, s, 0)),
        pl.BlockSpec((None, None, ts, SSM_CH), lambda b, s: (1, b, s, 0)),
        tok(MLA_HEADS * MLA_V), tok(SWA_HEADS * HEAD_DIM),
        pl.BlockSpec((None, 6, D), (lambda b, s: (b, 0, 0)) if per_batch_mod else (lambda b, s: (0, 0, 0))),
    ] + [full(lw[k]) for k in names]
    return pl.pallas_call(
        _post_kernel,
        grid=(B, S // ts),
        in_specs=in_specs,
        out_specs=[tok(D), tok(D), tok(LANES)],
        out_shape=[
            jax.ShapeDtypeStruct((B, S, D), F32),
            jax.ShapeDtypeStruct((B, S, D), BF16),
            jax.ShapeDtypeStruct((B, S, LANES), F32),
        ],
        compiler_params=_cparams(("arbitrary", "arbitrary")),
        name="post",
    )(x, u_tm, y2, y2, o_mla, o_swa, mod, *[lw[k] for k in names])


def _split3(x):
    hi = x.astype(BF16)
    r = x - hi.astype(F32)
    mid = r.astype(BF16)
    lo = (r - mid.astype(F32)).astype(BF16)
    return hi, mid, lo


def _moe_kernel(cnt_ref, h2_ref, gate_ref, crow_ref, x1_ref, mod_ref, wg_ref, wu_ref, wd_ref, fn_ref,
                o_ref, xs_s, gs_s, ys_s, acc_s, *, final, n_tiles):
    tm, d_model = acc_s.shape
    rb = MOE_ROW_BLOCK
    g = pl.program_id(2)
    e = pl.program_id(3)
    cnt = cnt_ref[(pl.program_id(0) * n_tiles + pl.program_id(1)) * MOE_GROUPS + g]
    n_blocks = (cnt + rb - 1) // rb

    def for_blocks(body, width=rb):
        def step(blk, carry):
            body(blk, pl.ds(pl.multiple_of(blk * width, width), width))
            return carry
        lax.fori_loop(0, (cnt + width - 1) // width, step, 0)

    @pl.when((g == 0) & (e == 0))
    def _():
        acc_s[...] = jnp.zeros_like(acc_s)

    @pl.when(e == 0)
    def _():
        rank = crow_ref[...] - g * MOE_CODE_STRIDE
        h2 = h2_ref[...].reshape(tm, d_model)
        gates = gate_ref[...].reshape(tm, LANES)

        def gather(blk, rows):
            row_id = lax.broadcasted_iota(jnp.int32, (rb, tm), 0) + blk * rb
            onehot = jnp.where(row_id == rank, 1.0, 0.0).astype(BF16)
            xs_s[rows, :] = _dot(onehot, h2).astype(BF16)
            g_hi, g_mid, g_lo = _split3(gates)
            gs_s[rows, :] = _dot(onehot, g_hi) + _dot(onehot, g_mid) + _dot(onehot, g_lo)
        for_blocks(gather)

        @pl.when(n_blocks % (MOE_SCATTER_BLOCK // rb) != 0)
        def _():
            ys_s[pl.ds(pl.multiple_of(n_blocks * rb, rb), rb), :] = jnp.zeros((rb, d_model), F32)

    n_exp = MOE_PER_GROUP // MOE_SPLIT

    def experts(blk, rows, first):
        x = xs_s[rows, :]
        gs = gs_s[rows, :]
        lane = lax.broadcasted_iota(jnp.int32, gs.shape, 1)
        first_expert = g * MOE_PER_GROUP + e * n_exp
        gate = jnp.concatenate(
            [jnp.broadcast_to(jnp.sum(jnp.where(lane == first_expert + k, gs, 0.0), axis=-1, keepdims=True),
                              (rb, MOE_HIDDEN)) for k in range(n_exp)], axis=-1)
        wide = lambda w_ref: jnp.concatenate([_dot(x, w_ref[k]) for k in range(n_exp)], axis=-1)
        hid = _silu(wide(wg_ref)) * wide(wu_ref) * gate
        y = _dot(hid.astype(BF16), wd_ref[...])
        if first:
            ys_s[rows, :] = y
        else:
            ys_s[rows, :] += y

    @pl.when(e == 0)
    def _():
        for_blocks(lambda blk, rows: experts(blk, rows, True))

    @pl.when(e != 0)
    def _():
        for_blocks(lambda blk, rows: experts(blk, rows, False))

    @pl.when(e == MOE_SPLIT - 1)
    def _():
        code_t = jnp.broadcast_to(crow_ref[...].astype(F32), (LANES, tm)).T
        rank = code_t.astype(jnp.int32) - g * MOE_CODE_STRIDE
        lane = lax.broadcasted_iota(jnp.int32, (tm, LANES), 1)

        sb = MOE_SCATTER_BLOCK

        def scatter(blk, rows):
            onehot = jnp.concatenate(
                [jnp.where(lane + (blk * sb + part * LANES) == rank, 1.0, 0.0) for part in range(sb // LANES)],
                axis=-1).astype(BF16)
            acc_s[...] += _dot(onehot, ys_s[rows, :].astype(BF16))
        for_blocks(scatter, sb)

    @pl.when((g == MOE_GROUPS - 1) & (e == MOE_SPLIT - 1))
    def _():
        xo = x1_ref[...].reshape(tm, d_model) + mod_ref[5:6, :] * acc_s[...]
        xo = _rms(xo, fn_ref[...]) if final else xo
        o_ref[...] = xo.reshape(o_ref.shape)


def _moe(h2, gates, x1, mod, lw, final_norm, *, per_batch_mod, final, rt):
    B, S, D = x1.shape
    tm = MOE_TILE
    if S >= tm:
        nb, nt = B, S // tm
        tok = lambda w: pl.BlockSpec((None, tm, w), lambda b, s, g, e, c: (b, s, 0))
    else:
        assert not per_batch_mod
        nb, nt = B * S // tm, 1
        tok = lambda w: pl.BlockSpec((tm // S, S, w), lambda b, s, g, e, c: (b, 0, 0))
    sub = tm // rt
    info = gates[..., GROUP_LANE:COUNT_LANE + MOE_GROUPS].astype(jnp.int32)
    grp = info[..., 0].reshape(nb, nt, sub, rt)
    sub_counts = info[:, rt - 1::rt, 2:].reshape(nb, nt, sub, MOE_GROUPS)
    offsets = jnp.cumsum(sub_counts, axis=2) - sub_counts
    onehot = grp[..., None] == jnp.arange(MOE_GROUPS, dtype=jnp.int32)
    rank = info[..., 1].reshape(nb, nt, sub, rt) + jnp.sum(jnp.where(onehot, offsets[:, :, :, None, :], 0), axis=-1)
    counts = jnp.sum(sub_counts, axis=2).reshape(-1)
    code = grp * MOE_CODE_STRIDE + rank
    code_row = code.reshape(nb, nt, 1, tm)

    n_exp = MOE_PER_GROUP // MOE_SPLIT
    step_map = lambda b, s, g, e, c: (g * MOE_SPLIT + e, 0, 0)
    mod_map = (lambda b, s, g, e, c: (b, 0, 0)) if per_batch_mod else (lambda b, s, g, e, c: (0, 0, 0))
    grid_spec = pltpu.PrefetchScalarGridSpec(
        num_scalar_prefetch=1,
        grid=(nb, nt, MOE_GROUPS, MOE_SPLIT),
        in_specs=[
            tok(D), tok(LANES),
            pl.BlockSpec((None, None, 1, tm), lambda b, s, g, e, c: (b, s, 0, 0)),
            tok(D),
            pl.BlockSpec((None, 6, D), mod_map),
            pl.BlockSpec((n_exp, D, MOE_HIDDEN), step_map), pl.BlockSpec((n_exp, D, MOE_HIDDEN), step_map),
            pl.BlockSpec((None, n_exp * MOE_HIDDEN, D), step_map),
            pl.BlockSpec(final_norm.shape, lambda b, s, g, e, c: (0, 0)),
        ],
        out_specs=tok(D),
        scratch_shapes=[
            pltpu.VMEM((tm, D), BF16),
            pltpu.VMEM((tm, LANES), F32),
            pltpu.VMEM((tm, D), F32),
            pltpu.VMEM((tm, D), F32),
        ],
    )
    return pl.pallas_call(
        functools.partial(_moe_kernel, final=final, n_tiles=nt),
        grid_spec=grid_spec,
        out_shape=jax.ShapeDtypeStruct((B, S, D), F32),
        compiler_params=_cparams(("arbitrary", "arbitrary", "arbitrary", "arbitrary")),
        name="moe",
    )(counts, h2, gates, code_row, x1, mod, lw["moe_wg"], lw["moe_wu"], lw["moe_wd"], final_norm)


def _swap_halves(w, n_heads, dim):
    k = w.shape[0]
    w = w.reshape(k, n_heads, 2, dim // 2)
    return w[:, :, ::-1, :].reshape(k, n_heads * dim)


def _pad_cols(w, width):
    return jnp.pad(w, ((0, 0), (0, width - w.shape[1])))


def _rope_tables(n_tokens, rot_dim, reps):
    t = jnp.arange(n_tokens)
    row = (t // GRID_W).astype(F32)
    col = (t % GRID_W).astype(F32)
    n_freq = rot_dim // 4
    inv_freq = ROPE_BASE ** (-jnp.arange(n_freq, dtype=F32) / n_freq)
    ang = jnp.concatenate([row[:, None] * inv_freq, col[:, None] * inv_freq], axis=-1)
    cos, sin = jnp.cos(ang), jnp.sin(ang)
    c = jnp.tile(jnp.concatenate([cos, cos], axis=-1), (1, reps))
    s = jnp.tile(jnp.concatenate([-sin, sin], axis=-1), (1, reps))
    return _pad_cols(c, LANES), _pad_cols(s, LANES)


def _layer_weights(l, P, q_abs, ab_re, ab_im, bb_re, bb_im):
    w_in = P["w_in"][l]
    seg = {}
    o = 0
    for name, width in (("u", SSM_CH), ("ql", MLA_Q_RANK), ("kvl", MLA_KV_RANK), ("kr", MLA_ROPE),
                        ("qs", SWA_HEADS * HEAD_DIM), ("ks", SWA_KV_HEADS * HEAD_DIM),
                        ("vs", SWA_KV_HEADS * HEAD_DIM)):
        seg[name] = w_in[:, o:o + width]
        o += width
    ctx_cols = [seg["u"], seg["ql"], seg["kvl"], seg["qs"], seg["ks"], seg["vs"], _pad_cols(seg["kr"], LANES)]
    lat_cols = ctx_cols + [
        _swap_halves(seg["qs"], SWA_HEADS, HEAD_DIM),
        _swap_halves(seg["ks"], SWA_KV_HEADS, HEAD_DIM),
        _pad_cols(_swap_halves(seg["kr"], 1, MLA_ROPE), LANES),
    ]
    w_qb = P["w_mla_qb"][l]
    w_rope = w_qb[:, :, MLA_NOPE:]
    w_rope_sw = w_rope.reshape(MLA_Q_RANK, MLA_HEADS, 2, MLA_ROPE // 2)[:, :, ::-1, :].reshape(w_rope.shape)
    flat_t = lambda w: w.reshape(MLA_Q_RANK, MLA_HEADS * MLA_ROPE).T
    wq_abs = jnp.transpose(q_abs[l], (0, 2, 1)).reshape(Q_ABS, MLA_Q_RANK)
    wq_ctx = jnp.concatenate([wq_abs, flat_t(w_rope)], axis=0)
    wq_lat = jnp.concatenate([wq_ctx, flat_t(w_rope_sw)], axis=0)

    eye = jnp.eye(SSM_GROUPS, dtype=F32)

    def block_diag_b(bb):
        bb = bb.reshape(2, SSM_GROUPS, SSM_STATE, SSM_GROUP)
        return jnp.einsum("dgpc,gh->dgchp", bb, eye).reshape(2, SSM_CH, SSM_N)

    def block_diag_c(cc):
        return jnp.einsum("dgcp,gh->dgphc", cc, eye).reshape(2, SSM_N, SSM_CH)

    sl = slice(2 * l, 2 * l + 2)
    w_router = jnp.concatenate([P["moe_w_expert"][l], P["moe_w_group"][l]], axis=1)
    b_router = jnp.concatenate([P["moe_b_expert"][l], P["moe_b_group"][l]])
    row = lambda v: v.reshape(1, -1)
    return dict(
        norm1=row(P["norm1"][l]), norm2=row(P["norm2"][l]),
        win_ctx=jnp.concatenate(ctx_cols, axis=1).astype(BF16),
        win_lat=jnp.concatenate(lat_cols, axis=1).astype(BF16),
        q_norm=row(P["mla_q_norm"][l]), kv_norm=row(P["mla_kv_norm"][l]),
        wq_ctx=wq_ctx.astype(BF16), wq_lat=wq_lat.astype(BF16),
        wv=jnp.transpose(P["w_mla_kvb"][l][:, :, MLA_NOPE:], (1, 0, 2)).astype(BF16),
        ssm_wb=jnp.concatenate([block_diag_b(bb_re[sl]), block_diag_b(bb_im[sl])], axis=2).astype(BF16),
        ssm_a=jnp.concatenate([ab_re[sl], ab_im[sl]], axis=1).reshape(2, 1, 2 * SSM_N),
        ssm_wc=jnp.concatenate([block_diag_c(P["ssm_c_re"][l]), -block_diag_c(P["ssm_c_im"][l])],
                               axis=1).astype(BF16),
        ssm_d=row(P["ssm_d"][l]), w_glu=P["w_ssm_glu"][l].astype(BF16),
        gn_ssm=row(P["gn_ssm"][l]), gn_mla=row(P["gn_mla"][l]), gn_swa=row(P["gn_swa"][l]),
        w_out=P["w_out"][l].astype(BF16),
        w_router=_pad_cols(w_router, LANES), b_router=_pad_cols(row(b_router), LANES),
        moe_wg=P["moe_w_gate"][l].astype(BF16).reshape(MOE_EXPERTS, D_MODEL, MOE_HIDDEN),
        moe_wu=P["moe_w_up"][l].astype(BF16).reshape(MOE_EXPERTS, D_MODEL, MOE_HIDDEN),
        moe_wd=P["moe_w_down"][l].astype(BF16).reshape(
            MOE_GROUPS * MOE_SPLIT, MOE_PER_GROUP // MOE_SPLIT * MOE_HIDDEN, D_MODEL),
        sink=P["swa_sink"][l],
    )


def _layer(x, mod, lw, final_norm, *, tables, ctx, per_batch_mod, final):
    B, S, _ = x.shape
    context_pass = ctx is None
    pre = _pre(x, mod, lw, tables, per_batch_mod=per_batch_mod, emit_ctx=context_pass)
    u_tm, qt, kcat, ckvt, qs, ks, vs = pre[:7]
    nb = B // SUBLANES
    if context_pass:
        h0 = jnp.zeros((2, nb, SUBLANES, 2 * SSM_N), F32)
        y2, hfin = _ssm(u_tm, lw, h0, B, S)
        o_mla = _mla(qt, kcat, ckvt, None, None, lw["wv"])
        o_swa = _swa(lw["sink"], qs, ks, vs, None, None)
        state = hfin.reshape(2, B, 2, SSM_GROUPS, SSM_STATE).transpose(1, 0, 2, 3, 4)
        new_ctx = (pre[7], pre[8], pre[9].reshape(B, S, SWA_KV_HEADS, HEAD_DIM),
                   pre[10].reshape(B, S, SWA_KV_HEADS, HEAD_DIM), state)
    else:
        kcat_c, ckvt_c, ks_c, vs_c, h0 = ctx
        y2, _ = _ssm(u_tm, lw, h0, B, S)
        o_mla = _mla(qt, kcat_c, ckvt_c, kcat, ckvt, lw["wv"])
        o_swa = _swa(lw["sink"], qs, ks_c, vs_c, ks, vs)
        new_ctx = None
    x1, h2, gates = _post(x, u_tm, y2, o_mla, o_swa, mod, lw, per_batch_mod=per_batch_mod)
    xo = _moe(h2, gates, x1, mod, lw, final_norm, per_batch_mod=per_batch_mod, final=final, rt=min(S, POST_TILE))
    return xo, new_ctx


def kernel(x_prompt, x_sample, c, cache_mla_ckv, cache_mla_krope, cache_swa_k, cache_swa_v, state_ssm, c_ctx, w_ada, b_ada, norm1, norm2, w_in, ssm_a_re, ssm_a_im, ssm_log_dt, ssm_b_re, ssm_b_im, ssm_c_re, ssm_c_im, ssm_d, w_ssm_glu, mla_q_norm, w_mla_qb, mla_kv_norm, w_mla_kvb, swa_sink, gn_ssm, gn_mla, gn_swa, w_out, moe_w_group, moe_b_group, moe_w_expert, moe_b_expert, moe_w_gate, moe_w_up, moe_w_down, final_norm):
    P = dict(w_ada=w_ada, b_ada=b_ada, norm1=norm1, norm2=norm2, w_in=w_in,
             ssm_c_re=ssm_c_re, ssm_c_im=ssm_c_im, ssm_d=ssm_d, w_ssm_glu=w_ssm_glu,
             mla_q_norm=mla_q_norm, w_mla_qb=w_mla_qb, mla_kv_norm=mla_kv_norm, w_mla_kvb=w_mla_kvb,
             swa_sink=swa_sink, gn_ssm=gn_ssm, gn_mla=gn_mla, gn_swa=gn_swa, w_out=w_out,
             moe_w_group=moe_w_group, moe_b_group=moe_b_group, moe_w_expert=moe_w_expert,
             moe_b_expert=moe_b_expert, moe_w_gate=moe_w_gate, moe_w_up=moe_w_up, moe_w_down=moe_w_down)
    n_dec = c.shape[0]
    n_cond = 2 * SUBLANES
    conds = jnp.zeros((n_cond, D_MODEL), F32).at[:n_dec].set(c).at[n_dec].set(c_ctx)
    mods = _modulation(conds, w_ada, b_ada).reshape(DEPTH, n_cond, 6, D_MODEL)

    ab_re, ab_im, bb_re, bb_im = _ssm_discretise(ssm_a_re, ssm_a_im, ssm_log_dt, ssm_b_re, ssm_b_im)
    q_abs = _absorb_q(jnp.transpose(w_mla_qb[..., :MLA_NOPE], (0, 2, 1, 3)),
                      jnp.transpose(w_mla_kvb[..., :MLA_NOPE], (0, 2, 1, 3)))
    lws = [_layer_weights(l, P, q_abs, ab_re.reshape(2 * DEPTH, SSM_N), ab_im.reshape(2 * DEPTH, SSM_N),
                          bb_re, bb_im) for l in range(DEPTH)]
    fnorm = final_norm.reshape(1, D_MODEL)

    xp = x_prompt
    ctx_states = []
    for l in range(DEPTH):
        xp, new = _layer(xp, mods[l, n_dec:n_dec + 1], lws[l], fnorm, tables=None, ctx=None,
                         per_batch_mod=False, final=l == DEPTH - 1)
        ctx_states.append(new)
    outs_ctx = tuple(jnp.stack([s[k] for s in ctx_states], axis=1) for k in range(5))

    n_lat = x_sample.shape[1]
    cm, sm = _rope_tables(n_lat, MLA_ROPE, 1)
    cs, ss = _rope_tables(n_lat, HEAD_DIM, LANES // HEAD_DIM)
    tables = (cm, sm, cm[:, :MLA_ROPE].T, sm[:, :MLA_ROPE].T, cs, ss)
    xs = x_sample
    past = cache_mla_ckv.shape[2]
    for l in range(DEPTH):
        kcat_c = jnp.concatenate(
            [cache_mla_ckv[:, l], cache_mla_krope[:, l],
             jnp.zeros((n_dec, past, LANES - MLA_ROPE), F32)], axis=-1).astype(BF16)
        ks_c = cache_swa_k[:, l].reshape(n_dec, past, LANES).astype(BF16)
        vs_c = cache_swa_v[:, l].reshape(n_dec, past, LANES).astype(BF16)
        h0 = state_ssm[:, l].transpose(1, 0, 2, 3, 4).reshape(2, n_dec // SUBLANES, SUBLANES, 2 * SSM_N)
        ckvt_c = jnp.transpose(cache_mla_ckv[:, l], (0, 2, 1)).astype(BF16)
        xs, _ = _layer(xs, mods[l, :n_dec], lws[l], fnorm, tables=tables, ctx=(kcat_c, ckvt_c, ks_c, vs_c, h0),
                       per_batch_mod=True, final=l == DEPTH - 1)
    return (xp, xs) + outs_ctx
```

```python
import functools
import math

import jax
import jax.numpy as jnp
from jax import lax
from jax.experimental import pallas as pl
from jax.experimental.pallas import tpu as pltpu

F32 = jnp.float32
BF16 = jnp.bfloat16

D_MODEL = 1024
DEPTH = 4
GRID_W = 64
HEAD_DIM = 64
SSM_CH = 256
SSM_GROUP = 16
SSM_GROUPS = SSM_CH // SSM_GROUP
SSM_STATE = 64
SSM_N = SSM_GROUPS * SSM_STATE
MLA_HEADS = 6
MLA_Q_RANK = 256
MLA_KV_RANK = 128
MLA_NOPE = 64
MLA_ROPE = 32
MLA_V = 64
SWA_HEADS = 6
SWA_KV_HEADS = 2
SWA_GROUP = SWA_HEADS // SWA_KV_HEADS
SWA_WINDOW = 128
SWA_BLOCK = 128
MOE_GROUPS = 4
MOE_PER_GROUP = 8
MOE_EXPERTS = MOE_GROUPS * MOE_PER_GROUP
MOE_HIDDEN = 256
GROUP_LANE = MOE_EXPERTS
RANK_LANE = GROUP_LANE + 1
COUNT_LANE = GROUP_LANE + 2
POST_TILE = 512
MOE_SPLIT = 2
MOE_TILE = 1024
MOE_ROW_BLOCK = 128
MOE_SCATTER_BLOCK = 256
MOE_CODE_STRIDE = 1 << 16
ROPE_BASE = 10000.0
EPS = 1e-6
MLA_SCALE = 1.0 / math.sqrt(MLA_NOPE + MLA_ROPE)
MLA_SCALE_LOG2 = MLA_SCALE * math.log2(math.e)
SWA_SCALE = 1.0 / math.sqrt(HEAD_DIM)

LANES = 128
SUBLANES = 8
VMEM_LIMIT = 52 * 1024 * 1024

OFF_U = 0
OFF_QL = OFF_U + SSM_CH
OFF_KVL = OFF_QL + MLA_Q_RANK
OFF_QS = OFF_KVL + MLA_KV_RANK
OFF_KS = OFF_QS + SWA_HEADS * HEAD_DIM
OFF_VS = OFF_KS + SWA_KV_HEADS * HEAD_DIM
OFF_KR = OFF_VS + SWA_KV_HEADS * HEAD_DIM
N_PACK_CTX = OFF_KR + LANES
OFF_QS_SW = N_PACK_CTX
OFF_KS_SW = OFF_QS_SW + SWA_HEADS * HEAD_DIM
OFF_KR_SW = OFF_KS_SW + SWA_KV_HEADS * HEAD_DIM
N_PACK_LAT = OFF_KR_SW + LANES
Q_ABS = MLA_HEADS * MLA_KV_RANK
MLA_QK = 2 * LANES


def _cparams(sem):
    return pltpu.CompilerParams(dimension_semantics=sem, vmem_limit_bytes=VMEM_LIMIT)


def _dot(a, b):
    return jnp.dot(a, b, preferred_element_type=F32)


def _dot_nt(a, b):
    return lax.dot_general(a, b, (((1,), (1,)), ((), ())), preferred_element_type=F32)


def _split(x):
    hi = x.astype(BF16)
    lo = (x - hi.astype(F32)).astype(BF16)
    return hi, lo


def _dot3(a, b):
    ah, al = _split(a)
    bh, bl = _split(b)
    return _dot(ah, bh) + _dot(al, bh) + _dot(ah, bl)


def _tree(op, x3):
    parts = [x3[i] for i in range(x3.shape[0])]
    while len(parts) > 1:
        parts = [op(parts[i], parts[i + 1]) for i in range(0, len(parts), 2)]
    return parts[0]


def _rms(x, g):
    return x * lax.rsqrt(jnp.mean(x * x, axis=-1, keepdims=True) + EPS) * g


def _silu(x):
    return x * jax.nn.sigmoid(x)


def _gelu_tanh(x):
    return 0.5 * x * (1.0 + jnp.tanh(math.sqrt(2.0 / math.pi) * (x + 0.044715 * (x * x * x))))


def _mod_kernel(c_ref, w_ref, b_ref, o_ref):
    o_ref[...] = _dot3(_silu(c_ref[...]), w_ref[...]) + b_ref[...]


def _modulation(conds, w_ada, b_ada):
    n = conds.shape[0]
    tn = 1536
    return pl.pallas_call(
        _mod_kernel,
        grid=(DEPTH, 6 * D_MODEL // tn),
        in_specs=[
            pl.BlockSpec((n, D_MODEL), lambda l, j: (0, 0)),
            pl.BlockSpec((None, D_MODEL, tn), lambda l, j: (l, 0, j)),
            pl.BlockSpec((None, 1, tn), lambda l, j: (l, 0, j)),
        ],
        out_specs=pl.BlockSpec((None, n, tn), lambda l, j: (l, 0, j)),
        out_shape=jax.ShapeDtypeStruct((DEPTH, n, 6 * D_MODEL), F32),
        compiler_params=_cparams(("arbitrary", "arbitrary")),
        name="modulation",
    )(conds, w_ada, b_ada.reshape(DEPTH, 1, 6 * D_MODEL))


def _ssm_disc_kernel(are_ref, aim_ref, ldt_ref, bre_ref, bim_ref, abre_ref, abim_ref, bbre_ref, bbim_ref):
    lam_re = are_ref[...]
    lam_im = aim_ref[...]
    dt = jnp.exp(ldt_ref[...])
    z_re = lam_re * dt
    z_im = lam_im * dt
    mag = jnp.exp(z_re)
    ab_re = mag * jnp.cos(z_im)
    ab_im = mag * jnp.sin(z_im)
    den = lam_re * lam_re + lam_im * lam_im
    f_re = ((ab_re - 1.0) * lam_re + ab_im * lam_im) / den
    f_im = (ab_im * lam_re - (ab_re - 1.0) * lam_im) / den
    b_re = bre_ref[...]
    b_im = bim_ref[...]
    abre_ref[...] = ab_re
    abim_ref[...] = ab_im
    bbre_ref[...] = f_re * b_re - f_im * b_im
    bbim_ref[...] = f_re * b_im + f_im * b_re


def _ssm_discretise(a_re, a_im, log_dt, b_re, b_im):
    n = DEPTH * 2
    col = lambda v: v.reshape(n, SSM_N, 1)
    ldt = jnp.broadcast_to(log_dt[..., None], (DEPTH, 2, SSM_GROUPS, SSM_STATE))
    cspec = pl.BlockSpec((None, SSM_N, 1), lambda i: (i, 0, 0))
    bspec = pl.BlockSpec((None, SSM_N, SSM_GROUP), lambda i: (i, 0, 0))
    return pl.pallas_call(
        _ssm_disc_kernel,
        grid=(n,),
        in_specs=[cspec, cspec, cspec, bspec, bspec],
        out_specs=[cspec, cspec, bspec, bspec],
        out_shape=[jax.ShapeDtypeStruct((n, SSM_N, 1), F32)] * 2
        + [jax.ShapeDtypeStruct((n, SSM_N, SSM_GROUP), F32)] * 2,
        compiler_params=_cparams(("arbitrary",)),
        name="ssm_discretise",
    )(col(a_re), col(a_im), col(ldt), b_re.reshape(n, SSM_N, SSM_GROUP), b_im.reshape(n, SSM_N, SSM_GROUP))


def _absorb_kernel(wq_ref, wk_ref, o_ref):
    a = wq_ref[...]
    b = wk_ref[...]
    ah, al = _split(a)
    bh, bl = _split(b)
    o_ref[...] = _dot_nt(ah, bh) + _dot_nt(al, bh) + _dot_nt(ah, bl)


def _absorb_q(wq_nope, wk_nope):
    return pl.pallas_call(
        _absorb_kernel,
        grid=(DEPTH, MLA_HEADS),
        in_specs=[
            pl.BlockSpec((None, None, MLA_Q_RANK, MLA_NOPE), lambda l, h: (l, h, 0, 0)),
            pl.BlockSpec((None, None, MLA_KV_RANK, MLA_NOPE), lambda l, h: (l, h, 0, 0)),
        ],
        out_specs=pl.BlockSpec((None, None, MLA_Q_RANK, MLA_KV_RANK), lambda l, h: (l, h, 0, 0)),
        out_shape=jax.ShapeDtypeStruct((DEPTH, MLA_HEADS, MLA_Q_RANK, MLA_KV_RANK), F32),
        compiler_params=_cparams(("arbitrary", "arbitrary")),
        name="mla_absorb",
    )(wq_nope, wk_nope)


def _pre_kernel(*refs, rope, emit_ctx):
    it = iter(refs)
    x_ref, mod_ref, n1_ref, win_ref, qn_ref, kvn_ref, wq_ref = (next(it) for _ in range(7))
    if rope:
        cm_ref, sm_ref, cmt_ref, smt_ref, cs_ref, ss_ref = (next(it) for _ in range(6))
    u_ref, qt_ref, kcat_ref, ckvt_ref, qs_ref, ks_ref, vs_ref = (next(it) for _ in range(7))
    if emit_ctx:
        ckv_o, kr_o, ks_o, vs_o = (next(it) for _ in range(4))

    x = x_ref[...]
    mod = mod_ref[...]
    h = _rms(x, n1_ref[...]) * (1.0 + mod[1:2]) + mod[0:1]
    proj = _dot(h.astype(BF16), win_ref[...])

    u_ref[...] = proj[:, OFF_U:OFF_U + SSM_CH]

    qln = _rms(proj[:, OFF_QL:OFF_QL + MLA_Q_RANK], qn_ref[...]).astype(BF16)
    qall = _dot_nt(wq_ref[...], qln)
    n_rope = MLA_HEADS * MLA_ROPE
    zero_rows = jnp.zeros((MLA_QK - MLA_KV_RANK - MLA_ROPE, qall.shape[1]), BF16)
    for i in range(MLA_HEADS):
        qa = qall[i * MLA_KV_RANK:(i + 1) * MLA_KV_RANK]
        qr = qall[Q_ABS + i * MLA_ROPE:Q_ABS + (i + 1) * MLA_ROPE]
        if rope:
            qr_sw = qall[Q_ABS + n_rope + i * MLA_ROPE:Q_ABS + n_rope + (i + 1) * MLA_ROPE]
            qr = qr * cmt_ref[...] + qr_sw * smt_ref[...]
        base = i * MLA_QK
        qt_ref[base:base + MLA_KV_RANK, :] = (qa * MLA_SCALE_LOG2).astype(BF16)
        qt_ref[base + MLA_KV_RANK:base + MLA_KV_RANK + MLA_ROPE, :] = (qr * MLA_SCALE_LOG2).astype(BF16)
        qt_ref[base + MLA_KV_RANK + MLA_ROPE:base + MLA_QK, :] = zero_rows

    ckv = _rms(proj[:, OFF_KVL:OFF_KVL + MLA_KV_RANK], kvn_ref[...])
    kr = proj[:, OFF_KR:OFF_KR + LANES]
    if emit_ctx:
        ckv_o[...] = ckv
        kr_o[...] = kr[:, :MLA_ROPE]
    if rope:
        kr = kr * cm_ref[...] + proj[:, OFF_KR_SW:OFF_KR_SW + LANES] * sm_ref[...]
    kcat_ref[...] = jnp.concatenate([ckv, kr], axis=-1).astype(BF16)
    ckvt_ref[...] = ckv.T.astype(BF16)

    qs = proj[:, OFF_QS:OFF_QS + SWA_HEADS * HEAD_DIM]
    ks = proj[:, OFF_KS:OFF_KS + SWA_KV_HEADS * HEAD_DIM]
    vs = proj[:, OFF_VS:OFF_VS + SWA_KV_HEADS * HEAD_DIM]
    if emit_ctx:
        ks_o[...] = ks
        vs_o[...] = vs
    if rope:
        cs = cs_ref[...]
        ss = ss_ref[...]
        qs_sw = proj[:, OFF_QS_SW:OFF_QS_SW + SWA_HEADS * HEAD_DIM]
        ks = ks * cs + proj[:, OFF_KS_SW:OFF_KS_SW + SWA_KV_HEADS * HEAD_DIM] * ss
        qs = jnp.concatenate(
            [qs[:, i * LANES:(i + 1) * LANES] * cs + qs_sw[:, i * LANES:(i + 1) * LANES] * ss
             for i in range(SWA_HEADS * HEAD_DIM // LANES)], axis=-1)
    qs_ref[...] = (qs * SWA_SCALE).astype(BF16)
    ks_ref[...] = ks.astype(BF16)
    vs_ref[...] = vs.astype(BF16)


def _pre(x, mod, lw, tables, *, per_batch_mod, emit_ctx):
    B, S, D = x.shape
    rope = tables is not None
    ts = min(S, 512)
    n_pack = N_PACK_LAT if rope else N_PACK_CTX
    win = lw["win_lat"] if rope else lw["win_ctx"]
    wq = lw["wq_lat"] if rope else lw["wq_ctx"]
    tok = lambda w: pl.BlockSpec((None, ts, w), lambda b, s: (b, s, 0))
    full = lambda a: pl.BlockSpec(a.shape, lambda b, s: (0,) * a.ndim)
    in_specs = [
        tok(D),
        pl.BlockSpec((None, 6, D), (lambda b, s: (b, 0, 0)) if per_batch_mod else (lambda b, s: (0, 0, 0))),
        full(lw["norm1"]), full(win), full(lw["q_norm"]), full(lw["kv_norm"]), full(wq),
    ]
    args = [x, mod, lw["norm1"], win, lw["q_norm"], lw["kv_norm"], wq]
    if rope:
        row_tab = pl.BlockSpec((ts, LANES), lambda b, s: (s, 0))
        col_tab = pl.BlockSpec((MLA_ROPE, ts), lambda b, s: (0, s))
        in_specs += [row_tab, row_tab, col_tab, col_tab, row_tab, row_tab]
        args += list(tables)
    feat = lambda w: pl.BlockSpec((None, w, ts), lambda b, s: (b, 0, s))
    out_specs = [
        tok(SSM_CH),
        feat(MLA_HEADS * MLA_QK), tok(MLA_QK), feat(MLA_KV_RANK), tok(SWA_HEADS * HEAD_DIM), tok(LANES), tok(LANES),
    ]
    out_shape = [
        jax.ShapeDtypeStruct((B, S, SSM_CH), F32),
        jax.ShapeDtypeStruct((B, MLA_HEADS * MLA_QK, S), BF16),
        jax.ShapeDtypeStruct((B, S, MLA_QK), BF16),
        jax.ShapeDtypeStruct((B, MLA_KV_RANK, S), BF16),
        jax.ShapeDtypeStruct((B, S, SWA_HEADS * HEAD_DIM), BF16),
        jax.ShapeDtypeStruct((B, S, LANES), BF16),
        jax.ShapeDtypeStruct((B, S, LANES), BF16),
    ]
    if emit_ctx:
        out_specs += [tok(MLA_KV_RANK), tok(MLA_ROPE), tok(LANES), tok(LANES)]
        out_shape += [
            jax.ShapeDtypeStruct((B, S, MLA_KV_RANK), F32),
            jax.ShapeDtypeStruct((B, S, MLA_ROPE), F32),
            jax.ShapeDtypeStruct((B, S, LANES), F32),
            jax.ShapeDtypeStruct((B, S, LANES), F32),
        ]
    return pl.pallas_call(
        functools.partial(_pre_kernel, rope=rope, emit_ctx=emit_ctx),
        grid=(B, S // ts),
        in_specs=in_specs,
        out_specs=out_specs,
        out_shape=out_shape,
        compiler_params=_cparams(("arbitrary", "arbitrary")),
        name="pre_lat" if rope else "pre_ctx",
    )(*args)


def _ssm_kernel(uf_ref, ub_ref, wb_ref, a_ref, wc_ref, h0_ref, yf_ref, yb_ref, hfin_ref,
                hre_f, him_f, hre_b, him_b, st_s, ut_s, *, tc):
    i = pl.program_id(1)
    n = pl.num_programs(1)

    @pl.when(i == 0)
    def _():
        st_s[...] = h0_ref[...]

    cblk = SSM_CH // LANES
    u_refs = (uf_ref, ub_ref)
    h_refs = ((hre_f, him_f), (hre_b, him_b))
    for d in range(2):
        ub = u_refs[d][...]
        for b in range(SUBLANES):
            for j in range(cblk):
                ut_s[d, j, pl.ds(b, tc, stride=SUBLANES), :] = ub[b, :, j * LANES:(j + 1) * LANES]
        u = jnp.concatenate([ut_s[d, j] for j in range(cblk)], axis=-1).astype(BF16)
        bu = _dot(u, wb_ref[d])
        h_refs[d][0][...] = bu[:, :SSM_N]
        h_refs[d][1][...] = bu[:, SSM_N:]

    a_re = [jnp.broadcast_to(a_ref[d][:, :SSM_N], (SUBLANES, SSM_N)) for d in range(2)]
    a_im = [jnp.broadcast_to(a_ref[d][:, SSM_N:], (SUBLANES, SSM_N)) for d in range(2)]
    h_re = [st_s[d][:, :SSM_N] for d in range(2)]
    h_im = [st_s[d][:, SSM_N:] for d in range(2)]
    for t in range(tc):
        for d in range(2):
            tt = t if d == 0 else tc - 1 - t
            rows = slice(tt * SUBLANES, (tt + 1) * SUBLANES)
            hre_s, him_s = h_refs[d]
            n_re = a_re[d] * h_re[d] - a_im[d] * h_im[d] + hre_s[rows, :]
            n_im = a_re[d] * h_im[d] + a_im[d] * h_re[d] + him_s[rows, :]
            hre_s[rows, :] = n_re
            him_s[rows, :] = n_im
            h_re[d], h_im[d] = n_re, n_im

    y_refs = (yf_ref, yb_ref)
    for d in range(2):
        st_s[d] = jnp.concatenate([h_re[d], h_im[d]], axis=-1)
        hre_s, him_s = h_refs[d]
        y = _dot(hre_s[...].astype(BF16), wc_ref[d, :SSM_N, :]) + _dot(him_s[...].astype(BF16), wc_ref[d, SSM_N:, :])
        for j in range(cblk):
            ut_s[d, j] = y[:, j * LANES:(j + 1) * LANES]
        for b in range(SUBLANES):
            y_refs[d][b] = jnp.concatenate(
                [ut_s[d, j, pl.ds(b, tc, stride=SUBLANES), :] for j in range(cblk)], axis=-1)

    @pl.when(i == n - 1)
    def _():
        hfin_ref[...] = st_s[...]


def _ssm(u, lw, h0, B, S):
    nb = B // SUBLANES
    tc = 128
    nchunk = S // tc
    full3 = lambda a: pl.BlockSpec(a.shape, lambda b, i: (0, 0, 0))
    state_spec = pl.BlockSpec((2, None, SUBLANES, 2 * SSM_N), lambda b, i: (0, b, 0, 0))
    fwd = pl.BlockSpec((SUBLANES, tc, SSM_CH), lambda b, i: (b, i, 0))
    bwd = pl.BlockSpec((SUBLANES, tc, SSM_CH), lambda b, i: (b, nchunk - 1 - i, 0))
    yf, yb, hfin = pl.pallas_call(
        functools.partial(_ssm_kernel, tc=tc),
        grid=(nb, nchunk),
        in_specs=[fwd, bwd, full3(lw["ssm_wb"]), full3(lw["ssm_a"]), full3(lw["ssm_wc"]), state_spec],
        out_specs=[fwd, bwd, state_spec],
        out_shape=[
            jax.ShapeDtypeStruct((B, S, SSM_CH), F32),
            jax.ShapeDtypeStruct((B, S, SSM_CH), F32),
            jax.ShapeDtypeStruct((2, nb, SUBLANES, 2 * SSM_N), F32),
        ],
        scratch_shapes=[pltpu.VMEM((tc * SUBLANES, SSM_N), F32)] * 4 + [
            pltpu.VMEM((2, SUBLANES, 2 * SSM_N), F32),
            pltpu.VMEM((2, SSM_CH // LANES, tc * SUBLANES, LANES), F32),
        ],
        compiler_params=_cparams(("arbitrary", "arbitrary")),
        name="ssm_scan",
    )(u, u, lw["ssm_wb"], lw["ssm_a"], lw["ssm_wc"], h0)
    return (yf, yb), hfin


def _mla_kernel(*refs, with_latent, tk):
    if with_latent:
        qt_ref, ka_ref, vat_ref, kb_ref, vbt_ref, wv_ref, o_ref = refs
    else:
        qt_ref, ka_ref, vat_ref, wv_ref, o_ref = refs
    tq = qt_ref.shape[1]
    q_of = lambda hd: qt_ref[hd * MLA_QK:(hd + 1) * MLA_QK, :]

    def tile(state, k, vt):
        new_state = []
        s_next = _dot(k, q_of(0))
        for hd in range(MLA_HEADS):
            m8, l8, acc = state[hd]
            s = s_next
            if hd + 1 < MLA_HEADS:
                s_next = _dot(k, q_of(hd + 1))
            s3 = s.reshape(s.shape[0] // SUBLANES, SUBLANES, tq)
            mloc = jnp.max(_tree(jnp.maximum, s3), axis=0, keepdims=True)
            m8_new = jnp.maximum(m8, jnp.broadcast_to(mloc, (SUBLANES, tq)))
            alpha8 = jnp.exp2(m8 - m8_new)
            p3 = jnp.exp2(s3 - m8_new[None])
            l8 = alpha8 * l8 + _tree(jnp.add, p3)
            pv = _dot(vt, p3.reshape(s.shape).astype(BF16))
            acc3 = acc.reshape(MLA_KV_RANK // SUBLANES, SUBLANES, tq) * alpha8[None]
            new_state.append((m8_new, l8, acc3.reshape(MLA_KV_RANK, tq) + pv))
        return tuple(new_state)

    init = tuple((jnp.full((SUBLANES, tq), -jnp.inf, F32), jnp.zeros((SUBLANES, tq), F32),
                  jnp.zeros((MLA_KV_RANK, tq), F32)) for _ in range(MLA_HEADS))
    state = tile(init, ka_ref[...], vat_ref[...])
    if with_latent:
        for j in range(kb_ref.shape[0] // tk):
            state = tile(state, kb_ref[j * tk:(j + 1) * tk, :], vbt_ref[:, j * tk:(j + 1) * tk])

    outs = []
    for hd in range(MLA_HEADS):
        _, l8, acc = state[hd]
        o_lat = (acc / jnp.sum(l8, axis=0, keepdims=True)).T.astype(BF16)
        outs.append(_dot(o_lat, wv_ref[hd]))
    o_ref[...] = jnp.concatenate(outs, axis=-1)


def _mla(qt, ka, vat, kb, vbt, wv):
    B, _, S = qt.shape
    tq = 256
    with_latent = kb is not None
    kspec = lambda a: pl.BlockSpec((None,) + a.shape[1:], lambda b, s: (b, 0, 0))
    in_specs = [pl.BlockSpec((None, MLA_HEADS * MLA_QK, tq), lambda b, s: (b, 0, s)), kspec(ka), kspec(vat)]
    args = [qt, ka, vat]
    if with_latent:
        in_specs += [kspec(kb), kspec(vbt)]
        args += [kb, vbt]
    in_specs.append(pl.BlockSpec(wv.shape, lambda b, s: (0, 0, 0)))
    args.append(wv)
    return pl.pallas_call(
        functools.partial(_mla_kernel, with_latent=with_latent, tk=4096),
        grid=(B, S // tq),
        in_specs=in_specs,
        out_specs=pl.BlockSpec((None, tq, MLA_HEADS * MLA_V), lambda b, s: (b, s, 0)),
        out_shape=jax.ShapeDtypeStruct((B, S, MLA_HEADS * MLA_V), F32),
        compiler_params=_cparams(("arbitrary", "arbitrary")),
        name="mla_lat" if with_latent else "mla_ctx",
    )(*args)


def _swa_kernel(*refs, with_latent):
    if with_latent:
        sink_ref, q_ref, ka_ref, va_ref, kb_ref, vb_ref, o_ref = refs
    else:
        sink_ref, q_ref, ka_ref, va_ref, o_ref = refs
    tq = q_ref.shape[0]
    q = q_ref[...]
    ka = ka_ref[...]
    va = va_ref[...]
    if with_latent:
        n = pl.program_id(1)
        nblk = kb_ref.shape[0] // SWA_BLOCK
        start = pl.multiple_of(jnp.clip(n - 1, 0, nblk - 3) * SWA_BLOCK, SWA_BLOCK)
        kb = kb_ref[pl.ds(start, 3 * SWA_BLOCK), :]
        vb = vb_ref[pl.ds(start, 3 * SWA_BLOCK), :]
        qpos = n * SWA_BLOCK + lax.broadcasted_iota(jnp.int32, (tq, 3 * SWA_BLOCK), 0)
        kpos = start + lax.broadcasted_iota(jnp.int32, (tq, 3 * SWA_BLOCK), 1)
        valid = jnp.abs(qpos - kpos) <= SWA_WINDOW
    outs = []
    for hd in range(SWA_HEADS):
        kh = hd // SWA_GROUP
        qh = q[:, hd * HEAD_DIM:(hd + 1) * HEAD_DIM]
        ksl = slice(kh * HEAD_DIM, (kh + 1) * HEAD_DIM)
        sink = sink_ref[hd]
        s_a = _dot_nt(qh, ka[:, ksl])
        m = jnp.maximum(jnp.max(s_a, axis=-1, keepdims=True), sink)
        if with_latent:
            s_b = jnp.where(valid, _dot_nt(qh, kb[:, ksl]), -jnp.inf)
            m = jnp.maximum(m, jnp.max(s_b, axis=-1, keepdims=True))
        p_a = jnp.exp(s_a - m)
        den = jnp.sum(p_a, axis=-1, keepdims=True) + jnp.exp(sink - m)
        o = _dot(p_a.astype(BF16), va[:, ksl])
        if with_latent:
            p_b = jnp.exp(s_b - m)
            den = den + jnp.sum(p_b, axis=-1, keepdims=True)
            o = o + _dot(p_b.astype(BF16), vb[:, ksl])
        outs.append(o / den)
    o_ref[...] = jnp.concatenate(outs, axis=-1)


def _swa(sink, q, ka, va, kb, vb):
    B, S, W = q.shape
    with_latent = kb is not None
    tq = SWA_BLOCK if with_latent else S
    tok = pl.BlockSpec((None, tq, W), lambda b, s: (b, s, 0))
    kspec = lambda a: pl.BlockSpec((None,) + a.shape[1:], lambda b, s: (b, 0, 0))
    in_specs = [pl.BlockSpec(memory_space=pltpu.SMEM), tok, kspec(ka), kspec(va)]
    args = [sink, q, ka, va]
    if with_latent:
        in_specs += [kspec(kb), kspec(vb)]
        args += [kb, vb]
    return pl.pallas_call(
        functools.partial(_swa_kernel, with_latent=with_latent),
        grid=(B, S // tq),
        in_specs=in_specs,
        out_specs=tok,
        out_shape=jax.ShapeDtypeStruct((B, S, W), F32),
        compiler_params=_cparams(("arbitrary", "arbitrary")),
        name="swa_lat" if with_latent else "swa_ctx",
    )(*args)


def _route(logits):
    lane = lax.broadcasted_iota(jnp.int32, logits.shape, 1)
    big = jnp.int32(1 << 20)
    is_g = (lane >= MOE_EXPERTS) & (lane < MOE_EXPERTS + MOE_GROUPS)
    lg = jnp.where(is_g, logits, -jnp.inf)
    mg = jnp.max(lg, axis=-1, keepdims=True)
    g_idx = jnp.min(jnp.where(lg == mg, lane - MOE_EXPERTS, big), axis=-1, keepdims=True)
    pg_top = 1.0 / jnp.sum(jnp.exp(lg - mg), axis=-1, keepdims=True)

    is_e = (lane < MOE_EXPERTS) & ((lane // MOE_PER_GROUP) == g_idx)
    le = jnp.where(is_e, logits, -jnp.inf)
    m1 = jnp.max(le, axis=-1, keepdims=True)
    e1 = jnp.min(jnp.where(le == m1, lane, big), axis=-1, keepdims=True)
    z = jnp.sum(jnp.exp(le - m1), axis=-1, keepdims=True)
    le2 = jnp.where(lane == e1, -jnp.inf, le)
    m2 = jnp.max(le2, axis=-1, keepdims=True)
    e2 = jnp.min(jnp.where(le2 == m2, lane, big), axis=-1, keepdims=True)
    p1 = 1.0 / z
    p2 = jnp.exp(m2 - m1) / z
    tot = p1 + p2
    gates = pg_top * (jnp.where(lane == e1, p1 / tot, 0.0) + jnp.where(lane == e2, p2 / tot, 0.0))
    t = logits.shape[0]
    chose = jnp.where(lane == g_idx + COUNT_LANE, 1.0, 0.0)
    tri = jnp.where(lax.broadcasted_iota(jnp.int32, (t, t), 0) >= lax.broadcasted_iota(jnp.int32, (t, t), 1),
                    1.0, 0.0).astype(BF16)
    counts = _dot(tri, chose.astype(BF16))
    rank = jnp.sum(chose * (counts - 1.0), axis=-1, keepdims=True)
    return (gates + counts + jnp.where(lane == GROUP_LANE, g_idx.astype(F32), 0.0)
            + jnp.where(lane == RANK_LANE, rank, 0.0))


def _post_kernel(x_ref, u_ref, yf_ref, yb_ref, om_ref, os_ref, mod_ref, d_ref, wglu_ref, gs_ref, gm_ref,
                 gw_ref, wout_ref, n2_ref, wr_ref, br_ref, x1_ref, h2_ref, gate_ref):
    mod = mod_ref[...]
    y = d_ref[...] * u_ref[...] + yf_ref[...] + yb_ref[...]
    ga = _dot(_gelu_tanh(y).astype(BF16), wglu_ref[...])
    y_ssm = ga[:, :SSM_CH] * jax.nn.sigmoid(ga[:, SSM_CH:])
    wout = wout_ref
    n_mla = MLA_HEADS * MLA_V
    mixed = (_dot(_rms(y_ssm, gs_ref[...]).astype(BF16), wout[0:SSM_CH, :])
             + _dot(_rms(om_ref[...], gm_ref[...]).astype(BF16), wout[SSM_CH:SSM_CH + n_mla, :])
             + _dot(_rms(os_ref[...], gw_ref[...]).astype(BF16), wout[SSM_CH + n_mla:, :]))
    x1 = x_ref[...] + mod[2:3] * mixed
    x1_ref[...] = x1
    h2 = _rms(x1, n2_ref[...]) * (1.0 + mod[4:5]) + mod[3:4]
    h2_ref[...] = h2.astype(BF16)
    gate_ref[...] = _route(_dot3(h2, wr_ref[...]) + br_ref[...])


def _post(x, u_tm, y2, o_mla, o_swa, mod, lw, *, per_batch_mod):
    B, S, D = x.shape
    ts = min(S, POST_TILE)
    tok = lambda w: pl.BlockSpec((None, ts, w), lambda b, s: (b, s, 0))
    full = lambda a: pl.BlockSpec(a.shape, lambda b, s: (0,) * a.ndim)
    names = ["ssm_d", "w_glu", "gn_ssm", "gn_mla", "gn_swa", "w_out", "norm2", "w_router", "b_router"]
    in_specs = [
        tok(D),
        tok(SSM_CH),
        tok(SSM_CH), tok(SSM_CH),
        tok(MLA_HEADS * MLA_V), tok(SWA_HEADS * HEAD_DIM),
        pl.BlockSpec((None, 6, D), (lambda b, s: (b, 0, 0)) if per_batch_mod else (lambda b, s: (0, 0, 0))),
    ] + [full(lw[k]) for k in names]
    return pl.pallas_call(
        _post_kernel,
        grid=(B, S // ts),
        in_specs=in_specs,
        out_specs=[tok(D), tok(D), tok(LANES)],
        out_shape=[
            jax.ShapeDtypeStruct((B, S, D), F32),
            jax.ShapeDtypeStruct((B, S, D), BF16),
            jax.ShapeDtypeStruct((B, S, LANES), F32),
        ],
        compiler_params=_cparams(("arbitrary", "arbitrary")),
        name="post",
    )(x, u_tm, y2[0], y2[1], o_mla, o_swa, mod, *[lw[k] for k in names])


def _split3(x):
    hi = x.astype(BF16)
    r = x - hi.astype(F32)
    mid = r.astype(BF16)
    lo = (r - mid.astype(F32)).astype(BF16)
    return hi, mid, lo


def _moe_kernel(cnt_ref, h2_ref, gate_ref, crow_ref, x1_ref, mod_ref, wg_ref, wu_ref, wd_ref, fn_ref,
                o_ref, xs_s, gs_s, ys_s, acc_s, *, final, n_tiles):
    tm, d_model = acc_s.shape
    rb = MOE_ROW_BLOCK
    g = pl.program_id(2)
    e = pl.program_id(3)
    cnt = cnt_ref[(pl.program_id(0) * n_tiles + pl.program_id(1)) * MOE_GROUPS + g]
    n_blocks = (cnt + rb - 1) // rb

    def for_blocks(body, width=rb):
        def step(blk, carry):
            body(blk, pl.ds(pl.multiple_of(blk * width, width), width))
            return carry
        lax.fori_loop(0, (cnt + width - 1) // width, step, 0)

    @pl.when((g == 0) & (e == 0))
    def _():
        acc_s[...] = jnp.zeros_like(acc_s)

    @pl.when(e == 0)
    def _():
        rank = crow_ref[...] - g * MOE_CODE_STRIDE
        h2 = h2_ref[...].reshape(tm, d_model)
        gates = gate_ref[...].reshape(tm, LANES)

        def gather(blk, rows):
            row_id = lax.broadcasted_iota(jnp.int32, (rb, tm), 0) + blk * rb
            onehot = jnp.where(row_id == rank, 1.0, 0.0).astype(BF16)
            xs_s[rows, :] = _dot(onehot, h2).astype(BF16)
            g_hi, g_mid, g_lo = _split3(gates)
            gs_s[rows, :] = _dot(onehot, g_hi) + _dot(onehot, g_mid) + _dot(onehot, g_lo)
        for_blocks(gather)

        @pl.when(n_blocks % (MOE_SCATTER_BLOCK // rb) != 0)
        def _():
            ys_s[pl.ds(pl.multiple_of(n_blocks * rb, rb), rb), :] = jnp.zeros((rb, d_model), F32)

    n_exp = MOE_PER_GROUP // MOE_SPLIT

    def experts(blk, rows, first):
        x = xs_s[rows, :]
        gs = gs_s[rows, :]
        lane = lax.broadcasted_iota(jnp.int32, gs.shape, 1)
        first_expert = g * MOE_PER_GROUP + e * n_exp
        gate = jnp.concatenate(
            [jnp.broadcast_to(jnp.sum(jnp.where(lane == first_expert + k, gs, 0.0), axis=-1, keepdims=True),
                              (rb, MOE_HIDDEN)) for k in range(n_exp)], axis=-1)
        wide = lambda w_ref: jnp.concatenate([_dot(x, w_ref[k]) for k in range(n_exp)], axis=-1)
        hid = _silu(wide(wg_ref)) * wide(wu_ref) * gate
        y = _dot(hid.astype(BF16), wd_ref[...])
        if first:
            ys_s[rows, :] = y
        else:
            ys_s[rows, :] += y

    @pl.when(e == 0)
    def _():
        for_blocks(lambda blk, rows: experts(blk, rows, True))

    @pl.when(e != 0)
    def _():
        for_blocks(lambda blk, rows: experts(blk, rows, False))

    @pl.when(e == MOE_SPLIT - 1)
    def _():
        code_t = jnp.broadcast_to(crow_ref[...].astype(F32), (LANES, tm)).T
        rank = code_t.astype(jnp.int32) - g * MOE_CODE_STRIDE
        lane = lax.broadcasted_iota(jnp.int32, (tm, LANES), 1)

        sb = MOE_SCATTER_BLOCK

        def scatter(blk, rows):
            onehot = jnp.concatenate(
                [jnp.where(lane + (blk * sb + part * LANES) == rank, 1.0, 0.0) for part in range(sb // LANES)],
                axis=-1).astype(BF16)
            acc_s[...] += _dot(onehot, ys_s[rows, :].astype(BF16))
        for_blocks(scatter, sb)

    @pl.when((g == MOE_GROUPS - 1) & (e == MOE_SPLIT - 1))
    def _():
        xo = x1_ref[...].reshape(tm, d_model) + mod_ref[5:6, :] * acc_s[...]
        xo = _rms(xo, fn_ref[...]) if final else xo
        o_ref[...] = xo.reshape(o_ref.shape)


def _moe(h2, gates, x1, mod, lw, final_norm, *, per_batch_mod, final, rt):
    B, S, D = x1.shape
    tm = MOE_TILE
    if S >= tm:
        nb, nt = B, S // tm
        tok = lambda w: pl.BlockSpec((None, tm, w), lambda b, s, g, e, c: (b, s, 0))
    else:
        assert not per_batch_mod
        nb, nt = B * S // tm, 1
        tok = lambda w: pl.BlockSpec((tm // S, S, w), lambda b, s, g, e, c: (b, 0, 0))
    sub = tm // rt
    info = gates[..., GROUP_LANE:COUNT_LANE + MOE_GROUPS].astype(jnp.int32)
    grp = info[..., 0].reshape(nb, nt, sub, rt)
    sub_counts = info[:, rt - 1::rt, 2:].reshape(nb, nt, sub, MOE_GROUPS)
    offsets = jnp.cumsum(sub_counts, axis=2) - sub_counts
    onehot = grp[..., None] == jnp.arange(MOE_GROUPS, dtype=jnp.int32)
    rank = info[..., 1].reshape(nb, nt, sub, rt) + jnp.sum(jnp.where(onehot, offsets[:, :, :, None, :], 0), axis=-1)
    counts = jnp.sum(sub_counts, axis=2).reshape(-1)
    code = grp * MOE_CODE_STRIDE + rank
    code_row = code.reshape(nb, nt, 1, tm)

    n_exp = MOE_PER_GROUP // MOE_SPLIT
    step_map = lambda b, s, g, e, c: (g * MOE_SPLIT + e, 0, 0)
    mod_map = (lambda b, s, g, e, c: (b, 0, 0)) if per_batch_mod else (lambda b, s, g, e, c: (0, 0, 0))
    grid_spec = pltpu.PrefetchScalarGridSpec(
        num_scalar_prefetch=1,
        grid=(nb, nt, MOE_GROUPS, MOE_SPLIT),
        in_specs=[
            tok(D), tok(LANES),
            pl.BlockSpec((None, None, 1, tm), lambda b, s, g, e, c: (b, s, 0, 0)),
            tok(D),
            pl.BlockSpec((None, 6, D), mod_map),
            pl.BlockSpec((n_exp, D, MOE_HIDDEN), step_map), pl.BlockSpec((n_exp, D, MOE_HIDDEN), step_map),
            pl.BlockSpec((None, n_exp * MOE_HIDDEN, D), step_map),
            pl.BlockSpec(final_norm.shape, lambda b, s, g, e, c: (0, 0)),
        ],
        out_specs=tok(D),
        scratch_shapes=[
            pltpu.VMEM((tm, D), BF16),
            pltpu.VMEM((tm, LANES), F32),
            pltpu.VMEM((tm, D), F32),
            pltpu.VMEM((tm, D), F32),
        ],
    )
    return pl.pallas_call(
        functools.partial(_moe_kernel, final=final, n_tiles=nt),
        grid_spec=grid_spec,
        out_shape=jax.ShapeDtypeStruct((B, S, D), F32),
        compiler_params=_cparams(("arbitrary", "arbitrary", "arbitrary", "arbitrary")),
        name="moe",
    )(counts, h2, gates, code_row, x1, mod, lw["moe_wg"], lw["moe_wu"], lw["moe_wd"], final_norm)


def _swap_halves(w, n_heads, dim):
    k = w.shape[0]
    w = w.reshape(k, n_heads, 2, dim // 2)
    return w[:, :, ::-1, :].reshape(k, n_heads * dim)


def _pad_cols(w, width):
    return jnp.pad(w, ((0, 0), (0, width - w.shape[1])))


def _rope_tables(n_tokens, rot_dim, reps):
    t = jnp.arange(n_tokens)
    row = (t // GRID_W).astype(F32)
    col = (t % GRID_W).astype(F32)
    n_freq = rot_dim // 4
    inv_freq = ROPE_BASE ** (-jnp.arange(n_freq, dtype=F32) / n_freq)
    ang = jnp.concatenate([row[:, None] * inv_freq, col[:, None] * inv_freq], axis=-1)
    cos, sin = jnp.cos(ang), jnp.sin(ang)
    c = jnp.tile(jnp.concatenate([cos, cos], axis=-1), (1, reps))
    s = jnp.tile(jnp.concatenate([-sin, sin], axis=-1), (1, reps))
    return _pad_cols(c, LANES), _pad_cols(s, LANES)


def _layer_weights(l, P, q_abs, ab_re, ab_im, bb_re, bb_im):
    w_in = P["w_in"][l]
    seg = {}
    o = 0
    for name, width in (("u", SSM_CH), ("ql", MLA_Q_RANK), ("kvl", MLA_KV_RANK), ("kr", MLA_ROPE),
                        ("qs", SWA_HEADS * HEAD_DIM), ("ks", SWA_KV_HEADS * HEAD_DIM),
                        ("vs", SWA_KV_HEADS * HEAD_DIM)):
        seg[name] = w_in[:, o:o + width]
        o += width
    ctx_cols = [seg["u"], seg["ql"], seg["kvl"], seg["qs"], seg["ks"], seg["vs"], _pad_cols(seg["kr"], LANES)]
    lat_cols = ctx_cols + [
        _swap_halves(seg["qs"], SWA_HEADS, HEAD_DIM),
        _swap_halves(seg["ks"], SWA_KV_HEADS, HEAD_DIM),
        _pad_cols(_swap_halves(seg["kr"], 1, MLA_ROPE), LANES),
    ]
    w_qb = P["w_mla_qb"][l]
    w_rope = w_qb[:, :, MLA_NOPE:]
    w_rope_sw = w_rope.reshape(MLA_Q_RANK, MLA_HEADS, 2, MLA_ROPE // 2)[:, :, ::-1, :].reshape(w_rope.shape)
    flat_t = lambda w: w.reshape(MLA_Q_RANK, MLA_HEADS * MLA_ROPE).T
    wq_abs = jnp.transpose(q_abs[l], (0, 2, 1)).reshape(Q_ABS, MLA_Q_RANK)
    wq_ctx = jnp.concatenate([wq_abs, flat_t(w_rope)], axis=0)
    wq_lat = jnp.concatenate([wq_ctx, flat_t(w_rope_sw)], axis=0)

    eye = jnp.eye(SSM_GROUPS, dtype=F32)

    def block_diag_b(bb):
        bb = bb.reshape(2, SSM_GROUPS, SSM_STATE, SSM_GROUP)
        return jnp.einsum("dgpc,gh->dgchp", bb, eye).reshape(2, SSM_CH, SSM_N)

    def block_diag_c(cc):
        return jnp.einsum("dgcp,gh->dgphc", cc, eye).reshape(2, SSM_N, SSM_CH)

    sl = slice(2 * l, 2 * l + 2)
    w_router = jnp.concatenate([P["moe_w_expert"][l], P["moe_w_group"][l]], axis=1)
    b_router = jnp.concatenate([P["moe_b_expert"][l], P["moe_b_group"][l]])
    row = lambda v: v.reshape(1, -1)
    return dict(
        norm1=row(P["norm1"][l]), norm2=row(P["norm2"][l]),
        win_ctx=jnp.concatenate(ctx_cols, axis=1).astype(BF16),
        win_lat=jnp.concatenate(lat_cols, axis=1).astype(BF16),
        q_norm=row(P["mla_q_norm"][l]), kv_norm=row(P["mla_kv_norm"][l]),
        wq_ctx=wq_ctx.astype(BF16), wq_lat=wq_lat.astype(BF16),
        wv=jnp.transpose(P["w_mla_kvb"][l][:, :, MLA_NOPE:], (1, 0, 2)).astype(BF16),
        ssm_wb=jnp.concatenate([block_diag_b(bb_re[sl]), block_diag_b(bb_im[sl])], axis=2).astype(BF16),
        ssm_a=jnp.concatenate([ab_re[sl], ab_im[sl]], axis=1).reshape(2, 1, 2 * SSM_N),
        ssm_wc=jnp.concatenate([block_diag_c(P["ssm_c_re"][l]), -block_diag_c(P["ssm_c_im"][l])],
                               axis=1).astype(BF16),
        ssm_d=row(P["ssm_d"][l]), w_glu=P["w_ssm_glu"][l].astype(BF16),
        gn_ssm=row(P["gn_ssm"][l]), gn_mla=row(P["gn_mla"][l]), gn_swa=row(P["gn_swa"][l]),
        w_out=P["w_out"][l].astype(BF16),
        w_router=_pad_cols(w_router, LANES), b_router=_pad_cols(row(b_router), LANES),
        moe_wg=P["moe_w_gate"][l].astype(BF16).reshape(MOE_EXPERTS, D_MODEL, MOE_HIDDEN),
        moe_wu=P["moe_w_up"][l].astype(BF16).reshape(MOE_EXPERTS, D_MODEL, MOE_HIDDEN),
        moe_wd=P["moe_w_down"][l].astype(BF16).reshape(
            MOE_GROUPS * MOE_SPLIT, MOE_PER_GROUP // MOE_SPLIT * MOE_HIDDEN, D_MODEL),
        sink=P["swa_sink"][l],
    )


def _layer(x, mod, lw, final_norm, *, tables, ctx, per_batch_mod, final):
    B, S, _ = x.shape
    context_pass = ctx is None
    pre = _pre(x, mod, lw, tables, per_batch_mod=per_batch_mod, emit_ctx=context_pass)
    u_tm, qt, kcat, ckvt, qs, ks, vs = pre[:7]
    nb = B // SUBLANES
    if context_pass:
        h0 = jnp.zeros((2, nb, SUBLANES, 2 * SSM_N), F32)
        y2, hfin = _ssm(u_tm, lw, h0, B, S)
        o_mla = _mla(qt, kcat, ckvt, None, None, lw["wv"])
        o_swa = _swa(lw["sink"], qs, ks, vs, None, None)
        state = hfin.reshape(2, B, 2, SSM_GROUPS, SSM_STATE).transpose(1, 0, 2, 3, 4)
        new_ctx = (pre[7], pre[8], pre[9].reshape(B, S, SWA_KV_HEADS, HEAD_DIM),
                   pre[10].reshape(B, S, SWA_KV_HEADS, HEAD_DIM), state)
    else:
        kcat_c, ckvt_c, ks_c, vs_c, h0 = ctx
        y2, _ = _ssm(u_tm, lw, h0, B, S)
        o_mla = _mla(qt, kcat_c, ckvt_c, kcat, ckvt, lw["wv"])
        o_swa = _swa(lw["sink"], qs, ks_c, vs_c, ks, vs)
        new_ctx = None
    x1, h2, gates = _post(x, u_tm, y2, o_mla, o_swa, mod, lw, per_batch_mod=per_batch_mod)
    xo = _moe(h2, gates, x1, mod, lw, final_norm, per_batch_mod=per_batch_mod, final=final, rt=min(S, POST_TILE))
    return xo, new_ctx


def kernel(x_prompt, x_sample, c, cache_mla_ckv, cache_mla_krope, cache_swa_k, cache_swa_v, state_ssm, c_ctx, w_ada, b_ada, norm1, norm2, w_in, ssm_a_re, ssm_a_im, ssm_log_dt, ssm_b_re, ssm_b_im, ssm_c_re, ssm_c_im, ssm_d, w_ssm_glu, mla_q_norm, w_mla_qb, mla_kv_norm, w_mla_kvb, swa_sink, gn_ssm, gn_mla, gn_swa, w_out, moe_w_group, moe_b_group, moe_w_expert, moe_b_expert, moe_w_gate, moe_w_up, moe_w_down, final_norm):
    P = dict(w_ada=w_ada, b_ada=b_ada, norm1=norm1, norm2=norm2, w_in=w_in,
             ssm_c_re=ssm_c_re, ssm_c_im=ssm_c_im, ssm_d=ssm_d, w_ssm_glu=w_ssm_glu,
             mla_q_norm=mla_q_norm, w_mla_qb=w_mla_qb, mla_kv_norm=mla_kv_norm, w_mla_kvb=w_mla_kvb,
             swa_sink=swa_sink, gn_ssm=gn_ssm, gn_mla=gn_mla, gn_swa=gn_swa, w_out=w_out,
             moe_w_group=moe_w_group, moe_b_group=moe_b_group, moe_w_expert=moe_w_expert,
             moe_b_expert=moe_b_expert, moe_w_gate=moe_w_gate, moe_w_up=moe_w_up, moe_w_down=moe_w_down)
    n_dec = c.shape[0]
    n_cond = 2 * SUBLANES
    conds = jnp.zeros((n_cond, D_MODEL), F32).at[:n_dec].set(c).at[n_dec].set(c_ctx)
    mods = _modulation(conds, w_ada, b_ada).reshape(DEPTH, n_cond, 6, D_MODEL)

    ab_re, ab_im, bb_re, bb_im = _ssm_discretise(ssm_a_re, ssm_a_im, ssm_log_dt, ssm_b_re, ssm_b_im)
    q_abs = _absorb_q(jnp.transpose(w_mla_qb[..., :MLA_NOPE], (0, 2, 1, 3)),
                      jnp.transpose(w_mla_kvb[..., :MLA_NOPE], (0, 2, 1, 3)))
    lws = [_layer_weights(l, P, q_abs, ab_re.reshape(2 * DEPTH, SSM_N), ab_im.reshape(2 * DEPTH, SSM_N),
                          bb_re, bb_im) for l in range(DEPTH)]
    fnorm = final_norm.reshape(1, D_MODEL)

    xp = x_prompt
    ctx_states = []
    for l in range(DEPTH):
        xp, new = _layer(xp, mods[l, n_dec:n_dec + 1], lws[l], fnorm, tables=None, ctx=None,
                         per_batch_mod=False, final=l == DEPTH - 1)
        ctx_states.append(new)
    outs_ctx = tuple(jnp.stack([s[k] for s in ctx_states], axis=1) for k in range(5))

    n_lat = x_sample.shape[1]
    cm, sm = _rope_tables(n_lat, MLA_ROPE, 1)
    cs, ss = _rope_tables(n_lat, HEAD_DIM, LANES // HEAD_DIM)
    tables = (cm, sm, cm[:, :MLA_ROPE].T, sm[:, :MLA_ROPE].T, cs, ss)
    xs = x_sample
    past = cache_mla_ckv.shape[2]
    for l in range(DEPTH):
        kcat_c = jnp.concatenate(
            [cache_mla_ckv[:, l], cache_mla_krope[:, l],
             jnp.zeros((n_dec, past, LANES - MLA_ROPE), F32)], axis=-1).astype(BF16)
        ks_c = cache_swa_k[:, l].reshape(n_dec, past, LANES).astype(BF16)
        vs_c = cache_swa_v[:, l].reshape(n_dec, past, LANES).astype(BF16)
        h0 = state_ssm[:, l].transpose(1, 0, 2, 3, 4).reshape(2, n_dec // SUBLANES, SUBLANES, 2 * SSM_N)
        ckvt_c = jnp.transpose(cache_mla_ckv[:, l], (0, 2, 1)).astype(BF16)
        xs, _ = _layer(xs, mods[l, :n_dec], lws[l], fnorm, tables=tables, ctx=(kcat_c, ckvt_c, ks_c, vs_c, h0),
                       per_batch_mod=True, final=l == DEPTH - 1)
    return (xp, xs) + outs_ctx
```

```python
import functools
import math

import jax
import jax.numpy as jnp
from jax import lax
from jax.experimental import pallas as pl
from jax.experimental.pallas import tpu as pltpu

F32 = jnp.float32
BF16 = jnp.bfloat16

D_MODEL = 1024
DEPTH = 4
GRID_W = 64
HEAD_DIM = 64
SSM_CH = 256
SSM_GROUP = 16
SSM_GROUPS = SSM_CH // SSM_GROUP
SSM_STATE = 64
SSM_N = SSM_GROUPS * SSM_STATE
MLA_HEADS = 6
MLA_Q_RANK = 256
MLA_KV_RANK = 128
MLA_NOPE = 64
MLA_ROPE = 32
MLA_V = 64
SWA_HEADS = 6
SWA_KV_HEADS = 2
SWA_GROUP = SWA_HEADS // SWA_KV_HEADS
SWA_WINDOW = 128
SWA_BLOCK = 128
MOE_GROUPS = 4
MOE_PER_GROUP = 8
MOE_EXPERTS = MOE_GROUPS * MOE_PER_GROUP
MOE_HIDDEN = 256
GROUP_LANE = MOE_EXPERTS
RANK_LANE = GROUP_LANE + 1
COUNT_LANE = GROUP_LANE + 2
POST_TILE = 512
MOE_SPLIT = 2
MOE_TILE = 1024
MOE_ROW_BLOCK = 128
MOE_SCATTER_BLOCK = 256
MOE_CODE_STRIDE = 1 << 16
ROPE_BASE = 10000.0
EPS = 1e-6
MLA_SCALE = 1.0 / math.sqrt(MLA_NOPE + MLA_ROPE)
MLA_SCALE_LOG2 = MLA_SCALE * math.log2(math.e)
SWA_SCALE = 1.0 / math.sqrt(HEAD_DIM)
SWA_SCALE_LOG2 = SWA_SCALE * math.log2(math.e)

LANES = 128
SUBLANES = 8
VMEM_LIMIT = 52 * 1024 * 1024

OFF_U = 0
OFF_QL = OFF_U + SSM_CH
OFF_KVL = OFF_QL + MLA_Q_RANK
OFF_KS = OFF_KVL + MLA_KV_RANK
OFF_VS = OFF_KS + SWA_KV_HEADS * HEAD_DIM
OFF_KR = OFF_VS + SWA_KV_HEADS * HEAD_DIM
OFF_TAIL = OFF_KR + LANES
OFF_QS = OFF_TAIL
N_PACK_CTX = OFF_QS + SWA_HEADS * HEAD_DIM
OFF_KS_SW = OFF_TAIL
OFF_KR_SW = OFF_KS_SW + SWA_KV_HEADS * HEAD_DIM
N_PACK_LAT = OFF_KR_SW + LANES
Q_ABS = MLA_HEADS * MLA_KV_RANK
MLA_QK = 2 * LANES


def _cparams(sem):
    return pltpu.CompilerParams(dimension_semantics=sem, vmem_limit_bytes=VMEM_LIMIT)


def _dot(a, b):
    return jnp.dot(a, b, preferred_element_type=F32)


def _dot_nt(a, b):
    return lax.dot_general(a, b, (((1,), (1,)), ((), ())), preferred_element_type=F32)


def _split(x):
    hi = x.astype(BF16)
    lo = (x - hi.astype(F32)).astype(BF16)
    return hi, lo


def _dot3(a, b):
    ah, al = _split(a)
    bh, bl = _split(b)
    return _dot(ah, bh) + _dot(al, bh) + _dot(ah, bl)


def _tree(op, x3):
    parts = [x3[i] for i in range(x3.shape[0])]
    while len(parts) > 1:
        pairs = [op(parts[i], parts[i + 1]) for i in range(0, len(parts) - 1, 2)]
        parts = pairs + parts[len(parts) - len(parts) % 2:]
    return parts[0]


def _rms(x, g):
    return x * lax.rsqrt(jnp.mean(x * x, axis=-1, keepdims=True) + EPS) * g


def _silu(x):
    return x * jax.nn.sigmoid(x)


def _gelu_tanh(x):
    return 0.5 * x * (1.0 + jnp.tanh(math.sqrt(2.0 / math.pi) * (x + 0.044715 * (x * x * x))))


def _mod_kernel(c_ref, w_ref, b_ref, o_ref):
    o_ref[...] = _dot3(_silu(c_ref[...]), w_ref[...]) + b_ref[...]


def _modulation(conds, w_ada, b_ada):
    n = conds.shape[0]
    tn = 1536
    return pl.pallas_call(
        _mod_kernel,
        grid=(DEPTH, 6 * D_MODEL // tn),
        in_specs=[
            pl.BlockSpec((n, D_MODEL), lambda l, j: (0, 0)),
            pl.BlockSpec((None, D_MODEL, tn), lambda l, j: (l, 0, j)),
            pl.BlockSpec((None, 1, tn), lambda l, j: (l, 0, j)),
        ],
        out_specs=pl.BlockSpec((None, n, tn), lambda l, j: (l, 0, j)),
        out_shape=jax.ShapeDtypeStruct((DEPTH, n, 6 * D_MODEL), F32),
        compiler_params=_cparams(("arbitrary", "arbitrary")),
        name="modulation",
    )(conds, w_ada, b_ada.reshape(DEPTH, 1, 6 * D_MODEL))


def _ssm_disc_kernel(are_ref, aim_ref, ldt_ref, bre_ref, bim_ref, abre_ref, abim_ref, bbre_ref, bbim_ref):
    lam_re = are_ref[...]
    lam_im = aim_ref[...]
    dt = jnp.exp(ldt_ref[...])
    z_re = lam_re * dt
    z_im = lam_im * dt
    mag = jnp.exp(z_re)
    ab_re = mag * jnp.cos(z_im)
    ab_im = mag * jnp.sin(z_im)
    den = lam_re * lam_re + lam_im * lam_im
    f_re = ((ab_re - 1.0) * lam_re + ab_im * lam_im) / den
    f_im = (ab_im * lam_re - (ab_re - 1.0) * lam_im) / den
    b_re = bre_ref[...]
    b_im = bim_ref[...]
    abre_ref[...] = ab_re
    abim_ref[...] = ab_im
    bbre_ref[...] = f_re * b_re - f_im * b_im
    bbim_ref[...] = f_re * b_im + f_im * b_re


def _ssm_discretise(a_re, a_im, log_dt, b_re, b_im):
    n = DEPTH * 2
    col = lambda v: v.reshape(n, SSM_N, 1)
    ldt = jnp.broadcast_to(log_dt[..., None], (DEPTH, 2, SSM_GROUPS, SSM_STATE))
    cspec = pl.BlockSpec((None, SSM_N, 1), lambda i: (i, 0, 0))
    bspec = pl.BlockSpec((None, SSM_N, SSM_GROUP), lambda i: (i, 0, 0))
    return pl.pallas_call(
        _ssm_disc_kernel,
        grid=(n,),
        in_specs=[cspec, cspec, cspec, bspec, bspec],
        out_specs=[cspec, cspec, bspec, bspec],
        out_shape=[jax.ShapeDtypeStruct((n, SSM_N, 1), F32)] * 2
        + [jax.ShapeDtypeStruct((n, SSM_N, SSM_GROUP), F32)] * 2,
        compiler_params=_cparams(("arbitrary",)),
        name="ssm_discretise",
    )(col(a_re), col(a_im), col(ldt), b_re.reshape(n, SSM_N, SSM_GROUP), b_im.reshape(n, SSM_N, SSM_GROUP))


def _absorb_kernel(wq_ref, wk_ref, o_ref):
    a = wq_ref[...]
    b = wk_ref[...]
    ah, al = _split(a)
    bh, bl = _split(b)
    o_ref[...] = _dot_nt(ah, bh) + _dot_nt(al, bh) + _dot_nt(ah, bl)


def _absorb_q(wq_nope, wk_nope):
    return pl.pallas_call(
        _absorb_kernel,
        grid=(DEPTH, MLA_HEADS),
        in_specs=[
            pl.BlockSpec((None, None, MLA_Q_RANK, MLA_NOPE), lambda l, h: (l, h, 0, 0)),
            pl.BlockSpec((None, None, MLA_KV_RANK, MLA_NOPE), lambda l, h: (l, h, 0, 0)),
        ],
        out_specs=pl.BlockSpec((None, None, MLA_Q_RANK, MLA_KV_RANK), lambda l, h: (l, h, 0, 0)),
        out_shape=jax.ShapeDtypeStruct((DEPTH, MLA_HEADS, MLA_Q_RANK, MLA_KV_RANK), F32),
        compiler_params=_cparams(("arbitrary", "arbitrary")),
        name="mla_absorb",
    )(wq_nope, wk_nope)


def _pre_kernel(*refs, rope, emit_ctx):
    it = iter(refs)
    x_ref, mod_ref, n1_ref, win_ref, qn_ref, kvn_ref, wq_ref = (next(it) for _ in range(7))
    if rope:
        cm_ref, sm_ref, cmt_ref, smt_ref, cs_ref, ss_ref, cst_ref, sst_ref, wqs_ref = (next(it) for _ in range(9))
    u_ref, qt_ref, kcat_ref, ckvt_ref, qs_ref, ks_ref, vs_ref = (next(it) for _ in range(7))
    if emit_ctx:
        ckv_o, kr_o, ks_o, vs_o = (next(it) for _ in range(4))

    x = x_ref[...]
    mod = mod_ref[...]
    h = _rms(x, n1_ref[...]) * (1.0 + mod[1:2]) + mod[0:1]
    hb = h.astype(BF16)
    proj = _dot(hb, win_ref[...])

    u_ref[...] = proj[:, OFF_U:OFF_U + SSM_CH]

    qln = _rms(proj[:, OFF_QL:OFF_QL + MLA_Q_RANK], qn_ref[...]).astype(BF16)
    qall = _dot_nt(wq_ref[...], qln)
    n_rope = MLA_HEADS * MLA_ROPE
    zero_rows = jnp.zeros((MLA_QK - MLA_KV_RANK - MLA_ROPE, qall.shape[1]), BF16)
    for i in range(MLA_HEADS):
        qa = qall[i * MLA_KV_RANK:(i + 1) * MLA_KV_RANK]
        qr = qall[Q_ABS + i * MLA_ROPE:Q_ABS + (i + 1) * MLA_ROPE]
        if rope:
            qr_sw = qall[Q_ABS + n_rope + i * MLA_ROPE:Q_ABS + n_rope + (i + 1) * MLA_ROPE]
            qr = qr * cmt_ref[...] + qr_sw * smt_ref[...]
        base = i * MLA_QK
        qt_ref[base:base + MLA_KV_RANK, :] = (qa * MLA_SCALE_LOG2).astype(BF16)
        qt_ref[base + MLA_KV_RANK:base + MLA_KV_RANK + MLA_ROPE, :] = (qr * MLA_SCALE_LOG2).astype(BF16)
        qt_ref[base + MLA_KV_RANK + MLA_ROPE:base + MLA_QK, :] = zero_rows

    ckv = _rms(proj[:, OFF_KVL:OFF_KVL + MLA_KV_RANK], kvn_ref[...])
    kr = proj[:, OFF_KR:OFF_KR + LANES]
    if emit_ctx:
        ckv_o[...] = ckv
        kr_o[...] = kr[:, :MLA_ROPE]
    if rope:
        kr = kr * cm_ref[...] + proj[:, OFF_KR_SW:OFF_KR_SW + LANES] * sm_ref[...]
    kcat_ref[...] = jnp.concatenate([ckv, kr], axis=-1).astype(BF16)
    ckvt_ref[...] = ckv.T.astype(BF16)

    ks = proj[:, OFF_KS:OFF_KS + SWA_KV_HEADS * HEAD_DIM]
    vs = proj[:, OFF_VS:OFF_VS + SWA_KV_HEADS * HEAD_DIM]
    if emit_ctx:
        ks_o[...] = ks
        vs_o[...] = vs
    if rope:
        ks = ks * cs_ref[...] + proj[:, OFF_KS_SW:OFF_KS_SW + SWA_KV_HEADS * HEAD_DIM] * ss_ref[...]
        n_q = SWA_HEADS * HEAD_DIM
        qst = _dot_nt(wqs_ref[...], hb)
        zero = jnp.zeros((HEAD_DIM, SWA_BLOCK), BF16)
        for hd in range(SWA_HEADS):
            rows = slice(hd * HEAD_DIM, (hd + 1) * HEAD_DIM)
            rot = qst[rows] * cst_ref[...] + qst[n_q + hd * HEAD_DIM:n_q + (hd + 1) * HEAD_DIM] * sst_ref[...]
            rot = (rot * SWA_SCALE_LOG2).astype(BF16)
            kh = hd // SWA_GROUP
            for j in range(qs_ref.shape[0]):
                lanes = slice(hd * SWA_BLOCK, (hd + 1) * SWA_BLOCK)
                qs_ref[j, kh * HEAD_DIM:(kh + 1) * HEAD_DIM, lanes] = rot[:, j * SWA_BLOCK:(j + 1) * SWA_BLOCK]
                qs_ref[j, (1 - kh) * HEAD_DIM:(2 - kh) * HEAD_DIM, lanes] = zero
        vs_ref[...] = vs.T.astype(BF16)
    else:
        qs_ref[...] = (proj[:, OFF_QS:OFF_QS + SWA_HEADS * HEAD_DIM] * SWA_SCALE).astype(BF16)
        vs_ref[...] = vs.astype(BF16)
    ks_ref[...] = ks.astype(BF16)


def _pre(x, mod, lw, tables, *, per_batch_mod, emit_ctx):
    B, S, D = x.shape
    rope = tables is not None
    ts = min(S, 512)
    n_pack = N_PACK_LAT if rope else N_PACK_CTX
    win = lw["win_lat"] if rope else lw["win_ctx"]
    wq = lw["wq_lat"] if rope else lw["wq_ctx"]
    tok = lambda w: pl.BlockSpec((None, ts, w), lambda b, s: (b, s, 0))
    full = lambda a: pl.BlockSpec(a.shape, lambda b, s: (0,) * a.ndim)
    in_specs = [
        tok(D),
        pl.BlockSpec((None, 6, D), (lambda b, s: (b, 0, 0)) if per_batch_mod else (lambda b, s: (0, 0, 0))),
        full(lw["norm1"]), full(win), full(lw["q_norm"]), full(lw["kv_norm"]), full(wq),
    ]
    args = [x, mod, lw["norm1"], win, lw["q_norm"], lw["kv_norm"], wq]
    if rope:
        row_tab = pl.BlockSpec((ts, LANES), lambda b, s: (s, 0))
        col_tab = pl.BlockSpec((MLA_ROPE, ts), lambda b, s: (0, s))
        col_tab_s = pl.BlockSpec((HEAD_DIM, ts), lambda b, s: (0, s))
        in_specs += [row_tab, row_tab, col_tab, col_tab, row_tab, row_tab, col_tab_s, col_tab_s, full(lw["wqs_t"])]
        args += list(tables) + [lw["wqs_t"]]
    feat = lambda w: pl.BlockSpec((None, w, ts), lambda b, s: (b, 0, s))
    if rope:
        q_tile = (2 * HEAD_DIM, SWA_HEADS * SWA_BLOCK)
        qs_spec = pl.BlockSpec((None, ts // SWA_BLOCK) + q_tile, lambda b, s: (b, s, 0, 0))
        qs_shape = jax.ShapeDtypeStruct((B, S // SWA_BLOCK) + q_tile, BF16)
        vs_spec, vs_shape = feat(LANES), jax.ShapeDtypeStruct((B, LANES, S), BF16)
    else:
        qs_spec, qs_shape = tok(SWA_HEADS * HEAD_DIM), jax.ShapeDtypeStruct((B, S, SWA_HEADS * HEAD_DIM), BF16)
        vs_spec, vs_shape = tok(LANES), jax.ShapeDtypeStruct((B, S, LANES), BF16)
    out_specs = [
        tok(SSM_CH),
        feat(MLA_HEADS * MLA_QK), tok(MLA_QK), feat(MLA_KV_RANK), qs_spec, tok(LANES), vs_spec,
    ]
    out_shape = [
        jax.ShapeDtypeStruct((B, S, SSM_CH), F32),
        jax.ShapeDtypeStruct((B, MLA_HEADS * MLA_QK, S), BF16),
        jax.ShapeDtypeStruct((B, S, MLA_QK), BF16),
        jax.ShapeDtypeStruct((B, MLA_KV_RANK, S), BF16),
        qs_shape,
        jax.ShapeDtypeStruct((B, S, LANES), BF16),
        vs_shape,
    ]
    if emit_ctx:
        out_specs += [tok(MLA_KV_RANK), tok(MLA_ROPE), tok(LANES), tok(LANES)]
        out_shape += [
            jax.ShapeDtypeStruct((B, S, MLA_KV_RANK), F32),
            jax.ShapeDtypeStruct((B, S, MLA_ROPE), F32),
            jax.ShapeDtypeStruct((B, S, LANES), F32),
            jax.ShapeDtypeStruct((B, S, LANES), F32),
        ]
    return pl.pallas_call(
        functools.partial(_pre_kernel, rope=rope, emit_ctx=emit_ctx),
        grid=(B, S // ts),
        in_specs=in_specs,
        out_specs=out_specs,
        out_shape=out_shape,
        compiler_params=_cparams(("arbitrary", "arbitrary")),
        name="pre_lat" if rope else "pre_ctx",
    )(*args)


def _ssm_kernel(uf_ref, ub_ref, wb_ref, a_ref, wc_ref, h0_ref, yf_ref, yb_ref, hfin_ref,
                hre_f, him_f, hre_b, him_b, st_s, ut_s, *, tc):
    i = pl.program_id(1)
    n = pl.num_programs(1)

    @pl.when(i == 0)
    def _():
        st_s[...] = h0_ref[...]

    cblk = SSM_CH // LANES
    u_refs = (uf_ref, ub_ref)
    h_refs = ((hre_f, him_f), (hre_b, him_b))
    for d in range(2):
        ub = u_refs[d][...]
        for b in range(SUBLANES):
            for j in range(cblk):
                ut_s[d, j, pl.ds(b, tc, stride=SUBLANES), :] = ub[b, :, j * LANES:(j + 1) * LANES]
        u = jnp.concatenate([ut_s[d, j] for j in range(cblk)], axis=-1).astype(BF16)
        bu = _dot(u, wb_ref[d])
        h_refs[d][0][...] = bu[:, :SSM_N]
        h_refs[d][1][...] = bu[:, SSM_N:]

    a_re = [jnp.broadcast_to(a_ref[d][:, :SSM_N], (SUBLANES, SSM_N)) for d in range(2)]
    a_im = [jnp.broadcast_to(a_ref[d][:, SSM_N:], (SUBLANES, SSM_N)) for d in range(2)]
    h_re = [st_s[d][:, :SSM_N] for d in range(2)]
    h_im = [st_s[d][:, SSM_N:] for d in range(2)]
    for t in range(tc):
        for d in range(2):
            tt = t if d == 0 else tc - 1 - t
            rows = slice(tt * SUBLANES, (tt + 1) * SUBLANES)
            hre_s, him_s = h_refs[d]
            n_re = a_re[d] * h_re[d] - a_im[d] * h_im[d] + hre_s[rows, :]
            n_im = a_re[d] * h_im[d] + a_im[d] * h_re[d] + him_s[rows, :]
            hre_s[rows, :] = n_re
            him_s[rows, :] = n_im
            h_re[d], h_im[d] = n_re, n_im

    y_refs = (yf_ref, yb_ref)
    for d in range(2):
        st_s[d] = jnp.concatenate([h_re[d], h_im[d]], axis=-1)
        hre_s, him_s = h_refs[d]
        y = _dot(hre_s[...].astype(BF16), wc_ref[d, :SSM_N, :]) + _dot(him_s[...].astype(BF16), wc_ref[d, SSM_N:, :])
        for j in range(cblk):
            ut_s[d, j] = y[:, j * LANES:(j + 1) * LANES]
        for b in range(SUBLANES):
            y_refs[d][b] = jnp.concatenate(
                [ut_s[d, j, pl.ds(b, tc, stride=SUBLANES), :] for j in range(cblk)], axis=-1)

    @pl.when(i == n - 1)
    def _():
        hfin_ref[...] = st_s[...]


def _ssm(u, lw, h0, B, S):
    nb = B // SUBLANES
    tc = 128
    nchunk = S // tc
    full3 = lambda a: pl.BlockSpec(a.shape, lambda b, i: (0, 0, 0))
    state_spec = pl.BlockSpec((2, None, SUBLANES, 2 * SSM_N), lambda b, i: (0, b, 0, 0))
    fwd = pl.BlockSpec((SUBLANES, tc, SSM_CH), lambda b, i: (b, i, 0))
    bwd = pl.BlockSpec((SUBLANES, tc, SSM_CH), lambda b, i: (b, nchunk - 1 - i, 0))
    yf, yb, hfin = pl.pallas_call(
        functools.partial(_ssm_kernel, tc=tc),
        grid=(nb, nchunk),
        in_specs=[fwd, bwd, full3(lw["ssm_wb"]), full3(lw["ssm_a"]), full3(lw["ssm_wc"]), state_spec],
        out_specs=[fwd, bwd, state_spec],
        out_shape=[
            jax.ShapeDtypeStruct((B, S, SSM_CH), F32),
            jax.ShapeDtypeStruct((B, S, SSM_CH), F32),
            jax.ShapeDtypeStruct((2, nb, SUBLANES, 2 * SSM_N), F32),
        ],
        scratch_shapes=[pltpu.VMEM((tc * SUBLANES, SSM_N), F32)] * 4 + [
            pltpu.VMEM((2, SUBLANES, 2 * SSM_N), F32),
            pltpu.VMEM((2, SSM_CH // LANES, tc * SUBLANES, LANES), F32),
        ],
        compiler_params=_cparams(("arbitrary", "arbitrary")),
        name="ssm_scan",
    )(u, u, lw["ssm_wb"], lw["ssm_a"], lw["ssm_wc"], h0)
    return (yf, yb), hfin


def _mla_kernel(*refs, with_latent, tk):
    if with_latent:
        qt_ref, ka_ref, vat_ref, kb_ref, vbt_ref, wv_ref, o_ref = refs
    else:
        qt_ref, ka_ref, vat_ref, wv_ref, o_ref = refs
    tq = qt_ref.shape[1]
    q_of = lambda hd: qt_ref[hd * MLA_QK:(hd + 1) * MLA_QK, :]

    def tile(state, k, vt):
        new_state = []
        s_next = _dot(k, q_of(0))
        for hd in range(MLA_HEADS):
            m8, l8, acc = state[hd]
            s = s_next
            if hd + 1 < MLA_HEADS:
                s_next = _dot(k, q_of(hd + 1))
            s3 = s.reshape(s.shape[0] // SUBLANES, SUBLANES, tq)
            mloc = jnp.max(_tree(jnp.maximum, s3), axis=0, keepdims=True)
            m8_new = jnp.maximum(m8, jnp.broadcast_to(mloc, (SUBLANES, tq)))
            alpha8 = jnp.exp2(m8 - m8_new)
            p3 = jnp.exp2(s3 - m8_new[None])
            l8 = alpha8 * l8 + _tree(jnp.add, p3)
            pv = _dot(vt, p3.reshape(s.shape).astype(BF16))
            acc3 = acc.reshape(MLA_KV_RANK // SUBLANES, SUBLANES, tq) * alpha8[None]
            new_state.append((m8_new, l8, acc3.reshape(MLA_KV_RANK, tq) + pv))
        return tuple(new_state)

    init = tuple((jnp.full((SUBLANES, tq), -jnp.inf, F32), jnp.zeros((SUBLANES, tq), F32),
                  jnp.zeros((MLA_KV_RANK, tq), F32)) for _ in range(MLA_HEADS))
    state = tile(init, ka_ref[...], vat_ref[...])
    if with_latent:
        for j in range(kb_ref.shape[0] // tk):
            state = tile(state, kb_ref[j * tk:(j + 1) * tk, :], vbt_ref[:, j * tk:(j + 1) * tk])

    outs = []
    for hd in range(MLA_HEADS):
        _, l8, acc = state[hd]
        o_lat = (acc / jnp.sum(l8, axis=0, keepdims=True)).T.astype(BF16)
        outs.append(_dot(o_lat, wv_ref[hd]))
    o_ref[...] = jnp.concatenate(outs, axis=-1)


def _mla(qt, ka, vat, kb, vbt, wv):
    B, _, S = qt.shape
    tq = 256
    with_latent = kb is not None
    kspec = lambda a: pl.BlockSpec((None,) + a.shape[1:], lambda b, s: (b, 0, 0))
    in_specs = [pl.BlockSpec((None, MLA_HEADS * MLA_QK, tq), lambda b, s: (b, 0, s)), kspec(ka), kspec(vat)]
    args = [qt, ka, vat]
    if with_latent:
        in_specs += [kspec(kb), kspec(vbt)]
        args += [kb, vbt]
    in_specs.append(pl.BlockSpec(wv.shape, lambda b, s: (0, 0, 0)))
    args.append(wv)
    return pl.pallas_call(
        functools.partial(_mla_kernel, with_latent=with_latent, tk=4096),
        grid=(B, S // tq),
        in_specs=in_specs,
        out_specs=pl.BlockSpec((None, tq, MLA_HEADS * MLA_V), lambda b, s: (b, s, 0)),
        out_shape=jax.ShapeDtypeStruct((B, S, MLA_HEADS * MLA_V), F32),
        compiler_params=_cparams(("arbitrary", "arbitrary")),
        name="mla_lat" if with_latent else "mla_ctx",
    )(*args)


def _swa_ctx_kernel(sink_ref, q_ref, k_ref, v_ref, o_ref):
    q = q_ref[...]
    k = k_ref[...]
    v = v_ref[...]
    outs = []
    for hd in range(SWA_HEADS):
        kh = hd // SWA_GROUP
        qh = q[:, hd * HEAD_DIM:(hd + 1) * HEAD_DIM]
        ksl = slice(kh * HEAD_DIM, (kh + 1) * HEAD_DIM)
        sink = sink_ref[hd]
        s = _dot_nt(qh, k[:, ksl])
        m = jnp.maximum(jnp.max(s, axis=-1, keepdims=True), sink)
        p = jnp.exp(s - m)
        den = jnp.sum(p, axis=-1, keepdims=True) + jnp.exp(sink - m)
        outs.append(_dot(p.astype(BF16), v[:, ksl]) / den)
    o_ref[...] = jnp.concatenate(outs, axis=-1)


def _swa_ctx(sink, q, k, v):
    B, S, W = q.shape
    seq = lambda a: pl.BlockSpec((None,) + a.shape[1:], lambda b: (b, 0, 0))
    return pl.pallas_call(
        _swa_ctx_kernel,
        grid=(B,),
        in_specs=[pl.BlockSpec(memory_space=pltpu.SMEM), seq(q), seq(k), seq(v)],
        out_specs=seq(q),
        out_shape=jax.ShapeDtypeStruct((B, S, W), F32),
        compiler_params=_cparams(("arbitrary",)),
        name="swa_ctx",
    )(sink, q, k, v)


def _swa_lat_kernel(sink_ref, q_ref, ka_ref, vat_ref, kb_ref, vbt_ref, o_ref):
    n = pl.program_id(1)
    nblk = kb_ref.shape[0] // SWA_BLOCK
    n_lane = SWA_HEADS * SWA_BLOCK
    win = 3 * SWA_BLOCK
    start = pl.multiple_of(jnp.clip(n - 1, 0, nblk - 3) * SWA_BLOCK, SWA_BLOCK)
    q = q_ref[...]
    s_a = _dot(ka_ref[...], q)
    s_b = _dot(kb_ref[pl.ds(start, win), :], q)
    kpos = start + lax.broadcasted_iota(jnp.int32, (win, n_lane), 0)
    qpos = n * SWA_BLOCK + (lax.broadcasted_iota(jnp.int32, (win, n_lane), 1) & (SWA_BLOCK - 1))
    s_b = jnp.where(jnp.abs(qpos - kpos) <= SWA_WINDOW, s_b, -jnp.inf)

    fold = lambda x: x.reshape(x.shape[0] // SUBLANES, SUBLANES, n_lane)
    sink = sink_ref[...]
    m8 = jnp.maximum(_tree(jnp.maximum, fold(s_a)), _tree(jnp.maximum, fold(s_b)))
    m = jnp.maximum(jnp.max(m8, axis=0, keepdims=True), sink)
    m8 = jnp.broadcast_to(m, (SUBLANES, n_lane))
    p_a = jnp.exp2(fold(s_a) - m8[None])
    p_b = jnp.exp2(fold(s_b) - m8[None])
    den = jnp.sum(_tree(jnp.add, p_a) + _tree(jnp.add, p_b), axis=0, keepdims=True) + jnp.exp2(sink - m)
    o_t = (_dot(vat_ref[...], p_a.reshape(s_a.shape).astype(BF16))
           + _dot(vbt_ref[:, pl.ds(start, win)], p_b.reshape(s_b.shape).astype(BF16))) / den
    o = o_t.T
    outs = []
    for hd in range(SWA_HEADS):
        kh = hd // SWA_GROUP
        outs.append(o[hd * SWA_BLOCK:(hd + 1) * SWA_BLOCK, kh * HEAD_DIM:(kh + 1) * HEAD_DIM])
    o_ref[...] = jnp.concatenate(outs, axis=-1)


def _swa_lat(sink_row, q_tiles, ka, vat, kb, vbt):
    B, nblk = q_tiles.shape[:2]
    width = SWA_HEADS * HEAD_DIM
    kspec = lambda a: pl.BlockSpec((None,) + a.shape[1:], lambda b, s: (b, 0, 0))
    return pl.pallas_call(
        _swa_lat_kernel,
        grid=(B, nblk),
        in_specs=[
            pl.BlockSpec(sink_row.shape, lambda b, s: (0, 0)),
            pl.BlockSpec((None, None) + q_tiles.shape[2:], lambda b, s: (b, s, 0, 0)),
            kspec(ka), kspec(vat), kspec(kb), kspec(vbt),
        ],
        out_specs=pl.BlockSpec((None, SWA_BLOCK, width), lambda b, s: (b, s, 0)),
        out_shape=jax.ShapeDtypeStruct((B, nblk * SWA_BLOCK, width), F32),
        compiler_params=_cparams(("arbitrary", "arbitrary")),
        name="swa_lat",
    )(sink_row, q_tiles, ka, vat, kb, vbt)


def _route(logits):
    lane = lax.broadcasted_iota(jnp.int32, logits.shape, 1)
    big = jnp.int32(1 << 20)
    is_g = (lane >= MOE_EXPERTS) & (lane < MOE_EXPERTS + MOE_GROUPS)
    lg = jnp.where(is_g, logits, -jnp.inf)
    mg = jnp.max(lg, axis=-1, keepdims=True)
    g_idx = jnp.min(jnp.where(lg == mg, lane - MOE_EXPERTS, big), axis=-1, keepdims=True)
    pg_top = 1.0 / jnp.sum(jnp.exp(lg - mg), axis=-1, keepdims=True)

    is_e = (lane < MOE_EXPERTS) & ((lane // MOE_PER_GROUP) == g_idx)
    le = jnp.where(is_e, logits, -jnp.inf)
    m1 = jnp.max(le, axis=-1, keepdims=True)
    e1 = jnp.min(jnp.where(le == m1, lane, big), axis=-1, keepdims=True)
    z = jnp.sum(jnp.exp(le - m1), axis=-1, keepdims=True)
    le2 = jnp.where(lane == e1, -jnp.inf, le)
    m2 = jnp.max(le2, axis=-1, keepdims=True)
    e2 = jnp.min(jnp.where(le2 == m2, lane, big), axis=-1, keepdims=True)
    p1 = 1.0 / z
    p2 = jnp.exp(m2 - m1) / z
    tot = p1 + p2
    gates = pg_top * (jnp.where(lane == e1, p1 / tot, 0.0) + jnp.where(lane == e2, p2 / tot, 0.0))
    t = logits.shape[0]
    chose = jnp.where(lane == g_idx + COUNT_LANE, 1.0, 0.0)
    tri = jnp.where(lax.broadcasted_iota(jnp.int32, (t, t), 0) >= lax.broadcasted_iota(jnp.int32, (t, t), 1),
                    1.0, 0.0).astype(BF16)
    counts = _dot(tri, chose.astype(BF16))
    rank = jnp.sum(chose * (counts - 1.0), axis=-1, keepdims=True)
    return (gates + counts + jnp.where(lane == GROUP_LANE, g_idx.astype(F32), 0.0)
            + jnp.where(lane == RANK_LANE, rank, 0.0))


def _post_kernel(x_ref, u_ref, yf_ref, yb_ref, om_ref, os_ref, mod_ref, d_ref, wglu_ref, gs_ref, gm_ref,
                 gw_ref, wout_ref, n2_ref, wr_ref, br_ref, x1_ref, h2_ref, gate_ref):
    mod = mod_ref[...]
    y = d_ref[...] * u_ref[...] + yf_ref[...] + yb_ref[...]
    ga = _dot(_gelu_tanh(y).astype(BF16), wglu_ref[...])
    y_ssm = ga[:, :SSM_CH] * jax.nn.sigmoid(ga[:, SSM_CH:])
    wout = wout_ref
    n_mla = MLA_HEADS * MLA_V
    mixed = (_dot(_rms(y_ssm, gs_ref[...]).astype(BF16), wout[0:SSM_CH, :])
             + _dot(_rms(om_ref[...], gm_ref[...]).astype(BF16), wout[SSM_CH:SSM_CH + n_mla, :])
             + _dot(_rms(os_ref[...], gw_ref[...]).astype(BF16), wout[SSM_CH + n_mla:, :]))
    x1 = x_ref[...] + mod[2:3] * mixed
    x1_ref[...] = x1
    h2 = _rms(x1, n2_ref[...]) * (1.0 + mod[4:5]) + mod[3:4]
    h2_ref[...] = h2.astype(BF16)
    gate_ref[...] = _route(_dot3(h2, wr_ref[...]) + br_ref[...])


def _post(x, u_tm, y2, o_mla, o_swa, mod, lw, *, per_batch_mod):
    B, S, D = x.shape
    ts = min(S, POST_TILE)
    tok = lambda w: pl.BlockSpec((None, ts, w), lambda b, s: (b, s, 0))
    full = lambda a: pl.BlockSpec(a.shape, lambda b, s: (0,) * a.ndim)
    names = ["ssm_d", "w_glu", "gn_ssm", "gn_mla", "gn_swa", "w_out", "norm2", "w_router", "b_router"]
    in_specs = [
        tok(D),
        tok(SSM_CH),
        tok(SSM_CH), tok(SSM_CH),
        tok(MLA_HEADS * MLA_V), tok(SWA_HEADS * HEAD_DIM),
        pl.BlockSpec((None, 6, D), (lambda b, s: (b, 0, 0)) if per_batch_mod else (lambda b, s: (0, 0, 0))),
    ] + [full(lw[k]) for k in names]
    return pl.pallas_call(
        _post_kernel,
        grid=(B, S // ts),
        in_specs=in_specs,
        out_specs=[tok(D), tok(D), tok(LANES)],
        out_shape=[
            jax.ShapeDtypeStruct((B, S, D), F32),
            jax.ShapeDtypeStruct((B, S, D), BF16),
            jax.ShapeDtypeStruct((B, S, LANES), F32),
        ],
        compiler_params=_cparams(("arbitrary", "arbitrary")),
        name="post",
    )(x, u_tm, y2[0], y2[1], o_mla, o_swa, mod, *[lw[k] for k in names])


def _split3(x):
    hi = x.astype(BF16)
    r = x - hi.astype(F32)
    mid = r.astype(BF16)
    lo = (r - mid.astype(F32)).astype(BF16)
    return hi, mid, lo


def _moe_kernel(cnt_ref, h2_ref, gate_ref, crow_ref, x1_ref, mod_ref, wg_ref, wu_ref, wd_ref, fn_ref,
                o_ref, xs_s, gs_s, ys_s, acc_s, *, final, n_tiles):
    tm, d_model = acc_s.shape
    rb = MOE_ROW_BLOCK
    g = pl.program_id(2)
    e = pl.program_id(3)
    cnt = cnt_ref[(pl.program_id(0) * n_tiles + pl.program_id(1)) * MOE_GROUPS + g]
    n_blocks = (cnt + rb - 1) // rb

    def for_blocks(body, width=rb):
        def step(blk, carry):
            body(blk, pl.ds(pl.multiple_of(blk * width, width), width))
            return carry
        lax.fori_loop(0, (cnt + width - 1) // width, step, 0)

    @pl.when((g == 0) & (e == 0))
    def _():
        acc_s[...] = jnp.zeros_like(acc_s)

    @pl.when(e == 0)
    def _():
        rank = crow_ref[...] - g * MOE_CODE_STRIDE
        h2 = h2_ref[...].reshape(tm, d_model)
        gates = gate_ref[...].reshape(tm, LANES)

        def gather(blk, rows):
            row_id = lax.broadcasted_iota(jnp.int32, (rb, tm), 0) + blk * rb
            onehot = jnp.where(row_id == rank, 1.0, 0.0).astype(BF16)
            xs_s[rows, :] = _dot(onehot, h2).astype(BF16)
            g_hi, g_mid, g_lo = _split3(gates)
            gs_s[rows, :] = _dot(onehot, g_hi) + _dot(onehot, g_mid) + _dot(onehot, g_lo)
        for_blocks(gather)

        @pl.when(n_blocks % (MOE_SCATTER_BLOCK // rb) != 0)
        def _():
            ys_s[pl.ds(pl.multiple_of(n_blocks * rb, rb), rb), :] = jnp.zeros((rb, d_model), F32)

    n_exp = MOE_PER_GROUP // MOE_SPLIT

    def experts(blk, rows, first):
        x = xs_s[rows, :]
        gs = gs_s[rows, :]
        lane = lax.broadcasted_iota(jnp.int32, gs.shape, 1)
        first_expert = g * MOE_PER_GROUP + e * n_exp
        gate = jnp.concatenate(
            [jnp.broadcast_to(jnp.sum(jnp.where(lane == first_expert + k, gs, 0.0), axis=-1, keepdims=True),
                              (rb, MOE_HIDDEN)) for k in range(n_exp)], axis=-1)
        wide = lambda w_ref: jnp.concatenate([_dot(x, w_ref[k]) for k in range(n_exp)], axis=-1)
        hid = _silu(wide(wg_ref)) * wide(wu_ref) * gate
        y = _dot(hid.astype(BF16), wd_ref[...])
        if first:
            ys_s[rows, :] = y
        else:
            ys_s[rows, :] += y

    @pl.when(e == 0)
    def _():
        for_blocks(lambda blk, rows: experts(blk, rows, True))

    @pl.when(e != 0)
    def _():
        for_blocks(lambda blk, rows: experts(blk, rows, False))

    @pl.when(e == MOE_SPLIT - 1)
    def _():
        code_t = jnp.broadcast_to(crow_ref[...].astype(F32), (LANES, tm)).T
        rank = code_t.astype(jnp.int32) - g * MOE_CODE_STRIDE
        lane = lax.broadcasted_iota(jnp.int32, (tm, LANES), 1)

        sb = MOE_SCATTER_BLOCK

        def scatter(blk, rows):
            onehot = jnp.concatenate(
                [jnp.where(lane + (blk * sb + part * LANES) == rank, 1.0, 0.0) for part in range(sb // LANES)],
                axis=-1).astype(BF16)
            acc_s[...] += _dot(onehot, ys_s[rows, :].astype(BF16))
        for_blocks(scatter, sb)

    @pl.when((g == MOE_GROUPS - 1) & (e == MOE_SPLIT - 1))
    def _():
        xo = x1_ref[...].reshape(tm, d_model) + mod_ref[5:6, :] * acc_s[...]
        xo = _rms(xo, fn_ref[...]) if final else xo
        o_ref[...] = xo.reshape(o_ref.shape)


def _moe(h2, gates, x1, mod, lw, final_norm, *, per_batch_mod, final, rt):
    B, S, D = x1.shape
    tm = MOE_TILE
    if S >= tm:
        nb, nt = B, S // tm
        tok = lambda w: pl.BlockSpec((None, tm, w), lambda b, s, g, e, c: (b, s, 0))
    else:
        assert not per_batch_mod
        nb, nt = B * S // tm, 1
        tok = lambda w: pl.BlockSpec((tm // S, S, w), lambda b, s, g, e, c: (b, 0, 0))
    sub = tm // rt
    info = gates[..., GROUP_LANE:COUNT_LANE + MOE_GROUPS].astype(jnp.int32)
    grp = info[..., 0].reshape(nb, nt, sub, rt)
    sub_counts = info[:, rt - 1::rt, 2:].reshape(nb, nt, sub, MOE_GROUPS)
    offsets = jnp.cumsum(sub_counts, axis=2) - sub_counts
    onehot = grp[..., None] == jnp.arange(MOE_GROUPS, dtype=jnp.int32)
    rank = info[..., 1].reshape(nb, nt, sub, rt) + jnp.sum(jnp.where(onehot, offsets[:, :, :, None, :], 0), axis=-1)
    counts = jnp.sum(sub_counts, axis=2).reshape(-1)
    code = grp * MOE_CODE_STRIDE + rank
    code_row = code.reshape(nb, nt, 1, tm)

    n_exp = MOE_PER_GROUP // MOE_SPLIT
    step_map = lambda b, s, g, e, c: (g * MOE_SPLIT + e, 0, 0)
    mod_map = (lambda b, s, g, e, c: (b, 0, 0)) if per_batch_mod else (lambda b, s, g, e, c: (0, 0, 0))
    grid_spec = pltpu.PrefetchScalarGridSpec(
        num_scalar_prefetch=1,
        grid=(nb, nt, MOE_GROUPS, MOE_SPLIT),
        in_specs=[
            tok(D), tok(LANES),
            pl.BlockSpec((None, None, 1, tm), lambda b, s, g, e, c: (b, s, 0, 0)),
            tok(D),
            pl.BlockSpec((None, 6, D), mod_map),
            pl.BlockSpec((n_exp, D, MOE_HIDDEN), step_map), pl.BlockSpec((n_exp, D, MOE_HIDDEN), step_map),
            pl.BlockSpec((None, n_exp * MOE_HIDDEN, D), step_map),
            pl.BlockSpec(final_norm.shape, lambda b, s, g, e, c: (0, 0)),
        ],
        out_specs=tok(D),
        scratch_shapes=[
            pltpu.VMEM((tm, D), BF16),
            pltpu.VMEM((tm, LANES), F32),
            pltpu.VMEM((tm, D), F32),
            pltpu.VMEM((tm, D), F32),
        ],
    )
    return pl.pallas_call(
        functools.partial(_moe_kernel, final=final, n_tiles=nt),
        grid_spec=grid_spec,
        out_shape=jax.ShapeDtypeStruct((B, S, D), F32),
        compiler_params=_cparams(("arbitrary", "arbitrary", "arbitrary", "arbitrary")),
        name="moe",
    )(counts, h2, gates, code_row, x1, mod, lw["moe_wg"], lw["moe_wu"], lw["moe_wd"], final_norm)


def _swap_halves(w, n_heads, dim):
    k = w.shape[0]
    w = w.reshape(k, n_heads, 2, dim // 2)
    return w[:, :, ::-1, :].reshape(k, n_heads * dim)


def _pad_cols(w, width):
    return jnp.pad(w, ((0, 0), (0, width - w.shape[1])))


def _rope_tables(n_tokens, rot_dim, reps):
    t = jnp.arange(n_tokens)
    row = (t // GRID_W).astype(F32)
    col = (t % GRID_W).astype(F32)
    n_freq = rot_dim // 4
    inv_freq = ROPE_BASE ** (-jnp.arange(n_freq, dtype=F32) / n_freq)
    ang = jnp.concatenate([row[:, None] * inv_freq, col[:, None] * inv_freq], axis=-1)
    cos, sin = jnp.cos(ang), jnp.sin(ang)
    c = jnp.tile(jnp.concatenate([cos, cos], axis=-1), (1, reps))
    s = jnp.tile(jnp.concatenate([-sin, sin], axis=-1), (1, reps))
    return _pad_cols(c, LANES), _pad_cols(s, LANES)


def _layer_weights(l, P, q_abs, ab_re, ab_im, bb_re, bb_im):
    w_in = P["w_in"][l]
    seg = {}
    o = 0
    for name, width in (("u", SSM_CH), ("ql", MLA_Q_RANK), ("kvl", MLA_KV_RANK), ("kr", MLA_ROPE),
                        ("qs", SWA_HEADS * HEAD_DIM), ("ks", SWA_KV_HEADS * HEAD_DIM),
                        ("vs", SWA_KV_HEADS * HEAD_DIM)):
        seg[name] = w_in[:, o:o + width]
        o += width
    shared_cols = [seg["u"], seg["ql"], seg["kvl"], seg["ks"], seg["vs"], _pad_cols(seg["kr"], LANES)]
    ctx_cols = shared_cols + [seg["qs"]]
    lat_cols = shared_cols + [
        _swap_halves(seg["ks"], SWA_KV_HEADS, HEAD_DIM),
        _pad_cols(_swap_halves(seg["kr"], 1, MLA_ROPE), LANES),
    ]
    wqs_t = jnp.concatenate([seg["qs"], _swap_halves(seg["qs"], SWA_HEADS, HEAD_DIM)], axis=1).T
    w_qb = P["w_mla_qb"][l]
    w_rope = w_qb[:, :, MLA_NOPE:]
    w_rope_sw = w_rope.reshape(MLA_Q_RANK, MLA_HEADS, 2, MLA_ROPE // 2)[:, :, ::-1, :].reshape(w_rope.shape)
    flat_t = lambda w: w.reshape(MLA_Q_RANK, MLA_HEADS * MLA_ROPE).T
    wq_abs = jnp.transpose(q_abs[l], (0, 2, 1)).reshape(Q_ABS, MLA_Q_RANK)
    wq_ctx = jnp.concatenate([wq_abs, flat_t(w_rope)], axis=0)
    wq_lat = jnp.concatenate([wq_ctx, flat_t(w_rope_sw)], axis=0)

    eye = jnp.eye(SSM_GROUPS, dtype=F32)

    def block_diag_b(bb):
        bb = bb.reshape(2, SSM_GROUPS, SSM_STATE, SSM_GROUP)
        return jnp.einsum("dgpc,gh->dgchp", bb, eye).reshape(2, SSM_CH, SSM_N)

    def block_diag_c(cc):
        return jnp.einsum("dgcp,gh->dgphc", cc, eye).reshape(2, SSM_N, SSM_CH)

    sl = slice(2 * l, 2 * l + 2)
    w_router = jnp.concatenate([P["moe_w_expert"][l], P["moe_w_group"][l]], axis=1)
    b_router = jnp.concatenate([P["moe_b_expert"][l], P["moe_b_group"][l]])
    row = lambda v: v.reshape(1, -1)
    return dict(
        norm1=row(P["norm1"][l]), norm2=row(P["norm2"][l]),
        win_ctx=jnp.concatenate(ctx_cols, axis=1).astype(BF16),
        win_lat=jnp.concatenate(lat_cols, axis=1).astype(BF16), wqs_t=wqs_t.astype(BF16),
        q_norm=row(P["mla_q_norm"][l]), kv_norm=row(P["mla_kv_norm"][l]),
        wq_ctx=wq_ctx.astype(BF16), wq_lat=wq_lat.astype(BF16),
        wv=jnp.transpose(P["w_mla_kvb"][l][:, :, MLA_NOPE:], (1, 0, 2)).astype(BF16),
        ssm_wb=jnp.concatenate([block_diag_b(bb_re[sl]), block_diag_b(bb_im[sl])], axis=2).astype(BF16),
        ssm_a=jnp.concatenate([ab_re[sl], ab_im[sl]], axis=1).reshape(2, 1, 2 * SSM_N),
        ssm_wc=jnp.concatenate([block_diag_c(P["ssm_c_re"][l]), -block_diag_c(P["ssm_c_im"][l])],
                               axis=1).astype(BF16),
        ssm_d=row(P["ssm_d"][l]), w_glu=P["w_ssm_glu"][l].astype(BF16),
        gn_ssm=row(P["gn_ssm"][l]), gn_mla=row(P["gn_mla"][l]), gn_swa=row(P["gn_swa"][l]),
        w_out=P["w_out"][l].astype(BF16),
        w_router=_pad_cols(w_router, LANES), b_router=_pad_cols(row(b_router), LANES),
        moe_wg=P["moe_w_gate"][l].astype(BF16).reshape(MOE_EXPERTS, D_MODEL, MOE_HIDDEN),
        moe_wu=P["moe_w_up"][l].astype(BF16).reshape(MOE_EXPERTS, D_MODEL, MOE_HIDDEN),
        moe_wd=P["moe_w_down"][l].astype(BF16).reshape(
            MOE_GROUPS * MOE_SPLIT, MOE_PER_GROUP // MOE_SPLIT * MOE_HIDDEN, D_MODEL),
        sink=P["swa_sink"][l],
        sink_row=jnp.repeat(P["swa_sink"][l] * math.log2(math.e), SWA_BLOCK).reshape(1, SWA_HEADS * SWA_BLOCK),
    )


def _layer(x, mod, lw, final_norm, *, tables, ctx, per_batch_mod, final):
    B, S, _ = x.shape
    context_pass = ctx is None
    pre = _pre(x, mod, lw, tables, per_batch_mod=per_batch_mod, emit_ctx=context_pass)
    u_tm, qt, kcat, ckvt, qs, ks, vs = pre[:7]
    nb = B // SUBLANES
    if context_pass:
        h0 = jnp.zeros((2, nb, SUBLANES, 2 * SSM_N), F32)
        y2, hfin = _ssm(u_tm, lw, h0, B, S)
        o_mla = _mla(qt, kcat, ckvt, None, None, lw["wv"])
        o_swa = _swa_ctx(lw["sink"], qs, ks, vs)
        state = hfin.reshape(2, B, 2, SSM_GROUPS, SSM_STATE).transpose(1, 0, 2, 3, 4)
        new_ctx = (pre[7], pre[8], pre[9].reshape(B, S, SWA_KV_HEADS, HEAD_DIM),
                   pre[10].reshape(B, S, SWA_KV_HEADS, HEAD_DIM), state)
    else:
        kcat_c, ckvt_c, ks_c, vs_c, h0 = ctx
        y2, _ = _ssm(u_tm, lw, h0, B, S)
        o_mla = _mla(qt, kcat_c, ckvt_c, kcat, ckvt, lw["wv"])
        o_swa = _swa_lat(lw["sink_row"], qs, ks_c, vs_c, ks, vs)
        new_ctx = None
    x1, h2, gates = _post(x, u_tm, y2, o_mla, o_swa, mod, lw, per_batch_mod=per_batch_mod)
    xo = _moe(h2, gates, x1, mod, lw, final_norm, per_batch_mod=per_batch_mod, final=final, rt=min(S, POST_TILE))
    return xo, new_ctx


def kernel(x_prompt, x_sample, c, cache_mla_ckv, cache_mla_krope, cache_swa_k, cache_swa_v, state_ssm, c_ctx, w_ada, b_ada, norm1, norm2, w_in, ssm_a_re, ssm_a_im, ssm_log_dt, ssm_b_re, ssm_b_im, ssm_c_re, ssm_c_im, ssm_d, w_ssm_glu, mla_q_norm, w_mla_qb, mla_kv_norm, w_mla_kvb, swa_sink, gn_ssm, gn_mla, gn_swa, w_out, moe_w_group, moe_b_group, moe_w_expert, moe_b_expert, moe_w_gate, moe_w_up, moe_w_down, final_norm):
    P = dict(w_ada=w_ada, b_ada=b_ada, norm1=norm1, norm2=norm2, w_in=w_in,
             ssm_c_re=ssm_c_re, ssm_c_im=ssm_c_im, ssm_d=ssm_d, w_ssm_glu=w_ssm_glu,
             mla_q_norm=mla_q_norm, w_mla_qb=w_mla_qb, mla_kv_norm=mla_kv_norm, w_mla_kvb=w_mla_kvb,
             swa_sink=swa_sink, gn_ssm=gn_ssm, gn_mla=gn_mla, gn_swa=gn_swa, w_out=w_out,
             moe_w_group=moe_w_group, moe_b_group=moe_b_group, moe_w_expert=moe_w_expert,
             moe_b_expert=moe_b_expert, moe_w_gate=moe_w_gate, moe_w_up=moe_w_up, moe_w_down=moe_w_down)
    n_dec = c.shape[0]
    n_cond = 2 * SUBLANES
    conds = jnp.zeros((n_cond, D_MODEL), F32).at[:n_dec].set(c).at[n_dec].set(c_ctx)
    mods = _modulation(conds, w_ada, b_ada).reshape(DEPTH, n_cond, 6, D_MODEL)

    ab_re, ab_im, bb_re, bb_im = _ssm_discretise(ssm_a_re, ssm_a_im, ssm_log_dt, ssm_b_re, ssm_b_im)
    q_abs = _absorb_q(jnp.transpose(w_mla_qb[..., :MLA_NOPE], (0, 2, 1, 3)),
                      jnp.transpose(w_mla_kvb[..., :MLA_NOPE], (0, 2, 1, 3)))
    lws = [_layer_weights(l, P, q_abs, ab_re.reshape(2 * DEPTH, SSM_N), ab_im.reshape(2 * DEPTH, SSM_N),
                          bb_re, bb_im) for l in range(DEPTH)]
    fnorm = final_norm.reshape(1, D_MODEL)

    xp = x_prompt
    ctx_states = []
    for l in range(DEPTH):
        xp, new = _layer(xp, mods[l, n_dec:n_dec + 1], lws[l], fnorm, tables=None, ctx=None,
                         per_batch_mod=False, final=l == DEPTH - 1)
        ctx_states.append(new)
    outs_ctx = tuple(jnp.stack([s[k] for s in ctx_states], axis=1) for k in range(5))

    n_lat = x_sample.shape[1]
    cm, sm = _rope_tables(n_lat, MLA_ROPE, 1)
    cs, ss = _rope_tables(n_lat, HEAD_DIM, LANES // HEAD_DIM)
    tables = (cm, sm, cm[:, :MLA_ROPE].T, sm[:, :MLA_ROPE].T, cs, ss, cs[:, :HEAD_DIM].T, ss[:, :HEAD_DIM].T)
    xs = x_sample
    past = cache_mla_ckv.shape[2]
    for l in range(DEPTH):
        kcat_c = jnp.concatenate(
            [cache_mla_ckv[:, l], cache_mla_krope[:, l],
             jnp.zeros((n_dec, past, LANES - MLA_ROPE), F32)], axis=-1).astype(BF16)
        ks_c = cache_swa_k[:, l].reshape(n_dec, past, LANES).astype(BF16)
        vs_c = jnp.transpose(cache_swa_v[:, l].reshape(n_dec, past, LANES), (0, 2, 1)).astype(BF16)
        h0 = state_ssm[:, l].transpose(1, 0, 2, 3, 4).reshape(2, n_dec // SUBLANES, SUBLANES, 2 * SSM_N)
        ckvt_c = jnp.transpose(cache_mla_ckv[:, l], (0, 2, 1)).astype(BF16)
        xs, _ = _layer(xs, mods[l, :n_dec], lws[l], fnorm, tables=tables, ctx=(kcat_c, ckvt_c, ks_c, vs_c, h0),
                       per_batch_mod=True, final=l == DEPTH - 1)
    return (xp, xs) + outs_ctx
```

```python
import functools
import math

import jax
import jax.numpy as jnp
from jax import lax
from jax.experimental import pallas as pl
from jax.experimental.pallas import tpu as pltpu

F32 = jnp.float32
BF16 = jnp.bfloat16

D_MODEL = 1024
DEPTH = 4
GRID_W = 64
HEAD_DIM = 64
SSM_CH = 256
SSM_GROUP = 16
SSM_GROUPS = SSM_CH // SSM_GROUP
SSM_STATE = 64
SSM_N = SSM_GROUPS * SSM_STATE
MLA_HEADS = 6
MLA_Q_RANK = 256
MLA_KV_RANK = 128
MLA_NOPE = 64
MLA_ROPE = 32
MLA_V = 64
SWA_HEADS = 6
SWA_KV_HEADS = 2
SWA_GROUP = SWA_HEADS // SWA_KV_HEADS
SWA_WINDOW = 128
SWA_BLOCK = 128
MOE_GROUPS = 4
MOE_PER_GROUP = 8
MOE_EXPERTS = MOE_GROUPS * MOE_PER_GROUP
MOE_HIDDEN = 256
GROUP_LANE = MOE_EXPERTS
RANK_LANE = GROUP_LANE + 1
COUNT_LANE = GROUP_LANE + 2
POST_TILE = 512
MOE_SPLIT = 2
MOE_TILE = 1024
MOE_ROW_BLOCK = 128
MOE_SCATTER_BLOCK = 256
MOE_CODE_STRIDE = 1 << 16
ROPE_BASE = 10000.0
EPS = 1e-6
MLA_SCALE = 1.0 / math.sqrt(MLA_NOPE + MLA_ROPE)
MLA_SCALE_LOG2 = MLA_SCALE * math.log2(math.e)
SWA_SCALE = 1.0 / math.sqrt(HEAD_DIM)
SWA_SCALE_LOG2 = SWA_SCALE * math.log2(math.e)

LANES = 128
SUBLANES = 8
VMEM_LIMIT = 52 * 1024 * 1024

OFF_U = 0
OFF_QL = OFF_U + SSM_CH
OFF_KVL = OFF_QL + MLA_Q_RANK
OFF_KS = OFF_KVL + MLA_KV_RANK
OFF_VS = OFF_KS + SWA_KV_HEADS * HEAD_DIM
OFF_KR = OFF_VS + SWA_KV_HEADS * HEAD_DIM
OFF_TAIL = OFF_KR + LANES
OFF_QS = OFF_TAIL
N_PACK_CTX = OFF_QS + SWA_HEADS * HEAD_DIM
OFF_KS_SW = OFF_TAIL
OFF_KR_SW = OFF_KS_SW + SWA_KV_HEADS * HEAD_DIM
N_PACK_LAT = OFF_KR_SW + LANES
Q_ABS = MLA_HEADS * MLA_KV_RANK
MLA_QK = 2 * LANES


def _cparams(sem):
    return pltpu.CompilerParams(dimension_semantics=sem, vmem_limit_bytes=VMEM_LIMIT)


def _dot(a, b):
    return jnp.dot(a, b, preferred_element_type=F32)


def _dot_nt(a, b):
    return lax.dot_general(a, b, (((1,), (1,)), ((), ())), preferred_element_type=F32)


def _split(x):
    hi = x.astype(BF16)
    lo = (x - hi.astype(F32)).astype(BF16)
    return hi, lo


def _dot3(a, b):
    ah, al = _split(a)
    bh, bl = _split(b)
    return _dot(ah, bh) + _dot(al, bh) + _dot(ah, bl)


def _tree(op, x3):
    parts = [x3[i] for i in range(x3.shape[0])]
    while len(parts) > 1:
        pairs = [op(parts[i], parts[i + 1]) for i in range(0, len(parts) - 1, 2)]
        parts = pairs + parts[len(parts) - len(parts) % 2:]
    return parts[0]


def _rms(x, g):
    return x * lax.rsqrt(jnp.mean(x * x, axis=-1, keepdims=True) + EPS) * g


def _silu(x):
    return x * jax.nn.sigmoid(x)


def _gelu_tanh(x):
    return 0.5 * x * (1.0 + jnp.tanh(math.sqrt(2.0 / math.pi) * (x + 0.044715 * (x * x * x))))


def _mod_kernel(c_ref, w_ref, b_ref, o_ref):
    o_ref[...] = _dot3(_silu(c_ref[...]), w_ref[...]) + b_ref[...]


def _modulation(conds, w_ada, b_ada):
    n = conds.shape[0]
    tn = 1536
    return pl.pallas_call(
        _mod_kernel,
        grid=(DEPTH, 6 * D_MODEL // tn),
        in_specs=[
            pl.BlockSpec((n, D_MODEL), lambda l, j: (0, 0)),
            pl.BlockSpec((None, D_MODEL, tn), lambda l, j: (l, 0, j)),
            pl.BlockSpec((None, 1, tn), lambda l, j: (l, 0, j)),
        ],
        out_specs=pl.BlockSpec((None, n, tn), lambda l, j: (l, 0, j)),
        out_shape=jax.ShapeDtypeStruct((DEPTH, n, 6 * D_MODEL), F32),
        compiler_params=_cparams(("arbitrary", "arbitrary")),
        name="modulation",
    )(conds, w_ada, b_ada.reshape(DEPTH, 1, 6 * D_MODEL))


def _ssm_disc_kernel(are_ref, aim_ref, ldt_ref, bre_ref, bim_ref, abre_ref, abim_ref, bbre_ref, bbim_ref):
    lam_re = are_ref[...]
    lam_im = aim_ref[...]
    dt = jnp.exp(ldt_ref[...])
    z_re = lam_re * dt
    z_im = lam_im * dt
    mag = jnp.exp(z_re)
    ab_re = mag * jnp.cos(z_im)
    ab_im = mag * jnp.sin(z_im)
    den = lam_re * lam_re + lam_im * lam_im
    f_re = ((ab_re - 1.0) * lam_re + ab_im * lam_im) / den
    f_im = (ab_im * lam_re - (ab_re - 1.0) * lam_im) / den
    b_re = bre_ref[...]
    b_im = bim_ref[...]
    abre_ref[...] = ab_re
    abim_ref[...] = ab_im
    bbre_ref[...] = f_re * b_re - f_im * b_im
    bbim_ref[...] = f_re * b_im + f_im * b_re


def _ssm_discretise(a_re, a_im, log_dt, b_re, b_im):
    n = DEPTH * 2
    col = lambda v: v.reshape(n, SSM_N, 1)
    ldt = jnp.broadcast_to(log_dt[..., None], (DEPTH, 2, SSM_GROUPS, SSM_STATE))
    cspec = pl.BlockSpec((None, SSM_N, 1), lambda i: (i, 0, 0))
    bspec = pl.BlockSpec((None, SSM_N, SSM_GROUP), lambda i: (i, 0, 0))
    return pl.pallas_call(
        _ssm_disc_kernel,
        grid=(n,),
        in_specs=[cspec, cspec, cspec, bspec, bspec],
        out_specs=[cspec, cspec, bspec, bspec],
        out_shape=[jax.ShapeDtypeStruct((n, SSM_N, 1), F32)] * 2
        + [jax.ShapeDtypeStruct((n, SSM_N, SSM_GROUP), F32)] * 2,
        compiler_params=_cparams(("arbitrary",)),
        name="ssm_discretise",
    )(col(a_re), col(a_im), col(ldt), b_re.reshape(n, SSM_N, SSM_GROUP), b_im.reshape(n, SSM_N, SSM_GROUP))


def _absorb_kernel(wq_ref, wk_ref, o_ref):
    a = wq_ref[...]
    b = wk_ref[...]
    ah, al = _split(a)
    bh, bl = _split(b)
    o_ref[...] = _dot_nt(ah, bh) + _dot_nt(al, bh) + _dot_nt(ah, bl)


def _absorb_q(wq_nope, wk_nope):
    return pl.pallas_call(
        _absorb_kernel,
        grid=(DEPTH, MLA_HEADS),
        in_specs=[
            pl.BlockSpec((None, None, MLA_Q_RANK, MLA_NOPE), lambda l, h: (l, h, 0, 0)),
            pl.BlockSpec((None, None, MLA_KV_RANK, MLA_NOPE), lambda l, h: (l, h, 0, 0)),
        ],
        out_specs=pl.BlockSpec((None, None, MLA_Q_RANK, MLA_KV_RANK), lambda l, h: (l, h, 0, 0)),
        out_shape=jax.ShapeDtypeStruct((DEPTH, MLA_HEADS, MLA_Q_RANK, MLA_KV_RANK), F32),
        compiler_params=_cparams(("arbitrary", "arbitrary")),
        name="mla_absorb",
    )(wq_nope, wk_nope)


def _pre_kernel(*refs, rope, emit_ctx):
    it = iter(refs)
    x_ref, mod_ref, n1_ref, win_ref, qn_ref, kvn_ref, wq_ref = (next(it) for _ in range(7))
    if rope:
        cm_ref, sm_ref, cmt_ref, smt_ref, cs_ref, ss_ref, cst_ref, sst_ref, wqs_ref = (next(it) for _ in range(9))
    u_ref, qt_ref, kcat_ref, ckvt_ref, qs_ref, ks_ref, vs_ref = (next(it) for _ in range(7))
    if emit_ctx:
        ckv_o, kr_o, ks_o, vs_o = (next(it) for _ in range(4))

    x = x_ref[...]
    mod = mod_ref[...]
    h = _rms(x, n1_ref[...]) * (1.0 + mod[1:2]) + mod[0:1]
    hb = h.astype(BF16)
    proj = _dot(hb, win_ref[...])

    u_ref[...] = proj[:, OFF_U:OFF_U + SSM_CH]

    qln = _rms(proj[:, OFF_QL:OFF_QL + MLA_Q_RANK], qn_ref[...]).astype(BF16)
    qall = _dot_nt(wq_ref[...], qln)
    n_rope = MLA_HEADS * MLA_ROPE
    zero_rows = jnp.zeros((MLA_QK - MLA_KV_RANK - MLA_ROPE, qall.shape[1]), BF16)
    for i in range(MLA_HEADS):
        qa = qall[i * MLA_KV_RANK:(i + 1) * MLA_KV_RANK]
        qr = qall[Q_ABS + i * MLA_ROPE:Q_ABS + (i + 1) * MLA_ROPE]
        if rope:
            qr_sw = qall[Q_ABS + n_rope + i * MLA_ROPE:Q_ABS + n_rope + (i + 1) * MLA_ROPE]
            qr = qr * cmt_ref[...] + qr_sw * smt_ref[...]
        base = i * MLA_QK
        qt_ref[base:base + MLA_KV_RANK, :] = (qa * MLA_SCALE_LOG2).astype(BF16)
        qt_ref[base + MLA_KV_RANK:base + MLA_KV_RANK + MLA_ROPE, :] = (qr * MLA_SCALE_LOG2).astype(BF16)
        qt_ref[base + MLA_KV_RANK + MLA_ROPE:base + MLA_QK, :] = zero_rows

    ckv = _rms(proj[:, OFF_KVL:OFF_KVL + MLA_KV_RANK], kvn_ref[...])
    kr = proj[:, OFF_KR:OFF_KR + LANES]
    if emit_ctx:
        ckv_o[...] = ckv
        kr_o[...] = kr[:, :MLA_ROPE]
    if rope:
        kr = kr * cm_ref[...] + proj[:, OFF_KR_SW:OFF_KR_SW + LANES] * sm_ref[...]
    kcat_ref[...] = jnp.concatenate([ckv, kr], axis=-1).astype(BF16)
    ckvt_ref[...] = ckv.T.astype(BF16)

    ks = proj[:, OFF_KS:OFF_KS + SWA_KV_HEADS * HEAD_DIM]
    vs = proj[:, OFF_VS:OFF_VS + SWA_KV_HEADS * HEAD_DIM]
    if emit_ctx:
        ks_o[...] = ks
        vs_o[...] = vs
    if rope:
        ks = ks * cs_ref[...] + proj[:, OFF_KS_SW:OFF_KS_SW + SWA_KV_HEADS * HEAD_DIM] * ss_ref[...]
        n_q = SWA_HEADS * HEAD_DIM
        qst = _dot_nt(wqs_ref[...], hb)
        zero = jnp.zeros((HEAD_DIM, SWA_BLOCK), BF16)
        for hd in range(SWA_HEADS):
            rows = slice(hd * HEAD_DIM, (hd + 1) * HEAD_DIM)
            rot = qst[rows] * cst_ref[...] + qst[n_q + hd * HEAD_DIM:n_q + (hd + 1) * HEAD_DIM] * sst_ref[...]
            rot = (rot * SWA_SCALE_LOG2).astype(BF16)
            kh = hd // SWA_GROUP
            for j in range(qs_ref.shape[0]):
                lanes = slice(hd * SWA_BLOCK, (hd + 1) * SWA_BLOCK)
                qs_ref[j, kh * HEAD_DIM:(kh + 1) * HEAD_DIM, lanes] = rot[:, j * SWA_BLOCK:(j + 1) * SWA_BLOCK]
                qs_ref[j, (1 - kh) * HEAD_DIM:(2 - kh) * HEAD_DIM, lanes] = zero
        vs_ref[...] = vs.T.astype(BF16)
    else:
        qs_ref[...] = (proj[:, OFF_QS:OFF_QS + SWA_HEADS * HEAD_DIM] * SWA_SCALE).astype(BF16)
        vs_ref[...] = vs.astype(BF16)
    ks_ref[...] = ks.astype(BF16)


def _pre(x, mod, lw, tables, *, per_batch_mod, emit_ctx):
    B, S, D = x.shape
    rope = tables is not None
    ts = min(S, 512)
    n_pack = N_PACK_LAT if rope else N_PACK_CTX
    win = lw["win_lat"] if rope else lw["win_ctx"]
    wq = lw["wq_lat"] if rope else lw["wq_ctx"]
    tok = lambda w: pl.BlockSpec((None, ts, w), lambda b, s: (b, s, 0))
    full = lambda a: pl.BlockSpec(a.shape, lambda b, s: (0,) * a.ndim)
    in_specs = [
        tok(D),
        pl.BlockSpec((None, 6, D), (lambda b, s: (b, 0, 0)) if per_batch_mod else (lambda b, s: (0, 0, 0))),
        full(lw["norm1"]), full(win), full(lw["q_norm"]), full(lw["kv_norm"]), full(wq),
    ]
    args = [x, mod, lw["norm1"], win, lw["q_norm"], lw["kv_norm"], wq]
    if rope:
        row_tab = pl.BlockSpec((ts, LANES), lambda b, s: (s, 0))
        col_tab = pl.BlockSpec((MLA_ROPE, ts), lambda b, s: (0, s))
        col_tab_s = pl.BlockSpec((HEAD_DIM, ts), lambda b, s: (0, s))
        in_specs += [row_tab, row_tab, col_tab, col_tab, row_tab, row_tab, col_tab_s, col_tab_s, full(lw["wqs_t"])]
        args += list(tables) + [lw["wqs_t"]]
    feat = lambda w: pl.BlockSpec((None, w, ts), lambda b, s: (b, 0, s))
    if rope:
        q_tile = (2 * HEAD_DIM, SWA_HEADS * SWA_BLOCK)
        qs_spec = pl.BlockSpec((None, ts // SWA_BLOCK) + q_tile, lambda b, s: (b, s, 0, 0))
        qs_shape = jax.ShapeDtypeStruct((B, S // SWA_BLOCK) + q_tile, BF16)
        vs_spec, vs_shape = feat(LANES), jax.ShapeDtypeStruct((B, LANES, S), BF16)
    else:
        qs_spec, qs_shape = tok(SWA_HEADS * HEAD_DIM), jax.ShapeDtypeStruct((B, S, SWA_HEADS * HEAD_DIM), BF16)
        vs_spec, vs_shape = tok(LANES), jax.ShapeDtypeStruct((B, S, LANES), BF16)
    out_specs = [
        tok(SSM_CH),
        feat(MLA_HEADS * MLA_QK), tok(MLA_QK), feat(MLA_KV_RANK), qs_spec, tok(LANES), vs_spec,
    ]
    out_shape = [
        jax.ShapeDtypeStruct((B, S, SSM_CH), F32),
        jax.ShapeDtypeStruct((B, MLA_HEADS * MLA_QK, S), BF16),
        jax.ShapeDtypeStruct((B, S, MLA_QK), BF16),
        jax.ShapeDtypeStruct((B, MLA_KV_RANK, S), BF16),
        qs_shape,
        jax.ShapeDtypeStruct((B, S, LANES), BF16),
        vs_shape,
    ]
    if emit_ctx:
        out_specs += [tok(MLA_KV_RANK), tok(MLA_ROPE), tok(LANES), tok(LANES)]
        out_shape += [
            jax.ShapeDtypeStruct((B, S, MLA_KV_RANK), F32),
            jax.ShapeDtypeStruct((B, S, MLA_ROPE), F32),
            jax.ShapeDtypeStruct((B, S, LANES), F32),
            jax.ShapeDtypeStruct((B, S, LANES), F32),
        ]
    return pl.pallas_call(
        functools.partial(_pre_kernel, rope=rope, emit_ctx=emit_ctx),
        grid=(B, S // ts),
        in_specs=in_specs,
        out_specs=out_specs,
        out_shape=out_shape,
        compiler_params=_cparams(("arbitrary", "arbitrary")),
        name="pre_lat" if rope else "pre_ctx",
    )(*args)


def _ssm_kernel(uf_ref, ub_ref, wb_ref, a_ref, wc_ref, h0_ref, yf_ref, yb_ref, hfin_ref,
                hre_f, him_f, hre_b, him_b, st_s, ut_s, *, tc):
    i = pl.program_id(1)
    n = pl.num_programs(1)

    @pl.when(i == 0)
    def _():
        st_s[...] = h0_ref[...]

    cblk = SSM_CH // LANES
    u_refs = (uf_ref, ub_ref)
    h_refs = ((hre_f, him_f), (hre_b, him_b))
    for d in range(2):
        ub = u_refs[d][...]
        for b in range(SUBLANES):
            for j in range(cblk):
                ut_s[d, j, pl.ds(b, tc, stride=SUBLANES), :] = ub[b, :, j * LANES:(j + 1) * LANES]
        u = jnp.concatenate([ut_s[d, j] for j in range(cblk)], axis=-1).astype(BF16)
        bu = _dot(u, wb_ref[d])
        h_refs[d][0][...] = bu[:, :SSM_N]
        h_refs[d][1][...] = bu[:, SSM_N:]

    a_re = [jnp.broadcast_to(a_ref[d][:, :SSM_N], (SUBLANES, SSM_N)) for d in range(2)]
    a_im = [jnp.broadcast_to(a_ref[d][:, SSM_N:], (SUBLANES, SSM_N)) for d in range(2)]
    h_re = [st_s[d][:, :SSM_N] for d in range(2)]
    h_im = [st_s[d][:, SSM_N:] for d in range(2)]
    for t in range(tc):
        for d in range(2):
            tt = t if d == 0 else tc - 1 - t
            rows = slice(tt * SUBLANES, (tt + 1) * SUBLANES)
            hre_s, him_s = h_refs[d]
            n_re = a_re[d] * h_re[d] - a_im[d] * h_im[d] + hre_s[rows, :]
            n_im = a_re[d] * h_im[d] + a_im[d] * h_re[d] + him_s[rows, :]
            hre_s[rows, :] = n_re
            him_s[rows, :] = n_im
            h_re[d], h_im[d] = n_re, n_im

    y_refs = (yf_ref, yb_ref)
    for d in range(2):
        st_s[d] = jnp.concatenate([h_re[d], h_im[d]], axis=-1)
        hre_s, him_s = h_refs[d]
        y = _dot(hre_s[...].astype(BF16), wc_ref[d, :SSM_N, :]) + _dot(him_s[...].astype(BF16), wc_ref[d, SSM_N:, :])
        for j in range(cblk):
            ut_s[d, j] = y[:, j * LANES:(j + 1) * LANES]
        for b in range(SUBLANES):
            y_refs[d][b] = jnp.concatenate(
                [ut_s[d, j, pl.ds(b, tc, stride=SUBLANES), :] for j in range(cblk)], axis=-1)

    @pl.when(i == n - 1)
    def _():
        hfin_ref[...] = st_s[...]


def _ssm(u, lw, h0, B, S):
    nb = B // SUBLANES
    tc = 128
    nchunk = S // tc
    full3 = lambda a: pl.BlockSpec(a.shape, lambda b, i: (0, 0, 0))
    state_spec = pl.BlockSpec((2, None, SUBLANES, 2 * SSM_N), lambda b, i: (0, b, 0, 0))
    fwd = pl.BlockSpec((SUBLANES, tc, SSM_CH), lambda b, i: (b, i, 0))
    bwd = pl.BlockSpec((SUBLANES, tc, SSM_CH), lambda b, i: (b, nchunk - 1 - i, 0))
    yf, yb, hfin = pl.pallas_call(
        functools.partial(_ssm_kernel, tc=tc),
        grid=(nb, nchunk),
        in_specs=[fwd, bwd, full3(lw["ssm_wb"]), full3(lw["ssm_a"]), full3(lw["ssm_wc"]), state_spec],
        out_specs=[fwd, bwd, state_spec],
        out_shape=[
            jax.ShapeDtypeStruct((B, S, SSM_CH), F32),
            jax.ShapeDtypeStruct((B, S, SSM_CH), F32),
            jax.ShapeDtypeStruct((2, nb, SUBLANES, 2 * SSM_N), F32),
        ],
        scratch_shapes=[pltpu.VMEM((tc * SUBLANES, SSM_N), F32)] * 4 + [
            pltpu.VMEM((2, SUBLANES, 2 * SSM_N), F32),
            pltpu.VMEM((2, SSM_CH // LANES, tc * SUBLANES, LANES), F32),
        ],
        compiler_params=_cparams(("arbitrary", "arbitrary")),
        name="ssm_scan",
    )(u, u, lw["ssm_wb"], lw["ssm_a"], lw["ssm_wc"], h0)
    return (yf, yb), hfin


def _mla_kernel(*refs, with_latent, tk):
    if with_latent:
        qt_ref, ka_ref, vat_ref, kb_ref, vbt_ref, wv_ref, o_ref = refs
    else:
        qt_ref, ka_ref, vat_ref, wv_ref, o_ref = refs
    tq = qt_ref.shape[1]
    q_of = lambda hd: qt_ref[hd * MLA_QK:(hd + 1) * MLA_QK, :]

    def tile(state, k, vt):
        new_state = []
        s_next = _dot(k, q_of(0))
        for hd in range(MLA_HEADS):
            m8, l8, acc = state[hd]
            s = s_next
            if hd + 1 < MLA_HEADS:
                s_next = _dot(k, q_of(hd + 1))
            s3 = s.reshape(s.shape[0] // SUBLANES, SUBLANES, tq)
            mloc = jnp.max(_tree(jnp.maximum, s3), axis=0, keepdims=True)
            m8_new = jnp.maximum(m8, jnp.broadcast_to(mloc, (SUBLANES, tq)))
            alpha8 = jnp.exp2(m8 - m8_new)
            p3 = jnp.exp2(s3 - m8_new[None])
            l8 = alpha8 * l8 + _tree(jnp.add, p3)
            pv = _dot(vt, p3.reshape(s.shape).astype(BF16))
            acc3 = acc.reshape(MLA_KV_RANK // SUBLANES, SUBLANES, tq) * alpha8[None]
            new_state.append((m8_new, l8, acc3.reshape(MLA_KV_RANK, tq) + pv))
        return tuple(new_state)

    init = tuple((jnp.full((SUBLANES, tq), -jnp.inf, F32), jnp.zeros((SUBLANES, tq), F32),
                  jnp.zeros((MLA_KV_RANK, tq), F32)) for _ in range(MLA_HEADS))
    state = tile(init, ka_ref[...], vat_ref[...])
    if with_latent:
        for j in range(kb_ref.shape[0] // tk):
            state = tile(state, kb_ref[j * tk:(j + 1) * tk, :], vbt_ref[:, j * tk:(j + 1) * tk])

    outs = []
    for hd in range(MLA_HEADS):
        _, l8, acc = state[hd]
        o_lat = (acc / jnp.sum(l8, axis=0, keepdims=True)).T.astype(BF16)
        outs.append(_dot(o_lat, wv_ref[hd]))
    o_ref[...] = jnp.concatenate(outs, axis=-1)


def _mla(qt, ka, vat, kb, vbt, wv):
    B, _, S = qt.shape
    tq = min(S, 512)
    with_latent = kb is not None
    kspec = lambda a: pl.BlockSpec((None,) + a.shape[1:], lambda b, s: (b, 0, 0))
    in_specs = [pl.BlockSpec((None, MLA_HEADS * MLA_QK, tq), lambda b, s: (b, 0, s)), kspec(ka), kspec(vat)]
    args = [qt, ka, vat]
    if with_latent:
        in_specs += [kspec(kb), kspec(vbt)]
        args += [kb, vbt]
    in_specs.append(pl.BlockSpec(wv.shape, lambda b, s: (0, 0, 0)))
    args.append(wv)
    return pl.pallas_call(
        functools.partial(_mla_kernel, with_latent=with_latent, tk=4096),
        grid=(B, S // tq),
        in_specs=in_specs,
        out_specs=pl.BlockSpec((None, tq, MLA_HEADS * MLA_V), lambda b, s: (b, s, 0)),
        out_shape=jax.ShapeDtypeStruct((B, S, MLA_HEADS * MLA_V), F32),
        compiler_params=_cparams(("arbitrary", "arbitrary")),
        name="mla_lat" if with_latent else "mla_ctx",
    )(*args)


def _swa_ctx_kernel(sink_ref, q_ref, k_ref, v_ref, o_ref):
    q = q_ref[...]
    k = k_ref[...]
    v = v_ref[...]
    outs = []
    for hd in range(SWA_HEADS):
        kh = hd // SWA_GROUP
        qh = q[:, hd * HEAD_DIM:(hd + 1) * HEAD_DIM]
        ksl = slice(kh * HEAD_DIM, (kh + 1) * HEAD_DIM)
        sink = sink_ref[hd]
        s = _dot_nt(qh, k[:, ksl])
        m = jnp.maximum(jnp.max(s, axis=-1, keepdims=True), sink)
        p = jnp.exp(s - m)
        den = jnp.sum(p, axis=-1, keepdims=True) + jnp.exp(sink - m)
        outs.append(_dot(p.astype(BF16), v[:, ksl]) / den)
    o_ref[...] = jnp.concatenate(outs, axis=-1)


def _swa_ctx(sink, q, k, v):
    B, S, W = q.shape
    seq = lambda a: pl.BlockSpec((None,) + a.shape[1:], lambda b: (b, 0, 0))
    return pl.pallas_call(
        _swa_ctx_kernel,
        grid=(B,),
        in_specs=[pl.BlockSpec(memory_space=pltpu.SMEM), seq(q), seq(k), seq(v)],
        out_specs=seq(q),
        out_shape=jax.ShapeDtypeStruct((B, S, W), F32),
        compiler_params=_cparams(("arbitrary",)),
        name="swa_ctx",
    )(sink, q, k, v)


def _swa_lat_kernel(sink_ref, q_ref, ka_ref, vat_ref, kb_ref, vbt_ref, o_ref):
    for j in range(q_ref.shape[0]):
        rows = slice(j * SWA_BLOCK, (j + 1) * SWA_BLOCK)
        o_ref[rows, :] = _swa_lat_block(pl.program_id(1) * q_ref.shape[0] + j, sink_ref, q_ref[j], ka_ref, vat_ref,
                                        kb_ref, vbt_ref)


def _swa_lat_block(n, sink_ref, q, ka_ref, vat_ref, kb_ref, vbt_ref):
    nblk = kb_ref.shape[0] // SWA_BLOCK
    n_lane = SWA_HEADS * SWA_BLOCK
    win = 3 * SWA_BLOCK
    start = pl.multiple_of(jnp.clip(n - 1, 0, nblk - 3) * SWA_BLOCK, SWA_BLOCK)
    s_a = _dot(ka_ref[...], q)
    s_b = _dot(kb_ref[pl.ds(start, win), :], q)
    kpos = start + lax.broadcasted_iota(jnp.int32, (win, n_lane), 0)
    qpos = n * SWA_BLOCK + (lax.broadcasted_iota(jnp.int32, (win, n_lane), 1) & (SWA_BLOCK - 1))
    s_b = jnp.where(jnp.abs(qpos - kpos) <= SWA_WINDOW, s_b, -jnp.inf)

    fold = lambda x: x.reshape(x.shape[0] // SUBLANES, SUBLANES, n_lane)
    sink = sink_ref[...]
    m8 = jnp.maximum(_tree(jnp.maximum, fold(s_a)), _tree(jnp.maximum, fold(s_b)))
    m = jnp.maximum(jnp.max(m8, axis=0, keepdims=True), sink)
    m8 = jnp.broadcast_to(m, (SUBLANES, n_lane))
    p_a = jnp.exp2(fold(s_a) - m8[None])
    p_b = jnp.exp2(fold(s_b) - m8[None])
    den = jnp.sum(_tree(jnp.add, p_a) + _tree(jnp.add, p_b), axis=0, keepdims=True) + jnp.exp2(sink - m)
    o_t = (_dot(vat_ref[...], p_a.reshape(s_a.shape).astype(BF16))
           + _dot(vbt_ref[:, pl.ds(start, win)], p_b.reshape(s_b.shape).astype(BF16))) / den
    o = o_t.T
    outs = []
    for hd in range(SWA_HEADS):
        kh = hd // SWA_GROUP
        outs.append(o[hd * SWA_BLOCK:(hd + 1) * SWA_BLOCK, kh * HEAD_DIM:(kh + 1) * HEAD_DIM])
    return jnp.concatenate(outs, axis=-1)


def _swa_lat(sink_row, q_tiles, ka, vat, kb, vbt):
    B, nblk = q_tiles.shape[:2]
    width = SWA_HEADS * HEAD_DIM
    per_step = 2
    kspec = lambda a: pl.BlockSpec((None,) + a.shape[1:], lambda b, s: (b, 0, 0))
    return pl.pallas_call(
        _swa_lat_kernel,
        grid=(B, nblk // per_step),
        in_specs=[
            pl.BlockSpec(sink_row.shape, lambda b, s: (0, 0)),
            pl.BlockSpec((None, per_step) + q_tiles.shape[2:], lambda b, s: (b, s, 0, 0)),
            kspec(ka), kspec(vat), kspec(kb), kspec(vbt),
        ],
        out_specs=pl.BlockSpec((None, per_step * SWA_BLOCK, width), lambda b, s: (b, s, 0)),
        out_shape=jax.ShapeDtypeStruct((B, nblk * SWA_BLOCK, width), F32),
        compiler_params=_cparams(("arbitrary", "arbitrary")),
        name="swa_lat",
    )(sink_row, q_tiles, ka, vat, kb, vbt)


def _route(logits):
    lane = lax.broadcasted_iota(jnp.int32, logits.shape, 1)
    big = jnp.int32(1 << 20)
    is_g = (lane >= MOE_EXPERTS) & (lane < MOE_EXPERTS + MOE_GROUPS)
    lg = jnp.where(is_g, logits, -jnp.inf)
    mg = jnp.max(lg, axis=-1, keepdims=True)
    g_idx = jnp.min(jnp.where(lg == mg, lane - MOE_EXPERTS, big), axis=-1, keepdims=True)
    pg_top = 1.0 / jnp.sum(jnp.exp(lg - mg), axis=-1, keepdims=True)

    is_e = (lane < MOE_EXPERTS) & ((lane // MOE_PER_GROUP) == g_idx)
    le = jnp.where(is_e, logits, -jnp.inf)
    m1 = jnp.max(le, axis=-1, keepdims=True)
    e1 = jnp.min(jnp.where(le == m1, lane, big), axis=-1, keepdims=True)
    z = jnp.sum(jnp.exp(le - m1), axis=-1, keepdims=True)
    le2 = jnp.where(lane == e1, -jnp.inf, le)
    m2 = jnp.max(le2, axis=-1, keepdims=True)
    e2 = jnp.min(jnp.where(le2 == m2, lane, big), axis=-1, keepdims=True)
    p1 = 1.0 / z
    p2 = jnp.exp(m2 - m1) / z
    tot = p1 + p2
    gates = pg_top * (jnp.where(lane == e1, p1 / tot, 0.0) + jnp.where(lane == e2, p2 / tot, 0.0))
    t = logits.shape[0]
    chose = jnp.where(lane == g_idx + COUNT_LANE, 1.0, 0.0)
    tri = jnp.where(lax.broadcasted_iota(jnp.int32, (t, t), 0) >= lax.broadcasted_iota(jnp.int32, (t, t), 1),
                    1.0, 0.0).astype(BF16)
    counts = _dot(tri, chose.astype(BF16))
    rank = jnp.sum(chose * (counts - 1.0), axis=-1, keepdims=True)
    return (gates + counts + jnp.where(lane == GROUP_LANE, g_idx.astype(F32), 0.0)
            + jnp.where(lane == RANK_LANE, rank, 0.0))


def _post_kernel(x_ref, u_ref, yf_ref, yb_ref, om_ref, os_ref, mod_ref, d_ref, wglu_ref, gs_ref, gm_ref,
                 gw_ref, wout_ref, n2_ref, wr_ref, br_ref, x1_ref, h2_ref, gate_ref):
    mod = mod_ref[...]
    y = d_ref[...] * u_ref[...] + yf_ref[...] + yb_ref[...]
    ga = _dot(_gelu_tanh(y).astype(BF16), wglu_ref[...])
    y_ssm = ga[:, :SSM_CH] * jax.nn.sigmoid(ga[:, SSM_CH:])
    wout = wout_ref
    n_mla = MLA_HEADS * MLA_V
    mixed = (_dot(_rms(y_ssm, gs_ref[...]).astype(BF16), wout[0:SSM_CH, :])
             + _dot(_rms(om_ref[...], gm_ref[...]).astype(BF16), wout[SSM_CH:SSM_CH + n_mla, :])
             + _dot(_rms(os_ref[...], gw_ref[...]).astype(BF16), wout[SSM_CH + n_mla:, :]))
    x1 = x_ref[...] + mod[2:3] * mixed
    x1_ref[...] = x1
    h2 = _rms(x1, n2_ref[...]) * (1.0 + mod[4:5]) + mod[3:4]
    h2_ref[...] = h2.astype(BF16)
    gate_ref[...] = _route(_dot3(h2, wr_ref[...]) + br_ref[...])


def _post(x, u_tm, y2, o_mla, o_swa, mod, lw, *, per_batch_mod):
    B, S, D = x.shape
    ts = min(S, POST_TILE)
    tok = lambda w: pl.BlockSpec((None, ts, w), lambda b, s: (b, s, 0))
    full = lambda a: pl.BlockSpec(a.shape, lambda b, s: (0,) * a.ndim)
    names = ["ssm_d", "w_glu", "gn_ssm", "gn_mla", "gn_swa", "w_out", "norm2", "w_router", "b_router"]
    in_specs = [
        tok(D),
        tok(SSM_CH),
        tok(SSM_CH), tok(SSM_CH),
        tok(MLA_HEADS * MLA_V), tok(SWA_HEADS * HEAD_DIM),
        pl.BlockSpec((None, 6, D), (lambda b, s: (b, 0, 0)) if per_batch_mod else (lambda b, s: (0, 0, 0))),
    ] + [full(lw[k]) for k in names]
    return pl.pallas_call(
        _post_kernel,
        grid=(B, S // ts),
        in_specs=in_specs,
        out_specs=[tok(D), tok(D), tok(LANES)],
        out_shape=[
            jax.ShapeDtypeStruct((B, S, D), F32),
            jax.ShapeDtypeStruct((B, S, D), BF16),
            jax.ShapeDtypeStruct((B, S, LANES), F32),
        ],
        compiler_params=_cparams(("arbitrary", "arbitrary")),
        name="post",
    )(x, u_tm, y2[0], y2[1], o_mla, o_swa, mod, *[lw[k] for k in names])


def _split3(x):
    hi = x.astype(BF16)
    r = x - hi.astype(F32)
    mid = r.astype(BF16)
    lo = (r - mid.astype(F32)).astype(BF16)
    return hi, mid, lo


def _moe_kernel(cnt_ref, h2_ref, gate_ref, crow_ref, x1_ref, mod_ref, wg_ref, wu_ref, wd_ref, fn_ref,
                o_ref, xs_s, gs_s, ys_s, acc_s, *, final, n_tiles):
    tm, d_model = acc_s.shape
    rb = MOE_ROW_BLOCK
    g = pl.program_id(2)
    e = pl.program_id(3)
    cnt = cnt_ref[(pl.program_id(0) * n_tiles + pl.program_id(1)) * MOE_GROUPS + g]
    n_blocks = (cnt + rb - 1) // rb

    def for_blocks(body, width=rb):
        def step(blk, carry):
            body(blk, pl.ds(pl.multiple_of(blk * width, width), width))
            return carry
        lax.fori_loop(0, (cnt + width - 1) // width, step, 0)

    @pl.when((g == 0) & (e == 0))
    def _():
        acc_s[...] = jnp.zeros_like(acc_s)

    @pl.when(e == 0)
    def _():
        rank = crow_ref[...] - g * MOE_CODE_STRIDE
        h2 = h2_ref[...].reshape(tm, d_model)
        gates = gate_ref[...].reshape(tm, LANES)

        def gather(blk, rows):
            row_id = lax.broadcasted_iota(jnp.int32, (rb, tm), 0) + blk * rb
            onehot = jnp.where(row_id == rank, 1.0, 0.0).astype(BF16)
            xs_s[rows, :] = _dot(onehot, h2).astype(BF16)
            g_hi, g_mid, g_lo = _split3(gates)
            gs_s[rows, :] = _dot(onehot, g_hi) + _dot(onehot, g_mid) + _dot(onehot, g_lo)
        for_blocks(gather)

        @pl.when(n_blocks % (MOE_SCATTER_BLOCK // rb) != 0)
        def _():
            ys_s[pl.ds(pl.multiple_of(n_blocks * rb, rb), rb), :] = jnp.zeros((rb, d_model), F32)

    n_exp = MOE_PER_GROUP // MOE_SPLIT

    def experts(blk, rows, first):
        x = xs_s[rows, :]
        gs = gs_s[rows, :]
        lane = lax.broadcasted_iota(jnp.int32, gs.shape, 1)
        first_expert = g * MOE_PER_GROUP + e * n_exp
        gate = jnp.concatenate(
            [jnp.broadcast_to(jnp.sum(jnp.where(lane == first_expert + k, gs, 0.0), axis=-1, keepdims=True),
                              (rb, MOE_HIDDEN)) for k in range(n_exp)], axis=-1)
        wide = lambda w_ref: jnp.concatenate([_dot(x, w_ref[k]) for k in range(n_exp)], axis=-1)
        hid = _silu(wide(wg_ref)) * wide(wu_ref) * gate
        y = _dot(hid.astype(BF16), wd_ref[...])
        if first:
            ys_s[rows, :] = y
        else:
            ys_s[rows, :] += y

    @pl.when(e == 0)
    def _():
        for_blocks(lambda blk, rows: experts(blk, rows, True))

    @pl.when(e != 0)
    def _():
        for_blocks(lambda blk, rows: experts(blk, rows, False))

    @pl.when(e == MOE_SPLIT - 1)
    def _():
        code_t = jnp.broadcast_to(crow_ref[...].astype(F32), (LANES, tm)).T
        rank = code_t.astype(jnp.int32) - g * MOE_CODE_STRIDE
        lane = lax.broadcasted_iota(jnp.int32, (tm, LANES), 1)

        sb = MOE_SCATTER_BLOCK

        def scatter(blk, rows):
            onehot = jnp.concatenate(
                [jnp.where(lane + (blk * sb + part * LANES) == rank, 1.0, 0.0) for part in range(sb // LANES)],
                axis=-1).astype(BF16)
            acc_s[...] += _dot(onehot, ys_s[rows, :].astype(BF16))
        for_blocks(scatter, sb)

    @pl.when((g == MOE_GROUPS - 1) & (e == MOE_SPLIT - 1))
    def _():
        xo = x1_ref[...].reshape(tm, d_model) + mod_ref[5:6, :] * acc_s[...]
        xo = _rms(xo, fn_ref[...]) if final else xo
        o_ref[...] = xo.reshape(o_ref.shape)


def _moe(h2, gates, x1, mod, lw, final_norm, *, per_batch_mod, final, rt):
    B, S, D = x1.shape
    tm = MOE_TILE
    if S >= tm:
        nb, nt = B, S // tm
        tok = lambda w: pl.BlockSpec((None, tm, w), lambda b, s, g, e, c: (b, s, 0))
    else:
        assert not per_batch_mod
        nb, nt = B * S // tm, 1
        tok = lambda w: pl.BlockSpec((tm // S, S, w), lambda b, s, g, e, c: (b, 0, 0))
    sub = tm // rt
    info = gates[..., GROUP_LANE:COUNT_LANE + MOE_GROUPS].astype(jnp.int32)
    grp = info[..., 0].reshape(nb, nt, sub, rt)
    sub_counts = info[:, rt - 1::rt, 2:].reshape(nb, nt, sub, MOE_GROUPS)
    offsets = jnp.cumsum(sub_counts, axis=2) - sub_counts
    onehot = grp[..., None] == jnp.arange(MOE_GROUPS, dtype=jnp.int32)
    rank = info[..., 1].reshape(nb, nt, sub, rt) + jnp.sum(jnp.where(onehot, offsets[:, :, :, None, :], 0), axis=-1)
    counts = jnp.sum(sub_counts, axis=2).reshape(-1)
    code = grp * MOE_CODE_STRIDE + rank
    code_row = code.reshape(nb, nt, 1, tm)

    n_exp = MOE_PER_GROUP // MOE_SPLIT
    step_map = lambda b, s, g, e, c: (g * MOE_SPLIT + e, 0, 0)
    mod_map = (lambda b, s, g, e, c: (b, 0, 0)) if per_batch_mod else (lambda b, s, g, e, c: (0, 0, 0))
    grid_spec = pltpu.PrefetchScalarGridSpec(
        num_scalar_prefetch=1,
        grid=(nb, nt, MOE_GROUPS, MOE_SPLIT),
        in_specs=[
            tok(D), tok(LANES),
            pl.BlockSpec((None, None, 1, tm), lambda b, s, g, e, c: (b, s, 0, 0)),
            tok(D),
            pl.BlockSpec((None, 6, D), mod_map),
            pl.BlockSpec((n_exp, D, MOE_HIDDEN), step_map), pl.BlockSpec((n_exp, D, MOE_HIDDEN), step_map),
            pl.BlockSpec((None, n_exp * MOE_HIDDEN, D), step_map),
            pl.BlockSpec(final_norm.shape, lambda b, s, g, e, c: (0, 0)),
        ],
        out_specs=tok(D),
        scratch_shapes=[
            pltpu.VMEM((tm, D), BF16),
            pltpu.VMEM((tm, LANES), F32),
            pltpu.VMEM((tm, D), F32),
            pltpu.VMEM((tm, D), F32),
        ],
    )
    return pl.pallas_call(
        functools.partial(_moe_kernel, final=final, n_tiles=nt),
        grid_spec=grid_spec,
        out_shape=jax.ShapeDtypeStruct((B, S, D), F32),
        compiler_params=_cparams(("arbitrary", "arbitrary", "arbitrary", "arbitrary")),
        name="moe",
    )(counts, h2, gates, code_row, x1, mod, lw["moe_wg"], lw["moe_wu"], lw["moe_wd"], final_norm)


def _swap_halves(w, n_heads, dim):
    k = w.shape[0]
    w = w.reshape(k, n_heads, 2, dim // 2)
    return w[:, :, ::-1, :].reshape(k, n_heads * dim)


def _pad_cols(w, width):
    return jnp.pad(w, ((0, 0), (0, width - w.shape[1])))


def _rope_tables(n_tokens, rot_dim, reps):
    t = jnp.arange(n_tokens)
    row = (t // GRID_W).astype(F32)
    col = (t % GRID_W).astype(F32)
    n_freq = rot_dim // 4
    inv_freq = ROPE_BASE ** (-jnp.arange(n_freq, dtype=F32) / n_freq)
    ang = jnp.concatenate([row[:, None] * inv_freq, col[:, None] * inv_freq], axis=-1)
    cos, sin = jnp.cos(ang), jnp.sin(ang)
    c = jnp.tile(jnp.concatenate([cos, cos], axis=-1), (1, reps))
    s = jnp.tile(jnp.concatenate([-sin, sin], axis=-1), (1, reps))
    return _pad_cols(c, LANES), _pad_cols(s, LANES)


def _layer_weights(l, P, q_abs, ab_re, ab_im, bb_re, bb_im):
    w_in = P["w_in"][l]
    seg = {}
    o = 0
    for name, width in (("u", SSM_CH), ("ql", MLA_Q_RANK), ("kvl", MLA_KV_RANK), ("kr", MLA_ROPE),
                        ("qs", SWA_HEADS * HEAD_DIM), ("ks", SWA_KV_HEADS * HEAD_DIM),
                        ("vs", SWA_KV_HEADS * HEAD_DIM)):
        seg[name] = w_in[:, o:o + width]
        o += width
    shared_cols = [seg["u"], seg["ql"], seg["kvl"], seg["ks"], seg["vs"], _pad_cols(seg["kr"], LANES)]
    ctx_cols = shared_cols + [seg["qs"]]
    lat_cols = shared_cols + [
        _swap_halves(seg["ks"], SWA_KV_HEADS, HEAD_DIM),
        _pad_cols(_swap_halves(seg["kr"], 1, MLA_ROPE), LANES),
    ]
    wqs_t = jnp.concatenate([seg["qs"], _swap_halves(seg["qs"], SWA_HEADS, HEAD_DIM)], axis=1).T
    w_qb = P["w_mla_qb"][l]
    w_rope = w_qb[:, :, MLA_NOPE:]
    w_rope_sw = w_rope.reshape(MLA_Q_RANK, MLA_HEADS, 2, MLA_ROPE // 2)[:, :, ::-1, :].reshape(w_rope.shape)
    flat_t = lambda w: w.reshape(MLA_Q_RANK, MLA_HEADS * MLA_ROPE).T
    wq_abs = jnp.transpose(q_abs[l], (0, 2, 1)).reshape(Q_ABS, MLA_Q_RANK)
    wq_ctx = jnp.concatenate([wq_abs, flat_t(w_rope)], axis=0)
    wq_lat = jnp.concatenate([wq_ctx, flat_t(w_rope_sw)], axis=0)

    eye = jnp.eye(SSM_GROUPS, dtype=F32)

    def block_diag_b(bb):
        bb = bb.reshape(2, SSM_GROUPS, SSM_STATE, SSM_GROUP)
        return jnp.einsum("dgpc,gh->dgchp", bb, eye).reshape(2, SSM_CH, SSM_N)

    def block_diag_c(cc):
        return jnp.einsum("dgcp,gh->dgphc", cc, eye).reshape(2, SSM_N, SSM_CH)

    sl = slice(2 * l, 2 * l + 2)
    w_router = jnp.concatenate([P["moe_w_expert"][l], P["moe_w_group"][l]], axis=1)
    b_router = jnp.concatenate([P["moe_b_expert"][l], P["moe_b_group"][l]])
    row = lambda v: v.reshape(1, -1)
    return dict(
        norm1=row(P["norm1"][l]), norm2=row(P["norm2"][l]),
        win_ctx=jnp.concatenate(ctx_cols, axis=1).astype(BF16),
        win_lat=jnp.concatenate(lat_cols, axis=1).astype(BF16), wqs_t=wqs_t.astype(BF16),
        q_norm=row(P["mla_q_norm"][l]), kv_norm=row(P["mla_kv_norm"][l]),
        wq_ctx=wq_ctx.astype(BF16), wq_lat=wq_lat.astype(BF16),
        wv=jnp.transpose(P["w_mla_kvb"][l][:, :, MLA_NOPE:], (1, 0, 2)).astype(BF16),
        ssm_wb=jnp.concatenate([block_diag_b(bb_re[sl]), block_diag_b(bb_im[sl])], axis=2).astype(BF16),
        ssm_a=jnp.concatenate([ab_re[sl], ab_im[sl]], axis=1).reshape(2, 1, 2 * SSM_N),
        ssm_wc=jnp.concatenate([block_diag_c(P["ssm_c_re"][l]), -block_diag_c(P["ssm_c_im"][l])],
                               axis=1).astype(BF16),
        ssm_d=row(P["ssm_d"][l]), w_glu=P["w_ssm_glu"][l].astype(BF16),
        gn_ssm=row(P["gn_ssm"][l]), gn_mla=row(P["gn_mla"][l]), gn_swa=row(P["gn_swa"][l]),
        w_out=P["w_out"][l].astype(BF16),
        w_router=_pad_cols(w_router, LANES), b_router=_pad_cols(row(b_router), LANES),
        moe_wg=P["moe_w_gate"][l].astype(BF16).reshape(MOE_EXPERTS, D_MODEL, MOE_HIDDEN),
        moe_wu=P["moe_w_up"][l].astype(BF16).reshape(MOE_EXPERTS, D_MODEL, MOE_HIDDEN),
        moe_wd=P["moe_w_down"][l].astype(BF16).reshape(
            MOE_GROUPS * MOE_SPLIT, MOE_PER_GROUP // MOE_SPLIT * MOE_HIDDEN, D_MODEL),
        sink=P["swa_sink"][l],
        sink_row=jnp.repeat(P["swa_sink"][l] * math.log2(math.e), SWA_BLOCK).reshape(1, SWA_HEADS * SWA_BLOCK),
    )


def _layer(x, mod, lw, final_norm, *, tables, ctx, per_batch_mod, final):
    B, S, _ = x.shape
    context_pass = ctx is None
    pre = _pre(x, mod, lw, tables, per_batch_mod=per_batch_mod, emit_ctx=context_pass)
    u_tm, qt, kcat, ckvt, qs, ks, vs = pre[:7]
    nb = B // SUBLANES
    if context_pass:
        h0 = jnp.zeros((2, nb, SUBLANES, 2 * SSM_N), F32)
        y2, hfin = _ssm(u_tm, lw, h0, B, S)
        o_mla = _mla(qt, kcat, ckvt, None, None, lw["wv"])
        o_swa = _swa_ctx(lw["sink"], qs, ks, vs)
        state = hfin.reshape(2, B, 2, SSM_GROUPS, SSM_STATE).transpose(1, 0, 2, 3, 4)
        new_ctx = (pre[7], pre[8], pre[9].reshape(B, S, SWA_KV_HEADS, HEAD_DIM),
                   pre[10].reshape(B, S, SWA_KV_HEADS, HEAD_DIM), state)
    else:
        kcat_c, ckvt_c, ks_c, vs_c, h0 = ctx
        y2, _ = _ssm(u_tm, lw, h0, B, S)
        o_mla = _mla(qt, kcat_c, ckvt_c, kcat, ckvt, lw["wv"])
        o_swa = _swa_lat(lw["sink_row"], qs, ks_c, vs_c, ks, vs)
        new_ctx = None
    x1, h2, gates = _post(x, u_tm, y2, o_mla, o_swa, mod, lw, per_batch_mod=per_batch_mod)
    xo = _moe(h2, gates, x1, mod, lw, final_norm, per_batch_mod=per_batch_mod, final=final, rt=min(S, POST_TILE))
    return xo, new_ctx


def kernel(x_prompt, x_sample, c, cache_mla_ckv, cache_mla_krope, cache_swa_k, cache_swa_v, state_ssm, c_ctx, w_ada, b_ada, norm1, norm2, w_in, ssm_a_re, ssm_a_im, ssm_log_dt, ssm_b_re, ssm_b_im, ssm_c_re, ssm_c_im, ssm_d, w_ssm_glu, mla_q_norm, w_mla_qb, mla_kv_norm, w_mla_kvb, swa_sink, gn_ssm, gn_mla, gn_swa, w_out, moe_w_group, moe_b_group, moe_w_expert, moe_b_expert, moe_w_gate, moe_w_up, moe_w_down, final_norm):
    P = dict(w_ada=w_ada, b_ada=b_ada, norm1=norm1, norm2=norm2, w_in=w_in,
             ssm_c_re=ssm_c_re, ssm_c_im=ssm_c_im, ssm_d=ssm_d, w_ssm_glu=w_ssm_glu,
             mla_q_norm=mla_q_norm, w_mla_qb=w_mla_qb, mla_kv_norm=mla_kv_norm, w_mla_kvb=w_mla_kvb,
             swa_sink=swa_sink, gn_ssm=gn_ssm, gn_mla=gn_mla, gn_swa=gn_swa, w_out=w_out,
             moe_w_group=moe_w_group, moe_b_group=moe_b_group, moe_w_expert=moe_w_expert,
             moe_b_expert=moe_b_expert, moe_w_gate=moe_w_gate, moe_w_up=moe_w_up, moe_w_down=moe_w_down)
    n_dec = c.shape[0]
    n_cond = 2 * SUBLANES
    conds = jnp.zeros((n_cond, D_MODEL), F32).at[:n_dec].set(c).at[n_dec].set(c_ctx)
    mods = _modulation(conds, w_ada, b_ada).reshape(DEPTH, n_cond, 6, D_MODEL)

    ab_re, ab_im, bb_re, bb_im = _ssm_discretise(ssm_a_re, ssm_a_im, ssm_log_dt, ssm_b_re, ssm_b_im)
    q_abs = _absorb_q(jnp.transpose(w_mla_qb[..., :MLA_NOPE], (0, 2, 1, 3)),
                      jnp.transpose(w_mla_kvb[..., :MLA_NOPE], (0, 2, 1, 3)))
    lws = [_layer_weights(l, P, q_abs, ab_re.reshape(2 * DEPTH, SSM_N), ab_im.reshape(2 * DEPTH, SSM_N),
                          bb_re, bb_im) for l in range(DEPTH)]
    fnorm = final_norm.reshape(1, D_MODEL)

    xp = x_prompt
    ctx_states = []
    for l in range(DEPTH):
        xp, new = _layer(xp, mods[l, n_dec:n_dec + 1], lws[l], fnorm, tables=None, ctx=None,
                         per_batch_mod=False, final=l == DEPTH - 1)
        ctx_states.append(new)
    outs_ctx = tuple(jnp.stack([s[k] for s in ctx_states], axis=1) for k in range(5))

    n_lat = x_sample.shape[1]
    cm, sm = _rope_tables(n_lat, MLA_ROPE, 1)
    cs, ss = _rope_tables(n_lat, HEAD_DIM, LANES // HEAD_DIM)
    tables = (cm, sm, cm[:, :MLA_ROPE].T, sm[:, :MLA_ROPE].T, cs, ss, cs[:, :HEAD_DIM].T, ss[:, :HEAD_DIM].T)
    xs = x_sample
    past = cache_mla_ckv.shape[2]
    for l in range(DEPTH):
        kcat_c = jnp.concatenate(
            [cache_mla_ckv[:, l], cache_mla_krope[:, l],
             jnp.zeros((n_dec, past, LANES - MLA_ROPE), F32)], axis=-1).astype(BF16)
        ks_c = cache_swa_k[:, l].reshape(n_dec, past, LANES).astype(BF16)
        vs_c = jnp.transpose(cache_swa_v[:, l].reshape(n_dec, past, LANES), (0, 2, 1)).astype(BF16)
        h0 = state_ssm[:, l].transpose(1, 0, 2, 3, 4).reshape(2, n_dec // SUBLANES, SUBLANES, 2 * SSM_N)
        ckvt_c = jnp.transpose(cache_mla_ckv[:, l], (0, 2, 1)).astype(BF16)
        xs, _ = _layer(xs, mods[l, :n_dec], lws[l], fnorm, tables=tables, ctx=(kcat_c, ckvt_c, ks_c, vs_c, h0),
                       per_batch_mod=True, final=l == DEPTH - 1)
    return (xp, xs) + outs_ctx
```

```python
import functools
import math

import jax
import jax.numpy as jnp
from jax import lax
from jax.experimental import pallas as pl
from jax.experimental.pallas import tpu as pltpu

F32 = jnp.float32
BF16 = jnp.bfloat16

D_MODEL = 1024
DEPTH = 4
GRID_W = 64
HEAD_DIM = 64
SSM_CH = 256
SSM_GROUP = 16
SSM_GROUPS = SSM_CH // SSM_GROUP
SSM_STATE = 64
SSM_N = SSM_GROUPS * SSM_STATE
MLA_HEADS = 6
MLA_Q_RANK = 256
MLA_KV_RANK = 128
MLA_NOPE = 64
MLA_ROPE = 32
MLA_V = 64
SWA_HEADS = 6
SWA_KV_HEADS = 2
SWA_GROUP = SWA_HEADS // SWA_KV_HEADS
SWA_WINDOW = 128
SWA_BLOCK = 128
MOE_GROUPS = 4
MOE_PER_GROUP = 8
MOE_EXPERTS = MOE_GROUPS * MOE_PER_GROUP
MOE_HIDDEN = 256
GROUP_LANE = MOE_EXPERTS
RANK_LANE = GROUP_LANE + 1
COUNT_LANE = GROUP_LANE + 2
POST_TILE = 512
MOE_SPLIT = 2
MOE_TILE = 1024
MOE_ROW_BLOCK = 128
MOE_SCATTER_BLOCK = 256
MOE_CODE_STRIDE = 1 << 16
ROPE_BASE = 10000.0
EPS = 1e-6
MLA_SCALE = 1.0 / math.sqrt(MLA_NOPE + MLA_ROPE)
MLA_SCALE_LOG2 = MLA_SCALE * math.log2(math.e)
SWA_SCALE = 1.0 / math.sqrt(HEAD_DIM)
SWA_SCALE_LOG2 = SWA_SCALE * math.log2(math.e)

LANES = 128
SUBLANES = 8
VMEM_LIMIT = 52 * 1024 * 1024

OFF_U = 0
OFF_QL = OFF_U + SSM_CH
OFF_KVL = OFF_QL + MLA_Q_RANK
OFF_KS = OFF_KVL + MLA_KV_RANK
OFF_VS = OFF_KS + SWA_KV_HEADS * HEAD_DIM
OFF_KR = OFF_VS + SWA_KV_HEADS * HEAD_DIM
OFF_TAIL = OFF_KR + LANES
OFF_QS = OFF_TAIL
N_PACK_CTX = OFF_QS + SWA_HEADS * HEAD_DIM
OFF_KS_SW = OFF_TAIL
OFF_KR_SW = OFF_KS_SW + SWA_KV_HEADS * HEAD_DIM
N_PACK_LAT = OFF_KR_SW + LANES
Q_ABS = MLA_HEADS * MLA_KV_RANK
MLA_QK = 2 * LANES


def _cparams(sem):
    return pltpu.CompilerParams(dimension_semantics=sem, vmem_limit_bytes=VMEM_LIMIT)


def _dot(a, b):
    return jnp.dot(a, b, preferred_element_type=F32)


def _dot_nt(a, b):
    return lax.dot_general(a, b, (((1,), (1,)), ((), ())), preferred_element_type=F32)


def _split(x):
    hi = x.astype(BF16)
    lo = (x - hi.astype(F32)).astype(BF16)
    return hi, lo


def _dot3(a, b):
    ah, al = _split(a)
    bh, bl = _split(b)
    return _dot(ah, bh) + _dot(al, bh) + _dot(ah, bl)


def _tree(op, x3):
    parts = [x3[i] for i in range(x3.shape[0])]
    while len(parts) > 1:
        pairs = [op(parts[i], parts[i + 1]) for i in range(0, len(parts) - 1, 2)]
        parts = pairs + parts[len(parts) - len(parts) % 2:]
    return parts[0]


def _rms(x, g):
    return x * lax.rsqrt(jnp.mean(x * x, axis=-1, keepdims=True) + EPS) * g


def _silu(x):
    return x * jax.nn.sigmoid(x)


def _gelu_tanh(x):
    return 0.5 * x * (1.0 + jnp.tanh(math.sqrt(2.0 / math.pi) * (x + 0.044715 * (x * x * x))))


def _mod_kernel(c_ref, w_ref, b_ref, o_ref):
    o_ref[...] = _dot3(_silu(c_ref[...]), w_ref[...]) + b_ref[...]


def _modulation(conds, w_ada, b_ada):
    n = conds.shape[0]
    tn = 1536
    return pl.pallas_call(
        _mod_kernel,
        grid=(DEPTH, 6 * D_MODEL // tn),
        in_specs=[
            pl.BlockSpec((n, D_MODEL), lambda l, j: (0, 0)),
            pl.BlockSpec((None, D_MODEL, tn), lambda l, j: (l, 0, j)),
            pl.BlockSpec((None, 1, tn), lambda l, j: (l, 0, j)),
        ],
        out_specs=pl.BlockSpec((None, n, tn), lambda l, j: (l, 0, j)),
        out_shape=jax.ShapeDtypeStruct((DEPTH, n, 6 * D_MODEL), F32),
        compiler_params=_cparams(("arbitrary", "arbitrary")),
        name="modulation",
    )(conds, w_ada, b_ada.reshape(DEPTH, 1, 6 * D_MODEL))


def _ssm_disc_kernel(are_ref, aim_ref, ldt_ref, bre_ref, bim_ref, abre_ref, abim_ref, bbre_ref, bbim_ref):
    lam_re = are_ref[...]
    lam_im = aim_ref[...]
    dt = jnp.exp(ldt_ref[...])
    z_re = lam_re * dt
    z_im = lam_im * dt
    mag = jnp.exp(z_re)
    ab_re = mag * jnp.cos(z_im)
    ab_im = mag * jnp.sin(z_im)
    den = lam_re * lam_re + lam_im * lam_im
    f_re = ((ab_re - 1.0) * lam_re + ab_im * lam_im) / den
    f_im = (ab_im * lam_re - (ab_re - 1.0) * lam_im) / den
    b_re = bre_ref[...]
    b_im = bim_ref[...]
    abre_ref[...] = ab_re
    abim_ref[...] = ab_im
    bbre_ref[...] = f_re * b_re - f_im * b_im
    bbim_ref[...] = f_re * b_im + f_im * b_re


def _ssm_discretise(a_re, a_im, log_dt, b_re, b_im):
    n = DEPTH * 2
    col = lambda v: v.reshape(n, SSM_N, 1)
    ldt = jnp.broadcast_to(log_dt[..., None], (DEPTH, 2, SSM_GROUPS, SSM_STATE))
    cspec = pl.BlockSpec((None, SSM_N, 1), lambda i: (i, 0, 0))
    bspec = pl.BlockSpec((None, SSM_N, SSM_GROUP), lambda i: (i, 0, 0))
    return pl.pallas_call(
        _ssm_disc_kernel,
        grid=(n,),
        in_specs=[cspec, cspec, cspec, bspec, bspec],
        out_specs=[cspec, cspec, bspec, bspec],
        out_shape=[jax.ShapeDtypeStruct((n, SSM_N, 1), F32)] * 2
        + [jax.ShapeDtypeStruct((n, SSM_N, SSM_GROUP), F32)] * 2,
        compiler_params=_cparams(("arbitrary",)),
        name="ssm_discretise",
    )(col(a_re), col(a_im), col(ldt), b_re.reshape(n, SSM_N, SSM_GROUP), b_im.reshape(n, SSM_N, SSM_GROUP))


def _absorb_kernel(wq_ref, wk_ref, o_ref):
    a = wq_ref[...]
    b = wk_ref[...]
    ah, al = _split(a)
    bh, bl = _split(b)
    o_ref[...] = _dot_nt(ah, bh) + _dot_nt(al, bh) + _dot_nt(ah, bl)


def _absorb_q(wq_nope, wk_nope):
    return pl.pallas_call(
        _absorb_kernel,
        grid=(DEPTH, MLA_HEADS),
        in_specs=[
            pl.BlockSpec((None, None, MLA_Q_RANK, MLA_NOPE), lambda l, h: (l, h, 0, 0)),
            pl.BlockSpec((None, None, MLA_KV_RANK, MLA_NOPE), lambda l, h: (l, h, 0, 0)),
        ],
        out_specs=pl.BlockSpec((None, None, MLA_Q_RANK, MLA_KV_RANK), lambda l, h: (l, h, 0, 0)),
        out_shape=jax.ShapeDtypeStruct((DEPTH, MLA_HEADS, MLA_Q_RANK, MLA_KV_RANK), F32),
        compiler_params=_cparams(("arbitrary", "arbitrary")),
        name="mla_absorb",
    )(wq_nope, wk_nope)


def _pre_kernel(*refs, rope, emit_ctx):
    it = iter(refs)
    x_ref, mod_ref, n1_ref, win_ref, qn_ref, kvn_ref, wq_ref = (next(it) for _ in range(7))
    if rope:
        cm_ref, sm_ref, cmt_ref, smt_ref, cs_ref, ss_ref, cst_ref, sst_ref, wqs_ref = (next(it) for _ in range(9))
    u_ref, qt_ref, kcat_ref, ckvt_ref, qs_ref, ks_ref, vs_ref = (next(it) for _ in range(7))
    if emit_ctx:
        ckv_o, kr_o, ks_o, vs_o = (next(it) for _ in range(4))

    x = x_ref[...]
    mod = mod_ref[...]
    h = _rms(x, n1_ref[...]) * (1.0 + mod[1:2]) + mod[0:1]
    hb = h.astype(BF16)
    proj = _dot(hb, win_ref[...])

    u_ref[...] = proj[:, OFF_U:OFF_U + SSM_CH]

    qln = _rms(proj[:, OFF_QL:OFF_QL + MLA_Q_RANK], qn_ref[...]).astype(BF16)
    qall = _dot_nt(wq_ref[...], qln)
    n_rope = MLA_HEADS * MLA_ROPE
    zero_rows = jnp.zeros((MLA_QK - MLA_KV_RANK - MLA_ROPE, qall.shape[1]), BF16)
    for i in range(MLA_HEADS):
        qa = qall[i * MLA_KV_RANK:(i + 1) * MLA_KV_RANK]
        qr = qall[Q_ABS + i * MLA_ROPE:Q_ABS + (i + 1) * MLA_ROPE]
        if rope:
            qr_sw = qall[Q_ABS + n_rope + i * MLA_ROPE:Q_ABS + n_rope + (i + 1) * MLA_ROPE]
            qr = qr * cmt_ref[...] + qr_sw * smt_ref[...]
        base = i * MLA_QK
        qt_ref[base:base + MLA_KV_RANK, :] = (qa * MLA_SCALE_LOG2).astype(BF16)
        qt_ref[base + MLA_KV_RANK:base + MLA_KV_RANK + MLA_ROPE, :] = (qr * MLA_SCALE_LOG2).astype(BF16)
        qt_ref[base + MLA_KV_RANK + MLA_ROPE:base + MLA_QK, :] = zero_rows

    ckv = _rms(proj[:, OFF_KVL:OFF_KVL + MLA_KV_RANK], kvn_ref[...])
    kr = proj[:, OFF_KR:OFF_KR + LANES]
    if emit_ctx:
        ckv_o[...] = ckv
        kr_o[...] = kr[:, :MLA_ROPE]
    if rope:
        kr = kr * cm_ref[...] + proj[:, OFF_KR_SW:OFF_KR_SW + LANES] * sm_ref[...]
    kcat_ref[...] = jnp.concatenate([ckv, kr], axis=-1).astype(BF16)
    ckvt_ref[...] = ckv.T.astype(BF16)

    ks = proj[:, OFF_KS:OFF_KS + SWA_KV_HEADS * HEAD_DIM]
    vs = proj[:, OFF_VS:OFF_VS + SWA_KV_HEADS * HEAD_DIM]
    if emit_ctx:
        ks_o[...] = ks
        vs_o[...] = vs
    if rope:
        ks = ks * cs_ref[...] + proj[:, OFF_KS_SW:OFF_KS_SW + SWA_KV_HEADS * HEAD_DIM] * ss_ref[...]
        n_q = SWA_HEADS * HEAD_DIM
        qst = _dot_nt(wqs_ref[...], hb)
        zero = jnp.zeros((HEAD_DIM, SWA_BLOCK), BF16)
        for hd in range(SWA_HEADS):
            rows = slice(hd * HEAD_DIM, (hd + 1) * HEAD_DIM)
            rot = qst[rows] * cst_ref[...] + qst[n_q + hd * HEAD_DIM:n_q + (hd + 1) * HEAD_DIM] * sst_ref[...]
            rot = (rot * SWA_SCALE_LOG2).astype(BF16)
            kh = hd // SWA_GROUP
            for j in range(qs_ref.shape[0]):
                lanes = slice(hd * SWA_BLOCK, (hd + 1) * SWA_BLOCK)
                qs_ref[j, kh * HEAD_DIM:(kh + 1) * HEAD_DIM, lanes] = rot[:, j * SWA_BLOCK:(j + 1) * SWA_BLOCK]
                qs_ref[j, (1 - kh) * HEAD_DIM:(2 - kh) * HEAD_DIM, lanes] = zero
        vs_ref[...] = vs.T.astype(BF16)
    else:
        qs_ref[...] = (proj[:, OFF_QS:OFF_QS + SWA_HEADS * HEAD_DIM] * SWA_SCALE).astype(BF16)
        vs_ref[...] = vs.astype(BF16)
    ks_ref[...] = ks.astype(BF16)


def _pre(x, mod, lw, tables, *, per_batch_mod, emit_ctx):
    B, S, D = x.shape
    rope = tables is not None
    ts = min(S, 512)
    n_pack = N_PACK_LAT if rope else N_PACK_CTX
    win = lw["win_lat"] if rope else lw["win_ctx"]
    wq = lw["wq_lat"] if rope else lw["wq_ctx"]
    tok = lambda w: pl.BlockSpec((None, ts, w), lambda b, s: (b, s, 0))
    full = lambda a: pl.BlockSpec(a.shape, lambda b, s: (0,) * a.ndim)
    in_specs = [
        tok(D),
        pl.BlockSpec((None, 6, D), (lambda b, s: (b, 0, 0)) if per_batch_mod else (lambda b, s: (0, 0, 0))),
        full(lw["norm1"]), full(win), full(lw["q_norm"]), full(lw["kv_norm"]), full(wq),
    ]
    args = [x, mod, lw["norm1"], win, lw["q_norm"], lw["kv_norm"], wq]
    if rope:
        row_tab = pl.BlockSpec((ts, LANES), lambda b, s: (s, 0))
        col_tab = pl.BlockSpec((MLA_ROPE, ts), lambda b, s: (0, s))
        col_tab_s = pl.BlockSpec((HEAD_DIM, ts), lambda b, s: (0, s))
        in_specs += [row_tab, row_tab, col_tab, col_tab, row_tab, row_tab, col_tab_s, col_tab_s, full(lw["wqs_t"])]
        args += list(tables) + [lw["wqs_t"]]
    feat = lambda w: pl.BlockSpec((None, w, ts), lambda b, s: (b, 0, s))
    if rope:
        q_tile = (2 * HEAD_DIM, SWA_HEADS * SWA_BLOCK)
        qs_spec = pl.BlockSpec((None, ts // SWA_BLOCK) + q_tile, lambda b, s: (b, s, 0, 0))
        qs_shape = jax.ShapeDtypeStruct((B, S // SWA_BLOCK) + q_tile, BF16)
        vs_spec, vs_shape = feat(LANES), jax.ShapeDtypeStruct((B, LANES, S), BF16)
    else:
        qs_spec, qs_shape = tok(SWA_HEADS * HEAD_DIM), jax.ShapeDtypeStruct((B, S, SWA_HEADS * HEAD_DIM), BF16)
        vs_spec, vs_shape = tok(LANES), jax.ShapeDtypeStruct((B, S, LANES), BF16)
    out_specs = [
        tok(SSM_CH),
        feat(MLA_HEADS * MLA_QK), tok(MLA_QK), feat(MLA_KV_RANK), qs_spec, tok(LANES), vs_spec,
    ]
    out_shape = [
        jax.ShapeDtypeStruct((B, S, SSM_CH), F32),
        jax.ShapeDtypeStruct((B, MLA_HEADS * MLA_QK, S), BF16),
        jax.ShapeDtypeStruct((B, S, MLA_QK), BF16),
        jax.ShapeDtypeStruct((B, MLA_KV_RANK, S), BF16),
        qs_shape,
        jax.ShapeDtypeStruct((B, S, LANES), BF16),
        vs_shape,
    ]
    if emit_ctx:
        out_specs += [tok(MLA_KV_RANK), tok(MLA_ROPE), tok(LANES), tok(LANES)]
        out_shape += [
            jax.ShapeDtypeStruct((B, S, MLA_KV_RANK), F32),
            jax.ShapeDtypeStruct((B, S, MLA_ROPE), F32),
            jax.ShapeDtypeStruct((B, S, LANES), F32),
            jax.ShapeDtypeStruct((B, S, LANES), F32),
        ]
    return pl.pallas_call(
        functools.partial(_pre_kernel, rope=rope, emit_ctx=emit_ctx),
        grid=(B, S // ts),
        in_specs=in_specs,
        out_specs=out_specs,
        out_shape=out_shape,
        compiler_params=_cparams(("arbitrary", "arbitrary")),
        name="pre_lat" if rope else "pre_ctx",
    )(*args)


def _ssm_kernel(uf_ref, ub_ref, wb_ref, a_ref, wc_ref, h0_ref, yf_ref, yb_ref, hfin_ref,
                hre_f, him_f, hre_b, him_b, st_s, ut_s, *, tc):
    i = pl.program_id(1)
    n = pl.num_programs(1)

    @pl.when(i == 0)
    def _():
        st_s[...] = h0_ref[...]

    cblk = SSM_CH // LANES
    u_refs = (uf_ref, ub_ref)
    h_refs = ((hre_f, him_f), (hre_b, him_b))
    for d in range(2):
        ub = u_refs[d][...]
        for b in range(SUBLANES):
            for j in range(cblk):
                ut_s[d, j, pl.ds(b, tc, stride=SUBLANES), :] = ub[b, :, j * LANES:(j + 1) * LANES]
        u = jnp.concatenate([ut_s[d, j] for j in range(cblk)], axis=-1).astype(BF16)
        bu = _dot(u, wb_ref[d])
        h_refs[d][0][...] = bu[:, :SSM_N]
        h_refs[d][1][...] = bu[:, SSM_N:]

    a_re = [jnp.broadcast_to(a_ref[d][:, :SSM_N], (SUBLANES, SSM_N)) for d in range(2)]
    a_im = [jnp.broadcast_to(a_ref[d][:, SSM_N:], (SUBLANES, SSM_N)) for d in range(2)]
    h_re = [st_s[d][:, :SSM_N] for d in range(2)]
    h_im = [st_s[d][:, SSM_N:] for d in range(2)]
    for t in range(tc):
        for d in range(2):
            tt = t if d == 0 else tc - 1 - t
            rows = slice(tt * SUBLANES, (tt + 1) * SUBLANES)
            hre_s, him_s = h_refs[d]
            n_re = a_re[d] * h_re[d] - a_im[d] * h_im[d] + hre_s[rows, :]
            n_im = a_re[d] * h_im[d] + a_im[d] * h_re[d] + him_s[rows, :]
            hre_s[rows, :] = n_re
            him_s[rows, :] = n_im
            h_re[d], h_im[d] = n_re, n_im

    y_refs = (yf_ref, yb_ref)
    for d in range(2):
        st_s[d] = jnp.concatenate([h_re[d], h_im[d]], axis=-1)
        hre_s, him_s = h_refs[d]
        y = _dot(hre_s[...].astype(BF16), wc_ref[d, :SSM_N, :]) + _dot(him_s[...].astype(BF16), wc_ref[d, SSM_N:, :])
        for j in range(cblk):
            ut_s[d, j] = y[:, j * LANES:(j + 1) * LANES]
        for b in range(SUBLANES):
            y_refs[d][b] = jnp.concatenate(
                [ut_s[d, j, pl.ds(b, tc, stride=SUBLANES), :] for j in range(cblk)], axis=-1)

    @pl.when(i == n - 1)
    def _():
        hfin_ref[...] = st_s[...]


def _ssm(u, lw, h0, B, S):
    nb = B // SUBLANES
    tc = 128
    nchunk = S // tc
    full3 = lambda a: pl.BlockSpec(a.shape, lambda b, i: (0, 0, 0))
    state_spec = pl.BlockSpec((2, None, SUBLANES, 2 * SSM_N), lambda b, i: (0, b, 0, 0))
    fwd = pl.BlockSpec((SUBLANES, tc, SSM_CH), lambda b, i: (b, i, 0))
    bwd = pl.BlockSpec((SUBLANES, tc, SSM_CH), lambda b, i: (b, nchunk - 1 - i, 0))
    yf, yb, hfin = pl.pallas_call(
        functools.partial(_ssm_kernel, tc=tc),
        grid=(nb, nchunk),
        in_specs=[fwd, bwd, full3(lw["ssm_wb"]), full3(lw["ssm_a"]), full3(lw["ssm_wc"]), state_spec],
        out_specs=[fwd, bwd, state_spec],
        out_shape=[
            jax.ShapeDtypeStruct((B, S, SSM_CH), F32),
            jax.ShapeDtypeStruct((B, S, SSM_CH), F32),
            jax.ShapeDtypeStruct((2, nb, SUBLANES, 2 * SSM_N), F32),
        ],
        scratch_shapes=[pltpu.VMEM((tc * SUBLANES, SSM_N), F32)] * 4 + [
            pltpu.VMEM((2, SUBLANES, 2 * SSM_N), F32),
            pltpu.VMEM((2, SSM_CH // LANES, tc * SUBLANES, LANES), F32),
        ],
        compiler_params=_cparams(("arbitrary", "arbitrary")),
        name="ssm_scan",
    )(u, u, lw["ssm_wb"], lw["ssm_a"], lw["ssm_wc"], h0)
    return (yf, yb), hfin


def _mla_kernel(*refs, with_latent, tk):
    if with_latent:
        qt_ref, ka_ref, vat_ref, kb_ref, vbt_ref, wv_ref, o_ref = refs
    else:
        qt_ref, ka_ref, vat_ref, wv_ref, o_ref = refs
    tq = qt_ref.shape[1]
    q_of = lambda hd: qt_ref[hd * MLA_QK:(hd + 1) * MLA_QK, :]

    def tile(state, k, vt):
        new_state = []
        s_next = _dot(k, q_of(0))
        for hd in range(MLA_HEADS):
            m8, l8, acc = state[hd]
            s = s_next
            if hd + 1 < MLA_HEADS:
                s_next = _dot(k, q_of(hd + 1))
            s3 = s.reshape(s.shape[0] // SUBLANES, SUBLANES, tq)
            mloc = jnp.max(_tree(jnp.maximum, s3), axis=0, keepdims=True)
            m8_new = jnp.maximum(m8, jnp.broadcast_to(mloc, (SUBLANES, tq)))
            alpha8 = jnp.exp2(m8 - m8_new)
            p3 = jnp.exp2(s3 - m8_new[None])
            l8 = alpha8 * l8 + _tree(jnp.add, p3)
            pv = _dot(vt, p3.reshape(s.shape).astype(BF16))
            acc3 = acc.reshape(MLA_KV_RANK // SUBLANES, SUBLANES, tq) * alpha8[None]
            new_state.append((m8_new, l8, acc3.reshape(MLA_KV_RANK, tq) + pv))
        return tuple(new_state)

    init = tuple((jnp.full((SUBLANES, tq), -jnp.inf, F32), jnp.zeros((SUBLANES, tq), F32),
                  jnp.zeros((MLA_KV_RANK, tq), F32)) for _ in range(MLA_HEADS))
    state = tile(init, ka_ref[...], vat_ref[...])
    if with_latent:
        for j in range(kb_ref.shape[0] // tk):
            state = tile(state, kb_ref[j * tk:(j + 1) * tk, :], vbt_ref[:, j * tk:(j + 1) * tk])

    outs = []
    for hd in range(MLA_HEADS):
        _, l8, acc = state[hd]
        o_lat = (acc / jnp.sum(l8, axis=0, keepdims=True)).T.astype(BF16)
        outs.append(_dot(o_lat, wv_ref[hd]))
    o_ref[...] = jnp.concatenate(outs, axis=-1)


def _mla(qt, ka, vat, kb, vbt, wv):
    B, _, S = qt.shape
    tq = min(S, 512)
    with_latent = kb is not None
    kspec = lambda a: pl.BlockSpec((None,) + a.shape[1:], lambda b, s: (b, 0, 0))
    in_specs = [pl.BlockSpec((None, MLA_HEADS * MLA_QK, tq), lambda b, s: (b, 0, s)), kspec(ka), kspec(vat)]
    args = [qt, ka, vat]
    if with_latent:
        in_specs += [kspec(kb), kspec(vbt)]
        args += [kb, vbt]
    in_specs.append(pl.BlockSpec(wv.shape, lambda b, s: (0, 0, 0)))
    args.append(wv)
    return pl.pallas_call(
        functools.partial(_mla_kernel, with_latent=with_latent, tk=4096),
        grid=(B, S // tq),
        in_specs=in_specs,
        out_specs=pl.BlockSpec((None, tq, MLA_HEADS * MLA_V), lambda b, s: (b, s, 0)),
        out_shape=jax.ShapeDtypeStruct((B, S, MLA_HEADS * MLA_V), F32),
        compiler_params=_cparams(("arbitrary", "arbitrary")),
        name="mla_lat" if with_latent else "mla_ctx",
    )(*args)


def _swa_ctx_kernel(sink_ref, q_ref, k_ref, v_ref, o_ref):
    q = q_ref[...]
    k = k_ref[...]
    v = v_ref[...]
    outs = []
    for hd in range(SWA_HEADS):
        kh = hd // SWA_GROUP
        qh = q[:, hd * HEAD_DIM:(hd + 1) * HEAD_DIM]
        ksl = slice(kh * HEAD_DIM, (kh + 1) * HEAD_DIM)
        sink = sink_ref[hd]
        s = _dot_nt(qh, k[:, ksl])
        m = jnp.maximum(jnp.max(s, axis=-1, keepdims=True), sink)
        p = jnp.exp(s - m)
        den = jnp.sum(p, axis=-1, keepdims=True) + jnp.exp(sink - m)
        outs.append(_dot(p.astype(BF16), v[:, ksl]) / den)
    o_ref[...] = jnp.concatenate(outs, axis=-1)


def _swa_ctx(sink, q, k, v):
    B, S, W = q.shape
    seq = lambda a: pl.BlockSpec((None,) + a.shape[1:], lambda b: (b, 0, 0))
    return pl.pallas_call(
        _swa_ctx_kernel,
        grid=(B,),
        in_specs=[pl.BlockSpec(memory_space=pltpu.SMEM), seq(q), seq(k), seq(v)],
        out_specs=seq(q),
        out_shape=jax.ShapeDtypeStruct((B, S, W), F32),
        compiler_params=_cparams(("arbitrary",)),
        name="swa_ctx",
    )(sink, q, k, v)


def _swa_lat_kernel(sink_ref, q_ref, ka_ref, vat_ref, kb_ref, vbt_ref, o_ref):
    for j in range(q_ref.shape[0]):
        rows = slice(j * SWA_BLOCK, (j + 1) * SWA_BLOCK)
        o_ref[rows, :] = _swa_lat_block(pl.program_id(1) * q_ref.shape[0] + j, sink_ref, q_ref[j], ka_ref, vat_ref,
                                        kb_ref, vbt_ref)


def _swa_lat_block(n, sink_ref, q, ka_ref, vat_ref, kb_ref, vbt_ref):
    nblk = kb_ref.shape[0] // SWA_BLOCK
    n_lane = SWA_HEADS * SWA_BLOCK
    win = 3 * SWA_BLOCK
    start = pl.multiple_of(jnp.clip(n - 1, 0, nblk - 3) * SWA_BLOCK, SWA_BLOCK)
    s_a = _dot(ka_ref[...], q)
    s_b = _dot(kb_ref[pl.ds(start, win), :], q)
    kpos = start + lax.broadcasted_iota(jnp.int32, (win, n_lane), 0)
    qpos = n * SWA_BLOCK + (lax.broadcasted_iota(jnp.int32, (win, n_lane), 1) & (SWA_BLOCK - 1))
    s_b = jnp.where(jnp.abs(qpos - kpos) <= SWA_WINDOW, s_b, -jnp.inf)

    fold = lambda x: x.reshape(x.shape[0] // SUBLANES, SUBLANES, n_lane)
    sink = sink_ref[...]
    m8 = jnp.maximum(_tree(jnp.maximum, fold(s_a)), _tree(jnp.maximum, fold(s_b)))
    m = jnp.maximum(jnp.max(m8, axis=0, keepdims=True), sink)
    m8 = jnp.broadcast_to(m, (SUBLANES, n_lane))
    p_a = jnp.exp2(fold(s_a) - m8[None])
    p_b = jnp.exp2(fold(s_b) - m8[None])
    den = jnp.sum(_tree(jnp.add, p_a) + _tree(jnp.add, p_b), axis=0, keepdims=True) + jnp.exp2(sink - m)
    o_t = (_dot(vat_ref[...], p_a.reshape(s_a.shape).astype(BF16))
           + _dot(vbt_ref[:, pl.ds(start, win)], p_b.reshape(s_b.shape).astype(BF16))) / den
    o = o_t.T
    outs = []
    for hd in range(SWA_HEADS):
        kh = hd // SWA_GROUP
        outs.append(o[hd * SWA_BLOCK:(hd + 1) * SWA_BLOCK, kh * HEAD_DIM:(kh + 1) * HEAD_DIM])
    return jnp.concatenate(outs, axis=-1)


def _swa_lat(sink_row, q_tiles, ka, vat, kb, vbt):
    B, nblk = q_tiles.shape[:2]
    width = SWA_HEADS * HEAD_DIM
    per_step = 4
    kspec = lambda a: pl.BlockSpec((None,) + a.shape[1:], lambda b, s: (b, 0, 0))
    return pl.pallas_call(
        _swa_lat_kernel,
        grid=(B, nblk // per_step),
        in_specs=[
            pl.BlockSpec(sink_row.shape, lambda b, s: (0, 0)),
            pl.BlockSpec((None, per_step) + q_tiles.shape[2:], lambda b, s: (b, s, 0, 0)),
            kspec(ka), kspec(vat), kspec(kb), kspec(vbt),
        ],
        out_specs=pl.BlockSpec((None, per_step * SWA_BLOCK, width), lambda b, s: (b, s, 0)),
        out_shape=jax.ShapeDtypeStruct((B, nblk * SWA_BLOCK, width), F32),
        compiler_params=_cparams(("arbitrary", "arbitrary")),
        name="swa_lat",
    )(sink_row, q_tiles, ka, vat, kb, vbt)


def _route(logits):
    lane = lax.broadcasted_iota(jnp.int32, logits.shape, 1)
    big = jnp.int32(1 << 20)
    is_g = (lane >= MOE_EXPERTS) & (lane < MOE_EXPERTS + MOE_GROUPS)
    lg = jnp.where(is_g, logits, -jnp.inf)
    mg = jnp.max(lg, axis=-1, keepdims=True)
    g_idx = jnp.min(jnp.where(lg == mg, lane - MOE_EXPERTS, big), axis=-1, keepdims=True)
    pg_top = 1.0 / jnp.sum(jnp.exp(lg - mg), axis=-1, keepdims=True)

    is_e = (lane < MOE_EXPERTS) & ((lane // MOE_PER_GROUP) == g_idx)
    le = jnp.where(is_e, logits, -jnp.inf)
    m1 = jnp.max(le, axis=-1, keepdims=True)
    e1 = jnp.min(jnp.where(le == m1, lane, big), axis=-1, keepdims=True)
    z = jnp.sum(jnp.exp(le - m1), axis=-1, keepdims=True)
    le2 = jnp.where(lane == e1, -jnp.inf, le)
    m2 = jnp.max(le2, axis=-1, keepdims=True)
    e2 = jnp.min(jnp.where(le2 == m2, lane, big), axis=-1, keepdims=True)
    p1 = 1.0 / z
    p2 = jnp.exp(m2 - m1) / z
    tot = p1 + p2
    gates = pg_top * (jnp.where(lane == e1, p1 / tot, 0.0) + jnp.where(lane == e2, p2 / tot, 0.0))
    t = logits.shape[0]
    chose = jnp.where(lane == g_idx + COUNT_LANE, 1.0, 0.0)
    tri = jnp.where(lax.broadcasted_iota(jnp.int32, (t, t), 0) >= lax.broadcasted_iota(jnp.int32, (t, t), 1),
                    1.0, 0.0).astype(BF16)
    counts = _dot(tri, chose.astype(BF16))
    rank = jnp.sum(chose * (counts - 1.0), axis=-1, keepdims=True)
    return (gates + counts + jnp.where(lane == GROUP_LANE, g_idx.astype(F32), 0.0)
            + jnp.where(lane == RANK_LANE, rank, 0.0))


def _post_kernel(x_ref, u_ref, yf_ref, yb_ref, om_ref, os_ref, mod_ref, d_ref, wglu_ref, gs_ref, gm_ref,
                 gw_ref, wout_ref, n2_ref, wr_ref, br_ref, x1_ref, h2_ref, gate_ref):
    mod = mod_ref[...]
    y = d_ref[...] * u_ref[...] + yf_ref[...] + yb_ref[...]
    ga = _dot(_gelu_tanh(y).astype(BF16), wglu_ref[...])
    y_ssm = ga[:, :SSM_CH] * jax.nn.sigmoid(ga[:, SSM_CH:])
    wout = wout_ref
    n_mla = MLA_HEADS * MLA_V
    mixed = (_dot(_rms(y_ssm, gs_ref[...]).astype(BF16), wout[0:SSM_CH, :])
             + _dot(_rms(om_ref[...], gm_ref[...]).astype(BF16), wout[SSM_CH:SSM_CH + n_mla, :])
             + _dot(_rms(os_ref[...], gw_ref[...]).astype(BF16), wout[SSM_CH + n_mla:, :]))
    x1 = x_ref[...] + mod[2:3] * mixed
    x1_ref[...] = x1
    h2 = _rms(x1, n2_ref[...]) * (1.0 + mod[4:5]) + mod[3:4]
    h2_ref[...] = h2.astype(BF16)
    gate_ref[...] = _route(_dot3(h2, wr_ref[...]) + br_ref[...])


def _post(x, u_tm, y2, o_mla, o_swa, mod, lw, *, per_batch_mod):
    B, S, D = x.shape
    ts = min(S, POST_TILE)
    tok = lambda w: pl.BlockSpec((None, ts, w), lambda b, s: (b, s, 0))
    full = lambda a: pl.BlockSpec(a.shape, lambda b, s: (0,) * a.ndim)
    names = ["ssm_d", "w_glu", "gn_ssm", "gn_mla", "gn_swa", "w_out", "norm2", "w_router", "b_router"]
    in_specs = [
        tok(D),
        tok(SSM_CH),
        tok(SSM_CH), tok(SSM_CH),
        tok(MLA_HEADS * MLA_V), tok(SWA_HEADS * HEAD_DIM),
        pl.BlockSpec((None, 6, D), (lambda b, s: (b, 0, 0)) if per_batch_mod else (lambda b, s: (0, 0, 0))),
    ] + [full(lw[k]) for k in names]
    return pl.pallas_call(
        _post_kernel,
        grid=(B, S // ts),
        in_specs=in_specs,
        out_specs=[tok(D), tok(D), tok(LANES)],
        out_shape=[
            jax.ShapeDtypeStruct((B, S, D), F32),
            jax.ShapeDtypeStruct((B, S, D), BF16),
            jax.ShapeDtypeStruct((B, S, LANES), F32),
        ],
        compiler_params=_cparams(("arbitrary", "arbitrary")),
        name="post",
    )(x, u_tm, y2[0], y2[1], o_mla, o_swa, mod, *[lw[k] for k in names])


def _split3(x):
    hi = x.astype(BF16)
    r = x - hi.astype(F32)
    mid = r.astype(BF16)
    lo = (r - mid.astype(F32)).astype(BF16)
    return hi, mid, lo


def _moe_kernel(cnt_ref, h2_ref, gate_ref, crow_ref, x1_ref, mod_ref, wg_ref, wu_ref, wd_ref, fn_ref,
                o_ref, xs_s, gs_s, ys_s, acc_s, *, final, n_tiles):
    tm, d_model = acc_s.shape
    rb = MOE_ROW_BLOCK
    g = pl.program_id(2)
    e = pl.program_id(3)
    cnt = cnt_ref[(pl.program_id(0) * n_tiles + pl.program_id(1)) * MOE_GROUPS + g]
    n_blocks = (cnt + rb - 1) // rb
    wide_rb = MOE_SCATTER_BLOCK
    n_wide = cnt // wide_rb
    n_tail = n_blocks - n_wide * (wide_rb // rb)

    def for_rows(body, count, width, base=0):
        def step(blk, carry):
            body(pl.multiple_of(base + blk * width, rb), width)
            return carry
        lax.fori_loop(0, count, step, 0)

    def for_blocks(body):
        for_rows(body, n_wide, wide_rb)
        for_rows(body, n_tail, rb, n_wide * wide_rb)

    @pl.when((g == 0) & (e == 0))
    def _():
        acc_s[...] = jnp.zeros_like(acc_s)

    @pl.when(e == 0)
    def _():
        rank = crow_ref[...] - g * MOE_CODE_STRIDE
        h2 = h2_ref[...].reshape(tm, d_model)
        gates = gate_ref[...].reshape(tm, LANES)

        def gather(start, width):
            rows = pl.ds(start, width)
            row_id = lax.broadcasted_iota(jnp.int32, (width, tm), 0) + start
            onehot = jnp.where(row_id == rank, 1.0, 0.0).astype(BF16)
            xs_s[rows, :] = _dot(onehot, h2).astype(BF16)
            g_hi, g_mid, g_lo = _split3(gates)
            gs_s[rows, :] = _dot(onehot, g_hi) + _dot(onehot, g_mid) + _dot(onehot, g_lo)
        for_blocks(gather)

        @pl.when(n_blocks % (MOE_SCATTER_BLOCK // rb) != 0)
        def _():
            ys_s[pl.ds(pl.multiple_of(n_blocks * rb, rb), rb), :] = jnp.zeros((rb, d_model), F32)

    n_exp = MOE_PER_GROUP // MOE_SPLIT

    def experts(start, width, first):
        rows = pl.ds(start, width)
        x = xs_s[rows, :]
        gs = gs_s[rows, :]
        lane = lax.broadcasted_iota(jnp.int32, gs.shape, 1)
        first_expert = g * MOE_PER_GROUP + e * n_exp
        gate = jnp.concatenate(
            [jnp.broadcast_to(jnp.sum(jnp.where(lane == first_expert + k, gs, 0.0), axis=-1, keepdims=True),
                              (width, MOE_HIDDEN)) for k in range(n_exp)], axis=-1)
        wide = lambda w_ref: jnp.concatenate([_dot(x, w_ref[k]) for k in range(n_exp)], axis=-1)
        hid = _silu(wide(wg_ref)) * wide(wu_ref) * gate
        y = _dot(hid.astype(BF16), wd_ref[...])
        if first:
            ys_s[rows, :] = y
        else:
            ys_s[rows, :] += y

    @pl.when(e == 0)
    def _():
        for_blocks(lambda start, width: experts(start, width, True))

    @pl.when(e != 0)
    def _():
        for_blocks(lambda start, width: experts(start, width, False))

    @pl.when(e == MOE_SPLIT - 1)
    def _():
        code_t = jnp.broadcast_to(crow_ref[...].astype(F32), (LANES, tm)).T
        rank = code_t.astype(jnp.int32) - g * MOE_CODE_STRIDE
        lane = lax.broadcasted_iota(jnp.int32, (tm, LANES), 1)

        def scatter(start, width):
            onehot = jnp.concatenate(
                [jnp.where(lane + (start + part * LANES) == rank, 1.0, 0.0) for part in range(width // LANES)],
                axis=-1).astype(BF16)
            acc_s[...] += _dot(onehot, ys_s[pl.ds(start, width), :].astype(BF16))
        for_rows(scatter, (cnt + wide_rb - 1) // wide_rb, wide_rb)

    @pl.when((g == MOE_GROUPS - 1) & (e == MOE_SPLIT - 1))
    def _():
        xo = x1_ref[...].reshape(tm, d_model) + mod_ref[5:6, :] * acc_s[...]
        xo = _rms(xo, fn_ref[...]) if final else xo
        o_ref[...] = xo.reshape(o_ref.shape)


def _moe(h2, gates, x1, mod, lw, final_norm, *, per_batch_mod, final, rt):
    B, S, D = x1.shape
    tm = MOE_TILE
    if S >= tm:
        nb, nt = B, S // tm
        tok = lambda w: pl.BlockSpec((None, tm, w), lambda b, s, g, e, c: (b, s, 0))
    else:
        assert not per_batch_mod
        nb, nt = B * S // tm, 1
        tok = lambda w: pl.BlockSpec((tm // S, S, w), lambda b, s, g, e, c: (b, 0, 0))
    sub = tm // rt
    info = gates[..., GROUP_LANE:COUNT_LANE + MOE_GROUPS].astype(jnp.int32)
    grp = info[..., 0].reshape(nb, nt, sub, rt)
    sub_counts = info[:, rt - 1::rt, 2:].reshape(nb, nt, sub, MOE_GROUPS)
    offsets = jnp.cumsum(sub_counts, axis=2) - sub_counts
    onehot = grp[..., None] == jnp.arange(MOE_GROUPS, dtype=jnp.int32)
    rank = info[..., 1].reshape(nb, nt, sub, rt) + jnp.sum(jnp.where(onehot, offsets[:, :, :, None, :], 0), axis=-1)
    counts = jnp.sum(sub_counts, axis=2).reshape(-1)
    code = grp * MOE_CODE_STRIDE + rank
    code_row = code.reshape(nb, nt, 1, tm)

    n_exp = MOE_PER_GROUP // MOE_SPLIT
    step_map = lambda b, s, g, e, c: (g * MOE_SPLIT + e, 0, 0)
    mod_map = (lambda b, s, g, e, c: (b, 0, 0)) if per_batch_mod else (lambda b, s, g, e, c: (0, 0, 0))
    grid_spec = pltpu.PrefetchScalarGridSpec(
        num_scalar_prefetch=1,
        grid=(nb, nt, MOE_GROUPS, MOE_SPLIT),
        in_specs=[
            tok(D), tok(LANES),
            pl.BlockSpec((None, None, 1, tm), lambda b, s, g, e, c: (b, s, 0, 0)),
            tok(D),
            pl.BlockSpec((None, 6, D), mod_map),
            pl.BlockSpec((n_exp, D, MOE_HIDDEN), step_map), pl.BlockSpec((n_exp, D, MOE_HIDDEN), step_map),
            pl.BlockSpec((None, n_exp * MOE_HIDDEN, D), step_map),
            pl.BlockSpec(final_norm.shape, lambda b, s, g, e, c: (0, 0)),
        ],
        out_specs=tok(D),
        scratch_shapes=[
            pltpu.VMEM((tm, D), BF16),
            pltpu.VMEM((tm, LANES), F32),
            pltpu.VMEM((tm, D), F32),
            pltpu.VMEM((tm, D), F32),
        ],
    )
    return pl.pallas_call(
        functools.partial(_moe_kernel, final=final, n_tiles=nt),
        grid_spec=grid_spec,
        out_shape=jax.ShapeDtypeStruct((B, S, D), F32),
        compiler_params=_cparams(("arbitrary", "arbitrary", "arbitrary", "arbitrary")),
        name="moe",
    )(counts, h2, gates, code_row, x1, mod, lw["moe_wg"], lw["moe_wu"], lw["moe_wd"], final_norm)


def _swap_halves(w, n_heads, dim):
    k = w.shape[0]
    w = w.reshape(k, n_heads, 2, dim // 2)
    return w[:, :, ::-1, :].reshape(k, n_heads * dim)


def _pad_cols(w, width):
    return jnp.pad(w, ((0, 0), (0, width - w.shape[1])))


def _rope_tables(n_tokens, rot_dim, reps):
    t = jnp.arange(n_tokens)
    row = (t // GRID_W).astype(F32)
    col = (t % GRID_W).astype(F32)
    n_freq = rot_dim // 4
    inv_freq = ROPE_BASE ** (-jnp.arange(n_freq, dtype=F32) / n_freq)
    ang = jnp.concatenate([row[:, None] * inv_freq, col[:, None] * inv_freq], axis=-1)
    cos, sin = jnp.cos(ang), jnp.sin(ang)
    c = jnp.tile(jnp.concatenate([cos, cos], axis=-1), (1, reps))
    s = jnp.tile(jnp.concatenate([-sin, sin], axis=-1), (1, reps))
    return _pad_cols(c, LANES), _pad_cols(s, LANES)


def _layer_weights(l, P, q_abs, ab_re, ab_im, bb_re, bb_im):
    w_in = P["w_in"][l]
    seg = {}
    o = 0
    for name, width in (("u", SSM_CH), ("ql", MLA_Q_RANK), ("kvl", MLA_KV_RANK), ("kr", MLA_ROPE),
                        ("qs", SWA_HEADS * HEAD_DIM), ("ks", SWA_KV_HEADS * HEAD_DIM),
                        ("vs", SWA_KV_HEADS * HEAD_DIM)):
        seg[name] = w_in[:, o:o + width]
        o += width
    shared_cols = [seg["u"], seg["ql"], seg["kvl"], seg["ks"], seg["vs"], _pad_cols(seg["kr"], LANES)]
    ctx_cols = shared_cols + [seg["qs"]]
    lat_cols = shared_cols + [
        _swap_halves(seg["ks"], SWA_KV_HEADS, HEAD_DIM),
        _pad_cols(_swap_halves(seg["kr"], 1, MLA_ROPE), LANES),
    ]
    wqs_t = jnp.concatenate([seg["qs"], _swap_halves(seg["qs"], SWA_HEADS, HEAD_DIM)], axis=1).T
    w_qb = P["w_mla_qb"][l]
    w_rope = w_qb[:, :, MLA_NOPE:]
    w_rope_sw = w_rope.reshape(MLA_Q_RANK, MLA_HEADS, 2, MLA_ROPE // 2)[:, :, ::-1, :].reshape(w_rope.shape)
    flat_t = lambda w: w.reshape(MLA_Q_RANK, MLA_HEADS * MLA_ROPE).T
    wq_abs = jnp.transpose(q_abs[l], (0, 2, 1)).reshape(Q_ABS, MLA_Q_RANK)
    wq_ctx = jnp.concatenate([wq_abs, flat_t(w_rope)], axis=0)
    wq_lat = jnp.concatenate([wq_ctx, flat_t(w_rope_sw)], axis=0)

    eye = jnp.eye(SSM_GROUPS, dtype=F32)

    def block_diag_b(bb):
        bb = bb.reshape(2, SSM_GROUPS, SSM_STATE, SSM_GROUP)
        return jnp.einsum("dgpc,gh->dgchp", bb, eye).reshape(2, SSM_CH, SSM_N)

    def block_diag_c(cc):
        return jnp.einsum("dgcp,gh->dgphc", cc, eye).reshape(2, SSM_N, SSM_CH)

    sl = slice(2 * l, 2 * l + 2)
    w_router = jnp.concatenate([P["moe_w_expert"][l], P["moe_w_group"][l]], axis=1)
    b_router = jnp.concatenate([P["moe_b_expert"][l], P["moe_b_group"][l]])
    row = lambda v: v.reshape(1, -1)
    return dict(
        norm1=row(P["norm1"][l]), norm2=row(P["norm2"][l]),
        win_ctx=jnp.concatenate(ctx_cols, axis=1).astype(BF16),
        win_lat=jnp.concatenate(lat_cols, axis=1).astype(BF16), wqs_t=wqs_t.astype(BF16),
        q_norm=row(P["mla_q_norm"][l]), kv_norm=row(P["mla_kv_norm"][l]),
        wq_ctx=wq_ctx.astype(BF16), wq_lat=wq_lat.astype(BF16),
        wv=jnp.transpose(P["w_mla_kvb"][l][:, :, MLA_NOPE:], (1, 0, 2)).astype(BF16),
        ssm_wb=jnp.concatenate([block_diag_b(bb_re[sl]), block_diag_b(bb_im[sl])], axis=2).astype(BF16),
        ssm_a=jnp.concatenate([ab_re[sl], ab_im[sl]], axis=1).reshape(2, 1, 2 * SSM_N),
        ssm_wc=jnp.concatenate([block_diag_c(P["ssm_c_re"][l]), -block_diag_c(P["ssm_c_im"][l])],
                               axis=1).astype(BF16),
        ssm_d=row(P["ssm_d"][l]), w_glu=P["w_ssm_glu"][l].astype(BF16),
        gn_ssm=row(P["gn_ssm"][l]), gn_mla=row(P["gn_mla"][l]), gn_swa=row(P["gn_swa"][l]),
        w_out=P["w_out"][l].astype(BF16),
        w_router=_pad_cols(w_router, LANES), b_router=_pad_cols(row(b_router), LANES),
        moe_wg=P["moe_w_gate"][l].astype(BF16).reshape(MOE_EXPERTS, D_MODEL, MOE_HIDDEN),
        moe_wu=P["moe_w_up"][l].astype(BF16).reshape(MOE_EXPERTS, D_MODEL, MOE_HIDDEN),
        moe_wd=P["moe_w_down"][l].astype(BF16).reshape(
            MOE_GROUPS * MOE_SPLIT, MOE_PER_GROUP // MOE_SPLIT * MOE_HIDDEN, D_MODEL),
        sink=P["swa_sink"][l],
        sink_row=jnp.repeat(P["swa_sink"][l] * math.log2(math.e), SWA_BLOCK).reshape(1, SWA_HEADS * SWA_BLOCK),
    )


def _layer(x, mod, lw, final_norm, *, tables, ctx, per_batch_mod, final):
    B, S, _ = x.shape
    context_pass = ctx is None
    pre = _pre(x, mod, lw, tables, per_batch_mod=per_batch_mod, emit_ctx=context_pass)
    u_tm, qt, kcat, ckvt, qs, ks, vs = pre[:7]
    nb = B // SUBLANES
    if context_pass:
        h0 = jnp.zeros((2, nb, SUBLANES, 2 * SSM_N), F32)
        y2, hfin = _ssm(u_tm, lw, h0, B, S)
        o_mla = _mla(qt, kcat, ckvt, None, None, lw["wv"])
        o_swa = _swa_ctx(lw["sink"], qs, ks, vs)
        state = hfin.reshape(2, B, 2, SSM_GROUPS, SSM_STATE).transpose(1, 0, 2, 3, 4)
        new_ctx = (pre[7], pre[8], pre[9].reshape(B, S, SWA_KV_HEADS, HEAD_DIM),
                   pre[10].reshape(B, S, SWA_KV_HEADS, HEAD_DIM), state)
    else:
        kcat_c, ckvt_c, ks_c, vs_c, h0 = ctx
        y2, _ = _ssm(u_tm, lw, h0, B, S)
        o_mla = _mla(qt, kcat_c, ckvt_c, kcat, ckvt, lw["wv"])
        o_swa = _swa_lat(lw["sink_row"], qs, ks_c, vs_c, ks, vs)
        new_ctx = None
    x1, h2, gates = _post(x, u_tm, y2, o_mla, o_swa, mod, lw, per_batch_mod=per_batch_mod)
    xo = _moe(h2, gates, x1, mod, lw, final_norm, per_batch_mod=per_batch_mod, final=final, rt=min(S, POST_TILE))
    return xo, new_ctx


def kernel(x_prompt, x_sample, c, cache_mla_ckv, cache_mla_krope, cache_swa_k, cache_swa_v, state_ssm, c_ctx, w_ada, b_ada, norm1, norm2, w_in, ssm_a_re, ssm_a_im, ssm_log_dt, ssm_b_re, ssm_b_im, ssm_c_re, ssm_c_im, ssm_d, w_ssm_glu, mla_q_norm, w_mla_qb, mla_kv_norm, w_mla_kvb, swa_sink, gn_ssm, gn_mla, gn_swa, w_out, moe_w_group, moe_b_group, moe_w_expert, moe_b_expert, moe_w_gate, moe_w_up, moe_w_down, final_norm):
    P = dict(w_ada=w_ada, b_ada=b_ada, norm1=norm1, norm2=norm2, w_in=w_in,
             ssm_c_re=ssm_c_re, ssm_c_im=ssm_c_im, ssm_d=ssm_d, w_ssm_glu=w_ssm_glu,
             mla_q_norm=mla_q_norm, w_mla_qb=w_mla_qb, mla_kv_norm=mla_kv_norm, w_mla_kvb=w_mla_kvb,
             swa_sink=swa_sink, gn_ssm=gn_ssm, gn_mla=gn_mla, gn_swa=gn_swa, w_out=w_out,
             moe_w_group=moe_w_group, moe_b_group=moe_b_group, moe_w_expert=moe_w_expert,
             moe_b_expert=moe_b_expert, moe_w_gate=moe_w_gate, moe_w_up=moe_w_up, moe_w_down=moe_w_down)
    n_dec = c.shape[0]
    n_cond = 2 * SUBLANES
    conds = jnp.zeros((n_cond, D_MODEL), F32).at[:n_dec].set(c).at[n_dec].set(c_ctx)
    mods = _modulation(conds, w_ada, b_ada).reshape(DEPTH, n_cond, 6, D_MODEL)

    ab_re, ab_im, bb_re, bb_im = _ssm_discretise(ssm_a_re, ssm_a_im, ssm_log_dt, ssm_b_re, ssm_b_im)
    q_abs = _absorb_q(jnp.transpose(w_mla_qb[..., :MLA_NOPE], (0, 2, 1, 3)),
                      jnp.transpose(w_mla_kvb[..., :MLA_NOPE], (0, 2, 1, 3)))
    lws = [_layer_weights(l, P, q_abs, ab_re.reshape(2 * DEPTH, SSM_N), ab_im.reshape(2 * DEPTH, SSM_N),
                          bb_re, bb_im) for l in range(DEPTH)]
    fnorm = final_norm.reshape(1, D_MODEL)

    xp = x_prompt
    ctx_states = []
    for l in range(DEPTH):
        xp, new = _layer(xp, mods[l, n_dec:n_dec + 1], lws[l], fnorm, tables=None, ctx=None,
                         per_batch_mod=False, final=l == DEPTH - 1)
        ctx_states.append(new)
    outs_ctx = tuple(jnp.stack([s[k] for s in ctx_states], axis=1) for k in range(5))

    n_lat = x_sample.shape[1]
    cm, sm = _rope_tables(n_lat, MLA_ROPE, 1)
    cs, ss = _rope_tables(n_lat, HEAD_DIM, LANES // HEAD_DIM)
    tables = (cm, sm, cm[:, :MLA_ROPE].T, sm[:, :MLA_ROPE].T, cs, ss, cs[:, :HEAD_DIM].T, ss[:, :HEAD_DIM].T)
    xs = x_sample
    past = cache_mla_ckv.shape[2]
    for l in range(DEPTH):
        kcat_c = jnp.concatenate(
            [cache_mla_ckv[:, l], cache_mla_krope[:, l],
             jnp.zeros((n_dec, past, LANES - MLA_ROPE), F32)], axis=-1).astype(BF16)
        ks_c = cache_swa_k[:, l].reshape(n_dec, past, LANES).astype(BF16)
        vs_c = jnp.transpose(cache_swa_v[:, l].reshape(n_dec, past, LANES), (0, 2, 1)).astype(BF16)
        h0 = state_ssm[:, l].transpose(1, 0, 2, 3, 4).reshape(2, n_dec // SUBLANES, SUBLANES, 2 * SSM_N)
        ckvt_c = jnp.transpose(cache_mla_ckv[:, l], (0, 2, 1)).astype(BF16)
        xs, _ = _layer(xs, mods[l, :n_dec], lws[l], fnorm, tables=tables, ctx=(kcat_c, ckvt_c, ks_c, vs_c, h0),
                       per_batch_mod=True, final=l == DEPTH - 1)
    return (xp, xs) + outs_ctx
```

```python
import functools
import math

import jax
import jax.numpy as jnp
from jax import lax
from jax.experimental import pallas as pl
from jax.experimental.pallas import tpu as pltpu

F32 = jnp.float32
BF16 = jnp.bfloat16

D_MODEL = 1024
DEPTH = 4
GRID_W = 64
HEAD_DIM = 64
SSM_CH = 256
SSM_GROUP = 16
SSM_GROUPS = SSM_CH // SSM_GROUP
SSM_STATE = 64
SSM_N = SSM_GROUPS * SSM_STATE
MLA_HEADS = 6
MLA_Q_RANK = 256
MLA_KV_RANK = 128
MLA_NOPE = 64
MLA_ROPE = 32
MLA_V = 64
SWA_HEADS = 6
SWA_KV_HEADS = 2
SWA_GROUP = SWA_HEADS // SWA_KV_HEADS
SWA_WINDOW = 128
SWA_BLOCK = 128
MOE_GROUPS = 4
MOE_PER_GROUP = 8
MOE_EXPERTS = MOE_GROUPS * MOE_PER_GROUP
MOE_HIDDEN = 256
GROUP_LANE = MOE_EXPERTS
RANK_LANE = GROUP_LANE + 1
COUNT_LANE = GROUP_LANE + 2
POST_TILE = 512
MOE_SPLIT = 2
MOE_TILE = 1024
MOE_ROW_BLOCK = 128
MOE_SCATTER_BLOCK = 256
MOE_CODE_STRIDE = 1 << 16
ROPE_BASE = 10000.0
EPS = 1e-6
MLA_SCALE = 1.0 / math.sqrt(MLA_NOPE + MLA_ROPE)
MLA_SCALE_LOG2 = MLA_SCALE * math.log2(math.e)
SWA_SCALE = 1.0 / math.sqrt(HEAD_DIM)
SWA_SCALE_LOG2 = SWA_SCALE * math.log2(math.e)

LANES = 128
SUBLANES = 8
VMEM_LIMIT = 52 * 1024 * 1024

OFF_U = 0
OFF_QL = OFF_U + SSM_CH
OFF_KVL = OFF_QL + MLA_Q_RANK
OFF_KS = OFF_KVL + MLA_KV_RANK
OFF_VS = OFF_KS + SWA_KV_HEADS * HEAD_DIM
OFF_KR = OFF_VS + SWA_KV_HEADS * HEAD_DIM
OFF_TAIL = OFF_KR + LANES
OFF_QS = OFF_TAIL
N_PACK_CTX = OFF_QS + SWA_HEADS * HEAD_DIM
OFF_KS_SW = OFF_TAIL
OFF_KR_SW = OFF_KS_SW + SWA_KV_HEADS * HEAD_DIM
N_PACK_LAT = OFF_KR_SW + LANES
Q_ABS = MLA_HEADS * MLA_KV_RANK
MLA_QK = 2 * LANES


def _cparams(sem):
    return pltpu.CompilerParams(dimension_semantics=sem, vmem_limit_bytes=VMEM_LIMIT)


def _dot(a, b):
    return jnp.dot(a, b, preferred_element_type=F32)


def _dot_nt(a, b):
    return lax.dot_general(a, b, (((1,), (1,)), ((), ())), preferred_element_type=F32)


def _split(x):
    hi = x.astype(BF16)
    lo = (x - hi.astype(F32)).astype(BF16)
    return hi, lo


def _dot3(a, b):
    ah, al = _split(a)
    bh, bl = _split(b)
    return _dot(ah, bh) + _dot(al, bh) + _dot(ah, bl)


def _tree(op, x3):
    parts = [x3[i] for i in range(x3.shape[0])]
    while len(parts) > 1:
        pairs = [op(parts[i], parts[i + 1]) for i in range(0, len(parts) - 1, 2)]
        parts = pairs + parts[len(parts) - len(parts) % 2:]
    return parts[0]


def _rms(x, g):
    return x * lax.rsqrt(jnp.mean(x * x, axis=-1, keepdims=True) + EPS) * g


def _silu(x):
    return x * jax.nn.sigmoid(x)


def _gelu_tanh(x):
    return 0.5 * x * (1.0 + jnp.tanh(math.sqrt(2.0 / math.pi) * (x + 0.044715 * (x * x * x))))


def _mod_kernel(c_ref, w_ref, b_ref, o_ref):
    o_ref[...] = _dot3(_silu(c_ref[...]), w_ref[...]) + b_ref[...]


def _modulation(conds, w_ada, b_ada):
    n = conds.shape[0]
    tn = 1536
    return pl.pallas_call(
        _mod_kernel,
        grid=(DEPTH, 6 * D_MODEL // tn),
        in_specs=[
            pl.BlockSpec((n, D_MODEL), lambda l, j: (0, 0)),
            pl.BlockSpec((None, D_MODEL, tn), lambda l, j: (l, 0, j)),
            pl.BlockSpec((None, 1, tn), lambda l, j: (l, 0, j)),
        ],
        out_specs=pl.BlockSpec((None, n, tn), lambda l, j: (l, 0, j)),
        out_shape=jax.ShapeDtypeStruct((DEPTH, n, 6 * D_MODEL), F32),
        compiler_params=_cparams(("arbitrary", "arbitrary")),
        name="modulation",
    )(conds, w_ada, b_ada.reshape(DEPTH, 1, 6 * D_MODEL))


def _ssm_disc_kernel(are_ref, aim_ref, ldt_ref, bre_ref, bim_ref, abre_ref, abim_ref, bbre_ref, bbim_ref):
    lam_re = are_ref[...]
    lam_im = aim_ref[...]
    dt = jnp.exp(ldt_ref[...])
    z_re = lam_re * dt
    z_im = lam_im * dt
    mag = jnp.exp(z_re)
    ab_re = mag * jnp.cos(z_im)
    ab_im = mag * jnp.sin(z_im)
    den = lam_re * lam_re + lam_im * lam_im
    f_re = ((ab_re - 1.0) * lam_re + ab_im * lam_im) / den
    f_im = (ab_im * lam_re - (ab_re - 1.0) * lam_im) / den
    b_re = bre_ref[...]
    b_im = bim_ref[...]
    abre_ref[...] = ab_re
    abim_ref[...] = ab_im
    bbre_ref[...] = f_re * b_re - f_im * b_im
    bbim_ref[...] = f_re * b_im + f_im * b_re


def _ssm_discretise(a_re, a_im, log_dt, b_re, b_im):
    n = DEPTH * 2
    col = lambda v: v.reshape(n, SSM_N, 1)
    ldt = jnp.broadcast_to(log_dt[..., None], (DEPTH, 2, SSM_GROUPS, SSM_STATE))
    cspec = pl.BlockSpec((None, SSM_N, 1), lambda i: (i, 0, 0))
    bspec = pl.BlockSpec((None, SSM_N, SSM_GROUP), lambda i: (i, 0, 0))
    return pl.pallas_call(
        _ssm_disc_kernel,
        grid=(n,),
        in_specs=[cspec, cspec, cspec, bspec, bspec],
        out_specs=[cspec, cspec, bspec, bspec],
        out_shape=[jax.ShapeDtypeStruct((n, SSM_N, 1), F32)] * 2
        + [jax.ShapeDtypeStruct((n, SSM_N, SSM_GROUP), F32)] * 2,
        compiler_params=_cparams(("arbitrary",)),
        name="ssm_discretise",
    )(col(a_re), col(a_im), col(ldt), b_re.reshape(n, SSM_N, SSM_GROUP), b_im.reshape(n, SSM_N, SSM_GROUP))


def _absorb_kernel(wq_ref, wk_ref, o_ref):
    a = wq_ref[...]
    b = wk_ref[...]
    ah, al = _split(a)
    bh, bl = _split(b)
    o_ref[...] = _dot_nt(ah, bh) + _dot_nt(al, bh) + _dot_nt(ah, bl)


def _absorb_q(wq_nope, wk_nope):
    return pl.pallas_call(
        _absorb_kernel,
        grid=(DEPTH, MLA_HEADS),
        in_specs=[
            pl.BlockSpec((None, None, MLA_Q_RANK, MLA_NOPE), lambda l, h: (l, h, 0, 0)),
            pl.BlockSpec((None, None, MLA_KV_RANK, MLA_NOPE), lambda l, h: (l, h, 0, 0)),
        ],
        out_specs=pl.BlockSpec((None, None, MLA_Q_RANK, MLA_KV_RANK), lambda l, h: (l, h, 0, 0)),
        out_shape=jax.ShapeDtypeStruct((DEPTH, MLA_HEADS, MLA_Q_RANK, MLA_KV_RANK), F32),
        compiler_params=_cparams(("arbitrary", "arbitrary")),
        name="mla_absorb",
    )(wq_nope, wk_nope)


def _pre_kernel(*refs, rope, emit_ctx):
    it = iter(refs)
    x_ref, mod_ref, n1_ref, win_ref, qn_ref, kvn_ref, wq_ref = (next(it) for _ in range(7))
    if rope:
        cm_ref, sm_ref, cmt_ref, smt_ref, cs_ref, ss_ref, cst_ref, sst_ref, wqs_ref = (next(it) for _ in range(9))
    u_ref, qt_ref, kcat_ref, ckvt_ref, qs_ref, ks_ref, vs_ref = (next(it) for _ in range(7))
    if emit_ctx:
        ckv_o, kr_o, ks_o, vs_o = (next(it) for _ in range(4))

    x = x_ref[...]
    mod = mod_ref[...]
    h = _rms(x, n1_ref[...]) * (1.0 + mod[1:2]) + mod[0:1]
    hb = h.astype(BF16)
    proj = _dot(hb, win_ref[...])

    u_ref[...] = proj[:, OFF_U:OFF_U + SSM_CH]

    qln = _rms(proj[:, OFF_QL:OFF_QL + MLA_Q_RANK], qn_ref[...]).astype(BF16)
    qall = _dot_nt(wq_ref[...], qln)
    n_rope = MLA_HEADS * MLA_ROPE
    zero_rows = jnp.zeros((MLA_QK - MLA_KV_RANK - MLA_ROPE, qall.shape[1]), BF16)
    for i in range(MLA_HEADS):
        qa = qall[i * MLA_KV_RANK:(i + 1) * MLA_KV_RANK]
        qr = qall[Q_ABS + i * MLA_ROPE:Q_ABS + (i + 1) * MLA_ROPE]
        if rope:
            qr_sw = qall[Q_ABS + n_rope + i * MLA_ROPE:Q_ABS + n_rope + (i + 1) * MLA_ROPE]
            qr = qr * cmt_ref[...] + qr_sw * smt_ref[...]
        base = i * MLA_QK
        qt_ref[base:base + MLA_KV_RANK, :] = (qa * MLA_SCALE_LOG2).astype(BF16)
        qt_ref[base + MLA_KV_RANK:base + MLA_KV_RANK + MLA_ROPE, :] = (qr * MLA_SCALE_LOG2).astype(BF16)
        qt_ref[base + MLA_KV_RANK + MLA_ROPE:base + MLA_QK, :] = zero_rows

    ckv = _rms(proj[:, OFF_KVL:OFF_KVL + MLA_KV_RANK], kvn_ref[...])
    kr = proj[:, OFF_KR:OFF_KR + LANES]
    if emit_ctx:
        ckv_o[...] = ckv
        kr_o[...] = kr[:, :MLA_ROPE]
    if rope:
        kr = kr * cm_ref[...] + proj[:, OFF_KR_SW:OFF_KR_SW + LANES] * sm_ref[...]
    kcat_ref[...] = jnp.concatenate([ckv, kr], axis=-1).astype(BF16)
    ckvt_ref[...] = ckv.T.astype(BF16)

    ks = proj[:, OFF_KS:OFF_KS + SWA_KV_HEADS * HEAD_DIM]
    vs = proj[:, OFF_VS:OFF_VS + SWA_KV_HEADS * HEAD_DIM]
    if emit_ctx:
        ks_o[...] = ks
        vs_o[...] = vs
    if rope:
        ks = ks * cs_ref[...] + proj[:, OFF_KS_SW:OFF_KS_SW + SWA_KV_HEADS * HEAD_DIM] * ss_ref[...]
        n_q = SWA_HEADS * HEAD_DIM
        qst = _dot_nt(wqs_ref[...], hb)
        zero = jnp.zeros((HEAD_DIM, SWA_BLOCK), BF16)
        for hd in range(SWA_HEADS):
            rows = slice(hd * HEAD_DIM, (hd + 1) * HEAD_DIM)
            rot = qst[rows] * cst_ref[...] + qst[n_q + hd * HEAD_DIM:n_q + (hd + 1) * HEAD_DIM] * sst_ref[...]
            rot = (rot * SWA_SCALE_LOG2).astype(BF16)
            kh = hd // SWA_GROUP
            for j in range(qs_ref.shape[0]):
                lanes = slice(hd * SWA_BLOCK, (hd + 1) * SWA_BLOCK)
                qs_ref[j, kh * HEAD_DIM:(kh + 1) * HEAD_DIM, lanes] = rot[:, j * SWA_BLOCK:(j + 1) * SWA_BLOCK]
                qs_ref[j, (1 - kh) * HEAD_DIM:(2 - kh) * HEAD_DIM, lanes] = zero
        vs_ref[...] = vs.T.astype(BF16)
    else:
        qs_ref[...] = (proj[:, OFF_QS:OFF_QS + SWA_HEADS * HEAD_DIM] * SWA_SCALE).astype(BF16)
        vs_ref[...] = vs.astype(BF16)
    ks_ref[...] = ks.astype(BF16)


def _pre(x, mod, lw, tables, *, per_batch_mod, emit_ctx):
    B, S, D = x.shape
    rope = tables is not None
    ts = min(S, 512)
    n_pack = N_PACK_LAT if rope else N_PACK_CTX
    win = lw["win_lat"] if rope else lw["win_ctx"]
    wq = lw["wq_lat"] if rope else lw["wq_ctx"]
    tok = lambda w: pl.BlockSpec((None, ts, w), lambda b, s: (b, s, 0))
    full = lambda a: pl.BlockSpec(a.shape, lambda b, s: (0,) * a.ndim)
    in_specs = [
        tok(D),
        pl.BlockSpec((None, 6, D), (lambda b, s: (b, 0, 0)) if per_batch_mod else (lambda b, s: (0, 0, 0))),
        full(lw["norm1"]), full(win), full(lw["q_norm"]), full(lw["kv_norm"]), full(wq),
    ]
    args = [x, mod, lw["norm1"], win, lw["q_norm"], lw["kv_norm"], wq]
    if rope:
        row_tab = pl.BlockSpec((ts, LANES), lambda b, s: (s, 0))
        col_tab = pl.BlockSpec((MLA_ROPE, ts), lambda b, s: (0, s))
        col_tab_s = pl.BlockSpec((HEAD_DIM, ts), lambda b, s: (0, s))
        in_specs += [row_tab, row_tab, col_tab, col_tab, row_tab, row_tab, col_tab_s, col_tab_s, full(lw["wqs_t"])]
        args += list(tables) + [lw["wqs_t"]]
    feat = lambda w: pl.BlockSpec((None, w, ts), lambda b, s: (b, 0, s))
    if rope:
        q_tile = (2 * HEAD_DIM, SWA_HEADS * SWA_BLOCK)
        qs_spec = pl.BlockSpec((None, ts // SWA_BLOCK) + q_tile, lambda b, s: (b, s, 0, 0))
        qs_shape = jax.ShapeDtypeStruct((B, S // SWA_BLOCK) + q_tile, BF16)
        vs_spec, vs_shape = feat(LANES), jax.ShapeDtypeStruct((B, LANES, S), BF16)
    else:
        qs_spec, qs_shape = tok(SWA_HEADS * HEAD_DIM), jax.ShapeDtypeStruct((B, S, SWA_HEADS * HEAD_DIM), BF16)
        vs_spec, vs_shape = tok(LANES), jax.ShapeDtypeStruct((B, S, LANES), BF16)
    out_specs = [
        tok(SSM_CH),
        feat(MLA_HEADS * MLA_QK), tok(MLA_QK), feat(MLA_KV_RANK), qs_spec, tok(LANES), vs_spec,
    ]
    out_shape = [
        jax.ShapeDtypeStruct((B, S, SSM_CH), F32),
        jax.ShapeDtypeStruct((B, MLA_HEADS * MLA_QK, S), BF16),
        jax.ShapeDtypeStruct((B, S, MLA_QK), BF16),
        jax.ShapeDtypeStruct((B, MLA_KV_RANK, S), BF16),
        qs_shape,
        jax.ShapeDtypeStruct((B, S, LANES), BF16),
        vs_shape,
    ]
    if emit_ctx:
        out_specs += [tok(MLA_KV_RANK), tok(MLA_ROPE), tok(LANES), tok(LANES)]
        out_shape += [
            jax.ShapeDtypeStruct((B, S, MLA_KV_RANK), F32),
            jax.ShapeDtypeStruct((B, S, MLA_ROPE), F32),
            jax.ShapeDtypeStruct((B, S, LANES), F32),
            jax.ShapeDtypeStruct((B, S, LANES), F32),
        ]
    return pl.pallas_call(
        functools.partial(_pre_kernel, rope=rope, emit_ctx=emit_ctx),
        grid=(B, S // ts),
        in_specs=in_specs,
        out_specs=out_specs,
        out_shape=out_shape,
        compiler_params=_cparams(("arbitrary", "arbitrary")),
        name="pre_lat" if rope else "pre_ctx",
    )(*args)


def _ssm_kernel(uf_ref, ub_ref, wb_ref, a_ref, wc_ref, h0_ref, yf_ref, yb_ref, hfin_ref,
                hre_f, him_f, hre_b, him_b, st_s, ut_s, *, tc):
    i = pl.program_id(1)
    n = pl.num_programs(1)

    @pl.when(i == 0)
    def _():
        st_s[...] = h0_ref[...]

    cblk = SSM_CH // LANES
    u_refs = (uf_ref, ub_ref)
    h_refs = ((hre_f, him_f), (hre_b, him_b))
    for d in range(2):
        ub = u_refs[d][...]
        for b in range(SUBLANES):
            for j in range(cblk):
                ut_s[d, j, pl.ds(b, tc, stride=SUBLANES), :] = ub[b, :, j * LANES:(j + 1) * LANES]
        u = jnp.concatenate([ut_s[d, j] for j in range(cblk)], axis=-1).astype(BF16)
        bu = _dot(u, wb_ref[d])
        h_refs[d][0][...] = bu[:, :SSM_N]
        h_refs[d][1][...] = bu[:, SSM_N:]

    a_re = [jnp.broadcast_to(a_ref[d][:, :SSM_N], (SUBLANES, SSM_N)) for d in range(2)]
    a_im = [jnp.broadcast_to(a_ref[d][:, SSM_N:], (SUBLANES, SSM_N)) for d in range(2)]
    h_re = [st_s[d][:, :SSM_N] for d in range(2)]
    h_im = [st_s[d][:, SSM_N:] for d in range(2)]
    for t in range(tc):
        for d in range(2):
            tt = t if d == 0 else tc - 1 - t
            rows = slice(tt * SUBLANES, (tt + 1) * SUBLANES)
            hre_s, him_s = h_refs[d]
            n_re = a_re[d] * h_re[d] - a_im[d] * h_im[d] + hre_s[rows, :]
            n_im = a_re[d] * h_im[d] + a_im[d] * h_re[d] + him_s[rows, :]
            hre_s[rows, :] = n_re
            him_s[rows, :] = n_im
            h_re[d], h_im[d] = n_re, n_im

    y_refs = (yf_ref, yb_ref)
    for d in range(2):
        st_s[d] = jnp.concatenate([h_re[d], h_im[d]], axis=-1)
        hre_s, him_s = h_refs[d]
        y = _dot(hre_s[...].astype(BF16), wc_ref[d, :SSM_N, :]) + _dot(him_s[...].astype(BF16), wc_ref[d, SSM_N:, :])
        for j in range(cblk):
            ut_s[d, j] = y[:, j * LANES:(j + 1) * LANES]
        for b in range(SUBLANES):
            y_refs[d][b] = jnp.concatenate(
                [ut_s[d, j, pl.ds(b, tc, stride=SUBLANES), :] for j in range(cblk)], axis=-1)

    @pl.when(i == n - 1)
    def _():
        hfin_ref[...] = st_s[...]


def _ssm(u, lw, h0, B, S):
    nb = B // SUBLANES
    tc = 128
    nchunk = S // tc
    full3 = lambda a: pl.BlockSpec(a.shape, lambda b, i: (0, 0, 0))
    state_spec = pl.BlockSpec((2, None, SUBLANES, 2 * SSM_N), lambda b, i: (0, b, 0, 0))
    fwd = pl.BlockSpec((SUBLANES, tc, SSM_CH), lambda b, i: (b, i, 0))
    bwd = pl.BlockSpec((SUBLANES, tc, SSM_CH), lambda b, i: (b, nchunk - 1 - i, 0))
    yf, yb, hfin = pl.pallas_call(
        functools.partial(_ssm_kernel, tc=tc),
        grid=(nb, nchunk),
        in_specs=[fwd, bwd, full3(lw["ssm_wb"]), full3(lw["ssm_a"]), full3(lw["ssm_wc"]), state_spec],
        out_specs=[fwd, bwd, state_spec],
        out_shape=[
            jax.ShapeDtypeStruct((B, S, SSM_CH), F32),
            jax.ShapeDtypeStruct((B, S, SSM_CH), F32),
            jax.ShapeDtypeStruct((2, nb, SUBLANES, 2 * SSM_N), F32),
        ],
        scratch_shapes=[pltpu.VMEM((tc * SUBLANES, SSM_N), F32)] * 4 + [
            pltpu.VMEM((2, SUBLANES, 2 * SSM_N), F32),
            pltpu.VMEM((2, SSM_CH // LANES, tc * SUBLANES, LANES), F32),
        ],
        compiler_params=_cparams(("arbitrary", "arbitrary")),
        name="ssm_scan",
    )(u, u, lw["ssm_wb"], lw["ssm_a"], lw["ssm_wc"], h0)
    return (yf, yb), hfin


def _mla_kernel(qt_ref, k_ref, vt_ref, wv_ref, o_ref):
    tq = qt_ref.shape[1]
    q_of = lambda hd: qt_ref[hd * MLA_QK:(hd + 1) * MLA_QK, :]
    k = k_ref[...]
    vt = vt_ref[...]
    heads = []
    s_next = _dot(k, q_of(0))
    for hd in range(MLA_HEADS):
        s = s_next
        if hd + 1 < MLA_HEADS:
            s_next = _dot(k, q_of(hd + 1))
        s3 = s.reshape(s.shape[0] // SUBLANES, SUBLANES, tq)
        m = jnp.max(_tree(jnp.maximum, s3), axis=0, keepdims=True)
        p3 = jnp.exp2(s3 - jnp.broadcast_to(m, (SUBLANES, tq))[None])
        heads.append((_dot(vt, p3.reshape(s.shape).astype(BF16)), _tree(jnp.add, p3)))

    outs = []
    for hd, (pv, l8) in enumerate(heads):
        o_lat = (pv / jnp.sum(l8, axis=0, keepdims=True)).T.astype(BF16)
        outs.append(_dot(o_lat, wv_ref[hd]))
    o_ref[...] = jnp.concatenate(outs, axis=-1)


def _mla(qt, k, vt, wv):
    B, _, S = qt.shape
    tq = min(S, 512)
    kspec = lambda a: pl.BlockSpec((None,) + a.shape[1:], lambda b, s: (b, 0, 0))
    return pl.pallas_call(
        _mla_kernel,
        grid=(B, S // tq),
        in_specs=[pl.BlockSpec((None, MLA_HEADS * MLA_QK, tq), lambda b, s: (b, 0, s)), kspec(k), kspec(vt),
                  pl.BlockSpec(wv.shape, lambda b, s: (0, 0, 0))],
        out_specs=pl.BlockSpec((None, tq, MLA_HEADS * MLA_V), lambda b, s: (b, s, 0)),
        out_shape=jax.ShapeDtypeStruct((B, S, MLA_HEADS * MLA_V), F32),
        compiler_params=_cparams(("arbitrary", "arbitrary")),
        name="mla",
    )(qt, k, vt, wv)


def _swa_ctx_kernel(sink_ref, q_ref, k_ref, v_ref, o_ref):
    q = q_ref[...]
    k = k_ref[...]
    v = v_ref[...]
    outs = []
    for hd in range(SWA_HEADS):
        kh = hd // SWA_GROUP
        qh = q[:, hd * HEAD_DIM:(hd + 1) * HEAD_DIM]
        ksl = slice(kh * HEAD_DIM, (kh + 1) * HEAD_DIM)
        sink = sink_ref[hd]
        s = _dot_nt(qh, k[:, ksl])
        m = jnp.maximum(jnp.max(s, axis=-1, keepdims=True), sink)
        p = jnp.exp(s - m)
        den = jnp.sum(p, axis=-1, keepdims=True) + jnp.exp(sink - m)
        outs.append(_dot(p.astype(BF16), v[:, ksl]) / den)
    o_ref[...] = jnp.concatenate(outs, axis=-1)


def _swa_ctx(sink, q, k, v):
    B, S, W = q.shape
    seq = lambda a: pl.BlockSpec((None,) + a.shape[1:], lambda b: (b, 0, 0))
    return pl.pallas_call(
        _swa_ctx_kernel,
        grid=(B,),
        in_specs=[pl.BlockSpec(memory_space=pltpu.SMEM), seq(q), seq(k), seq(v)],
        out_specs=seq(q),
        out_shape=jax.ShapeDtypeStruct((B, S, W), F32),
        compiler_params=_cparams(("arbitrary",)),
        name="swa_ctx",
    )(sink, q, k, v)


def _swa_lat_kernel(sink_ref, q_ref, ka_ref, vat_ref, kb_ref, vbt_ref, o_ref):
    for j in range(q_ref.shape[0]):
        rows = slice(j * SWA_BLOCK, (j + 1) * SWA_BLOCK)
        o_ref[rows, :] = _swa_lat_block(pl.program_id(1) * q_ref.shape[0] + j, sink_ref, q_ref[j], ka_ref, vat_ref,
                                        kb_ref, vbt_ref)


def _swa_lat_block(n, sink_ref, q, ka_ref, vat_ref, kb_ref, vbt_ref):
    nblk = kb_ref.shape[0] // SWA_BLOCK
    n_lane = SWA_HEADS * SWA_BLOCK
    win = 3 * SWA_BLOCK
    start = pl.multiple_of(jnp.clip(n - 1, 0, nblk - 3) * SWA_BLOCK, SWA_BLOCK)
    s_a = _dot(ka_ref[...], q)
    s_b = _dot(kb_ref[pl.ds(start, win), :], q)
    kpos = start + lax.broadcasted_iota(jnp.int32, (win, n_lane), 0)
    qpos = n * SWA_BLOCK + (lax.broadcasted_iota(jnp.int32, (win, n_lane), 1) & (SWA_BLOCK - 1))
    s_b = jnp.where(jnp.abs(qpos - kpos) <= SWA_WINDOW, s_b, -jnp.inf)

    fold = lambda x: x.reshape(x.shape[0] // SUBLANES, SUBLANES, n_lane)
    sink = sink_ref[...]
    m8 = jnp.maximum(_tree(jnp.maximum, fold(s_a)), _tree(jnp.maximum, fold(s_b)))
    m = jnp.maximum(jnp.max(m8, axis=0, keepdims=True), sink)
    m8 = jnp.broadcast_to(m, (SUBLANES, n_lane))
    p_a = jnp.exp2(fold(s_a) - m8[None])
    p_b = jnp.exp2(fold(s_b) - m8[None])
    den = jnp.sum(_tree(jnp.add, p_a) + _tree(jnp.add, p_b), axis=0, keepdims=True) + jnp.exp2(sink - m)
    o_t = (_dot(vat_ref[...], p_a.reshape(s_a.shape).astype(BF16))
           + _dot(vbt_ref[:, pl.ds(start, win)], p_b.reshape(s_b.shape).astype(BF16))) / den
    o = o_t.T
    outs = []
    for hd in range(SWA_HEADS):
        kh = hd // SWA_GROUP
        outs.append(o[hd * SWA_BLOCK:(hd + 1) * SWA_BLOCK, kh * HEAD_DIM:(kh + 1) * HEAD_DIM])
    return jnp.concatenate(outs, axis=-1)


def _swa_lat(sink_row, q_tiles, ka, vat, kb, vbt):
    B, nblk = q_tiles.shape[:2]
    width = SWA_HEADS * HEAD_DIM
    per_step = 4
    kspec = lambda a: pl.BlockSpec((None,) + a.shape[1:], lambda b, s: (b, 0, 0))
    return pl.pallas_call(
        _swa_lat_kernel,
        grid=(B, nblk // per_step),
        in_specs=[
            pl.BlockSpec(sink_row.shape, lambda b, s: (0, 0)),
            pl.BlockSpec((None, per_step) + q_tiles.shape[2:], lambda b, s: (b, s, 0, 0)),
            kspec(ka), kspec(vat), kspec(kb), kspec(vbt),
        ],
        out_specs=pl.BlockSpec((None, per_step * SWA_BLOCK, width), lambda b, s: (b, s, 0)),
        out_shape=jax.ShapeDtypeStruct((B, nblk * SWA_BLOCK, width), F32),
        compiler_params=_cparams(("arbitrary", "arbitrary")),
        name="swa_lat",
    )(sink_row, q_tiles, ka, vat, kb, vbt)


def _route(logits):
    lane = lax.broadcasted_iota(jnp.int32, logits.shape, 1)
    big = jnp.int32(1 << 20)
    is_g = (lane >= MOE_EXPERTS) & (lane < MOE_EXPERTS + MOE_GROUPS)
    lg = jnp.where(is_g, logits, -jnp.inf)
    mg = jnp.max(lg, axis=-1, keepdims=True)
    g_idx = jnp.min(jnp.where(lg == mg, lane - MOE_EXPERTS, big), axis=-1, keepdims=True)
    pg_top = 1.0 / jnp.sum(jnp.exp(lg - mg), axis=-1, keepdims=True)

    is_e = (lane < MOE_EXPERTS) & ((lane // MOE_PER_GROUP) == g_idx)
    le = jnp.where(is_e, logits, -jnp.inf)
    m1 = jnp.max(le, axis=-1, keepdims=True)
    e1 = jnp.min(jnp.where(le == m1, lane, big), axis=-1, keepdims=True)
    z = jnp.sum(jnp.exp(le - m1), axis=-1, keepdims=True)
    le2 = jnp.where(lane == e1, -jnp.inf, le)
    m2 = jnp.max(le2, axis=-1, keepdims=True)
    e2 = jnp.min(jnp.where(le2 == m2, lane, big), axis=-1, keepdims=True)
    p1 = 1.0 / z
    p2 = jnp.exp(m2 - m1) / z
    tot = p1 + p2
    gates = pg_top * (jnp.where(lane == e1, p1 / tot, 0.0) + jnp.where(lane == e2, p2 / tot, 0.0))
    t = logits.shape[0]
    chose = jnp.where(lane == g_idx + COUNT_LANE, 1.0, 0.0)
    tri = jnp.where(lax.broadcasted_iota(jnp.int32, (t, t), 0) >= lax.broadcasted_iota(jnp.int32, (t, t), 1),
                    1.0, 0.0).astype(BF16)
    counts = _dot(tri, chose.astype(BF16))
    rank = jnp.sum(chose * (counts - 1.0), axis=-1, keepdims=True)
    return (gates + counts + jnp.where(lane == GROUP_LANE, g_idx.astype(F32), 0.0)
            + jnp.where(lane == RANK_LANE, rank, 0.0))


def _post_kernel(x_ref, u_ref, yf_ref, yb_ref, om_ref, os_ref, mod_ref, d_ref, wglu_ref, gs_ref, gm_ref,
                 gw_ref, wout_ref, n2_ref, wr_ref, br_ref, x1_ref, h2_ref, gate_ref):
    mod = mod_ref[...]
    y = d_ref[...] * u_ref[...] + yf_ref[...] + yb_ref[...]
    ga = _dot(_gelu_tanh(y).astype(BF16), wglu_ref[...])
    y_ssm = ga[:, :SSM_CH] * jax.nn.sigmoid(ga[:, SSM_CH:])
    wout = wout_ref
    n_mla = MLA_HEADS * MLA_V
    mixed = (_dot(_rms(y_ssm, gs_ref[...]).astype(BF16), wout[0:SSM_CH, :])
             + _dot(_rms(om_ref[...], gm_ref[...]).astype(BF16), wout[SSM_CH:SSM_CH + n_mla, :])
             + _dot(_rms(os_ref[...], gw_ref[...]).astype(BF16), wout[SSM_CH + n_mla:, :]))
    x1 = x_ref[...] + mod[2:3] * mixed
    x1_ref[...] = x1
    h2 = _rms(x1, n2_ref[...]) * (1.0 + mod[4:5]) + mod[3:4]
    h2_ref[...] = h2.astype(BF16)
    gate_ref[...] = _route(_dot3(h2, wr_ref[...]) + br_ref[...])


def _post(x, u_tm, y2, o_mla, o_swa, mod, lw, *, per_batch_mod):
    B, S, D = x.shape
    ts = min(S, POST_TILE)
    tok = lambda w: pl.BlockSpec((None, ts, w), lambda b, s: (b, s, 0))
    full = lambda a: pl.BlockSpec(a.shape, lambda b, s: (0,) * a.ndim)
    names = ["ssm_d", "w_glu", "gn_ssm", "gn_mla", "gn_swa", "w_out", "norm2", "w_router", "b_router"]
    in_specs = [
        tok(D),
        tok(SSM_CH),
        tok(SSM_CH), tok(SSM_CH),
        tok(MLA_HEADS * MLA_V), tok(SWA_HEADS * HEAD_DIM),
        pl.BlockSpec((None, 6, D), (lambda b, s: (b, 0, 0)) if per_batch_mod else (lambda b, s: (0, 0, 0))),
    ] + [full(lw[k]) for k in names]
    return pl.pallas_call(
        _post_kernel,
        grid=(B, S // ts),
        in_specs=in_specs,
        out_specs=[tok(D), tok(D), tok(LANES)],
        out_shape=[
            jax.ShapeDtypeStruct((B, S, D), F32),
            jax.ShapeDtypeStruct((B, S, D), BF16),
            jax.ShapeDtypeStruct((B, S, LANES), F32),
        ],
        compiler_params=_cparams(("arbitrary", "arbitrary")),
        name="post",
    )(x, u_tm, y2[0], y2[1], o_mla, o_swa, mod, *[lw[k] for k in names])


def _split3(x):
    hi = x.astype(BF16)
    r = x - hi.astype(F32)
    mid = r.astype(BF16)
    lo = (r - mid.astype(F32)).astype(BF16)
    return hi, mid, lo


def _moe_kernel(cnt_ref, h2_ref, gate_ref, crow_ref, x1_ref, mod_ref, wg_ref, wu_ref, wd_ref, fn_ref,
                o_ref, xs_s, gs_s, ys_s, acc_s, *, final, n_tiles):
    tm, d_model = acc_s.shape
    rb = MOE_ROW_BLOCK
    g = pl.program_id(2)
    e = pl.program_id(3)
    cnt = cnt_ref[(pl.program_id(0) * n_tiles + pl.program_id(1)) * MOE_GROUPS + g]
    n_blocks = (cnt + rb - 1) // rb
    wide_rb = MOE_SCATTER_BLOCK
    n_wide = cnt // wide_rb
    n_tail = n_blocks - n_wide * (wide_rb // rb)

    def for_rows(body, count, width, base=0):
        def step(blk, carry):
            body(pl.multiple_of(base + blk * width, rb), width)
            return carry
        lax.fori_loop(0, count, step, 0)

    def for_blocks(body):
        for_rows(body, n_wide, wide_rb)
        for_rows(body, n_tail, rb, n_wide * wide_rb)

    @pl.when((g == 0) & (e == 0))
    def _():
        acc_s[...] = jnp.zeros_like(acc_s)

    @pl.when(e == 0)
    def _():
        rank = crow_ref[...] - g * MOE_CODE_STRIDE
        h2 = h2_ref[...].reshape(tm, d_model)
        g_hi, g_mid, g_lo = _split3(gate_ref[...].reshape(tm, LANES))

        def gather(start, width):
            rows = pl.ds(start, width)
            row_id = lax.broadcasted_iota(jnp.int32, (width, tm), 0) + start
            onehot = jnp.where(row_id == rank, 1.0, 0.0).astype(BF16)
            xs_s[rows, :] = _dot(onehot, h2).astype(BF16)
            gs_s[rows, :] = _dot(onehot, g_hi) + _dot(onehot, g_mid) + _dot(onehot, g_lo)
        for_blocks(gather)

        @pl.when(n_blocks % (MOE_SCATTER_BLOCK // rb) != 0)
        def _():
            ys_s[pl.ds(pl.multiple_of(n_blocks * rb, rb), rb), :] = jnp.zeros((rb, d_model), F32)

    n_exp = MOE_PER_GROUP // MOE_SPLIT

    def experts(start, width, first):
        rows = pl.ds(start, width)
        x = xs_s[rows, :]
        gs = gs_s[rows, :]
        lane = lax.broadcasted_iota(jnp.int32, gs.shape, 1)
        first_expert = g * MOE_PER_GROUP + e * n_exp
        gate = jnp.concatenate(
            [jnp.broadcast_to(jnp.sum(jnp.where(lane == first_expert + k, gs, 0.0), axis=-1, keepdims=True),
                              (width, MOE_HIDDEN)) for k in range(n_exp)], axis=-1)
        wide = lambda w_ref: jnp.concatenate([_dot(x, w_ref[k]) for k in range(n_exp)], axis=-1)
        hid = _silu(wide(wg_ref)) * wide(wu_ref) * gate
        y = _dot(hid.astype(BF16), wd_ref[...])
        if first:
            ys_s[rows, :] = y
        else:
            ys_s[rows, :] += y

    @pl.when(e == 0)
    def _():
        for_blocks(lambda start, width: experts(start, width, True))

    @pl.when(e != 0)
    def _():
        for_blocks(lambda start, width: experts(start, width, False))

    @pl.when(e == MOE_SPLIT - 1)
    def _():
        code_t = jnp.broadcast_to(crow_ref[...].astype(F32), (LANES, tm)).T
        rank = code_t.astype(jnp.int32) - g * MOE_CODE_STRIDE
        lane = lax.broadcasted_iota(jnp.int32, (tm, LANES), 1)

        def scatter(start, width):
            onehot = jnp.concatenate(
                [jnp.where(lane + (start + part * LANES) == rank, 1.0, 0.0) for part in range(width // LANES)],
                axis=-1).astype(BF16)
            acc_s[...] += _dot(onehot, ys_s[pl.ds(start, width), :].astype(BF16))
        for_rows(scatter, (cnt + wide_rb - 1) // wide_rb, wide_rb)

    @pl.when((g == MOE_GROUPS - 1) & (e == MOE_SPLIT - 1))
    def _():
        xo = x1_ref[...].reshape(tm, d_model) + mod_ref[5:6, :] * acc_s[...]
        xo = _rms(xo, fn_ref[...]) if final else xo
        o_ref[...] = xo.reshape(o_ref.shape)


def _moe(h2, gates, x1, mod, lw, final_norm, *, per_batch_mod, final, rt):
    B, S, D = x1.shape
    tm = MOE_TILE
    if S >= tm:
        nb, nt = B, S // tm
        tok = lambda w: pl.BlockSpec((None, tm, w), lambda b, s, g, e, c: (b, s, 0))
    else:
        assert not per_batch_mod
        nb, nt = B * S // tm, 1
        tok = lambda w: pl.BlockSpec((tm // S, S, w), lambda b, s, g, e, c: (b, 0, 0))
    sub = tm // rt
    info = gates[..., GROUP_LANE:COUNT_LANE + MOE_GROUPS].astype(jnp.int32)
    grp = info[..., 0].reshape(nb, nt, sub, rt)
    sub_counts = info[:, rt - 1::rt, 2:].reshape(nb, nt, sub, MOE_GROUPS)
    offsets = jnp.cumsum(sub_counts, axis=2) - sub_counts
    onehot = grp[..., None] == jnp.arange(MOE_GROUPS, dtype=jnp.int32)
    rank = info[..., 1].reshape(nb, nt, sub, rt) + jnp.sum(jnp.where(onehot, offsets[:, :, :, None, :], 0), axis=-1)
    counts = jnp.sum(sub_counts, axis=2).reshape(-1)
    code = grp * MOE_CODE_STRIDE + rank
    code_row = code.reshape(nb, nt, 1, tm)

    n_exp = MOE_PER_GROUP // MOE_SPLIT
    step_map = lambda b, s, g, e, c: (g * MOE_SPLIT + e, 0, 0)
    mod_map = (lambda b, s, g, e, c: (b, 0, 0)) if per_batch_mod else (lambda b, s, g, e, c: (0, 0, 0))
    grid_spec = pltpu.PrefetchScalarGridSpec(
        num_scalar_prefetch=1,
        grid=(nb, nt, MOE_GROUPS, MOE_SPLIT),
        in_specs=[
            tok(D), tok(LANES),
            pl.BlockSpec((None, None, 1, tm), lambda b, s, g, e, c: (b, s, 0, 0)),
            tok(D),
            pl.BlockSpec((None, 6, D), mod_map),
            pl.BlockSpec((n_exp, D, MOE_HIDDEN), step_map), pl.BlockSpec((n_exp, D, MOE_HIDDEN), step_map),
            pl.BlockSpec((None, n_exp * MOE_HIDDEN, D), step_map),
            pl.BlockSpec(final_norm.shape, lambda b, s, g, e, c: (0, 0)),
        ],
        out_specs=tok(D),
        scratch_shapes=[
            pltpu.VMEM((tm, D), BF16),
            pltpu.VMEM((tm, LANES), F32),
            pltpu.VMEM((tm, D), F32),
            pltpu.VMEM((tm, D), F32),
        ],
    )
    return pl.pallas_call(
        functools.partial(_moe_kernel, final=final, n_tiles=nt),
        grid_spec=grid_spec,
        out_shape=jax.ShapeDtypeStruct((B, S, D), F32),
        compiler_params=_cparams(("arbitrary", "arbitrary", "arbitrary", "arbitrary")),
        name="moe",
    )(counts, h2, gates, code_row, x1, mod, lw["moe_wg"], lw["moe_wu"], lw["moe_wd"], final_norm)


def _swap_halves(w, n_heads, dim):
    k = w.shape[0]
    w = w.reshape(k, n_heads, 2, dim // 2)
    return w[:, :, ::-1, :].reshape(k, n_heads * dim)


def _pad_cols(w, width):
    return jnp.pad(w, ((0, 0), (0, width - w.shape[1])))


def _rope_tables(n_tokens, rot_dim, reps):
    t = jnp.arange(n_tokens)
    row = (t // GRID_W).astype(F32)
    col = (t % GRID_W).astype(F32)
    n_freq = rot_dim // 4
    inv_freq = ROPE_BASE ** (-jnp.arange(n_freq, dtype=F32) / n_freq)
    ang = jnp.concatenate([row[:, None] * inv_freq, col[:, None] * inv_freq], axis=-1)
    cos, sin = jnp.cos(ang), jnp.sin(ang)
    c = jnp.tile(jnp.concatenate([cos, cos], axis=-1), (1, reps))
    s = jnp.tile(jnp.concatenate([-sin, sin], axis=-1), (1, reps))
    return _pad_cols(c, LANES), _pad_cols(s, LANES)


def _layer_weights(l, P, q_abs, ab_re, ab_im, bb_re, bb_im):
    w_in = P["w_in"][l]
    seg = {}
    o = 0
    for name, width in (("u", SSM_CH), ("ql", MLA_Q_RANK), ("kvl", MLA_KV_RANK), ("kr", MLA_ROPE),
                        ("qs", SWA_HEADS * HEAD_DIM), ("ks", SWA_KV_HEADS * HEAD_DIM),
                        ("vs", SWA_KV_HEADS * HEAD_DIM)):
        seg[name] = w_in[:, o:o + width]
        o += width
    shared_cols = [seg["u"], seg["ql"], seg["kvl"], seg["ks"], seg["vs"], _pad_cols(seg["kr"], LANES)]
    ctx_cols = shared_cols + [seg["qs"]]
    lat_cols = shared_cols + [
        _swap_halves(seg["ks"], SWA_KV_HEADS, HEAD_DIM),
        _pad_cols(_swap_halves(seg["kr"], 1, MLA_ROPE), LANES),
    ]
    wqs_t = jnp.concatenate([seg["qs"], _swap_halves(seg["qs"], SWA_HEADS, HEAD_DIM)], axis=1).T
    w_qb = P["w_mla_qb"][l]
    w_rope = w_qb[:, :, MLA_NOPE:]
    w_rope_sw = w_rope.reshape(MLA_Q_RANK, MLA_HEADS, 2, MLA_ROPE // 2)[:, :, ::-1, :].reshape(w_rope.shape)
    flat_t = lambda w: w.reshape(MLA_Q_RANK, MLA_HEADS * MLA_ROPE).T
    wq_abs = jnp.transpose(q_abs[l], (0, 2, 1)).reshape(Q_ABS, MLA_Q_RANK)
    wq_ctx = jnp.concatenate([wq_abs, flat_t(w_rope)], axis=0)
    wq_lat = jnp.concatenate([wq_ctx, flat_t(w_rope_sw)], axis=0)

    eye = jnp.eye(SSM_GROUPS, dtype=F32)

    def block_diag_b(bb):
        bb = bb.reshape(2, SSM_GROUPS, SSM_STATE, SSM_GROUP)
        return jnp.einsum("dgpc,gh->dgchp", bb, eye).reshape(2, SSM_CH, SSM_N)

    def block_diag_c(cc):
        return jnp.einsum("dgcp,gh->dgphc", cc, eye).reshape(2, SSM_N, SSM_CH)

    sl = slice(2 * l, 2 * l + 2)
    w_router = jnp.concatenate([P["moe_w_expert"][l], P["moe_w_group"][l]], axis=1)
    b_router = jnp.concatenate([P["moe_b_expert"][l], P["moe_b_group"][l]])
    row = lambda v: v.reshape(1, -1)
    return dict(
        norm1=row(P["norm1"][l]), norm2=row(P["norm2"][l]),
        win_ctx=jnp.concatenate(ctx_cols, axis=1).astype(BF16),
        win_lat=jnp.concatenate(lat_cols, axis=1).astype(BF16), wqs_t=wqs_t.astype(BF16),
        q_norm=row(P["mla_q_norm"][l]), kv_norm=row(P["mla_kv_norm"][l]),
        wq_ctx=wq_ctx.astype(BF16), wq_lat=wq_lat.astype(BF16),
        wv=jnp.transpose(P["w_mla_kvb"][l][:, :, MLA_NOPE:], (1, 0, 2)).astype(BF16),
        ssm_wb=jnp.concatenate([block_diag_b(bb_re[sl]), block_diag_b(bb_im[sl])], axis=2).astype(BF16),
        ssm_a=jnp.concatenate([ab_re[sl], ab_im[sl]], axis=1).reshape(2, 1, 2 * SSM_N),
        ssm_wc=jnp.concatenate([block_diag_c(P["ssm_c_re"][l]), -block_diag_c(P["ssm_c_im"][l])],
                               axis=1).astype(BF16),
        ssm_d=row(P["ssm_d"][l]), w_glu=P["w_ssm_glu"][l].astype(BF16),
        gn_ssm=row(P["gn_ssm"][l]), gn_mla=row(P["gn_mla"][l]), gn_swa=row(P["gn_swa"][l]),
        w_out=P["w_out"][l].astype(BF16),
        w_router=_pad_cols(w_router, LANES), b_router=_pad_cols(row(b_router), LANES),
        moe_wg=P["moe_w_gate"][l].astype(BF16).reshape(MOE_EXPERTS, D_MODEL, MOE_HIDDEN),
        moe_wu=P["moe_w_up"][l].astype(BF16).reshape(MOE_EXPERTS, D_MODEL, MOE_HIDDEN),
        moe_wd=P["moe_w_down"][l].astype(BF16).reshape(
            MOE_GROUPS * MOE_SPLIT, MOE_PER_GROUP // MOE_SPLIT * MOE_HIDDEN, D_MODEL),
        sink=P["swa_sink"][l],
        sink_row=jnp.repeat(P["swa_sink"][l] * math.log2(math.e), SWA_BLOCK).reshape(1, SWA_HEADS * SWA_BLOCK),
    )


def _layer(x, mod, lw, final_norm, *, tables, ctx, per_batch_mod, final):
    B, S, _ = x.shape
    context_pass = ctx is None
    pre = _pre(x, mod, lw, tables, per_batch_mod=per_batch_mod, emit_ctx=context_pass)
    u_tm, qt, kcat, ckvt, qs, ks, vs = pre[:7]
    nb = B // SUBLANES
    if context_pass:
        h0 = jnp.zeros((2, nb, SUBLANES, 2 * SSM_N), F32)
        y2, hfin = _ssm(u_tm, lw, h0, B, S)
        o_mla = _mla(qt, kcat, ckvt, lw["wv"])
        o_swa = _swa_ctx(lw["sink"], qs, ks, vs)
        state = hfin.reshape(2, B, 2, SSM_GROUPS, SSM_STATE).transpose(1, 0, 2, 3, 4)
        new_ctx = (pre[7], pre[8], pre[9].reshape(B, S, SWA_KV_HEADS, HEAD_DIM),
                   pre[10].reshape(B, S, SWA_KV_HEADS, HEAD_DIM), state)
    else:
        kcat_c, ckvt_c, ks_c, vs_c, h0 = ctx
        y2, _ = _ssm(u_tm, lw, h0, B, S)
        o_mla = _mla(qt, jnp.concatenate([kcat_c, kcat], axis=1), jnp.concatenate([ckvt_c, ckvt], axis=2), lw["wv"])
        o_swa = _swa_lat(lw["sink_row"], qs, ks_c, vs_c, ks, vs)
        new_ctx = None
    x1, h2, gates = _post(x, u_tm, y2, o_mla, o_swa, mod, lw, per_batch_mod=per_batch_mod)
    xo = _moe(h2, gates, x1, mod, lw, final_norm, per_batch_mod=per_batch_mod, final=final, rt=min(S, POST_TILE))
    return xo, new_ctx


def kernel(x_prompt, x_sample, c, cache_mla_ckv, cache_mla_krope, cache_swa_k, cache_swa_v, state_ssm, c_ctx, w_ada, b_ada, norm1, norm2, w_in, ssm_a_re, ssm_a_im, ssm_log_dt, ssm_b_re, ssm_b_im, ssm_c_re, ssm_c_im, ssm_d, w_ssm_glu, mla_q_norm, w_mla_qb, mla_kv_norm, w_mla_kvb, swa_sink, gn_ssm, gn_mla, gn_swa, w_out, moe_w_group, moe_b_group, moe_w_expert, moe_b_expert, moe_w_gate, moe_w_up, moe_w_down, final_norm):
    P = dict(w_ada=w_ada, b_ada=b_ada, norm1=norm1, norm2=norm2, w_in=w_in,
             ssm_c_re=ssm_c_re, ssm_c_im=ssm_c_im, ssm_d=ssm_d, w_ssm_glu=w_ssm_glu,
             mla_q_norm=mla_q_norm, w_mla_qb=w_mla_qb, mla_kv_norm=mla_kv_norm, w_mla_kvb=w_mla_kvb,
             swa_sink=swa_sink, gn_ssm=gn_ssm, gn_mla=gn_mla, gn_swa=gn_swa, w_out=w_out,
             moe_w_group=moe_w_group, moe_b_group=moe_b_group, moe_w_expert=moe_w_expert,
             moe_b_expert=moe_b_expert, moe_w_gate=moe_w_gate, moe_w_up=moe_w_up, moe_w_down=moe_w_down)
    n_dec = c.shape[0]
    n_cond = 2 * SUBLANES
    conds = jnp.zeros((n_cond, D_MODEL), F32).at[:n_dec].set(c).at[n_dec].set(c_ctx)
    mods = _modulation(conds, w_ada, b_ada).reshape(DEPTH, n_cond, 6, D_MODEL)

    ab_re, ab_im, bb_re, bb_im = _ssm_discretise(ssm_a_re, ssm_a_im, ssm_log_dt, ssm_b_re, ssm_b_im)
    q_abs = _absorb_q(jnp.transpose(w_mla_qb[..., :MLA_NOPE], (0, 2, 1, 3)),
                      jnp.transpose(w_mla_kvb[..., :MLA_NOPE], (0, 2, 1, 3)))
    lws = [_layer_weights(l, P, q_abs, ab_re.reshape(2 * DEPTH, SSM_N), ab_im.reshape(2 * DEPTH, SSM_N),
                          bb_re, bb_im) for l in range(DEPTH)]
    fnorm = final_norm.reshape(1, D_MODEL)

    xp = x_prompt
    ctx_states = []
    for l in range(DEPTH):
        xp, new = _layer(xp, mods[l, n_dec:n_dec + 1], lws[l], fnorm, tables=None, ctx=None,
                         per_batch_mod=False, final=l == DEPTH - 1)
        ctx_states.append(new)
    outs_ctx = tuple(jnp.stack([s[k] for s in ctx_states], axis=1) for k in range(5))

    n_lat = x_sample.shape[1]
    cm, sm = _rope_tables(n_lat, MLA_ROPE, 1)
    cs, ss = _rope_tables(n_lat, HEAD_DIM, LANES // HEAD_DIM)
    tables = (cm, sm, cm[:, :MLA_ROPE].T, sm[:, :MLA_ROPE].T, cs, ss, cs[:, :HEAD_DIM].T, ss[:, :HEAD_DIM].T)
    xs = x_sample
    past = cache_mla_ckv.shape[2]
    for l in range(DEPTH):
        kcat_c = jnp.concatenate(
            [cache_mla_ckv[:, l], cache_mla_krope[:, l],
             jnp.zeros((n_dec, past, LANES - MLA_ROPE), F32)], axis=-1).astype(BF16)
        ks_c = cache_swa_k[:, l].reshape(n_dec, past, LANES).astype(BF16)
        vs_c = jnp.transpose(cache_swa_v[:, l].reshape(n_dec, past, LANES), (0, 2, 1)).astype(BF16)
        h0 = state_ssm[:, l].transpose(1, 0, 2, 3, 4).reshape(2, n_dec // SUBLANES, SUBLANES, 2 * SSM_N)
        ckvt_c = jnp.transpose(cache_mla_ckv[:, l], (0, 2, 1)).astype(BF16)
        xs, _ = _layer(xs, mods[l, :n_dec], lws[l], fnorm, tables=tables, ctx=(kcat_c, ckvt_c, ks_c, vs_c, h0),
                       per_batch_mod=True, final=l == DEPTH - 1)
    return (xp, xs) + outs_ctx
```

```python
import functools
import math

import jax
import jax.numpy as jnp
from jax import lax
from jax.experimental import pallas as pl
from jax.experimental.pallas import tpu as pltpu

F32 = jnp.float32
BF16 = jnp.bfloat16

D_MODEL = 1024
DEPTH = 4
GRID_W = 64
HEAD_DIM = 64
SSM_CH = 256
SSM_GROUP = 16
SSM_GROUPS = SSM_CH // SSM_GROUP
SSM_STATE = 64
SSM_N = SSM_GROUPS * SSM_STATE
MLA_HEADS = 6
MLA_Q_RANK = 256
MLA_KV_RANK = 128
MLA_NOPE = 64
MLA_ROPE = 32
MLA_V = 64
SWA_HEADS = 6
SWA_KV_HEADS = 2
SWA_GROUP = SWA_HEADS // SWA_KV_HEADS
SWA_WINDOW = 128
SWA_BLOCK = 128
MOE_GROUPS = 4
MOE_PER_GROUP = 8
MOE_EXPERTS = MOE_GROUPS * MOE_PER_GROUP
MOE_HIDDEN = 256
GROUP_LANE = MOE_EXPERTS
RANK_LANE = GROUP_LANE + 1
COUNT_LANE = GROUP_LANE + 2
POST_TILE = 512
MOE_SPLIT = 2
MOE_TILE = 1024
MOE_ROW_BLOCK = 128
MOE_SCATTER_BLOCK = 256
MOE_CODE_STRIDE = 1 << 16
ROPE_BASE = 10000.0
EPS = 1e-6
MLA_SCALE = 1.0 / math.sqrt(MLA_NOPE + MLA_ROPE)
MLA_SCALE_LOG2 = MLA_SCALE * math.log2(math.e)
SWA_SCALE = 1.0 / math.sqrt(HEAD_DIM)
SWA_SCALE_LOG2 = SWA_SCALE * math.log2(math.e)

LANES = 128
SUBLANES = 8
VMEM_LIMIT = 52 * 1024 * 1024

OFF_U = 0
OFF_QL = OFF_U + SSM_CH
OFF_KVL = OFF_QL + MLA_Q_RANK
OFF_KS = OFF_KVL + MLA_KV_RANK
OFF_VS = OFF_KS + SWA_KV_HEADS * HEAD_DIM
OFF_KR = OFF_VS + SWA_KV_HEADS * HEAD_DIM
OFF_TAIL = OFF_KR + LANES
OFF_QS = OFF_TAIL
N_PACK_CTX = OFF_QS + SWA_HEADS * HEAD_DIM
OFF_KS_SW = OFF_TAIL
OFF_KR_SW = OFF_KS_SW + SWA_KV_HEADS * HEAD_DIM
N_PACK_LAT = OFF_KR_SW + LANES
Q_ABS = MLA_HEADS * MLA_KV_RANK
MLA_QK = 2 * LANES


def _cparams(sem):
    return pltpu.CompilerParams(dimension_semantics=sem, vmem_limit_bytes=VMEM_LIMIT)


def _dot(a, b):
    return jnp.dot(a, b, preferred_element_type=F32)


def _dot_nt(a, b):
    return lax.dot_general(a, b, (((1,), (1,)), ((), ())), preferred_element_type=F32)


def _split(x):
    hi = x.astype(BF16)
    lo = (x - hi.astype(F32)).astype(BF16)
    return hi, lo


def _dot3(a, b):
    ah, al = _split(a)
    bh, bl = _split(b)
    return _dot(ah, bh) + _dot(al, bh) + _dot(ah, bl)


def _tree(op, x3):
    parts = [x3[i] for i in range(x3.shape[0])]
    while len(parts) > 1:
        pairs = [op(parts[i], parts[i + 1]) for i in range(0, len(parts) - 1, 2)]
        parts = pairs + parts[len(parts) - len(parts) % 2:]
    return parts[0]


def _rms(x, g):
    return x * lax.rsqrt(jnp.mean(x * x, axis=-1, keepdims=True) + EPS) * g


def _silu(x):
    return x * jax.nn.sigmoid(x)


def _gelu_tanh(x):
    return 0.5 * x * (1.0 + jnp.tanh(math.sqrt(2.0 / math.pi) * (x + 0.044715 * (x * x * x))))


def _mod_kernel(c_ref, w_ref, b_ref, o_ref):
    o_ref[...] = _dot3(_silu(c_ref[...]), w_ref[...]) + b_ref[...]


def _modulation(conds, w_ada, b_ada):
    n = conds.shape[0]
    tn = 1536
    return pl.pallas_call(
        _mod_kernel,
        grid=(DEPTH, 6 * D_MODEL // tn),
        in_specs=[
            pl.BlockSpec((n, D_MODEL), lambda l, j: (0, 0)),
            pl.BlockSpec((None, D_MODEL, tn), lambda l, j: (l, 0, j)),
            pl.BlockSpec((None, 1, tn), lambda l, j: (l, 0, j)),
        ],
        out_specs=pl.BlockSpec((None, n, tn), lambda l, j: (l, 0, j)),
        out_shape=jax.ShapeDtypeStruct((DEPTH, n, 6 * D_MODEL), F32),
        compiler_params=_cparams(("arbitrary", "arbitrary")),
        name="modulation",
    )(conds, w_ada, b_ada.reshape(DEPTH, 1, 6 * D_MODEL))


def _ssm_disc_kernel(are_ref, aim_ref, ldt_ref, bre_ref, bim_ref, abre_ref, abim_ref, bbre_ref, bbim_ref):
    lam_re = are_ref[...]
    lam_im = aim_ref[...]
    dt = jnp.exp(ldt_ref[...])
    z_re = lam_re * dt
    z_im = lam_im * dt
    mag = jnp.exp(z_re)
    ab_re = mag * jnp.cos(z_im)
    ab_im = mag * jnp.sin(z_im)
    den = lam_re * lam_re + lam_im * lam_im
    f_re = ((ab_re - 1.0) * lam_re + ab_im * lam_im) / den
    f_im = (ab_im * lam_re - (ab_re - 1.0) * lam_im) / den
    b_re = bre_ref[...]
    b_im = bim_ref[...]
    abre_ref[...] = ab_re
    abim_ref[...] = ab_im
    bbre_ref[...] = f_re * b_re - f_im * b_im
    bbim_ref[...] = f_re * b_im + f_im * b_re


def _ssm_discretise(a_re, a_im, log_dt, b_re, b_im):
    n = DEPTH * 2
    col = lambda v: v.reshape(n, SSM_N, 1)
    ldt = jnp.broadcast_to(log_dt[..., None], (DEPTH, 2, SSM_GROUPS, SSM_STATE))
    cspec = pl.BlockSpec((None, SSM_N, 1), lambda i: (i, 0, 0))
    bspec = pl.BlockSpec((None, SSM_N, SSM_GROUP), lambda i: (i, 0, 0))
    return pl.pallas_call(
        _ssm_disc_kernel,
        grid=(n,),
        in_specs=[cspec, cspec, cspec, bspec, bspec],
        out_specs=[cspec, cspec, bspec, bspec],
        out_shape=[jax.ShapeDtypeStruct((n, SSM_N, 1), F32)] * 2
        + [jax.ShapeDtypeStruct((n, SSM_N, SSM_GROUP), F32)] * 2,
        compiler_params=_cparams(("arbitrary",)),
        name="ssm_discretise",
    )(col(a_re), col(a_im), col(ldt), b_re.reshape(n, SSM_N, SSM_GROUP), b_im.reshape(n, SSM_N, SSM_GROUP))


def _absorb_kernel(wq_ref, wk_ref, o_ref):
    a = wq_ref[...]
    b = wk_ref[...]
    ah, al = _split(a)
    bh, bl = _split(b)
    o_ref[...] = _dot_nt(ah, bh) + _dot_nt(al, bh) + _dot_nt(ah, bl)


def _absorb_q(wq_nope, wk_nope):
    return pl.pallas_call(
        _absorb_kernel,
        grid=(DEPTH, MLA_HEADS),
        in_specs=[
            pl.BlockSpec((None, None, MLA_Q_RANK, MLA_NOPE), lambda l, h: (l, h, 0, 0)),
            pl.BlockSpec((None, None, MLA_KV_RANK, MLA_NOPE), lambda l, h: (l, h, 0, 0)),
        ],
        out_specs=pl.BlockSpec((None, None, MLA_Q_RANK, MLA_KV_RANK), lambda l, h: (l, h, 0, 0)),
        out_shape=jax.ShapeDtypeStruct((DEPTH, MLA_HEADS, MLA_Q_RANK, MLA_KV_RANK), F32),
        compiler_params=_cparams(("arbitrary", "arbitrary")),
        name="mla_absorb",
    )(wq_nope, wk_nope)


def _pre_kernel(*refs, rope, emit_ctx):
    it = iter(refs)
    x_ref, mod_ref, n1_ref, win_ref, qn_ref, kvn_ref, wq_ref = (next(it) for _ in range(7))
    if rope:
        cm_ref, sm_ref, cmt_ref, smt_ref, cs_ref, ss_ref, cst_ref, sst_ref, wqs_ref = (next(it) for _ in range(9))
    u_ref, qt_ref, kcat_ref, ckvt_ref, qs_ref, ks_ref, vs_ref = (next(it) for _ in range(7))
    if emit_ctx:
        ckv_o, kr_o, ks_o, vs_o = (next(it) for _ in range(4))

    x = x_ref[...]
    mod = mod_ref[...]
    h = _rms(x, n1_ref[...]) * (1.0 + mod[1:2]) + mod[0:1]
    hb = h.astype(BF16)
    proj = _dot(hb, win_ref[...])

    u_ref[...] = proj[:, OFF_U:OFF_U + SSM_CH]

    qln = _rms(proj[:, OFF_QL:OFF_QL + MLA_Q_RANK], qn_ref[...]).astype(BF16)
    qall = _dot_nt(wq_ref[...], qln)
    n_rope = MLA_HEADS * MLA_ROPE
    zero_rows = jnp.zeros((MLA_QK - MLA_KV_RANK - MLA_ROPE, qall.shape[1]), BF16)
    for i in range(MLA_HEADS):
        qa = qall[i * MLA_KV_RANK:(i + 1) * MLA_KV_RANK]
        qr = qall[Q_ABS + i * MLA_ROPE:Q_ABS + (i + 1) * MLA_ROPE]
        if rope:
            qr_sw = qall[Q_ABS + n_rope + i * MLA_ROPE:Q_ABS + n_rope + (i + 1) * MLA_ROPE]
            qr = qr * cmt_ref[...] + qr_sw * smt_ref[...]
        base = i * MLA_QK
        qt_ref[base:base + MLA_KV_RANK, :] = (qa * MLA_SCALE_LOG2).astype(BF16)
        qt_ref[base + MLA_KV_RANK:base + MLA_KV_RANK + MLA_ROPE, :] = (qr * MLA_SCALE_LOG2).astype(BF16)
        qt_ref[base + MLA_KV_RANK + MLA_ROPE:base + MLA_QK, :] = zero_rows

    ckv = _rms(proj[:, OFF_KVL:OFF_KVL + MLA_KV_RANK], kvn_ref[...])
    kr = proj[:, OFF_KR:OFF_KR + LANES]
    if emit_ctx:
        ckv_o[...] = ckv
        kr_o[...] = kr[:, :MLA_ROPE]
    if rope:
        kr = kr * cm_ref[...] + proj[:, OFF_KR_SW:OFF_KR_SW + LANES] * sm_ref[...]
    kcat_ref[...] = jnp.concatenate([ckv, kr], axis=-1).astype(BF16)
    ckvt_ref[...] = ckv.T.astype(BF16)

    ks = proj[:, OFF_KS:OFF_KS + SWA_KV_HEADS * HEAD_DIM]
    vs = proj[:, OFF_VS:OFF_VS + SWA_KV_HEADS * HEAD_DIM]
    if emit_ctx:
        ks_o[...] = ks
        vs_o[...] = vs
    if rope:
        ks = ks * cs_ref[...] + proj[:, OFF_KS_SW:OFF_KS_SW + SWA_KV_HEADS * HEAD_DIM] * ss_ref[...]
        n_q = SWA_HEADS * HEAD_DIM
        qst = _dot_nt(wqs_ref[...], hb)
        zero = jnp.zeros((HEAD_DIM, SWA_BLOCK), BF16)
        for hd in range(SWA_HEADS):
            rows = slice(hd * HEAD_DIM, (hd + 1) * HEAD_DIM)
            rot = qst[rows] * cst_ref[...] + qst[n_q + hd * HEAD_DIM:n_q + (hd + 1) * HEAD_DIM] * sst_ref[...]
            rot = (rot * SWA_SCALE_LOG2).astype(BF16)
            kh = hd // SWA_GROUP
            for j in range(qs_ref.shape[0]):
                lanes = slice(hd * SWA_BLOCK, (hd + 1) * SWA_BLOCK)
                qs_ref[j, kh * HEAD_DIM:(kh + 1) * HEAD_DIM, lanes] = rot[:, j * SWA_BLOCK:(j + 1) * SWA_BLOCK]
                qs_ref[j, (1 - kh) * HEAD_DIM:(2 - kh) * HEAD_DIM, lanes] = zero
        vs_ref[...] = vs.T.astype(BF16)
    else:
        qs_ref[...] = (proj[:, OFF_QS:OFF_QS + SWA_HEADS * HEAD_DIM] * SWA_SCALE).astype(BF16)
        vs_ref[...] = vs.astype(BF16)
    ks_ref[...] = ks.astype(BF16)


def _pre(x, mod, lw, tables, *, per_batch_mod, emit_ctx):
    B, S, D = x.shape
    rope = tables is not None
    ts = min(S, 512)
    n_pack = N_PACK_LAT if rope else N_PACK_CTX
    win = lw["win_lat"] if rope else lw["win_ctx"]
    wq = lw["wq_lat"] if rope else lw["wq_ctx"]
    tok = lambda w: pl.BlockSpec((None, ts, w), lambda b, s: (b, s, 0))
    full = lambda a: pl.BlockSpec(a.shape, lambda b, s: (0,) * a.ndim)
    in_specs = [
        tok(D),
        pl.BlockSpec((None, 6, D), (lambda b, s: (b, 0, 0)) if per_batch_mod else (lambda b, s: (0, 0, 0))),
        full(lw["norm1"]), full(win), full(lw["q_norm"]), full(lw["kv_norm"]), full(wq),
    ]
    args = [x, mod, lw["norm1"], win, lw["q_norm"], lw["kv_norm"], wq]
    if rope:
        row_tab = pl.BlockSpec((ts, LANES), lambda b, s: (s, 0))
        col_tab = pl.BlockSpec((MLA_ROPE, ts), lambda b, s: (0, s))
        col_tab_s = pl.BlockSpec((HEAD_DIM, ts), lambda b, s: (0, s))
        in_specs += [row_tab, row_tab, col_tab, col_tab, row_tab, row_tab, col_tab_s, col_tab_s, full(lw["wqs_t"])]
        args += list(tables) + [lw["wqs_t"]]
    feat = lambda w: pl.BlockSpec((None, w, ts), lambda b, s: (b, 0, s))
    if rope:
        q_tile = (2 * HEAD_DIM, SWA_HEADS * SWA_BLOCK)
        qs_spec = pl.BlockSpec((None, ts // SWA_BLOCK) + q_tile, lambda b, s: (b, s, 0, 0))
        qs_shape = jax.ShapeDtypeStruct((B, S // SWA_BLOCK) + q_tile, BF16)
        vs_spec, vs_shape = feat(LANES), jax.ShapeDtypeStruct((B, LANES, S), BF16)
    else:
        qs_spec, qs_shape = tok(SWA_HEADS * HEAD_DIM), jax.ShapeDtypeStruct((B, S, SWA_HEADS * HEAD_DIM), BF16)
        vs_spec, vs_shape = tok(LANES), jax.ShapeDtypeStruct((B, S, LANES), BF16)
    out_specs = [
        tok(SSM_CH),
        feat(MLA_HEADS * MLA_QK), tok(MLA_QK), feat(MLA_KV_RANK), qs_spec, tok(LANES), vs_spec,
    ]
    out_shape = [
        jax.ShapeDtypeStruct((B, S, SSM_CH), F32),
        jax.ShapeDtypeStruct((B, MLA_HEADS * MLA_QK, S), BF16),
        jax.ShapeDtypeStruct((B, S, MLA_QK), BF16),
        jax.ShapeDtypeStruct((B, MLA_KV_RANK, S), BF16),
        qs_shape,
        jax.ShapeDtypeStruct((B, S, LANES), BF16),
        vs_shape,
    ]
    if emit_ctx:
        out_specs += [tok(MLA_KV_RANK), tok(MLA_ROPE), tok(LANES), tok(LANES)]
        out_shape += [
            jax.ShapeDtypeStruct((B, S, MLA_KV_RANK), F32),
            jax.ShapeDtypeStruct((B, S, MLA_ROPE), F32),
            jax.ShapeDtypeStruct((B, S, LANES), F32),
            jax.ShapeDtypeStruct((B, S, LANES), F32),
        ]
    return pl.pallas_call(
        functools.partial(_pre_kernel, rope=rope, emit_ctx=emit_ctx),
        grid=(B, S // ts),
        in_specs=in_specs,
        out_specs=out_specs,
        out_shape=out_shape,
        compiler_params=_cparams(("arbitrary", "arbitrary")),
        name="pre_lat" if rope else "pre_ctx",
    )(*args)


def _ssm_kernel(uf_ref, ub_ref, wb_ref, a_ref, wc_ref, h0_ref, yf_ref, yb_ref, hfin_ref,
                hre_f, him_f, hre_b, him_b, st_s, ut_s, *, tc):
    i = pl.program_id(1)
    n = pl.num_programs(1)

    @pl.when(i == 0)
    def _():
        st_s[...] = h0_ref[...]

    cblk = SSM_CH // LANES
    u_refs = (uf_ref, ub_ref)
    h_refs = ((hre_f, him_f), (hre_b, him_b))
    for d in range(2):
        ub = u_refs[d][...]
        for b in range(SUBLANES):
            for j in range(cblk):
                ut_s[d, j, pl.ds(b, tc, stride=SUBLANES), :] = ub[b, :, j * LANES:(j + 1) * LANES]
        u = jnp.concatenate([ut_s[d, j] for j in range(cblk)], axis=-1).astype(BF16)
        bu = _dot(u, wb_ref[d])
        h_refs[d][0][...] = bu[:, :SSM_N]
        h_refs[d][1][...] = bu[:, SSM_N:]

    a_re = [jnp.broadcast_to(a_ref[d][:, :SSM_N], (SUBLANES, SSM_N)) for d in range(2)]
    a_im = [jnp.broadcast_to(a_ref[d][:, SSM_N:], (SUBLANES, SSM_N)) for d in range(2)]
    h_re = [st_s[d][:, :SSM_N] for d in range(2)]
    h_im = [st_s[d][:, SSM_N:] for d in range(2)]
    for t in range(tc):
        for d in range(2):
            tt = t if d == 0 else tc - 1 - t
            rows = slice(tt * SUBLANES, (tt + 1) * SUBLANES)
            hre_s, him_s = h_refs[d]
            n_re = a_re[d] * h_re[d] - a_im[d] * h_im[d] + hre_s[rows, :]
            n_im = a_re[d] * h_im[d] + a_im[d] * h_re[d] + him_s[rows, :]
            hre_s[rows, :] = n_re
            him_s[rows, :] = n_im
            h_re[d], h_im[d] = n_re, n_im

    y_refs = (yf_ref, yb_ref)
    for d in range(2):
        st_s[d] = jnp.concatenate([h_re[d], h_im[d]], axis=-1)
        hre_s, him_s = h_refs[d]
        y = _dot(hre_s[...].astype(BF16), wc_ref[d, :SSM_N, :]) + _dot(him_s[...].astype(BF16), wc_ref[d, SSM_N:, :])
        for j in range(cblk):
            ut_s[d, j] = y[:, j * LANES:(j + 1) * LANES]
        for b in range(SUBLANES):
            y_refs[d][b] = jnp.concatenate(
                [ut_s[d, j, pl.ds(b, tc, stride=SUBLANES), :] for j in range(cblk)], axis=-1)

    @pl.when(i == n - 1)
    def _():
        hfin_ref[...] = st_s[...]


def _ssm(u, lw, h0, B, S):
    nb = B // SUBLANES
    tc = 128
    nchunk = S // tc
    full3 = lambda a: pl.BlockSpec(a.shape, lambda b, i: (0, 0, 0))
    state_spec = pl.BlockSpec((2, None, SUBLANES, 2 * SSM_N), lambda b, i: (0, b, 0, 0))
    fwd = pl.BlockSpec((SUBLANES, tc, SSM_CH), lambda b, i: (b, i, 0))
    bwd = pl.BlockSpec((SUBLANES, tc, SSM_CH), lambda b, i: (b, nchunk - 1 - i, 0))
    yf, yb, hfin = pl.pallas_call(
        functools.partial(_ssm_kernel, tc=tc),
        grid=(nb, nchunk),
        in_specs=[fwd, bwd, full3(lw["ssm_wb"]), full3(lw["ssm_a"]), full3(lw["ssm_wc"]), state_spec],
        out_specs=[fwd, bwd, state_spec],
        out_shape=[
            jax.ShapeDtypeStruct((B, S, SSM_CH), F32),
            jax.ShapeDtypeStruct((B, S, SSM_CH), F32),
            jax.ShapeDtypeStruct((2, nb, SUBLANES, 2 * SSM_N), F32),
        ],
        scratch_shapes=[pltpu.VMEM((tc * SUBLANES, SSM_N), F32)] * 4 + [
            pltpu.VMEM((2, SUBLANES, 2 * SSM_N), F32),
            pltpu.VMEM((2, SSM_CH // LANES, tc * SUBLANES, LANES), F32),
        ],
        compiler_params=_cparams(("arbitrary", "arbitrary")),
        name="ssm_scan",
    )(u, u, lw["ssm_wb"], lw["ssm_a"], lw["ssm_wc"], h0)
    return (yf, yb), hfin


def _mla_kernel(qt_ref, k_ref, vt_ref, wv_ref, o_ref):
    tq = qt_ref.shape[1]
    q_of = lambda hd: qt_ref[hd * MLA_QK:(hd + 1) * MLA_QK, :]
    k = k_ref[...]
    vt = vt_ref[...]
    heads = []
    s_next = _dot(k, q_of(0))
    for hd in range(MLA_HEADS):
        s = s_next
        if hd + 1 < MLA_HEADS:
            s_next = _dot(k, q_of(hd + 1))
        s3 = s.reshape(s.shape[0] // SUBLANES, SUBLANES, tq)
        m = jnp.max(_tree(jnp.maximum, s3), axis=0, keepdims=True)
        p3 = jnp.exp2(s3 - jnp.broadcast_to(m, (SUBLANES, tq))[None])
        heads.append((_dot(vt, p3.reshape(s.shape).astype(BF16)), _tree(jnp.add, p3)))

    outs = []
    for hd, (pv, l8) in enumerate(heads):
        o_lat = (pv / jnp.sum(l8, axis=0, keepdims=True)).T.astype(BF16)
        outs.append(_dot(o_lat, wv_ref[hd]))
    o_ref[...] = jnp.concatenate(outs, axis=-1)


def _mla(qt, k, vt, wv):
    B, _, S = qt.shape
    tq = min(S, 512)
    kspec = lambda a: pl.BlockSpec((None,) + a.shape[1:], lambda b, s: (b, 0, 0))
    return pl.pallas_call(
        _mla_kernel,
        grid=(B, S // tq),
        in_specs=[pl.BlockSpec((None, MLA_HEADS * MLA_QK, tq), lambda b, s: (b, 0, s)), kspec(k), kspec(vt),
                  pl.BlockSpec(wv.shape, lambda b, s: (0, 0, 0))],
        out_specs=pl.BlockSpec((None, tq, MLA_HEADS * MLA_V), lambda b, s: (b, s, 0)),
        out_shape=jax.ShapeDtypeStruct((B, S, MLA_HEADS * MLA_V), F32),
        compiler_params=_cparams(("arbitrary", "arbitrary")),
        name="mla",
    )(qt, k, vt, wv)


def _swa_ctx_kernel(sink_ref, q_ref, k_ref, v_ref, o_ref):
    q = q_ref[...]
    k = k_ref[...]
    v = v_ref[...]
    outs = []
    for hd in range(SWA_HEADS):
        kh = hd // SWA_GROUP
        qh = q[:, hd * HEAD_DIM:(hd + 1) * HEAD_DIM]
        ksl = slice(kh * HEAD_DIM, (kh + 1) * HEAD_DIM)
        sink = sink_ref[hd]
        s = _dot_nt(qh, k[:, ksl])
        m = jnp.maximum(jnp.max(s, axis=-1, keepdims=True), sink)
        p = jnp.exp(s - m)
        den = jnp.sum(p, axis=-1, keepdims=True) + jnp.exp(sink - m)
        outs.append(_dot(p.astype(BF16), v[:, ksl]) / den)
    o_ref[...] = jnp.concatenate(outs, axis=-1)


def _swa_ctx(sink, q, k, v):
    B, S, W = q.shape
    seq = lambda a: pl.BlockSpec((None,) + a.shape[1:], lambda b: (b, 0, 0))
    return pl.pallas_call(
        _swa_ctx_kernel,
        grid=(B,),
        in_specs=[pl.BlockSpec(memory_space=pltpu.SMEM), seq(q), seq(k), seq(v)],
        out_specs=seq(q),
        out_shape=jax.ShapeDtypeStruct((B, S, W), F32),
        compiler_params=_cparams(("arbitrary",)),
        name="swa_ctx",
    )(sink, q, k, v)


def _swa_lat_kernel(sink_ref, q_ref, ka_ref, vat_ref, kb_ref, vbt_ref, o_ref):
    for j in range(q_ref.shape[0]):
        rows = slice(j * SWA_BLOCK, (j + 1) * SWA_BLOCK)
        o_ref[rows, :] = _swa_lat_block(pl.program_id(1) * q_ref.shape[0] + j, sink_ref, q_ref[j], ka_ref, vat_ref,
                                        kb_ref, vbt_ref)


def _swa_lat_block(n, sink_ref, q, ka_ref, vat_ref, kb_ref, vbt_ref):
    nblk = kb_ref.shape[0] // SWA_BLOCK
    n_lane = SWA_HEADS * SWA_BLOCK
    win = 3 * SWA_BLOCK
    start = pl.multiple_of(jnp.clip(n - 1, 0, nblk - 3) * SWA_BLOCK, SWA_BLOCK)
    s_a = _dot(ka_ref[...], q)
    s_b = _dot(kb_ref[pl.ds(start, win), :], q)
    kpos = start + lax.broadcasted_iota(jnp.int32, (win, n_lane), 0)
    qpos = n * SWA_BLOCK + (lax.broadcasted_iota(jnp.int32, (win, n_lane), 1) & (SWA_BLOCK - 1))
    s_b = jnp.where(jnp.abs(qpos - kpos) <= SWA_WINDOW, s_b, -jnp.inf)

    fold = lambda x: x.reshape(x.shape[0] // SUBLANES, SUBLANES, n_lane)
    sink = sink_ref[...]
    m8 = jnp.maximum(_tree(jnp.maximum, fold(s_a)), _tree(jnp.maximum, fold(s_b)))
    m = jnp.maximum(jnp.max(m8, axis=0, keepdims=True), sink)
    m8 = jnp.broadcast_to(m, (SUBLANES, n_lane))
    p_a = jnp.exp2(fold(s_a) - m8[None])
    p_b = jnp.exp2(fold(s_b) - m8[None])
    den = jnp.sum(_tree(jnp.add, p_a) + _tree(jnp.add, p_b), axis=0, keepdims=True) + jnp.exp2(sink - m)
    o_t = (_dot(vat_ref[...], p_a.reshape(s_a.shape).astype(BF16))
           + _dot(vbt_ref[:, pl.ds(start, win)], p_b.reshape(s_b.shape).astype(BF16))) / den
    o = o_t.T
    outs = []
    for hd in range(SWA_HEADS):
        kh = hd // SWA_GROUP
        outs.append(o[hd * SWA_BLOCK:(hd + 1) * SWA_BLOCK, kh * HEAD_DIM:(kh + 1) * HEAD_DIM])
    return jnp.concatenate(outs, axis=-1)


def _swa_lat(sink_row, q_tiles, ka, vat, kb, vbt):
    B, nblk = q_tiles.shape[:2]
    width = SWA_HEADS * HEAD_DIM
    per_step = 8
    kspec = lambda a: pl.BlockSpec((None,) + a.shape[1:], lambda b, s: (b, 0, 0))
    return pl.pallas_call(
        _swa_lat_kernel,
        grid=(B, nblk // per_step),
        in_specs=[
            pl.BlockSpec(sink_row.shape, lambda b, s: (0, 0)),
            pl.BlockSpec((None, per_step) + q_tiles.shape[2:], lambda b, s: (b, s, 0, 0)),
            kspec(ka), kspec(vat), kspec(kb), kspec(vbt),
        ],
        out_specs=pl.BlockSpec((None, per_step * SWA_BLOCK, width), lambda b, s: (b, s, 0)),
        out_shape=jax.ShapeDtypeStruct((B, nblk * SWA_BLOCK, width), F32),
        compiler_params=_cparams(("arbitrary", "arbitrary")),
        name="swa_lat",
    )(sink_row, q_tiles, ka, vat, kb, vbt)


def _route(logits):
    lane = lax.broadcasted_iota(jnp.int32, logits.shape, 1)
    big = jnp.int32(1 << 20)
    is_g = (lane >= MOE_EXPERTS) & (lane < MOE_EXPERTS + MOE_GROUPS)
    lg = jnp.where(is_g, logits, -jnp.inf)
    mg = jnp.max(lg, axis=-1, keepdims=True)
    g_idx = jnp.min(jnp.where(lg == mg, lane - MOE_EXPERTS, big), axis=-1, keepdims=True)
    pg_top = 1.0 / jnp.sum(jnp.exp(lg - mg), axis=-1, keepdims=True)

    is_e = (lane < MOE_EXPERTS) & ((lane // MOE_PER_GROUP) == g_idx)
    le = jnp.where(is_e, logits, -jnp.inf)
    m1 = jnp.max(le, axis=-1, keepdims=True)
    e1 = jnp.min(jnp.where(le == m1, lane, big), axis=-1, keepdims=True)
    z = jnp.sum(jnp.exp(le - m1), axis=-1, keepdims=True)
    le2 = jnp.where(lane == e1, -jnp.inf, le)
    m2 = jnp.max(le2, axis=-1, keepdims=True)
    e2 = jnp.min(jnp.where(le2 == m2, lane, big), axis=-1, keepdims=True)
    p1 = 1.0 / z
    p2 = jnp.exp(m2 - m1) / z
    tot = p1 + p2
    gates = pg_top * (jnp.where(lane == e1, p1 / tot, 0.0) + jnp.where(lane == e2, p2 / tot, 0.0))
    t = logits.shape[0]
    chose = jnp.where(lane == g_idx + COUNT_LANE, 1.0, 0.0)
    tri = jnp.where(lax.broadcasted_iota(jnp.int32, (t, t), 0) >= lax.broadcasted_iota(jnp.int32, (t, t), 1),
                    1.0, 0.0).astype(BF16)
    counts = _dot(tri, chose.astype(BF16))
    rank = jnp.sum(chose * (counts - 1.0), axis=-1, keepdims=True)
    return (gates + counts + jnp.where(lane == GROUP_LANE, g_idx.astype(F32), 0.0)
            + jnp.where(lane == RANK_LANE, rank, 0.0))


def _post_kernel(x_ref, u_ref, yf_ref, yb_ref, om_ref, os_ref, mod_ref, d_ref, wglu_ref, gs_ref, gm_ref,
                 gw_ref, wout_ref, n2_ref, wr_ref, br_ref, x1_ref, h2_ref, gate_ref):
    mod = mod_ref[...]
    y = d_ref[...] * u_ref[...] + yf_ref[...] + yb_ref[...]
    ga = _dot(_gelu_tanh(y).astype(BF16), wglu_ref[...])
    y_ssm = ga[:, :SSM_CH] * jax.nn.sigmoid(ga[:, SSM_CH:])
    wout = wout_ref
    n_mla = MLA_HEADS * MLA_V
    mixed = (_dot(_rms(y_ssm, gs_ref[...]).astype(BF16), wout[0:SSM_CH, :])
             + _dot(_rms(om_ref[...], gm_ref[...]).astype(BF16), wout[SSM_CH:SSM_CH + n_mla, :])
             + _dot(_rms(os_ref[...], gw_ref[...]).astype(BF16), wout[SSM_CH + n_mla:, :]))
    x1 = x_ref[...] + mod[2:3] * mixed
    x1_ref[...] = x1
    h2 = _rms(x1, n2_ref[...]) * (1.0 + mod[4:5]) + mod[3:4]
    h2_ref[...] = h2.astype(BF16)
    gate_ref[...] = _route(_dot3(h2, wr_ref[...]) + br_ref[...])


def _post(x, u_tm, y2, o_mla, o_swa, mod, lw, *, per_batch_mod):
    B, S, D = x.shape
    ts = min(S, POST_TILE)
    tok = lambda w: pl.BlockSpec((None, ts, w), lambda b, s: (b, s, 0))
    full = lambda a: pl.BlockSpec(a.shape, lambda b, s: (0,) * a.ndim)
    names = ["ssm_d", "w_glu", "gn_ssm", "gn_mla", "gn_swa", "w_out", "norm2", "w_router", "b_router"]
    in_specs = [
        tok(D),
        tok(SSM_CH),
        tok(SSM_CH), tok(SSM_CH),
        tok(MLA_HEADS * MLA_V), tok(SWA_HEADS * HEAD_DIM),
        pl.BlockSpec((None, 6, D), (lambda b, s: (b, 0, 0)) if per_batch_mod else (lambda b, s: (0, 0, 0))),
    ] + [full(lw[k]) for k in names]
    return pl.pallas_call(
        _post_kernel,
        grid=(B, S // ts),
        in_specs=in_specs,
        out_specs=[tok(D), tok(D), tok(LANES)],
        out_shape=[
            jax.ShapeDtypeStruct((B, S, D), F32),
            jax.ShapeDtypeStruct((B, S, D), BF16),
            jax.ShapeDtypeStruct((B, S, LANES), F32),
        ],
        compiler_params=_cparams(("arbitrary", "arbitrary")),
        name="post",
    )(x, u_tm, y2[0], y2[1], o_mla, o_swa, mod, *[lw[k] for k in names])


def _split3(x):
    hi = x.astype(BF16)
    r = x - hi.astype(F32)
    mid = r.astype(BF16)
    lo = (r - mid.astype(F32)).astype(BF16)
    return hi, mid, lo


def _moe_kernel(cnt_ref, h2_ref, gate_ref, crow_ref, x1_ref, mod_ref, wg_ref, wu_ref, wd_ref, fn_ref,
                o_ref, xs_s, gs_s, ys_s, acc_s, *, final, n_tiles):
    tm, d_model = acc_s.shape
    rb = MOE_ROW_BLOCK
    g = pl.program_id(2)
    e = pl.program_id(3)
    cnt = cnt_ref[(pl.program_id(0) * n_tiles + pl.program_id(1)) * MOE_GROUPS + g]
    n_blocks = (cnt + rb - 1) // rb
    wide_rb = MOE_SCATTER_BLOCK
    n_wide = cnt // wide_rb
    n_tail = n_blocks - n_wide * (wide_rb // rb)

    def for_rows(body, count, width, base=0):
        def step(blk, carry):
            body(pl.multiple_of(base + blk * width, rb), width)
            return carry
        lax.fori_loop(0, count, step, 0)

    def for_blocks(body):
        for_rows(body, n_wide, wide_rb)
        for_rows(body, n_tail, rb, n_wide * wide_rb)

    @pl.when((g == 0) & (e == 0))
    def _():
        acc_s[...] = jnp.zeros_like(acc_s)

    @pl.when(e == 0)
    def _():
        rank = crow_ref[...] - g * MOE_CODE_STRIDE
        h2 = h2_ref[...].reshape(tm, d_model)
        g_hi, g_mid, g_lo = _split3(gate_ref[...].reshape(tm, LANES))

        def gather(start, width):
            rows = pl.ds(start, width)
            row_id = lax.broadcasted_iota(jnp.int32, (width, tm), 0) + start
            onehot = jnp.where(row_id == rank, 1.0, 0.0).astype(BF16)
            xs_s[rows, :] = _dot(onehot, h2).astype(BF16)
            gs_s[rows, :] = _dot(onehot, g_hi) + _dot(onehot, g_mid) + _dot(onehot, g_lo)
        for_blocks(gather)

        @pl.when(n_blocks % (MOE_SCATTER_BLOCK // rb) != 0)
        def _():
            ys_s[pl.ds(pl.multiple_of(n_blocks * rb, rb), rb), :] = jnp.zeros((rb, d_model), F32)

    n_exp = MOE_PER_GROUP // MOE_SPLIT

    def experts(start, width, first):
        rows = pl.ds(start, width)
        x = xs_s[rows, :]
        gs = gs_s[rows, :]
        lane = lax.broadcasted_iota(jnp.int32, gs.shape, 1)
        first_expert = g * MOE_PER_GROUP + e * n_exp
        gate = jnp.concatenate(
            [jnp.broadcast_to(jnp.sum(jnp.where(lane == first_expert + k, gs, 0.0), axis=-1, keepdims=True),
                              (width, MOE_HIDDEN)) for k in range(n_exp)], axis=-1)
        wide = lambda w_ref: jnp.concatenate([_dot(x, w_ref[k]) for k in range(n_exp)], axis=-1)
        hid = _silu(wide(wg_ref)) * wide(wu_ref) * gate
        y = _dot(hid.astype(BF16), wd_ref[...])
        if first:
            ys_s[rows, :] = y
        else:
            ys_s[rows, :] += y

    @pl.when(e == 0)
    def _():
        for_blocks(lambda start, width: experts(start, width, True))

    @pl.when(e != 0)
    def _():
        for_blocks(lambda start, width: experts(start, width, False))

    @pl.when(e == MOE_SPLIT - 1)
    def _():
        code_t = jnp.broadcast_to(crow_ref[...].astype(F32), (LANES, tm)).T
        rank = code_t.astype(jnp.int32) - g * MOE_CODE_STRIDE
        lane = lax.broadcasted_iota(jnp.int32, (tm, LANES), 1)

        def scatter(start, width):
            onehot = jnp.concatenate(
                [jnp.where(lane + (start + part * LANES) == rank, 1.0, 0.0) for part in range(width // LANES)],
                axis=-1).astype(BF16)
            acc_s[...] += _dot(onehot, ys_s[pl.ds(start, width), :].astype(BF16))
        for_rows(scatter, (cnt + wide_rb - 1) // wide_rb, wide_rb)

    @pl.when((g == MOE_GROUPS - 1) & (e == MOE_SPLIT - 1))
    def _():
        xo = x1_ref[...].reshape(tm, d_model) + mod_ref[5:6, :] * acc_s[...]
        xo = _rms(xo, fn_ref[...]) if final else xo
        o_ref[...] = xo.reshape(o_ref.shape)


def _moe(h2, gates, x1, mod, lw, final_norm, *, per_batch_mod, final, rt):
    B, S, D = x1.shape
    tm = MOE_TILE
    if S >= tm:
        nb, nt = B, S // tm
        tok = lambda w: pl.BlockSpec((None, tm, w), lambda b, s, g, e, c: (b, s, 0))
    else:
        assert not per_batch_mod
        nb, nt = B * S // tm, 1
        tok = lambda w: pl.BlockSpec((tm // S, S, w), lambda b, s, g, e, c: (b, 0, 0))
    sub = tm // rt
    info = gates[..., GROUP_LANE:COUNT_LANE + MOE_GROUPS].astype(jnp.int32)
    grp = info[..., 0].reshape(nb, nt, sub, rt)
    sub_counts = info[:, rt - 1::rt, 2:].reshape(nb, nt, sub, MOE_GROUPS)
    offsets = jnp.cumsum(sub_counts, axis=2) - sub_counts
    onehot = grp[..., None] == jnp.arange(MOE_GROUPS, dtype=jnp.int32)
    rank = info[..., 1].reshape(nb, nt, sub, rt) + jnp.sum(jnp.where(onehot, offsets[:, :, :, None, :], 0), axis=-1)
    counts = jnp.sum(sub_counts, axis=2).reshape(-1)
    code = grp * MOE_CODE_STRIDE + rank
    code_row = code.reshape(nb, nt, 1, tm)

    n_exp = MOE_PER_GROUP // MOE_SPLIT
    step_map = lambda b, s, g, e, c: (g * MOE_SPLIT + e, 0, 0)
    mod_map = (lambda b, s, g, e, c: (b, 0, 0)) if per_batch_mod else (lambda b, s, g, e, c: (0, 0, 0))
    grid_spec = pltpu.PrefetchScalarGridSpec(
        num_scalar_prefetch=1,
        grid=(nb, nt, MOE_GROUPS, MOE_SPLIT),
        in_specs=[
            tok(D), tok(LANES),
            pl.BlockSpec((None, None, 1, tm), lambda b, s, g, e, c: (b, s, 0, 0)),
            tok(D),
            pl.BlockSpec((None, 6, D), mod_map),
            pl.BlockSpec((n_exp, D, MOE_HIDDEN), step_map), pl.BlockSpec((n_exp, D, MOE_HIDDEN), step_map),
            pl.BlockSpec((None, n_exp * MOE_HIDDEN, D), step_map),
            pl.BlockSpec(final_norm.shape, lambda b, s, g, e, c: (0, 0)),
        ],
        out_specs=tok(D),
        scratch_shapes=[
            pltpu.VMEM((tm, D), BF16),
            pltpu.VMEM((tm, LANES), F32),
            pltpu.VMEM((tm, D), F32),
            pltpu.VMEM((tm, D), F32),
        ],
    )
    return pl.pallas_call(
        functools.partial(_moe_kernel, final=final, n_tiles=nt),
        grid_spec=grid_spec,
        out_shape=jax.ShapeDtypeStruct((B, S, D), F32),
        compiler_params=_cparams(("arbitrary", "arbitrary", "arbitrary", "arbitrary")),
        name="moe",
    )(counts, h2, gates, code_row, x1, mod, lw["moe_wg"], lw["moe_wu"], lw["moe_wd"], final_norm)


def _swap_halves(w, n_heads, dim):
    k = w.shape[0]
    w = w.reshape(k, n_heads, 2, dim // 2)
    return w[:, :, ::-1, :].reshape(k, n_heads * dim)


def _pad_cols(w, width):
    return jnp.pad(w, ((0, 0), (0, width - w.shape[1])))


def _rope_tables(n_tokens, rot_dim, reps):
    t = jnp.arange(n_tokens)
    row = (t // GRID_W).astype(F32)
    col = (t % GRID_W).astype(F32)
    n_freq = rot_dim // 4
    inv_freq = ROPE_BASE ** (-jnp.arange(n_freq, dtype=F32) / n_freq)
    ang = jnp.concatenate([row[:, None] * inv_freq, col[:, None] * inv_freq], axis=-1)
    cos, sin = jnp.cos(ang), jnp.sin(ang)
    c = jnp.tile(jnp.concatenate([cos, cos], axis=-1), (1, reps))
    s = jnp.tile(jnp.concatenate([-sin, sin], axis=-1), (1, reps))
    return _pad_cols(c, LANES), _pad_cols(s, LANES)


def _layer_weights(l, P, q_abs, ab_re, ab_im, bb_re, bb_im):
    w_in = P["w_in"][l]
    seg = {}
    o = 0
    for name, width in (("u", SSM_CH), ("ql", MLA_Q_RANK), ("kvl", MLA_KV_RANK), ("kr", MLA_ROPE),
                        ("qs", SWA_HEADS * HEAD_DIM), ("ks", SWA_KV_HEADS * HEAD_DIM),
                        ("vs", SWA_KV_HEADS * HEAD_DIM)):
        seg[name] = w_in[:, o:o + width]
        o += width
    shared_cols = [seg["u"], seg["ql"], seg["kvl"], seg["ks"], seg["vs"], _pad_cols(seg["kr"], LANES)]
    ctx_cols = shared_cols + [seg["qs"]]
    lat_cols = shared_cols + [
        _swap_halves(seg["ks"], SWA_KV_HEADS, HEAD_DIM),
        _pad_cols(_swap_halves(seg["kr"], 1, MLA_ROPE), LANES),
    ]
    wqs_t = jnp.concatenate([seg["qs"], _swap_halves(seg["qs"], SWA_HEADS, HEAD_DIM)], axis=1).T
    w_qb = P["w_mla_qb"][l]
    w_rope = w_qb[:, :, MLA_NOPE:]
    w_rope_sw = w_rope.reshape(MLA_Q_RANK, MLA_HEADS, 2, MLA_ROPE // 2)[:, :, ::-1, :].reshape(w_rope.shape)
    flat_t = lambda w: w.reshape(MLA_Q_RANK, MLA_HEADS * MLA_ROPE).T
    wq_abs = jnp.transpose(q_abs[l], (0, 2, 1)).reshape(Q_ABS, MLA_Q_RANK)
    wq_ctx = jnp.concatenate([wq_abs, flat_t(w_rope)], axis=0)
    wq_lat = jnp.concatenate([wq_ctx, flat_t(w_rope_sw)], axis=0)

    eye = jnp.eye(SSM_GROUPS, dtype=F32)

    def block_diag_b(bb):
        bb = bb.reshape(2, SSM_GROUPS, SSM_STATE, SSM_GROUP)
        return jnp.einsum("dgpc,gh->dgchp", bb, eye).reshape(2, SSM_CH, SSM_N)

    def block_diag_c(cc):
        return jnp.einsum("dgcp,gh->dgphc", cc, eye).reshape(2, SSM_N, SSM_CH)

    sl = slice(2 * l, 2 * l + 2)
    w_router = jnp.concatenate([P["moe_w_expert"][l], P["moe_w_group"][l]], axis=1)
    b_router = jnp.concatenate([P["moe_b_expert"][l], P["moe_b_group"][l]])
    row = lambda v: v.reshape(1, -1)
    return dict(
        norm1=row(P["norm1"][l]), norm2=row(P["norm2"][l]),
        win_ctx=jnp.concatenate(ctx_cols, axis=1).astype(BF16),
        win_lat=jnp.concatenate(lat_cols, axis=1).astype(BF16), wqs_t=wqs_t.astype(BF16),
        q_norm=row(P["mla_q_norm"][l]), kv_norm=row(P["mla_kv_norm"][l]),
        wq_ctx=wq_ctx.astype(BF16), wq_lat=wq_lat.astype(BF16),
        wv=jnp.transpose(P["w_mla_kvb"][l][:, :, MLA_NOPE:], (1, 0, 2)).astype(BF16),
        ssm_wb=jnp.concatenate([block_diag_b(bb_re[sl]), block_diag_b(bb_im[sl])], axis=2).astype(BF16),
        ssm_a=jnp.concatenate([ab_re[sl], ab_im[sl]], axis=1).reshape(2, 1, 2 * SSM_N),
        ssm_wc=jnp.concatenate([block_diag_c(P["ssm_c_re"][l]), -block_diag_c(P["ssm_c_im"][l])],
                               axis=1).astype(BF16),
        ssm_d=row(P["ssm_d"][l]), w_glu=P["w_ssm_glu"][l].astype(BF16),
        gn_ssm=row(P["gn_ssm"][l]), gn_mla=row(P["gn_mla"][l]), gn_swa=row(P["gn_swa"][l]),
        w_out=P["w_out"][l].astype(BF16),
        w_router=_pad_cols(w_router, LANES), b_router=_pad_cols(row(b_router), LANES),
        moe_wg=P["moe_w_gate"][l].astype(BF16).reshape(MOE_EXPERTS, D_MODEL, MOE_HIDDEN),
        moe_wu=P["moe_w_up"][l].astype(BF16).reshape(MOE_EXPERTS, D_MODEL, MOE_HIDDEN),
        moe_wd=P["moe_w_down"][l].astype(BF16).reshape(
            MOE_GROUPS * MOE_SPLIT, MOE_PER_GROUP // MOE_SPLIT * MOE_HIDDEN, D_MODEL),
        sink=P["swa_sink"][l],
        sink_row=jnp.repeat(P["swa_sink"][l] * math.log2(math.e), SWA_BLOCK).reshape(1, SWA_HEADS * SWA_BLOCK),
    )


def _layer(x, mod, lw, final_norm, *, tables, ctx, per_batch_mod, final):
    B, S, _ = x.shape
    context_pass = ctx is None
    pre = _pre(x, mod, lw, tables, per_batch_mod=per_batch_mod, emit_ctx=context_pass)
    u_tm, qt, kcat, ckvt, qs, ks, vs = pre[:7]
    nb = B // SUBLANES
    if context_pass:
        h0 = jnp.zeros((2, nb, SUBLANES, 2 * SSM_N), F32)
        y2, hfin = _ssm(u_tm, lw, h0, B, S)
        o_mla = _mla(qt, kcat, ckvt, lw["wv"])
        o_swa = _swa_ctx(lw["sink"], qs, ks, vs)
        state = hfin.reshape(2, B, 2, SSM_GROUPS, SSM_STATE).transpose(1, 0, 2, 3, 4)
        new_ctx = (pre[7], pre[8], pre[9].reshape(B, S, SWA_KV_HEADS, HEAD_DIM),
                   pre[10].reshape(B, S, SWA_KV_HEADS, HEAD_DIM), state)
    else:
        kcat_c, ckvt_c, ks_c, vs_c, h0 = ctx
        y2, _ = _ssm(u_tm, lw, h0, B, S)
        o_mla = _mla(qt, jnp.concatenate([kcat_c, kcat], axis=1), jnp.concatenate([ckvt_c, ckvt], axis=2), lw["wv"])
        o_swa = _swa_lat(lw["sink_row"], qs, ks_c, vs_c, ks, vs)
        new_ctx = None
    x1, h2, gates = _post(x, u_tm, y2, o_mla, o_swa, mod, lw, per_batch_mod=per_batch_mod)
    xo = _moe(h2, gates, x1, mod, lw, final_norm, per_batch_mod=per_batch_mod, final=final, rt=min(S, POST_TILE))
    return xo, new_ctx


def kernel(x_prompt, x_sample, c, cache_mla_ckv, cache_mla_krope, cache_swa_k, cache_swa_v, state_ssm, c_ctx, w_ada, b_ada, norm1, norm2, w_in, ssm_a_re, ssm_a_im, ssm_log_dt, ssm_b_re, ssm_b_im, ssm_c_re, ssm_c_im, ssm_d, w_ssm_glu, mla_q_norm, w_mla_qb, mla_kv_norm, w_mla_kvb, swa_sink, gn_ssm, gn_mla, gn_swa, w_out, moe_w_group, moe_b_group, moe_w_expert, moe_b_expert, moe_w_gate, moe_w_up, moe_w_down, final_norm):
    P = dict(w_ada=w_ada, b_ada=b_ada, norm1=norm1, norm2=norm2, w_in=w_in,
             ssm_c_re=ssm_c_re, ssm_c_im=ssm_c_im, ssm_d=ssm_d, w_ssm_glu=w_ssm_glu,
             mla_q_norm=mla_q_norm, w_mla_qb=w_mla_qb, mla_kv_norm=mla_kv_norm, w_mla_kvb=w_mla_kvb,
             swa_sink=swa_sink, gn_ssm=gn_ssm, gn_mla=gn_mla, gn_swa=gn_swa, w_out=w_out,
             moe_w_group=moe_w_group, moe_b_group=moe_b_group, moe_w_expert=moe_w_expert,
             moe_b_expert=moe_b_expert, moe_w_gate=moe_w_gate, moe_w_up=moe_w_up, moe_w_down=moe_w_down)
    n_dec = c.shape[0]
    n_cond = 2 * SUBLANES
    conds = jnp.zeros((n_cond, D_MODEL), F32).at[:n_dec].set(c).at[n_dec].set(c_ctx)
    mods = _modulation(conds, w_ada, b_ada).reshape(DEPTH, n_cond, 6, D_MODEL)

    ab_re, ab_im, bb_re, bb_im = _ssm_discretise(ssm_a_re, ssm_a_im, ssm_log_dt, ssm_b_re, ssm_b_im)
    q_abs = _absorb_q(jnp.transpose(w_mla_qb[..., :MLA_NOPE], (0, 2, 1, 3)),
                      jnp.transpose(w_mla_kvb[..., :MLA_NOPE], (0, 2, 1, 3)))
    lws = [_layer_weights(l, P, q_abs, ab_re.reshape(2 * DEPTH, SSM_N), ab_im.reshape(2 * DEPTH, SSM_N),
                          bb_re, bb_im) for l in range(DEPTH)]
    fnorm = final_norm.reshape(1, D_MODEL)

    xp = x_prompt
    ctx_states = []
    for l in range(DEPTH):
        xp, new = _layer(xp, mods[l, n_dec:n_dec + 1], lws[l], fnorm, tables=None, ctx=None,
                         per_batch_mod=False, final=l == DEPTH - 1)
        ctx_states.append(new)
    outs_ctx = tuple(jnp.stack([s[k] for s in ctx_states], axis=1) for k in range(5))

    n_lat = x_sample.shape[1]
    cm, sm = _rope_tables(n_lat, MLA_ROPE, 1)
    cs, ss = _rope_tables(n_lat, HEAD_DIM, LANES // HEAD_DIM)
    tables = (cm, sm, cm[:, :MLA_ROPE].T, sm[:, :MLA_ROPE].T, cs, ss, cs[:, :HEAD_DIM].T, ss[:, :HEAD_DIM].T)
    xs = x_sample
    past = cache_mla_ckv.shape[2]
    for l in range(DEPTH):
        kcat_c = jnp.concatenate(
            [cache_mla_ckv[:, l], cache_mla_krope[:, l],
             jnp.zeros((n_dec, past, LANES - MLA_ROPE), F32)], axis=-1).astype(BF16)
        ks_c = cache_swa_k[:, l].reshape(n_dec, past, LANES).astype(BF16)
        vs_c = jnp.transpose(cache_swa_v[:, l].reshape(n_dec, past, LANES), (0, 2, 1)).astype(BF16)
        h0 = state_ssm[:, l].transpose(1, 0, 2, 3, 4).reshape(2, n_dec // SUBLANES, SUBLANES, 2 * SSM_N)
        ckvt_c = jnp.transpose(cache_mla_ckv[:, l], (0, 2, 1)).astype(BF16)
        xs, _ = _layer(xs, mods[l, :n_dec], lws[l], fnorm, tables=tables, ctx=(kcat_c, ckvt_c, ks_c, vs_c, h0),
                       per_batch_mod=True, final=l == DEPTH - 1)
    return (xp, xs) + outs_ctx
```
